```python
import math
import jax, jax.numpy as jnp
from jax import lax
import numpy as np

D_MODEL = 2048
BATCH = 8
SEQ = 8192
DEPTH = 1

CHUNK = 64
Q_BLOCK = 128

MLA_HEADS = 8
MLA_Q_LORA = 512
MLA_KV_LORA = 256
MLA_NOPE = 128
MLA_ROPE = 64
MLA_V = 128
ROPE_THETA = 10000.0

FOX_HEADS = 8
FOX_HEAD_DIM = 128

D_FF = 4 * D_MODEL

EPS = 1e-6

OFF_CQ = 0
OFF_CKV = OFF_CQ + MLA_Q_LORA
OFF_KR = OFF_CKV + MLA_KV_LORA
OFF_FQ = OFF_KR + MLA_ROPE
OFF_FK = OFF_FQ + FOX_HEADS * FOX_HEAD_DIM
OFF_FV = OFF_FK + FOX_HEADS * FOX_HEAD_DIM
OFF_FF = OFF_FV + FOX_HEADS * FOX_HEAD_DIM
OFF_G = OFF_FF + FOX_HEADS
D_IN = OFF_G + 2 * D_MODEL

kernel_name = "hybrid_mla_fox_gated_block"


def rmsnorm(x, g):
    xf = x.astype(jnp.float32)
    y = xf * lax.rsqrt(jnp.mean(xf * xf, axis=-1, keepdims=True) + EPS)
    return (y * g.astype(jnp.float32)).astype(x.dtype)


def rope_tables(seq_len):
    pos = jnp.arange(seq_len, dtype=jnp.float32)
    inv = 1.0 / (ROPE_THETA ** (jnp.arange(0, MLA_ROPE, 2, dtype=jnp.float32) / MLA_ROPE))
    ang = pos[:, None] * inv[None, :]
    return jnp.cos(ang), jnp.sin(ang)


def apply_rope(x, cos, sin):
    half = x.shape[-1] // 2
    x1, x2 = x[..., :half], x[..., half:]
    c = cos.astype(x.dtype)
    s = sin.astype(x.dtype)
    return jnp.concatenate([x1 * c - x2 * s, x1 * s + x2 * c], axis=-1)


def block_attention(q, k, v, scale, chunk_causal, cum=None):
    B, H, S, Dq = q.shape
    Dv = v.shape[-1]
    nb = S // Q_BLOCK
    qb = q.reshape(B, H, nb, Q_BLOCK, Dq).transpose(2, 0, 1, 3, 4)
    k_pos = jnp.arange(S)
    xs = (jnp.arange(nb), qb)
    if cum is not None:
        cb = cum.reshape(B, H, nb, Q_BLOCK).transpose(2, 0, 1, 3)
        xs = xs + (cb,)

    def one_block(args):
        i, q_blk = args[0], args[1]
        s = jnp.einsum('bhqd,bhkd->bhqk', q_blk, k,
                       preferred_element_type=jnp.float32) * scale
        q_pos = i * Q_BLOCK + jnp.arange(Q_BLOCK)
        if chunk_causal:
            mask = (k_pos // CHUNK)[None, :] <= (q_pos // CHUNK)[:, None]
        else:
            mask = k_pos[None, :] <= q_pos[:, None]
        if cum is not None:
            c_blk = args[2]
            s = s + c_blk[..., :, None] - cum[:, :, None, :]
        s = jnp.where(mask, s, -jnp.inf)
        p = jax.nn.softmax(s, axis=-1)
        return jnp.einsum('bhqk,bhkd->bhqd', p.astype(v.dtype), v)

    out = lax.map(one_block, xs)
    return out.transpose(1, 2, 0, 3, 4).reshape(B, H, S, Dv)


def mla_branch(proj, q_norm, w_uq, kv_norm, w_ukv, cos, sin):
    B, S, _ = proj.shape
    cq = rmsnorm(proj[..., OFF_CQ:OFF_CKV], q_norm)
    q = (cq @ w_uq).reshape(B, S, MLA_HEADS, MLA_NOPE + MLA_ROPE).transpose(0, 2, 1, 3)
    ckv = rmsnorm(proj[..., OFF_CKV:OFF_KR], kv_norm)
    kv = (ckv @ w_ukv).reshape(B, S, MLA_HEADS, MLA_NOPE + MLA_V).transpose(0, 2, 1, 3)
    k_nope, v = kv[..., :MLA_NOPE], kv[..., MLA_NOPE:]
    k_rope = apply_rope(proj[:, None, :, OFF_KR:OFF_FQ], cos, sin)
    q = jnp.concatenate([q[..., :MLA_NOPE], apply_rope(q[..., MLA_NOPE:], cos, sin)], axis=-1)
    k = jnp.concatenate(
        [k_nope, jnp.broadcast_to(k_rope, (B, MLA_HEADS, S, MLA_ROPE))], axis=-1)
    o = block_attention(q, k, v, 1.0 / math.sqrt(MLA_NOPE + MLA_ROPE), chunk_causal=True)
    return o.transpose(0, 2, 1, 3).reshape(B, S, MLA_HEADS * MLA_V)


def fox_branch(proj, f_bias):
    B, S, _ = proj.shape

    def heads(a):
        return a.reshape(B, S, FOX_HEADS, FOX_HEAD_DIM).transpose(0, 2, 1, 3)

    q = heads(proj[..., OFF_FQ:OFF_FK])
    k = heads(proj[..., OFF_FK:OFF_FV])
    v = heads(proj[..., OFF_FV:OFF_FF])
    logf = jax.nn.log_sigmoid((proj[..., OFF_FF:OFF_G] + f_bias).astype(jnp.float32))
    cum = jnp.cumsum(logf, axis=1).transpose(0, 2, 1)
    o = block_attention(q, k, v, 1.0 / math.sqrt(FOX_HEAD_DIM), chunk_causal=False, cum=cum)
    return o.transpose(0, 2, 1, 3).reshape(B, S, FOX_HEADS * FOX_HEAD_DIM)


def _fwd_setup_inputs(seed: int = 0) -> dict:
    key = jax.random.key(seed)
    ks = jax.random.split(key, 20)
    f32 = jnp.float32

    def w(k, shape, fan_in):
        return jax.random.normal(k, shape, f32) * (fan_in ** -0.5)

    def gain(k, shape):
        return 1.0 + 0.02 * jax.random.normal(k, shape, f32)

    return {
        "x": jax.random.normal(ks[0], (BATCH, SEQ, D_MODEL), f32),
        "attn_norm": gain(ks[1], (DEPTH, D_MODEL)),
        "w_in": w(ks[2], (DEPTH, D_MODEL, D_IN), D_MODEL),
        "fox_f_bias": jax.random.uniform(ks[3], (DEPTH, FOX_HEADS), f32, 1.0, 6.0),
        "q_norm": gain(ks[4], (DEPTH, MLA_Q_LORA)),
        "w_uq": w(ks[5], (DEPTH, MLA_Q_LORA, MLA_HEADS * (MLA_NOPE + MLA_ROPE)), MLA_Q_LORA),
        "kv_norm": gain(ks[6], (DEPTH, MLA_KV_LORA)),
        "w_ukv": w(ks[7], (DEPTH, MLA_KV_LORA, MLA_HEADS * (MLA_NOPE + MLA_V)), MLA_KV_LORA),
        "w_mla_branch": w(ks[8], (DEPTH, MLA_HEADS * MLA_V, D_MODEL), MLA_HEADS * MLA_V),
        "w_fox_branch": w(ks[9], (DEPTH, FOX_HEADS * FOX_HEAD_DIM, D_MODEL), FOX_HEADS * FOX_HEAD_DIM),
        "w_out": w(ks[10], (DEPTH, D_MODEL, D_MODEL), D_MODEL),
        "mlp_norm": gain(ks[11], (DEPTH, D_MODEL)),
        "w_up": w(ks[12], (DEPTH, D_MODEL, D_FF), D_MODEL),
        "w_down": w(ks[13], (DEPTH, D_FF, D_MODEL), D_FF),
        "final_norm": gain(ks[14], (D_MODEL,)),
    }


def _fwd_reference(x, attn_norm, w_in, fox_f_bias, q_norm, w_uq, kv_norm, w_ukv,
              w_mla_branch, w_fox_branch, w_out, mlp_norm, w_up, w_down, final_norm):
    S = x.shape[1]
    cos, sin = rope_tables(S)
    h = x
    for l in range(DEPTH):
        xn = rmsnorm(h, attn_norm[l])
        proj = xn @ w_in[l]
        y_mla = mla_branch(proj, q_norm[l], w_uq[l], kv_norm[l], w_ukv[l], cos, sin) @ w_mla_branch[l]
        y_fox = fox_branch(proj, fox_f_bias[l]) @ w_fox_branch[l]
        gates = jax.nn.sigmoid(proj[..., OFF_G:])
        g_mla, g_fox = gates[..., :D_MODEL], gates[..., D_MODEL:]
        h = h + (g_mla * y_mla + g_fox * y_fox) @ w_out[l]
        hn = rmsnorm(h, mlp_norm[l])
        u = jnp.square(jax.nn.relu(hn @ w_up[l]))
        h = h + u @ w_down[l]
    return rmsnorm(h, final_norm)


import jax as _jax
import jax.numpy as _jnp

TWIN_FORMAT = 'train_step'
FWD_PARAMS = ['x', 'attn_norm', 'w_in', 'fox_f_bias', 'q_norm', 'w_uq', 'kv_norm', 'w_ukv', 'w_mla_branch', 'w_fox_branch', 'w_out', 'mlp_norm', 'w_up', 'w_down', 'final_norm']
TWIN_WEIGHTS = ['attn_norm', 'w_in', 'fox_f_bias', 'q_norm', 'w_uq', 'kv_norm', 'w_ukv', 'w_mla_branch', 'w_fox_branch', 'w_out', 'mlp_norm', 'w_up', 'w_down', 'final_norm']
TWIN_DIFF_INPUT = 'x'
TWIN_INPUTS = ['x', 'attn_norm', 'w_in', 'fox_f_bias', 'q_norm', 'w_uq', 'kv_norm', 'w_ukv', 'w_mla_branch', 'w_fox_branch', 'w_out', 'mlp_norm', 'w_up', 'w_down', 'final_norm', 'loss_target', 'm_attn_norm', 'm_w_in', 'm_fox_f_bias', 'm_q_norm', 'm_w_uq', 'm_kv_norm', 'm_w_ukv', 'm_w_mla_branch', 'm_w_fox_branch', 'm_w_out', 'm_mlp_norm', 'm_w_up', 'm_w_down', 'm_final_norm', 'v_attn_norm', 'v_w_in', 'v_fox_f_bias', 'v_q_norm', 'v_w_uq', 'v_kv_norm', 'v_w_ukv', 'v_w_mla_branch', 'v_w_fox_branch', 'v_w_out', 'v_mlp_norm', 'v_w_up', 'v_w_down', 'v_final_norm']
TWIN_OUTPUTS = ['loss', 'grad_x', 'grad_attn_norm', 'grad_w_in', 'grad_fox_f_bias', 'grad_q_norm', 'grad_w_uq', 'grad_kv_norm', 'grad_w_ukv', 'grad_w_mla_branch', 'grad_w_fox_branch', 'grad_w_out', 'grad_mlp_norm', 'grad_w_up', 'grad_w_down', 'grad_final_norm', 'delta_attn_norm', 'delta_w_in', 'delta_fox_f_bias', 'delta_q_norm', 'delta_w_uq', 'delta_kv_norm', 'delta_w_ukv', 'delta_w_mla_branch', 'delta_w_fox_branch', 'delta_w_out', 'delta_mlp_norm', 'delta_w_up', 'delta_w_down', 'delta_final_norm', 'new_m_attn_norm', 'new_m_w_in', 'new_m_fox_f_bias', 'new_m_q_norm', 'new_m_w_uq', 'new_m_kv_norm', 'new_m_w_ukv', 'new_m_w_mla_branch', 'new_m_w_fox_branch', 'new_m_w_out', 'new_m_mlp_norm', 'new_m_w_up', 'new_m_w_down', 'new_m_final_norm', 'new_v_attn_norm', 'new_v_w_in', 'new_v_fox_f_bias', 'new_v_q_norm', 'new_v_w_uq', 'new_v_kv_norm', 'new_v_w_ukv', 'new_v_w_mla_branch', 'new_v_w_fox_branch', 'new_v_w_out', 'new_v_mlp_norm', 'new_v_w_up', 'new_v_w_down', 'new_v_final_norm']
TWIN_LEAF_KINDS = {'loss': 'loss', 'grad_x': 'grad_x', 'grad_attn_norm': 'grad_w', 'grad_w_in': 'grad_w', 'grad_fox_f_bias': 'grad_w', 'grad_q_norm': 'grad_w', 'grad_w_uq': 'grad_w', 'grad_kv_norm': 'grad_w', 'grad_w_ukv': 'grad_w', 'grad_w_mla_branch': 'grad_w', 'grad_w_fox_branch': 'grad_w', 'grad_w_out': 'grad_w', 'grad_mlp_norm': 'grad_w', 'grad_w_up': 'grad_w', 'grad_w_down': 'grad_w', 'grad_final_norm': 'grad_w', 'delta_attn_norm': 'delta_w', 'delta_w_in': 'delta_w', 'delta_fox_f_bias': 'delta_w', 'delta_q_norm': 'delta_w', 'delta_w_uq': 'delta_w', 'delta_kv_norm': 'delta_w', 'delta_w_ukv': 'delta_w', 'delta_w_mla_branch': 'delta_w', 'delta_w_fox_branch': 'delta_w', 'delta_w_out': 'delta_w', 'delta_mlp_norm': 'delta_w', 'delta_w_up': 'delta_w', 'delta_w_down': 'delta_w', 'delta_final_norm': 'delta_w', 'new_m_attn_norm': 'new_m', 'new_m_w_in': 'new_m', 'new_m_fox_f_bias': 'new_m', 'new_m_q_norm': 'new_m', 'new_m_w_uq': 'new_m', 'new_m_kv_norm': 'new_m', 'new_m_w_ukv': 'new_m', 'new_m_w_mla_branch': 'new_m', 'new_m_w_fox_branch': 'new_m', 'new_m_w_out': 'new_m', 'new_m_mlp_norm': 'new_m', 'new_m_w_up': 'new_m', 'new_m_w_down': 'new_m', 'new_m_final_norm': 'new_m', 'new_v_attn_norm': 'new_v', 'new_v_w_in': 'new_v', 'new_v_fox_f_bias': 'new_v', 'new_v_q_norm': 'new_v', 'new_v_w_uq': 'new_v', 'new_v_kv_norm': 'new_v', 'new_v_w_ukv': 'new_v', 'new_v_w_mla_branch': 'new_v', 'new_v_w_fox_branch': 'new_v', 'new_v_w_out': 'new_v', 'new_v_mlp_norm': 'new_v', 'new_v_w_up': 'new_v', 'new_v_w_down': 'new_v', 'new_v_final_norm': 'new_v'}


def _forward(args):
    return _fwd_reference(*[args[k] for k in FWD_PARAMS])


def _output_shape():
    def fwd():
        inp = _fwd_setup_inputs(0)
        return _fwd_reference(*[inp[k] for k in FWD_PARAMS])
    out = _jax.eval_shape(fwd)
    return out.shape, out.dtype

N_MICROBATCH = 1
ADAM_LR = 0.001
ADAM_B1 = 0.9
ADAM_B2 = 0.999
ADAM_EPS = 1e-08
ADAM_WD = 0.01
ADAM_STEP = 10
PER_EXAMPLE_BATCH_AXIS = {'x': 0, 'loss_target': 0}
SHARED_INPUTS = []
_WEIGHT_DTYPES = {'attn_norm': _jnp.float32, 'w_in': _jnp.float32, 'fox_f_bias': _jnp.float32, 'q_norm': _jnp.float32, 'w_uq': _jnp.float32, 'kv_norm': _jnp.float32, 'w_ukv': _jnp.float32, 'w_mla_branch': _jnp.float32, 'w_fox_branch': _jnp.float32, 'w_out': _jnp.float32, 'mlp_norm': _jnp.float32, 'w_up': _jnp.float32, 'w_down': _jnp.float32, 'final_norm': _jnp.float32}
MOMENT_SCALE = {'attn_norm': 4.265041e-02, 'w_in': 2.128120e-02, 'fox_f_bias': 1.610596e-01, 'q_norm': 2.167137e-02, 'w_uq': 1.237711e-02, 'kv_norm': 4.272762e-02, 'w_ukv': 1.433683e-02, 'w_mla_branch': 1.111386e-02, 'w_fox_branch': 2.460904e-02, 'w_out': 2.700102e-02, 'mlp_norm': 1.126747e-01, 'w_up': 5.587554e-02, 'w_down': 1.199873e-01, 'final_norm': 3.226518e+01}


def _to_microbatches(a, axis):
    t = _jnp.moveaxis(a, axis, 0)
    t = t.reshape((N_MICROBATCH, t.shape[0] // N_MICROBATCH) + t.shape[1:])
    return _jnp.moveaxis(t, 1, axis + 1)


def setup_inputs(seed: int = 0) -> dict:
    inp = _fwd_setup_inputs(seed)
    key = _jax.random.fold_in(_jax.random.key(seed), 7919)
    shape, _ = _output_shape()
    out = dict(inp)
    out["loss_target"] = _jax.random.normal(_jax.random.fold_in(key, 0), shape, _jnp.float32)
    for i, name in enumerate(TWIN_WEIGHTS):
        w = inp[name].astype(_jnp.float32)
        if MOMENT_SCALE is None:
            s = _jnp.sqrt(_jnp.mean(_jnp.square(w)) + 1e-30)
        else:
            s = MOMENT_SCALE[name]
        km, kv = _jax.random.split(_jax.random.fold_in(key, i + 1))
        out[name] = w
        out["m_" + name] = s * _jax.random.normal(km, w.shape, _jnp.float32)
        out["v_" + name] = (s * s) * _jax.random.uniform(kv, w.shape, _jnp.float32, 0.5, 1.5)
    if N_MICROBATCH > 1:
        for name, axis in PER_EXAMPLE_BATCH_AXIS.items():
            out[name] = _to_microbatches(out[name], axis)
    return {'x': out['x'], 'attn_norm': out['attn_norm'], 'w_in': out['w_in'], 'fox_f_bias': out['fox_f_bias'], 'q_norm': out['q_norm'], 'w_uq': out['w_uq'], 'kv_norm': out['kv_norm'], 'w_ukv': out['w_ukv'], 'w_mla_branch': out['w_mla_branch'], 'w_fox_branch': out['w_fox_branch'], 'w_out': out['w_out'], 'mlp_norm': out['mlp_norm'], 'w_up': out['w_up'], 'w_down': out['w_down'], 'final_norm': out['final_norm'], 'loss_target': out['loss_target'], 'm_attn_norm': out['m_attn_norm'], 'm_w_in': out['m_w_in'], 'm_fox_f_bias': out['m_fox_f_bias'], 'm_q_norm': out['m_q_norm'], 'm_w_uq': out['m_w_uq'], 'm_kv_norm': out['m_kv_norm'], 'm_w_ukv': out['m_w_ukv'], 'm_w_mla_branch': out['m_w_mla_branch'], 'm_w_fox_branch': out['m_w_fox_branch'], 'm_w_out': out['m_w_out'], 'm_mlp_norm': out['m_mlp_norm'], 'm_w_up': out['m_w_up'], 'm_w_down': out['m_w_down'], 'm_final_norm': out['m_final_norm'], 'v_attn_norm': out['v_attn_norm'], 'v_w_in': out['v_w_in'], 'v_fox_f_bias': out['v_fox_f_bias'], 'v_q_norm': out['v_q_norm'], 'v_w_uq': out['v_w_uq'], 'v_kv_norm': out['v_kv_norm'], 'v_w_ukv': out['v_w_ukv'], 'v_w_mla_branch': out['v_w_mla_branch'], 'v_w_fox_branch': out['v_w_fox_branch'], 'v_w_out': out['v_w_out'], 'v_mlp_norm': out['v_mlp_norm'], 'v_w_up': out['v_w_up'], 'v_w_down': out['v_w_down'], 'v_final_norm': out['v_final_norm']}


def _loss(weights, diff, rest, loss_target):
    with _jax.named_scope("forward"):
        args = {**rest, TWIN_DIFF_INPUT: diff, **{k: w.astype(_WEIGHT_DTYPES[k]) for k, w in weights.items()}}
        y = _forward(args)
    with _jax.named_scope("loss_head"):
        err = _jnp.square(y.astype(_jnp.float32) - loss_target)
        return 0.5 * _jnp.sum(_jnp.mean(err, axis=-1)) if err.ndim else 0.5 * err


def _adamw(w, g, m, v):
    m = ADAM_B1 * m + (1.0 - ADAM_B1) * g
    v = ADAM_B2 * v + (1.0 - ADAM_B2) * _jnp.square(g)
    m_hat = m / (1.0 - ADAM_B1 ** ADAM_STEP)
    v_hat = v / (1.0 - ADAM_B2 ** ADAM_STEP)
    delta = -ADAM_LR * (m_hat / (_jnp.sqrt(v_hat) + ADAM_EPS) + ADAM_WD * w)
    return delta, m, v


def reference(x, attn_norm, w_in, fox_f_bias, q_norm, w_uq, kv_norm, w_ukv, w_mla_branch, w_fox_branch, w_out, mlp_norm, w_up, w_down, final_norm, loss_target, m_attn_norm, m_w_in, m_fox_f_bias, m_q_norm, m_w_uq, m_kv_norm, m_w_ukv, m_w_mla_branch, m_w_fox_branch, m_w_out, m_mlp_norm, m_w_up, m_w_down, m_final_norm, v_attn_norm, v_w_in, v_fox_f_bias, v_q_norm, v_w_uq, v_kv_norm, v_w_ukv, v_w_mla_branch, v_w_fox_branch, v_w_out, v_mlp_norm, v_w_up, v_w_down, v_final_norm):
    given = dict(x=x, attn_norm=attn_norm, w_in=w_in, fox_f_bias=fox_f_bias, q_norm=q_norm, w_uq=w_uq, kv_norm=kv_norm, w_ukv=w_ukv, w_mla_branch=w_mla_branch, w_fox_branch=w_fox_branch, w_out=w_out, mlp_norm=mlp_norm, w_up=w_up, w_down=w_down, final_norm=final_norm, loss_target=loss_target, m_attn_norm=m_attn_norm, m_w_in=m_w_in, m_fox_f_bias=m_fox_f_bias, m_q_norm=m_q_norm, m_w_uq=m_w_uq, m_kv_norm=m_kv_norm, m_w_ukv=m_w_ukv, m_w_mla_branch=m_w_mla_branch, m_w_fox_branch=m_w_fox_branch, m_w_out=m_w_out, m_mlp_norm=m_mlp_norm, m_w_up=m_w_up, m_w_down=m_w_down, m_final_norm=m_final_norm, v_attn_norm=v_attn_norm, v_w_in=v_w_in, v_fox_f_bias=v_fox_f_bias, v_q_norm=v_q_norm, v_w_uq=v_w_uq, v_kv_norm=v_kv_norm, v_w_ukv=v_w_ukv, v_w_mla_branch=v_w_mla_branch, v_w_fox_branch=v_w_fox_branch, v_w_out=v_w_out, v_mlp_norm=v_mlp_norm, v_w_up=v_w_up, v_w_down=v_w_down, v_final_norm=v_final_norm)
    weights = {n: given[n] for n in TWIN_WEIGHTS}
    shared = {n: given[n] for n in SHARED_INPUTS}
    per_example = {n: given[n] for n in ['x']}
    grad_fn = _jax.value_and_grad(_loss, argnums=(0, 1))

    def one_microbatch(ex, loss_target):
        ex = dict(ex)
        diff = ex.pop(TWIN_DIFF_INPUT)
        return grad_fn(weights, diff, {**shared, **ex}, loss_target)

    if N_MICROBATCH == 1:
        loss, (grad_w, grad_x) = one_microbatch(per_example, given["loss_target"])
    else:
        def body(carry, xs):
            loss_sum, grad_sum = carry
            l_k, (gw_k, gx_k) = one_microbatch(xs[0], xs[1])
            with _jax.named_scope("update"):
                return (loss_sum + l_k, _jax.tree.map(_jnp.add, grad_sum, gw_k)), gx_k

        init = (_jnp.zeros((), _jnp.float32), _jax.tree.map(_jnp.zeros_like, weights))
        (loss, grad_w), grad_x = _jax.lax.scan(body, init, (per_example, given["loss_target"]))
    with _jax.named_scope("update"):
        delta_w, new_m, new_v = {}, {}, {}
        for n in TWIN_WEIGHTS:
            delta_w[n], new_m[n], new_v[n] = _adamw(weights[n], grad_w[n], given["m_" + n], given["v_" + n])
    return (loss, grad_x, *[grad_w[n] for n in TWIN_WEIGHTS], *[delta_w[n] for n in TWIN_WEIGHTS],
            *[new_m[n] for n in TWIN_WEIGHTS], *[new_v[n] for n in TWIN_WEIGHTS])
```

```python
import functools
import math

import jax
import jax.numpy as jnp
from jax import lax
from jax.experimental import pallas as pl
from jax.experimental.pallas import tpu as pltpu

F32 = jnp.float32
BF16 = jnp.bfloat16
MESH = pl.DeviceIdType.MESH

D_MODEL = 2048
HEADS = 8
Q_LORA = 512
KV_LORA = 256
NOPE = 128
ROPE = 64
HEAD_V = 128
D_FF = 4 * D_MODEL
CHUNK = 64
EPS = 1e-6
ROPE_THETA = 10000.0
OFF_KR = Q_LORA + KV_LORA
OFF_FQ = OFF_KR + ROPE
OFF_FF = OFF_FQ + 3 * HEADS * HEAD_V
OFF_G = OFF_FF + HEADS
D_IN = OFF_G + 2 * D_MODEL

LAT_W = 896
FOX_W = 3 * HEADS * HEAD_V
GATE_W = 2 * D_MODEL
PROJ_W = LAT_W + FOX_W + GATE_W
QK_PAD = 256

ADAM_LR = 0.001
ADAM_B1 = 0.9
ADAM_B2 = 0.999
ADAM_EPS = 1e-08
ADAM_WD = 0.01
ADAM_STEP = 10

N_DEV = 8
PACK_C = 1024
NEG = -1e30

VMEM_LIMIT = 56 * 1024 * 1024

BIG = ("w_in", "w_uq", "w_ukv", "w_mla_branch", "w_fox_branch", "w_out", "w_up", "w_down")
ROW_SHARDED = ("w_out", "w_down")
SMALL = ("attn_norm", "fox_f_bias", "q_norm", "kv_norm", "mlp_norm", "final_norm")
SMALL_N = {"attn_norm": D_MODEL, "fox_f_bias": HEADS, "q_norm": Q_LORA, "kv_norm": KV_LORA,
           "mlp_norm": D_MODEL, "final_norm": D_MODEL}
SMALL_ROWS = 8


def _params(sem):
    return pltpu.CompilerParams(dimension_semantics=sem, vmem_limit_bytes=VMEM_LIMIT)


def _rows(name, fn, row_ins, const_ins, outs, reds=(), tr=256):
    norm = [(a, a.shape[1], 0) if not isinstance(a, tuple) else a for a in row_ins]
    n_rows = norm[0][0].shape[0]
    tr = min(tr, n_rows)
    assert n_rows % tr == 0, (name, n_rows, tr)
    n_in, n_out, n_red = len(norm) + len(const_ins), len(outs), len(reds)

    def body(*refs):
        vals = [r[...] for r in refs[:n_in]]
        out_refs = refs[n_in:n_in + n_out]
        red_refs = refs[n_in + n_out:]
        out_vals, red_vals = fn(*vals)
        for r, v in zip(out_refs, out_vals):
            r[...] = v.astype(r.dtype)
        if n_red:
            @pl.when(pl.program_id(0) == 0)
            def _():
                for r in red_refs:
                    r[...] = jnp.zeros_like(r)
            for r, v in zip(red_refs, red_vals):
                r[...] += v

    in_specs = [pl.BlockSpec((tr, w), functools.partial(lambda i, cb: (i, cb), cb=cb)) for _, w, cb in norm]
    in_specs += [pl.BlockSpec(a.shape, lambda i: (0, 0)) for a in const_ins]
    out_specs = [pl.BlockSpec((tr, c), lambda i: (i, 0)) for c, _ in outs]
    out_specs += [pl.BlockSpec((1, c), lambda i: (0, 0)) for c in reds]
    out_shape = [jax.ShapeDtypeStruct((n_rows, c), dt) for c, dt in outs]
    out_shape += [jax.ShapeDtypeStruct((1, c), F32) for c in reds]
    res = pl.pallas_call(
        body, name=name, grid=(n_rows // tr,), in_specs=in_specs, out_specs=out_specs, out_shape=out_shape,
        compiler_params=_params(("arbitrary",)),
    )(*[a for a, _, _ in norm], *const_ins)
    return res


def _rstd(x):
    return lax.rsqrt(jnp.mean(x * x, axis=-1, keepdims=True) + EPS)


def rms_fwd(name, x, gain, tr=256):
    width = x[1] if isinstance(x, tuple) else x.shape[1]

    def fn(xv, g):
        return (xv * _rstd(xv) * g,), ()
    return _rows(name, fn, [x], [gain], [(width, BF16)], tr=tr)[0]


def rms_bwd(name, x, dy, gain, dres=None, out_dtype=F32, tr=256):
    width = x[1] if isinstance(x, tuple) else x.shape[1]

    def fn(xv, dyv, *rest):
        g = rest[-1]
        r = _rstd(xv)
        n = xv * r
        dyv = dyv.astype(F32)
        dn = dyv * g
        dx = r * (dn - n * jnp.mean(dn * n, axis=-1, keepdims=True))
        if dres is not None:
            dx = dx + rest[0]
        return (dx,), (jnp.sum(dyv * n, axis=0, keepdims=True),)

    ins = [x, dy] + ([dres] if dres is not None else [])
    return _rows(name, fn, ins, [gain], [(width, out_dtype)], [width], tr=tr)


def _rope_lanes(t, c, s1, s2):
    return t * c + pltpu.roll(t, 96, 1) * s1 + pltpu.roll(t, 32, 1) * s2


def rope_heads(name, x, tabs, out_dtype):
    def fn(xv, c, s1, s2):
        xv = xv.astype(F32)
        parts = []
        for h in range(HEADS):
            parts.append(xv[:, h * QK_PAD:h * QK_PAD + NOPE])
            parts.append(_rope_lanes(xv[:, h * QK_PAD + NOPE:(h + 1) * QK_PAD], c, s1, s2))
        return (jnp.concatenate(parts, axis=1),), ()
    return _rows(name, fn, [x, *tabs], [], [(HEADS * QK_PAD, out_dtype)])[0]


def rope_block(name, x, col_block, tabs, out_dtype):
    def fn(xv, c, s1, s2):
        return (_rope_lanes(xv.astype(F32), c, s1, s2),), ()
    return _rows(name, fn, [(x, 128, col_block), *tabs], [], [(128, out_dtype)])[0]


def k_assemble(name, kvn, kr):
    def fn(knp, krv):
        parts = []
        for h in range(HEADS):
            parts.append(knp[:, h * NOPE:(h + 1) * NOPE])
            parts.append(krv)
        return (jnp.concatenate(parts, axis=1),), ()
    return _rows(name, fn, [(kvn, HEADS * NOPE, 0), kr], [], [(HEADS * QK_PAD, BF16)])[0]


def dk_split(name, dk, dv, inv_tabs):
    def fn(dkv, dvv, c, s1, s2):
        parts = []
        acc = None
        for h in range(HEADS):
            parts.append(dkv[:, h * QK_PAD:h * QK_PAD + NOPE].astype(BF16))
            t = dkv[:, h * QK_PAD + NOPE:(h + 1) * QK_PAD]
            acc = t if acc is None else acc + t
        parts.append(dvv)
        return (jnp.concatenate(parts, axis=1), _rope_lanes(acc, c, s1, s2)), ()
    return _rows(name, fn, [dk, dv, *inv_tabs], [], [(2 * HEADS * NOPE, BF16), (128, F32)])


def gate_mix(name, graw, y_mla, y_fox):
    def fn(g, ya, yb):
        ga = jax.nn.sigmoid(g[:, :D_MODEL])
        gb = jax.nn.sigmoid(g[:, D_MODEL:])
        return (ga * ya + gb * yb,), ()
    return _rows(name, fn, [graw, y_mla, y_fox], [], [(D_MODEL, BF16)])[0]


def gate_mix_bwd(name, graw, y_mla, y_fox, dmix):
    def fn(g, ya, yb, dm):
        ga = jax.nn.sigmoid(g[:, :D_MODEL])
        gb = jax.nn.sigmoid(g[:, D_MODEL:])
        dgraw = jnp.concatenate([dm * ya * ga * (1.0 - ga), dm * yb * gb * (1.0 - gb)], axis=1)
        return (dgraw, dm * ga, dm * gb), ()
    return _rows(name, fn, [graw, y_mla, y_fox, dmix], [], [(GATE_W, BF16), (D_MODEL, BF16), (D_MODEL, BF16)], tr=128)


def loss_head(name, h2, target, gain):
    inv_d = 1.0 / D_MODEL

    def fn(h, t, g):
        r = _rstd(h)
        n = h * r
        err = n * g - t
        dy = err * inv_d
        dn = dy * g
        dh = r * (dn - n * jnp.mean(dn * n, axis=-1, keepdims=True))
        part = 0.5 * inv_d * jnp.sum(jnp.sum(err * err, axis=1, keepdims=True), axis=0, keepdims=True)
        return (dh,), (jnp.sum(dy * n, axis=0, keepdims=True), jnp.broadcast_to(part, (1, 128)))
    return _rows(name, fn, [h2, target], [gain], [(D_MODEL, F32)], [D_MODEL, 128])


def adamw(name, w, g, m, v, tr=256):
    c1 = 1.0 - ADAM_B1 ** ADAM_STEP
    c2 = 1.0 - ADAM_B2 ** ADAM_STEP

    def fn(wv, gv, mv, vv):
        m_new = ADAM_B1 * mv + (1.0 - ADAM_B1) * gv
        v_new = ADAM_B2 * vv + (1.0 - ADAM_B2) * (gv * gv)
        delta = -ADAM_LR * ((m_new / c1) / (jnp.sqrt(v_new / c2) + ADAM_EPS) + ADAM_WD * wv)
        return (delta, m_new, v_new), ()
    cols = w.shape[1]
    return _rows(name, fn, [w, g, m, v], [], [(cols, F32)] * 3, tr=tr)


def add_pairs(name, a, b):
    def fn(av, bv):
        return (av.astype(F32) + bv.astype(F32),), ()
    return _rows(name, fn, [a, b], [], [(a.shape[1], BF16)], tr=384)[0]


def add_final(name, a, b, r0, r1, r2):
    def fn(av, bv, r0v, r1v, r2v):
        return (((av.astype(F32) + bv.astype(F32)) + r0v.astype(F32)) + r1v.astype(F32) + r2v.astype(F32),), ()
    return _rows(name, fn, [a, b, r0, r1, r2], [], [(a.shape[1], F32)], tr=384)[0]


_DIMS = {"nn": (((1,), (0,)), ((), ())), "nt": (((1,), (1,)), ((), ())), "tn": (((0,), (0,)), ((), ()))}


def matmul(name, a, b, mode, outs, epilogue=None, extras=(), tm=1024, tn=1024, tk=2048):
    if mode == "tn":
        kdim, m = a.shape
    else:
        m, kdim = a.shape
    n = b.shape[0] if mode == "nt" else b.shape[1]
    tm, tn, tk = min(tm, m), min(tn, n), min(tk, kdim)
    assert m % tm == 0 and n % tn == 0 and kdim % tk == 0, (name, a.shape, b.shape)
    nk = kdim // tk
    n_ex, n_out = len(extras), len(outs)
    dims = _DIMS[mode]

    def body(a_ref, b_ref, *rest):
        ex_refs = rest[:n_ex]
        out_refs = rest[n_ex:n_ex + n_out]

        def finish(acc):
            vals = (acc,) if epilogue is None else epilogue(acc, *[r[...] for r in ex_refs])
            for r, v in zip(out_refs, vals):
                r[...] = v.astype(r.dtype)

        part = lax.dot_general(a_ref[...].astype(BF16), b_ref[...].astype(BF16), dims, preferred_element_type=F32)
        if nk == 1:
            finish(part)
        else:
            acc_ref = rest[-1]
            k = pl.program_id(2)

            @pl.when(k == 0)
            def _():
                acc_ref[...] = part

            @pl.when(k > 0)
            def _():
                acc_ref[...] += part

            @pl.when(k == nk - 1)
            def _():
                finish(acc_ref[...])

    a_spec = pl.BlockSpec((tk, tm), lambda i, j, k: (k, i)) if mode == "tn" else pl.BlockSpec((tm, tk), lambda i, j, k: (i, k))
    b_spec = pl.BlockSpec((tn, tk), lambda i, j, k: (j, k)) if mode == "nt" else pl.BlockSpec((tk, tn), lambda i, j, k: (k, j))
    tile = pl.BlockSpec((tm, tn), lambda i, j, k: (i, j))
    res = pl.pallas_call(
        body, name=name, grid=(m // tm, n // tn, nk),
        in_specs=[a_spec, b_spec] + [tile] * n_ex,
        out_specs=[tile] * n_out,
        out_shape=[jax.ShapeDtypeStruct((m, n), dt) for dt in outs],
        scratch_shapes=[pltpu.VMEM((tm, tn), F32)] if nk > 1 else [],
        compiler_params=_params(("parallel", "parallel", "arbitrary")),
    )(a, b, *extras)
    return res


_NT = (((1,), (1,)), ((), ()))
_NN = (((1,), (0,)), ((), ()))


def _mask(bq, chunk, transposed):
    row = lax.broadcasted_iota(jnp.int32, (bq, bq), 0)
    col = lax.broadcasted_iota(jnp.int32, (bq, bq), 1)
    if chunk > 1:
        row, col = row // chunk, col // chunk
    return (row <= col) if transposed else (col <= row)


def _row_layout(a, bq):
    h, s, _ = a.shape
    return a.reshape(h, s // bq, 1, bq)


def attn_fwd(name, q, k, v, offs, dqk, chunk, scale, cum=None, bq=512):
    s_len = q.shape[0]
    nq = s_len // bq
    qoff, koff, voff = offs
    has_bias = cum is not None

    def body(*refs):
        if has_bias:
            q_ref, k_ref, v_ref, cc_ref, cr_ref, o_ref, lse_ref, m_s, l_s, acc_s = refs
        else:
            q_ref, k_ref, v_ref, o_ref, lse_ref, m_s, l_s, acc_s = refs
        i = pl.program_id(1)
        qv = q_ref[...]
        m_s[...] = jnp.full_like(m_s, NEG)
        l_s[...] = jnp.zeros_like(l_s)
        acc_s[...] = jnp.zeros_like(acc_s)

        def step(j, masked):
            off = pl.multiple_of(j * bq, bq)
            kj = k_ref[pl.ds(off, bq), :]
            vj = v_ref[pl.ds(off, bq), :]
            s = lax.dot_general(qv, kj, _NT, preferred_element_type=F32) * scale
            if has_bias:
                s = s + cc_ref[...] - cr_ref[j]
            if masked:
                s = jnp.where(_mask(bq, chunk, False), s, NEG)
            m_prev = m_s[...]
            m_new = jnp.maximum(m_prev, jnp.max(s, axis=-1, keepdims=True))
            alpha = jnp.exp(m_prev - m_new)
            p = jnp.exp(s - m_new)
            l_s[...] = alpha * l_s[...] + jnp.sum(p, axis=-1, keepdims=True)
            acc_s[...] = alpha * acc_s[...] + lax.dot_general(p.astype(BF16), vj, _NN, preferred_element_type=F32)
            m_s[...] = m_new

        def loop_body(j, carry):
            step(j, False)
            return carry

        lax.fori_loop(0, i, loop_body, 0)
        step(i, True)
        o_ref[...] = (acc_s[...] / l_s[...]).astype(o_ref.dtype)
        lse_ref[...] = m_s[...] + jnp.log(l_s[...])

    in_specs = [
        pl.BlockSpec((bq, dqk), lambda h, i: (i, qoff + h)),
        pl.BlockSpec((s_len, dqk), lambda h, i: (0, koff + h)),
        pl.BlockSpec((s_len, HEAD_V), lambda h, i: (0, voff + h)),
    ]
    args = [q, k, v]
    if has_bias:
        in_specs += [pl.BlockSpec((None, bq, 1), lambda h, i: (h, i, 0)),
                     pl.BlockSpec((None, nq, 1, bq), lambda h, i: (h, 0, 0, 0))]
        args += [cum, _row_layout(cum, bq)]
    return pl.pallas_call(
        body, name=name, grid=(HEADS, nq), in_specs=in_specs,
        out_specs=[pl.BlockSpec((bq, HEAD_V), lambda h, i: (i, h)), pl.BlockSpec((None, bq, 1), lambda h, i: (h, i, 0))],
        out_shape=[jax.ShapeDtypeStruct((s_len, HEADS * HEAD_V), F32), jax.ShapeDtypeStruct((HEADS, s_len, 1), F32)],
        scratch_shapes=[pltpu.VMEM((bq, 1), F32), pltpu.VMEM((bq, 1), F32), pltpu.VMEM((bq, HEAD_V), F32)],
        compiler_params=_params(("parallel", "arbitrary")),
    )(*args)


def attn_dq(name, q, k, v, o, do, lse, offs, dqk, chunk, scale, out_dtype, cum=None, bq=512):
    s_len = q.shape[0]
    nq = s_len // bq
    qoff, koff, voff = offs
    has_bias = cum is not None

    def body(*refs):
        if has_bias:
            q_ref, k_ref, v_ref, o_ref, do_ref, lse_ref, cc_ref, cr_ref, dq_ref, delta_ref, acc_s, pk_s, dl_s = refs
        else:
            q_ref, k_ref, v_ref, o_ref, do_ref, lse_ref, dq_ref, delta_ref, acc_s = refs
        i = pl.program_id(1)
        qv = q_ref[...]
        dov = do_ref[...].astype(F32)
        do_b = dov.astype(BF16)
        lse_v = lse_ref[...]
        acc_s[...] = jnp.zeros_like(acc_s)
        if has_bias:
            pk_s[...] = jnp.zeros_like(pk_s)
            dl_s[...] = jnp.zeros_like(dl_s)
        else:
            delta = jnp.sum(dov * o_ref[...].astype(F32), axis=-1, keepdims=True)
            delta_ref[...] = delta

        def step(j, masked):
            off = pl.multiple_of(j * bq, bq)
            kj = k_ref[pl.ds(off, bq), :]
            vj = v_ref[pl.ds(off, bq), :]
            s = lax.dot_general(qv, kj, _NT, preferred_element_type=F32) * scale
            if has_bias:
                s = s + cc_ref[...] - cr_ref[j]
            if masked:
                s = jnp.where(_mask(bq, chunk, False), s, NEG)
            p = jnp.exp(s - lse_v)
            dp = lax.dot_general(do_b, vj, _NT, preferred_element_type=F32)
            if has_bias:
                pd = p * dp
                dl_s[...] += jnp.sum(pd, axis=-1, keepdims=True)
                acc_s[...] += lax.dot_general(pd.astype(BF16), kj, _NN, preferred_element_type=F32)
                pk_s[...] += lax.dot_general(p.astype(BF16), kj, _NN, preferred_element_type=F32)
            else:
                ds = p * (dp - delta)
                acc_s[...] += lax.dot_general(ds.astype(BF16), kj, _NN, preferred_element_type=F32)

        def loop_body(j, carry):
            step(j, False)
            return carry

        lax.fori_loop(0, i, loop_body, 0)
        step(i, True)
        if has_bias:
            delta_ref[...] = dl_s[...]
            dq_ref[...] = ((acc_s[...] - dl_s[...] * pk_s[...]) * scale).astype(dq_ref.dtype)
        else:
            dq_ref[...] = (acc_s[...] * scale).astype(dq_ref.dtype)

    in_specs = [
        pl.BlockSpec((bq, dqk), lambda h, i: (i, qoff + h)),
        pl.BlockSpec((s_len, dqk), lambda h, i: (0, koff + h)),
        pl.BlockSpec((s_len, HEAD_V), lambda h, i: (0, voff + h)),
        pl.BlockSpec((bq, HEAD_V), lambda h, i: (i, h)),
        pl.BlockSpec((bq, HEAD_V), lambda h, i: (i, h)),
        pl.BlockSpec((None, bq, 1), lambda h, i: (h, i, 0)),
    ]
    args = [q, k, v, o, do, lse]
    if has_bias:
        in_specs += [pl.BlockSpec((None, bq, 1), lambda h, i: (h, i, 0)),
                     pl.BlockSpec((None, nq, 1, bq), lambda h, i: (h, 0, 0, 0))]
        args += [cum, _row_layout(cum, bq)]
    return pl.pallas_call(
        body, name=name, grid=(HEADS, nq), in_specs=in_specs,
        out_specs=[pl.BlockSpec((bq, dqk), lambda h, i: (i, h)), pl.BlockSpec((None, bq, 1), lambda h, i: (h, i, 0))],
        out_shape=[jax.ShapeDtypeStruct((s_len, HEADS * dqk), out_dtype), jax.ShapeDtypeStruct((HEADS, s_len, 1), F32)],
        scratch_shapes=[pltpu.VMEM((bq, dqk), F32)] + (
            [pltpu.VMEM((bq, dqk), F32), pltpu.VMEM((bq, 1), F32)] if has_bias else []),
        compiler_params=_params(("parallel", "arbitrary")),
    )(*args)


def attn_dkv(name, q, k, v, do, lse, delta, offs, dqk, chunk, scale, dk_dtype, cum=None, bq=512):
    s_len = q.shape[0]
    nq = s_len // bq
    qoff, koff, voff = offs
    has_bias = cum is not None

    def body(*refs):
        if has_bias:
            (k_ref, v_ref, q_ref, do_ref, lse_ref, delta_ref, cc_ref, cr_ref,
             dk_ref, dv_ref, dc_ref, dk_s, dv_s, dc_s) = refs
        else:
            k_ref, v_ref, q_ref, do_ref, lse_ref, delta_ref, dk_ref, dv_ref, dk_s, dv_s = refs
        j = pl.program_id(1)
        kv = k_ref[...]
        vv = v_ref[...]
        dk_s[...] = jnp.zeros_like(dk_s)
        dv_s[...] = jnp.zeros_like(dv_s)
        if has_bias:
            dc_s[...] = jnp.zeros_like(dc_s)

        def step(i, masked):
            off = pl.multiple_of(i * bq, bq)
            qi = q_ref[pl.ds(off, bq), :]
            doi = do_ref[pl.ds(off, bq), :].astype(BF16)
            st = lax.dot_general(kv, qi, _NT, preferred_element_type=F32) * scale
            if has_bias:
                st = st + cr_ref[i] - cc_ref[...]
            if masked:
                st = jnp.where(_mask(bq, chunk, True), st, NEG)
            pt = jnp.exp(st - lse_ref[i])
            dv_s[...] += lax.dot_general(pt.astype(BF16), doi, _NN, preferred_element_type=F32)
            dpt = lax.dot_general(vv, doi, _NT, preferred_element_type=F32)
            dst = pt * (dpt - delta_ref[i])
            dk_s[...] += lax.dot_general(dst.astype(BF16), qi, _NN, preferred_element_type=F32)
            if has_bias:
                dc_s[...] -= jnp.sum(dst, axis=-1, keepdims=True)

        step(j, True)

        def loop_body(i, carry):
            step(i, False)
            return carry

        lax.fori_loop(j + 1, nq, loop_body, 0)
        dk_ref[...] = (dk_s[...] * scale).astype(dk_ref.dtype)
        dv_ref[...] = dv_s[...].astype(dv_ref.dtype)
        if has_bias:
            dc_ref[...] = dc_s[...]

    in_specs = [
        pl.BlockSpec((bq, dqk), lambda h, j: (j, koff + h)),
        pl.BlockSpec((bq, HEAD_V), lambda h, j: (j, voff + h)),
        pl.BlockSpec((s_len, dqk), lambda h, j: (0, qoff + h)),
        pl.BlockSpec((s_len, HEAD_V), lambda h, j: (0, h)),
        pl.BlockSpec((None, nq, 1, bq), lambda h, j: (h, 0, 0, 0)),
        pl.BlockSpec((None, nq, 1, bq), lambda h, j: (h, 0, 0, 0)),
    ]
    args = [k, v, q, do, _row_layout(lse, bq), _row_layout(delta, bq)]
    out_specs = [pl.BlockSpec((bq, dqk), lambda h, j: (j, h)), pl.BlockSpec((bq, HEAD_V), lambda h, j: (j, h))]
    out_shape = [jax.ShapeDtypeStruct((s_len, HEADS * dqk), dk_dtype), jax.ShapeDtypeStruct((s_len, HEADS * HEAD_V), BF16)]
    scratch = [pltpu.VMEM((bq, dqk), F32), pltpu.VMEM((bq, HEAD_V), F32)]
    if has_bias:
        in_specs += [pl.BlockSpec((None, bq, 1), lambda h, j: (h, j, 0)),
                     pl.BlockSpec((None, nq, 1, bq), lambda h, j: (h, 0, 0, 0))]
        args += [cum, _row_layout(cum, bq)]
        out_specs.append(pl.BlockSpec((None, bq, 1), lambda h, j: (h, j, 0)))
        out_shape.append(jax.ShapeDtypeStruct((HEADS, s_len, 1), F32))
        scratch.append(pltpu.VMEM((bq, 1), F32))
    return pl.pallas_call(
        body, name=name, grid=(HEADS, nq), in_specs=in_specs, out_specs=out_specs, out_shape=out_shape,
        scratch_shapes=scratch, compiler_params=_params(("parallel", "arbitrary")),
    )(*args)


_CUM_BLK = 512


def _split3(x):
    hi = x.astype(BF16)
    r1 = x - hi.astype(F32)
    mid = r1.astype(BF16)
    lo = (r1 - mid.astype(F32)).astype(BF16)
    return hi, mid, lo


def _tri_dot(x, tri):
    hi, mid, lo = _split3(x)
    out = lax.dot_general(lo, tri, _NN, preferred_element_type=F32)
    out = out + lax.dot_general(mid, tri, _NN, preferred_element_type=F32)
    return out + lax.dot_general(hi, tri, _NN, preferred_element_type=F32)


def fox_cum_fwd(name, ff_t, bias):
    s_len = ff_t.shape[1]
    nb = s_len // _CUM_BLK

    def body(ff_ref, b_ref, cum_ref):
        row = lax.broadcasted_iota(jnp.int32, (_CUM_BLK, _CUM_BLK), 0)
        col = lax.broadcasted_iota(jnp.int32, (_CUM_BLK, _CUM_BLK), 1)
        tri = (row <= col).astype(BF16)
        carry = jnp.zeros((HEADS, 1), F32)
        for b in range(nb):
            z = ff_ref[:, b * _CUM_BLK:(b + 1) * _CUM_BLK] + b_ref[...]
            logf = jnp.minimum(z, 0.0) - jnp.log1p(jnp.exp(-jnp.abs(z)))
            blk = _tri_dot(logf, tri) + carry
            cum_ref[:, b * _CUM_BLK:(b + 1) * _CUM_BLK] = blk
            carry = blk[:, _CUM_BLK - 1:_CUM_BLK]

    return pl.pallas_call(
        body, name=name, out_shape=jax.ShapeDtypeStruct((HEADS, s_len), F32),
        compiler_params=pltpu.CompilerParams(vmem_limit_bytes=VMEM_LIMIT),
    )(ff_t, bias)


def fox_cum_bwd(name, ff_t, bias, dcum):
    s_len = ff_t.shape[1]
    nb = s_len // _CUM_BLK

    def body(ff_ref, b_ref, dc_ref, dff_ref, db_ref):
        row = lax.broadcasted_iota(jnp.int32, (_CUM_BLK, _CUM_BLK), 0)
        col = lax.broadcasted_iota(jnp.int32, (_CUM_BLK, _CUM_BLK), 1)
        tri = (row >= col).astype(BF16)
        carry = jnp.zeros((HEADS, 1), F32)
        dbias = jnp.zeros((HEADS, 1), F32)
        for b in reversed(range(nb)):
            sl = slice(b * _CUM_BLK, (b + 1) * _CUM_BLK)
            dlogf = _tri_dot(dc_ref[:, sl], tri) + carry
            carry = dlogf[:, 0:1]
            z = ff_ref[:, sl] + b_ref[...]
            dz = dlogf / (1.0 + jnp.exp(z))
            dff_ref[:, sl] = dz
            dbias = dbias + jnp.sum(dz, axis=-1, keepdims=True)
        db_ref[...] = dbias

    return pl.pallas_call(
        body, name=name,
        out_shape=[jax.ShapeDtypeStruct((HEADS, s_len), F32), jax.ShapeDtypeStruct((HEADS, 1), F32)],
        compiler_params=pltpu.CompilerParams(vmem_limit_bytes=VMEM_LIMIT),
    )(ff_t, bias, dcum)


_ANY = pl.BlockSpec(memory_space=pl.ANY)


def _me():
    return lax.axis_index("x"), lax.axis_index("y"), lax.axis_index("c")


def comm_allgather(name, mine):
    n_rows, n_cols = mine.shape

    def body(x_ref, out_ref, send_sems, recv_sems, local_sem):
        x, y, c = _me()
        sibling = (x, y, 1 - c)
        chips = [(1 - x, y), (x, 1 - y), (1 - x, 1 - y)]

        def blk(px, py, pc):
            return out_ref.at[pc * 4 + px * 2 + py]

        def copy(k, block, to, src=None):
            return pltpu.make_async_remote_copy(
                src_ref=blk(*block) if src is None else src, dst_ref=blk(*block),
                send_sem=send_sems.at[k], recv_sem=recv_sems.at[k], device_id=to, device_id_type=MESH)

        own = pltpu.make_async_copy(x_ref, blk(x, y, c), local_sem)
        own.start()
        first = [copy(0, (x, y, c), sibling, src=x_ref)]
        first += [copy(1 + j, (x, y, c), (*chip, c), src=x_ref) for j, chip in enumerate(chips)]
        for cp in first:
            cp.start()
        passed = [copy(4 + j, (*chip, c), sibling) for j, chip in enumerate(chips)]
        for j, chip in enumerate(chips):
            copy(1 + j, (*chip, c), (x, y, c)).wait_recv()
            passed[j].start()
        copy(0, sibling, (x, y, c)).wait_recv()
        for j, chip in enumerate(chips):
            copy(4 + j, (*chip, 1 - c), (x, y, c)).wait_recv()
        for cp in first + passed:
            cp.wait_send()
        own.wait()

    return pl.pallas_call(
        body, name=name, out_shape=jax.ShapeDtypeStruct((N_DEV, n_rows, n_cols), mine.dtype),
        in_specs=[_ANY], out_specs=_ANY,
        scratch_shapes=[pltpu.SemaphoreType.DMA((7,)), pltpu.SemaphoreType.DMA((7,)), pltpu.SemaphoreType.DMA],
    )(mine)


def comm_swap_sibling(name, parts):
    _, n_rows, n_cols = parts.shape

    def body(p_ref, got_ref, send_sem, recv_sem):
        x, y, c = _me()
        cp = pltpu.make_async_remote_copy(
            src_ref=p_ref.at[pl.ds((1 - c) * 4, 4)], dst_ref=got_ref, send_sem=send_sem, recv_sem=recv_sem,
            device_id=(x, y, 1 - c), device_id_type=MESH)
        cp.start()
        cp.wait()

    return pl.pallas_call(
        body, name=name, out_shape=jax.ShapeDtypeStruct((4, n_rows, n_cols), parts.dtype),
        in_specs=[_ANY], out_specs=_ANY,
        scratch_shapes=[pltpu.SemaphoreType.DMA, pltpu.SemaphoreType.DMA],
    )(parts)


def comm_swap_chips(name, parts):
    _, n_rows, n_cols = parts.shape

    def body(p_ref, got_ref, send_sems, recv_sems):
        x, y, c = _me()
        chips = [(1 - x, y), (x, 1 - y), (1 - x, 1 - y)]
        cps = [pltpu.make_async_remote_copy(
            src_ref=p_ref.at[2 * px + py], dst_ref=got_ref.at[k], send_sem=send_sems.at[k], recv_sem=recv_sems.at[k],
            device_id=(px, py, c), device_id_type=MESH) for k, (px, py) in enumerate(chips)]
        for cp in cps:
            cp.start()
        for cp in cps:
            cp.wait()

    return pl.pallas_call(
        body, name=name, out_shape=jax.ShapeDtypeStruct((3, n_rows, n_cols), parts.dtype),
        in_specs=[_ANY], out_specs=_ANY,
        scratch_shapes=[pltpu.SemaphoreType.DMA((3,)), pltpu.SemaphoreType.DMA((3,))],
    )(parts)


def comm_allreduce_small(name, mine):
    shape = mine.shape

    def body(x_ref, out_ref, buf, send_sems, recv_sems):
        x, y, c = _me()
        my_slot = c * 4 + x * 2 + y
        buf[my_slot] = x_ref[...]
        cps = []
        for k in range(1, N_DEV):
            dx, dy, dc = (k >> 2) & 1, (k >> 1) & 1, k & 1
            px, py, pc = x ^ dx, y ^ dy, c ^ dc
            send = pltpu.make_async_remote_copy(
                src_ref=x_ref, dst_ref=buf.at[my_slot], send_sem=send_sems.at[k - 1], recv_sem=recv_sems.at[k - 1],
                device_id=(px, py, pc), device_id_type=MESH)
            send.start()
            recv = pltpu.make_async_remote_copy(
                src_ref=x_ref, dst_ref=buf.at[pc * 4 + px * 2 + py], send_sem=send_sems.at[k - 1],
                recv_sem=recv_sems.at[k - 1], device_id=(px, py, pc), device_id_type=MESH)
            cps.append((send, recv))
        for send, recv in cps:
            send.wait_send()
            recv.wait_recv()
        total = buf[0]
        for s in range(1, N_DEV):
            total = total + buf[s]
        out_ref[...] = total

    vmem = pl.BlockSpec(memory_space=pltpu.VMEM)
    return pl.pallas_call(
        body, name=name, out_shape=jax.ShapeDtypeStruct(shape, F32), in_specs=[vmem], out_specs=vmem,
        scratch_shapes=[pltpu.VMEM((N_DEV,) + shape, F32), pltpu.SemaphoreType.DMA((7,)), pltpu.SemaphoreType.DMA((7,))],
    )(mine)


def _shard_shapes(shards):
    return [tuple(shards[n].shape[1:]) for n in BIG]


def pack_local(shards):
    flat = [shards[n].reshape(-1, PACK_C) for n in BIG]
    rows = sum(f.shape[0] for f in flat)
    pad = (-rows) % 16
    return jnp.concatenate(flat + [jnp.zeros((pad, PACK_C), flat[0].dtype)], axis=0)


def unpack_full(gathered, shard_shapes):
    _, n_rows, _ = gathered.shape
    by_block = gathered.reshape(2, 4, n_rows, PACK_C).transpose(1, 0, 2, 3).reshape(N_DEV, n_rows, PACK_C)
    full = {}
    r0 = 0
    for name, (rs, cs) in zip(BIG, shard_shapes):
        nr = rs * cs // PACK_C
        piece = by_block[:, r0:r0 + nr, :].reshape(N_DEV, rs, cs)
        r0 += nr
        if name in ROW_SHARDED:
            full[name] = piece.reshape(N_DEV * rs, cs)
        else:
            full[name] = piece.transpose(1, 0, 2).reshape(rs, N_DEV * cs)
    w_in = full.pop("w_in")
    zeros = jnp.zeros((D_MODEL, LAT_W - OFF_FQ - HEADS), w_in.dtype)
    full["w_lat"] = jnp.concatenate([w_in[:, :OFF_FQ], w_in[:, OFF_FF:OFF_G], zeros], axis=1)
    full["w_fox"] = w_in[:, OFF_FQ:OFF_FF]
    full["w_gate"] = w_in[:, OFF_G:]
    w_uq = full.pop("w_uq").reshape(Q_LORA, HEADS, NOPE + ROPE)
    full["w_uq"] = jnp.pad(w_uq, ((0, 0), (0, 0), (0, QK_PAD - NOPE - ROPE))).reshape(Q_LORA, HEADS * QK_PAD)
    w_ukv = full.pop("w_ukv").reshape(KV_LORA, HEADS, 2, NOPE)
    full["w_kv"] = jnp.concatenate([w_ukv[:, :, 0, :].reshape(KV_LORA, HEADS * NOPE),
                                    w_ukv[:, :, 1, :].reshape(KV_LORA, HEADS * HEAD_V)], axis=1)
    return full


def pack_small(vals, loss):
    flat = [vals[n].reshape(-1) for n in SMALL] + [loss.reshape(-1)]
    used = sum(f.shape[0] for f in flat)
    flat.append(jnp.zeros((SMALL_ROWS * PACK_C - used,), F32))
    return jnp.concatenate(flat).reshape(SMALL_ROWS, PACK_C)


def unpack_small(packed):
    flat = packed.reshape(-1)
    out, off = {}, 0
    for n in SMALL:
        out[n] = flat[off:off + SMALL_N[n]]
        off += SMALL_N[n]
    return out, flat[off]


def rope_tables(s_len):
    pos = jnp.arange(s_len, dtype=F32)
    inv = 1.0 / (ROPE_THETA ** (jnp.arange(0, ROPE, 2, dtype=F32) / ROPE))
    ang = pos[:, None] * inv[None, :]
    cos, sin = jnp.cos(ang), jnp.sin(ang)
    zero = jnp.zeros_like(cos)
    c = jnp.concatenate([cos, cos, zero, zero], axis=1)
    s1 = jnp.concatenate([-sin, zero, zero, zero], axis=1)
    s2 = jnp.concatenate([zero, sin, zero, zero], axis=1)
    return (c, s1, s2), (c, -s1, -s2)


def local_step(x, target, w, small):
    s_len = x.shape[0]
    tabs, inv_tabs = rope_tables(s_len)
    g_attn = small["attn_norm"].reshape(1, D_MODEL)
    g_q = small["q_norm"].reshape(1, Q_LORA)
    g_kv = small["kv_norm"].reshape(1, KV_LORA)
    g_mlp = small["mlp_norm"].reshape(1, D_MODEL)
    g_final = small["final_norm"].reshape(1, D_MODEL)
    f_bias = small["fox_f_bias"].reshape(HEADS, 1)
    mla_scale = 1.0 / math.sqrt(NOPE + ROPE)
    fox_scale = 1.0 / math.sqrt(HEAD_V)
    mla_offs = (0, 0, HEADS)
    fox_offs = (0, HEADS, 2 * HEADS)

    xn = rms_fwd("rms_attn", x, g_attn)
    lat, = matmul("proj_lat", xn, w["w_lat"], "nn", [F32])
    fox, = matmul("proj_fox", xn, w["w_fox"], "nn", [BF16])
    graw, = matmul("proj_gate", xn, w["w_gate"], "nn", [F32])
    cq = rms_fwd("rms_q", (lat, Q_LORA, 0), g_q)
    ckv = rms_fwd("rms_kv", (lat, KV_LORA, Q_LORA // KV_LORA), g_kv)
    qraw, = matmul("up_q", cq, w["w_uq"], "nn", [F32])
    q = rope_heads("rope_q", qraw, tabs, BF16)
    kvn, = matmul("up_kv", ckv, w["w_kv"], "nn", [BF16])
    kr = rope_block("rope_k", lat, OFF_KR // 128, tabs, BF16)
    k = k_assemble("k_assemble", kvn, kr)
    o_mla, lse_mla = attn_fwd("mla_fwd", q, k, kvn, mla_offs, QK_PAD, CHUNK, mla_scale)
    ff_t = lat[:, OFF_FQ:OFF_FQ + HEADS].T
    cum = fox_cum_fwd("fox_cum", ff_t, f_bias).reshape(HEADS, s_len, 1)
    o_fox, lse_fox = attn_fwd("fox_fwd", fox, fox, fox, fox_offs, HEAD_V, 1, fox_scale, cum=cum)
    y_mla, = matmul("branch_mla", o_mla, w["w_mla_branch"], "nn", [F32])
    y_fox, = matmul("branch_fox", o_fox, w["w_fox_branch"], "nn", [F32])
    mix = gate_mix("gate_mix", graw, y_mla, y_fox)
    h1, = matmul("out_proj", mix, w["w_out"], "nn", [F32], epilogue=lambda acc, res: (res + acc,), extras=[x])
    hn = rms_fwd("rms_mlp", h1, g_mlp)

    def relu2(acc):
        r = jnp.maximum(acc, 0.0)
        return r * r, r
    u, relu_up = matmul("mlp_up", hn, w["w_up"], "nn", [BF16, BF16], epilogue=relu2)
    h2, = matmul("mlp_down", u, w["w_down"], "nn", [F32], epilogue=lambda acc, res: (res + acc,), extras=[h1])
    dh2, d_final, loss = loss_head("loss_head", h2, target, g_final)

    grads = {}
    dup, = matmul("d_mlp_down", dh2, w["w_down"], "nt", [BF16],
                  epilogue=lambda acc, r: (acc * (2.0 * r.astype(F32)),), extras=[relu_up])
    grads["w_down"], = matmul("gw_down", u, dh2, "tn", [BF16], tm=2048, tn=1024, tk=512)
    dhn, = matmul("d_mlp_up", dup, w["w_up"], "nt", [F32])
    grads["w_up"], = matmul("gw_up", hn, dup, "tn", [BF16], tm=2048, tn=1024, tk=512)
    dh1, d_mlp = rms_bwd("rms_mlp_bwd", h1, dhn, g_mlp, dres=dh2)
    dmix, = matmul("d_out_proj", dh1, w["w_out"], "nt", [F32])
    grads["w_out"], = matmul("gw_out", mix, dh1, "tn", [BF16], tm=2048, tn=1024, tk=512)
    dgraw, dy_mla, dy_fox = gate_mix_bwd("gate_mix_bwd", graw, y_mla, y_fox, dmix)
    do_mla, = matmul("d_branch_mla", dy_mla, w["w_mla_branch"], "nt", [F32])
    grads["w_mla_branch"], = matmul("gw_branch_mla", o_mla, dy_mla, "tn", [BF16], tk=512)
    do_fox, = matmul("d_branch_fox", dy_fox, w["w_fox_branch"], "nt", [F32])
    grads["w_fox_branch"], = matmul("gw_branch_fox", o_fox, dy_fox, "tn", [BF16], tk=512)

    dq, delta_mla = attn_dq("mla_dq", q, k, kvn, o_mla, do_mla, lse_mla, mla_offs, QK_PAD, CHUNK, mla_scale, F32)
    dk, dv = attn_dkv("mla_dkv", q, k, kvn, do_mla, lse_mla, delta_mla, mla_offs, QK_PAD, CHUNK, mla_scale, F32)
    dq_r = rope_heads("rope_q_bwd", dq, inv_tabs, BF16)
    dkvn, dkr = dk_split("dk_split", dk, dv, inv_tabs)
    dcq, = matmul("d_up_q", dq_r, w["w_uq"], "nt", [F32])
    grads["w_uq"], = matmul("gw_uq", cq, dq_r, "tn", [BF16], tk=512)
    dckv, = matmul("d_up_kv", dkvn, w["w_kv"], "nt", [F32])
    grads["w_kv"], = matmul("gw_kv", ckv, dkvn, "tn", [BF16], tk=512)
    dcq_raw, d_qn = rms_bwd("rms_q_bwd", (lat, Q_LORA, 0), dcq, g_q, out_dtype=BF16)
    dckv_raw, d_kvn = rms_bwd("rms_kv_bwd", (lat, KV_LORA, Q_LORA // KV_LORA), dckv, g_kv, out_dtype=BF16)

    dfq, delta_fox = attn_dq("fox_dq", fox, fox, fox, o_fox, do_fox, lse_fox, fox_offs, HEAD_V, 1, fox_scale, BF16, cum=cum)
    dfk, dfv, dcum = attn_dkv("fox_dkv", fox, fox, fox, do_fox, lse_fox, delta_fox, fox_offs, HEAD_V, 1, fox_scale, BF16,
                              cum=cum)
    dff_t, d_bias = fox_cum_bwd("fox_cum_bwd", ff_t, f_bias, dcum.reshape(HEADS, s_len))

    pad = jnp.zeros((s_len, LAT_W - OFF_FQ - HEADS), BF16)
    dproj = jnp.concatenate([dcq_raw, dckv_raw, dkr[:, :ROPE].astype(BF16), dff_t.T.astype(BF16), pad,
                             dfq, dfk, dfv, dgraw], axis=1)
    w_in_p = jnp.concatenate([w["w_lat"], w["w_fox"], w["w_gate"]], axis=1)
    dxn, = matmul("d_proj", dproj, w_in_p, "nt", [F32], tk=1152)
    gw_in, = matmul("gw_in", xn, dproj, "tn", [BF16], tm=2048, tn=1152, tk=512)
    grads["w_lat"], grads["w_fox"], grads["w_gate"] = (gw_in[:, :LAT_W], gw_in[:, LAT_W:LAT_W + FOX_W],
                                                      gw_in[:, LAT_W + FOX_W:])
    dx, d_attn = rms_bwd("rms_attn_bwd", x, dxn, g_attn, dres=dh1)

    small_grads = {"attn_norm": d_attn, "fox_f_bias": d_bias, "q_norm": d_qn, "kv_norm": d_kvn,
                   "mlp_norm": d_mlp, "final_norm": d_final}
    return loss, dx, grads, small_grads


def kernel(x, attn_norm, w_in, fox_f_bias, q_norm, w_uq, kv_norm, w_ukv, w_mla_branch, w_fox_branch, w_out, mlp_norm, w_up, w_down, final_norm, loss_target, m_attn_norm, m_w_in, m_fox_f_bias, m_q_norm, m_w_uq, m_kv_norm, m_w_ukv, m_w_mla_branch, m_w_fox_branch, m_w_out, m_mlp_norm, m_w_up, m_w_down, m_final_norm, v_attn_norm, v_w_in, v_fox_f_bias, v_q_norm, v_w_uq, v_kv_norm, v_w_ukv, v_w_mla_branch, v_w_fox_branch, v_w_out, v_mlp_norm, v_w_up, v_w_down, v_final_norm):
    given = dict(locals())
    big = {n: given[n][0] for n in BIG}
    small = {n: given[n] for n in SMALL}
    shard_shapes = _shard_shapes({n: given[n] for n in BIG})
    cx, cy, cc = _me()

    packed = pack_local({n: big[n].astype(BF16) for n in BIG})
    n_rows = packed.shape[0]
    gathered = comm_allgather("comm_allgather", packed)
    unpack = functools.partial(unpack_full, shard_shapes=shard_shapes)
    w_full = unpack(gathered)

    loss_part, dx, grads, small_grads = local_step(x[0], loss_target[0], w_full, small)

    to_packed = jax.linear_transpose(unpack, jax.ShapeDtypeStruct(gathered.shape, BF16))
    parts, = to_packed({n: grads[n] for n in w_full})
    from_sibling = comm_swap_sibling("comm_rs_sibling", parts)
    mine4 = lax.dynamic_slice_in_dim(parts, cc * 4, 4, axis=0)
    pair = add_pairs("rs_pair_sum", mine4.reshape(4 * n_rows, PACK_C), from_sibling.reshape(4 * n_rows, PACK_C))
    from_chips = comm_swap_chips("comm_rs_chips", pair.reshape(4, n_rows, PACK_C))
    own = cx * 2 + cy
    g_packed = add_final("rs_final_sum", lax.dynamic_index_in_dim(mine4, own, 0, keepdims=False),
                         lax.dynamic_index_in_dim(from_sibling, own, 0, keepdims=False),
                         from_chips[0], from_chips[1], from_chips[2])

    small_sum = comm_allreduce_small("comm_allreduce_small", pack_small(small_grads, loss_part[0, 0]))
    g_small, loss = unpack_small(small_sum)

    grad_w, delta_w, new_m, new_v = {}, {}, {}, {}
    r0 = 0
    for n, (rs, cs) in zip(BIG, shard_shapes):
        nr = rs * cs // PACK_C
        g = g_packed[r0:r0 + nr].reshape(rs, cs)
        r0 += nr
        d, m_new, v_new = adamw("adamw_" + n, big[n], g, given["m_" + n][0], given["v_" + n][0])
        grad_w[n], delta_w[n], new_m[n], new_v[n] = g[None], d[None], m_new[None], v_new[None]
    zero = jnp.zeros((), F32)
    d_s, m_s, v_s = adamw("adamw_small", pack_small(small, zero), small_sum * _small_mask(),
                          pack_small({n: given["m_" + n] for n in SMALL}, zero),
                          pack_small({n: given["v_" + n] for n in SMALL}, zero), tr=SMALL_ROWS)
    d_small, _ = unpack_small(d_s)
    m_small, _ = unpack_small(m_s)
    v_small, _ = unpack_small(v_s)
    for n in SMALL:
        shape = given[n].shape
        grad_w[n], delta_w[n] = g_small[n].reshape(shape), d_small[n].reshape(shape)
        new_m[n], new_v[n] = m_small[n].reshape(shape), v_small[n].reshape(shape)

    order = ["attn_norm", "w_in", "fox_f_bias", "q_norm", "w_uq", "kv_norm", "w_ukv", "w_mla_branch", "w_fox_branch",
             "w_out", "mlp_norm", "w_up", "w_down", "final_norm"]
    return (loss, dx[None], *[grad_w[n] for n in order], *[delta_w[n] for n in order],
            *[new_m[n] for n in order], *[new_v[n] for n in order])


def _small_mask():
    used = sum(SMALL_N[n] for n in SMALL)
    return (jnp.arange(SMALL_ROWS * PACK_C) < used).astype(F32).reshape(SMALL_ROWS, PACK_C)
```

```python
import functools
import math

import jax
import jax.numpy as jnp
from jax import lax
from jax.experimental import pallas as pl
from jax.experimental.pallas import tpu as pltpu

F32 = jnp.float32
BF16 = jnp.bfloat16
MESH = pl.DeviceIdType.MESH

D_MODEL = 2048
HEADS = 8
Q_LORA = 512
KV_LORA = 256
NOPE = 128
ROPE = 64
HEAD_V = 128
D_FF = 4 * D_MODEL
CHUNK = 64
EPS = 1e-6
ROPE_THETA = 10000.0
OFF_KR = Q_LORA + KV_LORA
OFF_FQ = OFF_KR + ROPE
OFF_FF = OFF_FQ + 3 * HEADS * HEAD_V
OFF_G = OFF_FF + HEADS
D_IN = OFF_G + 2 * D_MODEL

LAT_W = 896
FOX_W = 3 * HEADS * HEAD_V
GATE_W = 2 * D_MODEL
PROJ_W = LAT_W + FOX_W + GATE_W
QK_PAD = 256

ADAM_LR = 0.001
ADAM_B1 = 0.9
ADAM_B2 = 0.999
ADAM_EPS = 1e-08
ADAM_WD = 0.01
ADAM_STEP = 10

N_DEV = 8
PACK_C = 1024
NEG = -1e30

VMEM_LIMIT = 56 * 1024 * 1024

BIG = ("w_in", "w_uq", "w_ukv", "w_mla_branch", "w_fox_branch", "w_out", "w_up", "w_down")
GROUP_A = ("w_in", "w_uq", "w_ukv")
GROUP_B = ("w_mla_branch", "w_fox_branch", "w_out", "w_up", "w_down")
ROW_SHARDED = ("w_out", "w_down")
SMALL = ("attn_norm", "fox_f_bias", "q_norm", "kv_norm", "mlp_norm", "final_norm")
SMALL_N = {"attn_norm": D_MODEL, "fox_f_bias": HEADS, "q_norm": Q_LORA, "kv_norm": KV_LORA,
           "mlp_norm": D_MODEL, "final_norm": D_MODEL}
SMALL_ROWS = 8


def _params(sem):
    return pltpu.CompilerParams(dimension_semantics=sem, vmem_limit_bytes=VMEM_LIMIT)


def _rows(name, fn, row_ins, const_ins, outs, reds=(), tr=256):
    norm = [(a, a.shape[1], 0) if not isinstance(a, tuple) else a for a in row_ins]
    n_rows = norm[0][0].shape[0]
    tr = min(tr, n_rows)
    assert n_rows % tr == 0, (name, n_rows, tr)
    n_in, n_out, n_red = len(norm) + len(const_ins), len(outs), len(reds)

    def body(*refs):
        vals = [r[...] for r in refs[:n_in]]
        out_refs = refs[n_in:n_in + n_out]
        red_refs = refs[n_in + n_out:]
        out_vals, red_vals = fn(*vals)
        for r, v in zip(out_refs, out_vals):
            r[...] = v.astype(r.dtype)
        if n_red:
            @pl.when(pl.program_id(0) == 0)
            def _():
                for r in red_refs:
                    r[...] = jnp.zeros_like(r)
            for r, v in zip(red_refs, red_vals):
                r[...] += v

    in_specs = [pl.BlockSpec((tr, w), functools.partial(lambda i, cb: (i, cb), cb=cb)) for _, w, cb in norm]
    in_specs += [pl.BlockSpec(a.shape, lambda i: (0, 0)) for a in const_ins]
    out_specs = [pl.BlockSpec((tr, c), lambda i: (i, 0)) for c, _ in outs]
    out_specs += [pl.BlockSpec((1, c), lambda i: (0, 0)) for c in reds]
    out_shape = [jax.ShapeDtypeStruct((n_rows, c), dt) for c, dt in outs]
    out_shape += [jax.ShapeDtypeStruct((1, c), F32) for c in reds]
    res = pl.pallas_call(
        body, name=name, grid=(n_rows // tr,), in_specs=in_specs, out_specs=out_specs, out_shape=out_shape,
        compiler_params=_params(("arbitrary",)),
    )(*[a for a, _, _ in norm], *const_ins)
    return res


def _rstd(x):
    return lax.rsqrt(jnp.mean(x * x, axis=-1, keepdims=True) + EPS)


def rms_fwd(name, x, gain, tr=256):
    width = x[1] if isinstance(x, tuple) else x.shape[1]

    def fn(xv, g):
        return (xv * _rstd(xv) * g,), ()
    return _rows(name, fn, [x], [gain], [(width, BF16)], tr=tr)[0]


def rms_bwd(name, x, dy, gain, dres=None, out_dtype=F32, tr=256):
    width = x[1] if isinstance(x, tuple) else x.shape[1]

    def fn(xv, dyv, *rest):
        g = rest[-1]
        r = _rstd(xv)
        n = xv * r
        dyv = dyv.astype(F32)
        dn = dyv * g
        dx = r * (dn - n * jnp.mean(dn * n, axis=-1, keepdims=True))
        if dres is not None:
            dx = dx + rest[0]
        return (dx,), (jnp.sum(dyv * n, axis=0, keepdims=True),)

    ins = [x, dy] + ([dres] if dres is not None else [])
    return _rows(name, fn, ins, [gain], [(width, out_dtype)], [width], tr=tr)


def _rope_lanes(t, c, s1, s2):
    return t * c + pltpu.roll(t, 96, 1) * s1 + pltpu.roll(t, 32, 1) * s2


def rope_heads(name, x, tabs, out_dtype):
    def fn(xv, c, s1, s2):
        xv = xv.astype(F32)
        parts = []
        for h in range(HEADS):
            parts.append(xv[:, h * QK_PAD:h * QK_PAD + NOPE])
            parts.append(_rope_lanes(xv[:, h * QK_PAD + NOPE:(h + 1) * QK_PAD], c, s1, s2))
        return (jnp.concatenate(parts, axis=1),), ()
    return _rows(name, fn, [x, *tabs], [], [(HEADS * QK_PAD, out_dtype)])[0]


def rope_block(name, x, col_block, tabs, out_dtype):
    def fn(xv, c, s1, s2):
        return (_rope_lanes(xv.astype(F32), c, s1, s2),), ()
    return _rows(name, fn, [(x, 128, col_block), *tabs], [], [(128, out_dtype)])[0]


def k_assemble(name, kvn, kr):
    def fn(knp, krv):
        parts = []
        for h in range(HEADS):
            parts.append(knp[:, h * NOPE:(h + 1) * NOPE])
            parts.append(krv)
        return (jnp.concatenate(parts, axis=1),), ()
    return _rows(name, fn, [(kvn, HEADS * NOPE, 0), kr], [], [(HEADS * QK_PAD, BF16)])[0]


def dk_split(name, dk, dv, inv_tabs):
    def fn(dkv, dvv, c, s1, s2):
        parts = []
        acc = None
        for h in range(HEADS):
            parts.append(dkv[:, h * QK_PAD:h * QK_PAD + NOPE].astype(BF16))
            t = dkv[:, h * QK_PAD + NOPE:(h + 1) * QK_PAD]
            acc = t if acc is None else acc + t
        parts.append(dvv)
        return (jnp.concatenate(parts, axis=1), _rope_lanes(acc, c, s1, s2)), ()
    return _rows(name, fn, [dk, dv, *inv_tabs], [], [(2 * HEADS * NOPE, BF16), (128, F32)])


def gate_mix(name, graw, y_mla, y_fox):
    def fn(g, ya, yb):
        ga = jax.nn.sigmoid(g[:, :D_MODEL])
        gb = jax.nn.sigmoid(g[:, D_MODEL:])
        return (ga * ya + gb * yb,), ()
    return _rows(name, fn, [graw, y_mla, y_fox], [], [(D_MODEL, BF16)])[0]


def gate_mix_bwd(name, graw, y_mla, y_fox, dmix):
    def fn(g, ya, yb, dm):
        ga = jax.nn.sigmoid(g[:, :D_MODEL])
        gb = jax.nn.sigmoid(g[:, D_MODEL:])
        dgraw = jnp.concatenate([dm * ya * ga * (1.0 - ga), dm * yb * gb * (1.0 - gb)], axis=1)
        return (dgraw, dm * ga, dm * gb), ()
    return _rows(name, fn, [graw, y_mla, y_fox, dmix], [], [(GATE_W, BF16), (D_MODEL, BF16), (D_MODEL, BF16)], tr=128)


def loss_head(name, h2, target, gain):
    inv_d = 1.0 / D_MODEL

    def fn(h, t, g):
        r = _rstd(h)
        n = h * r
        err = n * g - t
        dy = err * inv_d
        dn = dy * g
        dh = r * (dn - n * jnp.mean(dn * n, axis=-1, keepdims=True))
        part = 0.5 * inv_d * jnp.sum(jnp.sum(err * err, axis=1, keepdims=True), axis=0, keepdims=True)
        return (dh,), (jnp.sum(dy * n, axis=0, keepdims=True), jnp.broadcast_to(part, (1, 128)))
    return _rows(name, fn, [h2, target], [gain], [(D_MODEL, F32)], [D_MODEL, 128])


def adamw(name, w, g, m, v, tr=256):
    c1 = 1.0 - ADAM_B1 ** ADAM_STEP
    c2 = 1.0 - ADAM_B2 ** ADAM_STEP

    def fn(wv, gv, mv, vv):
        m_new = ADAM_B1 * mv + (1.0 - ADAM_B1) * gv
        v_new = ADAM_B2 * vv + (1.0 - ADAM_B2) * (gv * gv)
        delta = -ADAM_LR * ((m_new / c1) / (jnp.sqrt(v_new / c2) + ADAM_EPS) + ADAM_WD * wv)
        return (delta, m_new, v_new), ()
    cols = w.shape[1]
    return _rows(name, fn, [w, g, m, v], [], [(cols, F32)] * 3, tr=tr)


def _row_tile(n_rows, cap=640):
    return max(t for t in range(16, cap + 1, 16) if n_rows % t == 0)


def add_pairs(name, a, b):
    def fn(av, bv):
        return (av.astype(F32) + bv.astype(F32),), ()
    return _rows(name, fn, [a, b], [], [(a.shape[1], BF16)], tr=_row_tile(a.shape[0]))[0]


def add_final(name, a, b, r0, r1, r2):
    def fn(av, bv, r0v, r1v, r2v):
        return (((av.astype(F32) + bv.astype(F32)) + r0v.astype(F32)) + r1v.astype(F32) + r2v.astype(F32),), ()
    return _rows(name, fn, [a, b, r0, r1, r2], [], [(a.shape[1], F32)], tr=_row_tile(a.shape[0]))[0]


_DIMS = {"nn": (((1,), (0,)), ((), ())), "nt": (((1,), (1,)), ((), ())), "tn": (((0,), (0,)), ((), ()))}


def matmul(name, a, b, mode, outs, epilogue=None, extras=(), tm=1024, tn=1024, tk=2048):
    if mode == "tn":
        kdim, m = a.shape
    else:
        m, kdim = a.shape
    n = b.shape[0] if mode == "nt" else b.shape[1]
    tm, tn, tk = min(tm, m), min(tn, n), min(tk, kdim)
    assert m % tm == 0 and n % tn == 0 and kdim % tk == 0, (name, a.shape, b.shape)
    nk = kdim // tk
    n_ex, n_out = len(extras), len(outs)
    dims = _DIMS[mode]

    def body(a_ref, b_ref, *rest):
        ex_refs = rest[:n_ex]
        out_refs = rest[n_ex:n_ex + n_out]

        def finish(acc):
            vals = (acc,) if epilogue is None else epilogue(acc, *[r[...] for r in ex_refs])
            for r, v in zip(out_refs, vals):
                r[...] = v.astype(r.dtype)

        part = lax.dot_general(a_ref[...].astype(BF16), b_ref[...].astype(BF16), dims, preferred_element_type=F32)
        if nk == 1:
            finish(part)
        else:
            acc_ref = rest[-1]
            k = pl.program_id(2)

            @pl.when(k == 0)
            def _():
                acc_ref[...] = part

            @pl.when(k > 0)
            def _():
                acc_ref[...] += part

            @pl.when(k == nk - 1)
            def _():
                finish(acc_ref[...])

    a_spec = pl.BlockSpec((tk, tm), lambda i, j, k: (k, i)) if mode == "tn" else pl.BlockSpec((tm, tk), lambda i, j, k: (i, k))
    b_spec = pl.BlockSpec((tn, tk), lambda i, j, k: (j, k)) if mode == "nt" else pl.BlockSpec((tk, tn), lambda i, j, k: (k, j))
    tile = pl.BlockSpec((tm, tn), lambda i, j, k: (i, j))
    res = pl.pallas_call(
        body, name=name, grid=(m // tm, n // tn, nk),
        in_specs=[a_spec, b_spec] + [tile] * n_ex,
        out_specs=[tile] * n_out,
        out_shape=[jax.ShapeDtypeStruct((m, n), dt) for dt in outs],
        scratch_shapes=[pltpu.VMEM((tm, tn), F32)] if nk > 1 else [],
        compiler_params=_params(("parallel", "parallel", "arbitrary")),
    )(a, b, *extras)
    return res


_NT = (((1,), (1,)), ((), ()))
_NN = (((1,), (0,)), ((), ()))


def _mask(bq, chunk, transposed, row0=0, shape=None):
    shape = (bq, bq) if shape is None else shape
    row = lax.broadcasted_iota(jnp.int32, shape, 0) + row0
    col = lax.broadcasted_iota(jnp.int32, shape, 1)
    if chunk > 1:
        row, col = row // chunk, col // chunk
    return (row <= col) if transposed else (col <= row)


def _row_layout(a, bq):
    h, s, _ = a.shape
    return a.reshape(h, s // bq, 1, bq)


def _split_refs(refs, n_in, comm, n_out, n_scr):
    n_ci = len(comm.args) if comm else 0
    n_co = len(comm.out_shapes) if comm else 0
    cuts = [n_in, n_ci, n_out, n_co, n_scr, 3 if comm else 0]
    parts, at = [], 0
    for n in cuts:
        parts.append(list(refs[at:at + n]))
        at += n
    assert at == len(refs), (at, len(refs))
    return parts


def _comm_start(comm, c_in, c_out, c_sems, n0, n1):
    if comm is not None:
        @pl.when(jnp.logical_and(pl.program_id(0) == 0, pl.program_id(1) == 0))
        def _():
            comm.start(c_in, c_out, c_sems)


def _comm_wait(comm, c_in, c_out, c_sems, n0, n1):
    if comm is not None:
        @pl.when(jnp.logical_and(pl.program_id(0) == n0 - 1, pl.program_id(1) == n1 - 1))
        def _():
            comm.wait(c_in, c_out, c_sems)


def comm_call(name, comm):
    def body(*refs):
        _, c_in, _, c_out, _, c_sems = _split_refs(refs, 0, comm, 0, 0)
        comm.start(c_in, c_out, c_sems)
        comm.wait(c_in, c_out, c_sems)

    c_ins, c_outs, c_shapes, c_scratch, aliases = _with_comm(comm, 0, 0)
    return pl.pallas_call(body, name=name, in_specs=c_ins, out_specs=c_outs, out_shape=c_shapes,
                          scratch_shapes=c_scratch, input_output_aliases=aliases)(*comm.args)


def attn_fwd(name, q, k, v, offs, dqk, chunk, scale, cum=None, bq=512, comm=None):
    s_len = q.shape[0]
    nq = s_len // bq
    qoff, koff, voff = offs
    has_bias = cum is not None

    n_in = 5 if has_bias else 3

    def body(*refs):
        ins, c_in, outs, c_out, scr, c_sems = _split_refs(refs, n_in, comm, 2, 3)
        q_ref, k_ref, v_ref = ins[:3]
        if has_bias:
            cc_ref, cr_ref = ins[3:]
        o_ref, lse_ref = outs
        m_s, l_s, acc_s = scr
        _comm_start(comm, c_in, c_out, c_sems, HEADS, nq)
        i = pl.program_id(1)
        qv = q_ref[...]
        m_s[...] = jnp.full_like(m_s, NEG)
        l_s[...] = jnp.zeros_like(l_s)
        acc_s[...] = jnp.zeros_like(acc_s)

        def step(j, masked):
            off = pl.multiple_of(j * bq, bq)
            kj = k_ref[pl.ds(off, bq), :]
            vj = v_ref[pl.ds(off, bq), :]
            st = lax.dot_general(kj, qv, _NT, preferred_element_type=F32) * scale
            if has_bias:
                st = st + cr_ref[...] - cc_ref[pl.ds(off, bq), :]
            if masked:
                st = jnp.where(_mask(bq, chunk, True), st, NEG)
            m_prev = m_s[...]
            m_new = jnp.maximum(m_prev, jnp.max(st, axis=0, keepdims=True))
            alpha = jnp.exp(m_prev - m_new)
            pt = jnp.exp(st - m_new)
            l_s[...] = alpha * l_s[...] + jnp.sum(pt, axis=0, keepdims=True)
            acc_s[...] = alpha * acc_s[...] + lax.dot_general(vj, pt.astype(BF16), _DIMS["tn"],
                                                              preferred_element_type=F32)
            m_s[...] = m_new

        def pair_body(jj, carry):
            step(2 * jj, False)
            step(2 * jj + 1, False)
            return carry

        lax.fori_loop(0, i // 2, pair_body, 0)

        @pl.when(i % 2 == 1)
        def _():
            step(i - 1, False)

        step(i, True)
        o_ref[...] = (acc_s[...] / l_s[...]).T.astype(o_ref.dtype)
        lse_ref[...] = m_s[...] + jnp.log(l_s[...])
        _comm_wait(comm, c_in, c_out, c_sems, HEADS, nq)

    in_specs = [
        pl.BlockSpec((bq, dqk), lambda h, i: (i, qoff + h)),
        pl.BlockSpec((s_len, dqk), lambda h, i: (0, koff + h)),
        pl.BlockSpec((s_len, HEAD_V), lambda h, i: (0, voff + h)),
    ]
    args = [q, k, v]
    if has_bias:
        in_specs += [pl.BlockSpec((None, s_len, 1), lambda h, i: (h, 0, 0)),
                     pl.BlockSpec((None, None, 1, bq), lambda h, i: (h, i, 0, 0))]
        args += [cum, _row_layout(cum, bq)]
    c_ins, c_outs, c_shapes, c_scratch, aliases = _with_comm(comm, len(args), 2)
    o, lse_rows, *comm_out = pl.pallas_call(
        body, name=name, grid=(HEADS, nq), in_specs=in_specs + c_ins,
        out_specs=[pl.BlockSpec((bq, HEAD_V), lambda h, i: (i, h)),
                   pl.BlockSpec((None, None, 1, bq), lambda h, i: (h, i, 0, 0))] + c_outs,
        out_shape=[jax.ShapeDtypeStruct((s_len, HEADS * HEAD_V), F32),
                   jax.ShapeDtypeStruct((HEADS, nq, 1, bq), F32)] + c_shapes,
        scratch_shapes=[pltpu.VMEM((1, bq), F32), pltpu.VMEM((1, bq), F32), pltpu.VMEM((HEAD_V, bq), F32)] + c_scratch,
        input_output_aliases=aliases,
        compiler_params=_params(("arbitrary", "arbitrary")),
    )(*args, *(comm.args if comm else []))
    return (o, lse_rows.reshape(HEADS, s_len, 1), *comm_out)


def attn_dq(name, q, k, v, o, do, lse, offs, dqk, chunk, scale, out_dtype, cum=None, bq=512, comm=None):
    s_len = q.shape[0]
    nq = s_len // bq
    qoff, koff, voff = offs
    has_bias = cum is not None

    n_in, n_scr = (8, 3) if has_bias else (6, 1)

    def body(*refs):
        ins, c_in, outs, c_out, scr, c_sems = _split_refs(refs, n_in, comm, 2, n_scr)
        q_ref, k_ref, v_ref, o_ref, do_ref, lse_ref = ins[:6]
        dq_ref, delta_ref = outs
        acc_s = scr[0]
        if has_bias:
            cc_ref, cr_ref = ins[6:]
            pk_s, dl_s = scr[1:]
        _comm_start(comm, c_in, c_out, c_sems, HEADS, nq)
        i = pl.program_id(1)
        qv = q_ref[...]
        dov = do_ref[...].astype(F32)
        do_b = dov.astype(BF16)
        lse_v = lse_ref[...]
        acc_s[...] = jnp.zeros_like(acc_s)
        if has_bias:
            pk_s[...] = jnp.zeros_like(pk_s)
            dl_s[...] = jnp.zeros_like(dl_s)
        else:
            delta = jnp.sum(dov * o_ref[...].astype(F32), axis=-1, keepdims=True)
            delta_ref[...] = delta

        def step(j, masked):
            off = pl.multiple_of(j * bq, bq)
            kj = k_ref[pl.ds(off, bq), :]
            vj = v_ref[pl.ds(off, bq), :]
            s = lax.dot_general(qv, kj, _NT, preferred_element_type=F32) * scale
            if has_bias:
                s = s + cc_ref[...] - cr_ref[j]
            if masked:
                s = jnp.where(_mask(bq, chunk, False), s, NEG)
            p = jnp.exp(s - lse_v)
            dp = lax.dot_general(do_b, vj, _NT, preferred_element_type=F32)
            if has_bias:
                pd = p * dp
                dl_s[...] += jnp.sum(pd, axis=-1, keepdims=True)
                acc_s[...] += lax.dot_general(pd.astype(BF16), kj, _NN, preferred_element_type=F32)
                pk_s[...] += lax.dot_general(p.astype(BF16), kj, _NN, preferred_element_type=F32)
            else:
                ds = p * (dp - delta)
                acc_s[...] += lax.dot_general(ds.astype(BF16), kj, _NN, preferred_element_type=F32)

        def loop_body(j, carry):
            step(j, False)
            return carry

        lax.fori_loop(0, i, loop_body, 0)
        step(i, True)
        if has_bias:
            delta_ref[...] = dl_s[...]
            dq_ref[...] = ((acc_s[...] - dl_s[...] * pk_s[...]) * scale).astype(dq_ref.dtype)
        else:
            dq_ref[...] = (acc_s[...] * scale).astype(dq_ref.dtype)
        _comm_wait(comm, c_in, c_out, c_sems, HEADS, nq)

    in_specs = [
        pl.BlockSpec((bq, dqk), lambda h, i: (i, qoff + h)),
        pl.BlockSpec((s_len, dqk), lambda h, i: (0, koff + h)),
        pl.BlockSpec((s_len, HEAD_V), lambda h, i: (0, voff + h)),
        pl.BlockSpec((bq, HEAD_V), lambda h, i: (i, h)),
        pl.BlockSpec((bq, HEAD_V), lambda h, i: (i, h)),
        pl.BlockSpec((None, bq, 1), lambda h, i: (h, i, 0)),
    ]
    args = [q, k, v, o, do, lse]
    if has_bias:
        in_specs += [pl.BlockSpec((None, bq, 1), lambda h, i: (h, i, 0)),
                     pl.BlockSpec((None, nq, 1, bq), lambda h, i: (h, 0, 0, 0))]
        args += [cum, _row_layout(cum, bq)]
    c_ins, c_outs, c_shapes, c_scratch, aliases = _with_comm(comm, len(args), 2)
    return pl.pallas_call(
        body, name=name, grid=(HEADS, nq), in_specs=in_specs + c_ins,
        out_specs=[pl.BlockSpec((bq, dqk), lambda h, i: (i, h)),
                   pl.BlockSpec((None, bq, 1), lambda h, i: (h, i, 0))] + c_outs,
        out_shape=[jax.ShapeDtypeStruct((s_len, HEADS * dqk), out_dtype),
                   jax.ShapeDtypeStruct((HEADS, s_len, 1), F32)] + c_shapes,
        scratch_shapes=[pltpu.VMEM((bq, dqk), F32)] + (
            [pltpu.VMEM((bq, dqk), F32), pltpu.VMEM((bq, 1), F32)] if has_bias else []) + c_scratch,
        input_output_aliases=aliases,
        compiler_params=_params(("arbitrary", "arbitrary")),
    )(*args, *(comm.args if comm else []))


def attn_dkv(name, q, k, v, do, lse, delta, offs, dqk, chunk, scale, dk_dtype, cum=None, bq=512, comm=None):
    s_len = q.shape[0]
    nq = s_len // bq
    qoff, koff, voff = offs
    has_bias = cum is not None

    n_in, n_out = (8, 3) if has_bias else (6, 2)

    def body(*refs):
        ins, c_in, outs, c_out, scr, c_sems = _split_refs(refs, n_in, comm, n_out, n_out)
        k_ref, v_ref, q_ref, do_ref, lse_ref, delta_ref = ins[:6]
        dk_ref, dv_ref = outs[:2]
        dk_s, dv_s = scr[:2]
        if has_bias:
            cc_ref, cr_ref = ins[6:]
            dc_ref, dc_s = outs[2], scr[2]
        _comm_start(comm, c_in, c_out, c_sems, HEADS, nq)
        j = pl.program_id(1)
        kv = k_ref[...]
        vv = v_ref[...]
        dk_s[...] = jnp.zeros_like(dk_s)
        dv_s[...] = jnp.zeros_like(dv_s)
        if has_bias:
            dc_s[...] = jnp.zeros_like(dc_s)

        def step(i, masked):
            off = pl.multiple_of(i * bq, bq)
            qi = q_ref[pl.ds(off, bq), :]
            doi = do_ref[pl.ds(off, bq), :].astype(BF16)
            st = lax.dot_general(kv, qi, _NT, preferred_element_type=F32) * scale
            if has_bias:
                st = st + cr_ref[i] - cc_ref[...]
            if masked:
                st = jnp.where(_mask(bq, chunk, True), st, NEG)
            pt = jnp.exp(st - lse_ref[i])
            dv_s[...] += lax.dot_general(pt.astype(BF16), doi, _NN, preferred_element_type=F32)
            dpt = lax.dot_general(vv, doi, _NT, preferred_element_type=F32)
            dst = pt * (dpt - delta_ref[i])
            dk_s[...] += lax.dot_general(dst.astype(BF16), qi, _NN, preferred_element_type=F32)
            if has_bias:
                dc_s[...] -= jnp.sum(dst, axis=-1, keepdims=True)

        step(j, True)

        def loop_body(i, carry):
            step(i, False)
            return carry

        lax.fori_loop(j + 1, nq, loop_body, 0)
        dk_ref[...] = (dk_s[...] * scale).astype(dk_ref.dtype)
        dv_ref[...] = dv_s[...].astype(dv_ref.dtype)
        if has_bias:
            dc_ref[...] = dc_s[...]
        _comm_wait(comm, c_in, c_out, c_sems, HEADS, nq)

    in_specs = [
        pl.BlockSpec((bq, dqk), lambda h, j: (j, koff + h)),
        pl.BlockSpec((bq, HEAD_V), lambda h, j: (j, voff + h)),
        pl.BlockSpec((s_len, dqk), lambda h, j: (0, qoff + h)),
        pl.BlockSpec((s_len, HEAD_V), lambda h, j: (0, h)),
        pl.BlockSpec((None, nq, 1, bq), lambda h, j: (h, 0, 0, 0)),
        pl.BlockSpec((None, nq, 1, bq), lambda h, j: (h, 0, 0, 0)),
    ]
    args = [k, v, q, do, _row_layout(lse, bq), _row_layout(delta, bq)]
    out_specs = [pl.BlockSpec((bq, dqk), lambda h, j: (j, h)), pl.BlockSpec((bq, HEAD_V), lambda h, j: (j, h))]
    out_shape = [jax.ShapeDtypeStruct((s_len, HEADS * dqk), dk_dtype), jax.ShapeDtypeStruct((s_len, HEADS * HEAD_V), BF16)]
    scratch = [pltpu.VMEM((bq, dqk), F32), pltpu.VMEM((bq, HEAD_V), F32)]
    if has_bias:
        in_specs += [pl.BlockSpec((None, bq, 1), lambda h, j: (h, j, 0)),
                     pl.BlockSpec((None, nq, 1, bq), lambda h, j: (h, 0, 0, 0))]
        args += [cum, _row_layout(cum, bq)]
        out_specs.append(pl.BlockSpec((None, bq, 1), lambda h, j: (h, j, 0)))
        out_shape.append(jax.ShapeDtypeStruct((HEADS, s_len, 1), F32))
        scratch.append(pltpu.VMEM((bq, 1), F32))
    c_ins, c_outs, c_shapes, c_scratch, aliases = _with_comm(comm, len(args), n_out)
    return pl.pallas_call(
        body, name=name, grid=(HEADS, nq), in_specs=in_specs + c_ins, out_specs=out_specs + c_outs,
        out_shape=out_shape + c_shapes, scratch_shapes=scratch + c_scratch, input_output_aliases=aliases,
        compiler_params=_params(("arbitrary", "arbitrary")),
    )(*args, *(comm.args if comm else []))


_CUM_BLK = 512


def _split3(x):
    hi = x.astype(BF16)
    r1 = x - hi.astype(F32)
    mid = r1.astype(BF16)
    lo = (r1 - mid.astype(F32)).astype(BF16)
    return hi, mid, lo


def _tri_dot(x, tri):
    hi, mid, lo = _split3(x)
    out = lax.dot_general(lo, tri, _NN, preferred_element_type=F32)
    out = out + lax.dot_general(mid, tri, _NN, preferred_element_type=F32)
    return out + lax.dot_general(hi, tri, _NN, preferred_element_type=F32)


def fox_cum_fwd(name, ff_t, bias):
    s_len = ff_t.shape[1]
    nb = s_len // _CUM_BLK

    def body(ff_ref, b_ref, cum_ref):
        row = lax.broadcasted_iota(jnp.int32, (_CUM_BLK, _CUM_BLK), 0)
        col = lax.broadcasted_iota(jnp.int32, (_CUM_BLK, _CUM_BLK), 1)
        tri = (row <= col).astype(BF16)
        carry = jnp.zeros((HEADS, 1), F32)
        for b in range(nb):
            z = ff_ref[:, b * _CUM_BLK:(b + 1) * _CUM_BLK] + b_ref[...]
            logf = jnp.minimum(z, 0.0) - jnp.log1p(jnp.exp(-jnp.abs(z)))
            blk = _tri_dot(logf, tri) + carry
            cum_ref[:, b * _CUM_BLK:(b + 1) * _CUM_BLK] = blk
            carry = blk[:, _CUM_BLK - 1:_CUM_BLK]

    return pl.pallas_call(
        body, name=name, out_shape=jax.ShapeDtypeStruct((HEADS, s_len), F32),
        compiler_params=pltpu.CompilerParams(vmem_limit_bytes=VMEM_LIMIT),
    )(ff_t, bias)


def fox_cum_bwd(name, ff_t, bias, dcum):
    s_len = ff_t.shape[1]
    nb = s_len // _CUM_BLK

    def body(ff_ref, b_ref, dc_ref, dff_ref, db_ref):
        row = lax.broadcasted_iota(jnp.int32, (_CUM_BLK, _CUM_BLK), 0)
        col = lax.broadcasted_iota(jnp.int32, (_CUM_BLK, _CUM_BLK), 1)
        tri = (row >= col).astype(BF16)
        carry = jnp.zeros((HEADS, 1), F32)
        dbias = jnp.zeros((HEADS, 1), F32)
        for b in reversed(range(nb)):
            sl = slice(b * _CUM_BLK, (b + 1) * _CUM_BLK)
            dlogf = _tri_dot(dc_ref[:, sl], tri) + carry
            carry = dlogf[:, 0:1]
            z = ff_ref[:, sl] + b_ref[...]
            dz = dlogf / (1.0 + jnp.exp(z))
            dff_ref[:, sl] = dz
            dbias = dbias + jnp.sum(dz, axis=-1, keepdims=True)
        db_ref[...] = dbias

    return pl.pallas_call(
        body, name=name,
        out_shape=[jax.ShapeDtypeStruct((HEADS, s_len), F32), jax.ShapeDtypeStruct((HEADS, 1), F32)],
        compiler_params=pltpu.CompilerParams(vmem_limit_bytes=VMEM_LIMIT),
    )(ff_t, bias, dcum)


_ANY = pl.BlockSpec(memory_space=pl.ANY)


def _me():
    return lax.axis_index("x"), lax.axis_index("y"), lax.axis_index("c")


def comm_allgather(name, mine):
    n_rows, n_cols = mine.shape

    def body(x_ref, out_ref, send_sems, recv_sems, local_sem):
        x, y, c = _me()
        sibling = (x, y, 1 - c)
        chips = [(1 - x, y), (x, 1 - y), (1 - x, 1 - y)]

        def blk(px, py, pc):
            return out_ref.at[pc * 4 + px * 2 + py]

        def copy(k, block, to, src=None):
            return pltpu.make_async_remote_copy(
                src_ref=blk(*block) if src is None else src, dst_ref=blk(*block),
                send_sem=send_sems.at[k], recv_sem=recv_sems.at[k], device_id=to, device_id_type=MESH)

        own = pltpu.make_async_copy(x_ref, blk(x, y, c), local_sem)
        own.start()
        first = [copy(0, (x, y, c), sibling, src=x_ref)]
        first += [copy(1 + j, (x, y, c), (*chip, c), src=x_ref) for j, chip in enumerate(chips)]
        for cp in first:
            cp.start()
        passed = [copy(4 + j, (*chip, c), sibling) for j, chip in enumerate(chips)]
        for j, chip in enumerate(chips):
            copy(1 + j, (*chip, c), (x, y, c)).wait_recv()
            passed[j].start()
        copy(0, sibling, (x, y, c)).wait_recv()
        for j, chip in enumerate(chips):
            copy(4 + j, (*chip, 1 - c), (x, y, c)).wait_recv()
        for cp in first + passed:
            cp.wait_send()
        own.wait()

    return pl.pallas_call(
        body, name=name, out_shape=jax.ShapeDtypeStruct((N_DEV, n_rows, n_cols), mine.dtype),
        in_specs=[_ANY], out_specs=_ANY,
        scratch_shapes=[pltpu.SemaphoreType.DMA((7,)), pltpu.SemaphoreType.DMA((7,)), pltpu.SemaphoreType.DMA],
    )(mine)


def comm_swap_sibling(name, parts):
    _, n_rows, n_cols = parts.shape

    def body(p_ref, got_ref, send_sem, recv_sem):
        x, y, c = _me()
        cp = pltpu.make_async_remote_copy(
            src_ref=p_ref.at[pl.ds((1 - c) * 4, 4)], dst_ref=got_ref, send_sem=send_sem, recv_sem=recv_sem,
            device_id=(x, y, 1 - c), device_id_type=MESH)
        cp.start()
        cp.wait()

    return pl.pallas_call(
        body, name=name, out_shape=jax.ShapeDtypeStruct((4, n_rows, n_cols), parts.dtype),
        in_specs=[_ANY], out_specs=_ANY,
        scratch_shapes=[pltpu.SemaphoreType.DMA, pltpu.SemaphoreType.DMA],
    )(parts)


def comm_swap_chips(name, parts):
    _, n_rows, n_cols = parts.shape

    def body(p_ref, got_ref, send_sems, recv_sems):
        x, y, c = _me()
        chips = [(1 - x, y), (x, 1 - y), (1 - x, 1 - y)]
        cps = [pltpu.make_async_remote_copy(
            src_ref=p_ref.at[2 * px + py], dst_ref=got_ref.at[k], send_sem=send_sems.at[k], recv_sem=recv_sems.at[k],
            device_id=(px, py, c), device_id_type=MESH) for k, (px, py) in enumerate(chips)]
        for cp in cps:
            cp.start()
        for cp in cps:
            cp.wait()

    return pl.pallas_call(
        body, name=name, out_shape=jax.ShapeDtypeStruct((3, n_rows, n_cols), parts.dtype),
        in_specs=[_ANY], out_specs=_ANY,
        scratch_shapes=[pltpu.SemaphoreType.DMA((3,)), pltpu.SemaphoreType.DMA((3,))],
    )(parts)


class CommHook:
    def __init__(self, args, out_shapes, n_copies, copies, aliases=None):
        self.args, self.out_shapes, self.n_copies, self.copies = list(args), list(out_shapes), n_copies, copies
        self.aliases = aliases or {}

    def scratch(self):
        return [pltpu.SemaphoreType.DMA((self.n_copies,)), pltpu.SemaphoreType.DMA((self.n_copies,)),
                pltpu.SemaphoreType.DMA((1,))]

    def start(self, in_refs, out_refs, sems):
        sends, _, locs = self.copies(in_refs, out_refs, *sems)
        for cp in locs() + sends():
            cp.start()

    def wait(self, in_refs, out_refs, sems):
        sends, recvs, locs = self.copies(in_refs, out_refs, *sems)
        for cp in sends():
            cp.wait_send()
        for cp in recvs():
            cp.wait_recv()
        for cp in locs():
            cp.wait()


def _remote(src, dst, send_sem, recv_sem, to):
    return pltpu.make_async_remote_copy(src_ref=src, dst_ref=dst, send_sem=send_sem, recv_sem=recv_sem,
                                        device_id=to, device_id_type=MESH)


def hook_gather_first(mine):
    n_rows, n_cols = mine.shape

    def copies(ins, outs, send, recv, local):
        (x_ref,), (out_ref,) = ins, outs
        x, y, c = _me()
        me = c * 4 + x * 2 + y
        peers = [(x, y, 1 - c), (1 - x, y, c), (x, 1 - y, c), (1 - x, 1 - y, c)]

        def sends():
            return [_remote(x_ref, out_ref.at[me], send.at[k], recv.at[k], p) for k, p in enumerate(peers)]

        def recvs():
            return [_remote(x_ref, out_ref.at[pc * 4 + px * 2 + py], send.at[k], recv.at[k], (px, py, pc))
                    for k, (px, py, pc) in enumerate(peers)]

        return sends, recvs, lambda: [pltpu.make_async_copy(x_ref, out_ref.at[me], local.at[0])]

    return CommHook([mine], [jax.ShapeDtypeStruct((N_DEV, n_rows, n_cols), mine.dtype)], 4, copies)


def hook_gather_second(gathered):
    def copies(ins, outs, send, recv, local):
        (g_in,), (g_out,) = ins, outs
        x, y, c = _me()
        chips = [(1 - x, y), (x, 1 - y), (1 - x, 1 - y)]

        def sends():
            return [_remote(g_in.at[c * 4 + px * 2 + py], g_out.at[c * 4 + px * 2 + py], send.at[k], recv.at[k],
                            (x, y, 1 - c)) for k, (px, py) in enumerate(chips)]

        def recvs():
            return [_remote(g_in.at[(1 - c) * 4 + px * 2 + py], g_out.at[(1 - c) * 4 + px * 2 + py], send.at[k],
                            recv.at[k], (x, y, 1 - c)) for k, (px, py) in enumerate(chips)]

        return sends, recvs, lambda: []

    return CommHook([gathered], [jax.ShapeDtypeStruct(gathered.shape, gathered.dtype)], 3, copies, aliases={0: 0})


def hook_swap_sibling(parts):
    _, n_rows, n_cols = parts.shape

    def copies(ins, outs, send, recv, local):
        (p_ref,), (got_ref,) = ins, outs
        x, y, c = _me()

        def swap():
            return [_remote(p_ref.at[pl.ds((1 - c) * 4, 4)], got_ref, send.at[0], recv.at[0], (x, y, 1 - c))]

        return swap, swap, lambda: []

    return CommHook([parts], [jax.ShapeDtypeStruct((4, n_rows, n_cols), parts.dtype)], 1, copies)


def hook_swap_chips(parts):
    _, n_rows, n_cols = parts.shape

    def copies(ins, outs, send, recv, local):
        (p_ref,), (got_ref,) = ins, outs
        x, y, c = _me()
        chips = [(1 - x, y), (x, 1 - y), (1 - x, 1 - y)]

        def swaps():
            return [_remote(p_ref.at[2 * px + py], got_ref.at[k], send.at[k], recv.at[k], (px, py, c))
                    for k, (px, py) in enumerate(chips)]

        return swaps, swaps, lambda: []

    return CommHook([parts], [jax.ShapeDtypeStruct((3, n_rows, n_cols), parts.dtype)], 3, copies)


def _with_comm(comm, n_args, n_outs):
    if comm is None:
        return [], [], [], [], {}
    aliases = {n_args + a: n_outs + o for a, o in comm.aliases.items()}
    return [_ANY] * len(comm.args), [_ANY] * len(comm.out_shapes), comm.out_shapes, comm.scratch(), aliases


def comm_allreduce_small(name, mine):
    shape = mine.shape

    def body(x_ref, out_ref, buf, send_sems, recv_sems):
        x, y, c = _me()
        my_slot = c * 4 + x * 2 + y
        buf[my_slot] = x_ref[...]
        cps = []
        for k in range(1, N_DEV):
            dx, dy, dc = (k >> 2) & 1, (k >> 1) & 1, k & 1
            px, py, pc = x ^ dx, y ^ dy, c ^ dc
            send = pltpu.make_async_remote_copy(
                src_ref=x_ref, dst_ref=buf.at[my_slot], send_sem=send_sems.at[k - 1], recv_sem=recv_sems.at[k - 1],
                device_id=(px, py, pc), device_id_type=MESH)
            send.start()
            recv = pltpu.make_async_remote_copy(
                src_ref=x_ref, dst_ref=buf.at[pc * 4 + px * 2 + py], send_sem=send_sems.at[k - 1],
                recv_sem=recv_sems.at[k - 1], device_id=(px, py, pc), device_id_type=MESH)
            cps.append((send, recv))
        for send, recv in cps:
            send.wait_send()
            recv.wait_recv()
        total = buf[0]
        for s in range(1, N_DEV):
            total = total + buf[s]
        out_ref[...] = total

    vmem = pl.BlockSpec(memory_space=pltpu.VMEM)
    return pl.pallas_call(
        body, name=name, out_shape=jax.ShapeDtypeStruct(shape, F32), in_specs=[vmem], out_specs=vmem,
        scratch_shapes=[pltpu.VMEM((N_DEV,) + shape, F32), pltpu.SemaphoreType.DMA((7,)), pltpu.SemaphoreType.DMA((7,))],
    )(mine)


def pack_local(shards, names):
    flat = [shards[n].reshape(-1, PACK_C) for n in names]
    rows = sum(f.shape[0] for f in flat)
    pad = (-rows) % 128
    return jnp.concatenate(flat + [jnp.zeros((pad, PACK_C), flat[0].dtype)], axis=0)


def unpack_group(gathered, names, shard_shapes):
    _, n_rows, _ = gathered.shape
    by_block = gathered.reshape(2, 4, n_rows, PACK_C).transpose(1, 0, 2, 3).reshape(N_DEV, n_rows, PACK_C)
    full = {}
    r0 = 0
    for name in names:
        rs, cs = shard_shapes[name]
        nr = rs * cs // PACK_C
        piece = by_block[:, r0:r0 + nr, :].reshape(N_DEV, rs, cs)
        r0 += nr
        if name in ROW_SHARDED:
            full[name] = piece.reshape(N_DEV * rs, cs)
        else:
            full[name] = piece.transpose(1, 0, 2).reshape(rs, N_DEV * cs)
    return full


def unpack_a(gathered, shard_shapes):
    full = unpack_group(gathered, GROUP_A, shard_shapes)
    w_in = full.pop("w_in")
    zeros = jnp.zeros((D_MODEL, LAT_W - OFF_FQ - HEADS), w_in.dtype)
    full["w_lat"] = jnp.concatenate([w_in[:, :OFF_FQ], w_in[:, OFF_FF:OFF_G], zeros], axis=1)
    full["w_fox"] = w_in[:, OFF_FQ:OFF_FF]
    full["w_gate"] = w_in[:, OFF_G:]
    w_uq = full.pop("w_uq").reshape(Q_LORA, HEADS, NOPE + ROPE)
    full["w_uq"] = jnp.pad(w_uq, ((0, 0), (0, 0), (0, QK_PAD - NOPE - ROPE))).reshape(Q_LORA, HEADS * QK_PAD)
    w_ukv = full.pop("w_ukv").reshape(KV_LORA, HEADS, 2, NOPE)
    full["w_kv"] = jnp.concatenate([w_ukv[:, :, 0, :].reshape(KV_LORA, HEADS * NOPE),
                                    w_ukv[:, :, 1, :].reshape(KV_LORA, HEADS * HEAD_V)], axis=1)
    return full


def pack_small(vals, loss):
    flat = [vals[n].reshape(-1) for n in SMALL] + [loss.reshape(-1)]
    used = sum(f.shape[0] for f in flat)
    flat.append(jnp.zeros((SMALL_ROWS * PACK_C - used,), F32))
    return jnp.concatenate(flat).reshape(SMALL_ROWS, PACK_C)


def unpack_small(packed):
    flat = packed.reshape(-1)
    out, off = {}, 0
    for n in SMALL:
        out[n] = flat[off:off + SMALL_N[n]]
        off += SMALL_N[n]
    return out, flat[off]


def rope_tables(s_len):
    pos = jnp.arange(s_len, dtype=F32)
    inv = 1.0 / (ROPE_THETA ** (jnp.arange(0, ROPE, 2, dtype=F32) / ROPE))
    ang = pos[:, None] * inv[None, :]
    cos, sin = jnp.cos(ang), jnp.sin(ang)
    zero = jnp.zeros_like(cos)
    c = jnp.concatenate([cos, cos, zero, zero], axis=1)
    s1 = jnp.concatenate([-sin, zero, zero, zero], axis=1)
    s2 = jnp.concatenate([zero, sin, zero, zero], axis=1)
    return (c, s1, s2), (c, -s1, -s2)


def reduce_scatter_tail(parts, from_sibling, from_chips_fn, names):
    cx, cy, cc = _me()
    n_rows = parts.shape[1]
    mine4 = lax.dynamic_slice_in_dim(parts, cc * 4, 4, axis=0)
    pair = add_pairs("rs_pair_sum_" + names, mine4.reshape(4 * n_rows, PACK_C), from_sibling.reshape(4 * n_rows, PACK_C))
    from_chips, extra = from_chips_fn(pair.reshape(4, n_rows, PACK_C))
    own = cx * 2 + cy
    total = add_final("rs_final_sum_" + names, lax.dynamic_index_in_dim(mine4, own, 0, keepdims=False),
                      lax.dynamic_index_in_dim(from_sibling, own, 0, keepdims=False),
                      from_chips[0], from_chips[1], from_chips[2])
    return total, extra


def local_step(x, target, w, small, packed_b, shard_shapes):
    s_len = x.shape[0]
    tabs, inv_tabs = rope_tables(s_len)
    g_attn = small["attn_norm"].reshape(1, D_MODEL)
    g_q = small["q_norm"].reshape(1, Q_LORA)
    g_kv = small["kv_norm"].reshape(1, KV_LORA)
    g_mlp = small["mlp_norm"].reshape(1, D_MODEL)
    g_final = small["final_norm"].reshape(1, D_MODEL)
    f_bias = small["fox_f_bias"].reshape(HEADS, 1)
    mla_scale = 1.0 / math.sqrt(NOPE + ROPE)
    fox_scale = 1.0 / math.sqrt(HEAD_V)
    mla_offs = (0, 0, HEADS)
    fox_offs = (0, HEADS, 2 * HEADS)

    xn = rms_fwd("rms_attn", x, g_attn)
    lat, = matmul("proj_lat", xn, w["w_lat"], "nn", [F32])
    fox, = matmul("proj_fox", xn, w["w_fox"], "nn", [BF16])
    graw, = matmul("proj_gate", xn, w["w_gate"], "nn", [F32])
    cq = rms_fwd("rms_q", (lat, Q_LORA, 0), g_q)
    ckv = rms_fwd("rms_kv", (lat, KV_LORA, Q_LORA // KV_LORA), g_kv)
    qraw, = matmul("up_q", cq, w["w_uq"], "nn", [F32])
    q = rope_heads("rope_q", qraw, tabs, BF16)
    kvn, = matmul("up_kv", ckv, w["w_kv"], "nn", [BF16])
    kr = rope_block("rope_k", lat, OFF_KR // 128, tabs, BF16)
    k = k_assemble("k_assemble", kvn, kr)
    o_mla, lse_mla, gathered_b = attn_fwd("mla_fwd", q, k, kvn, mla_offs, QK_PAD, CHUNK, mla_scale,
                                          comm=hook_gather_first(packed_b))
    ff_t = lat[:, OFF_FQ:OFF_FQ + HEADS].T
    cum = fox_cum_fwd("fox_cum", ff_t, f_bias).reshape(HEADS, s_len, 1)
    o_fox, lse_fox, gathered_b = attn_fwd("fox_fwd", fox, fox, fox, fox_offs, HEAD_V, 1, fox_scale, cum=cum,
                                          comm=hook_gather_second(gathered_b))
    unpack_b = functools.partial(unpack_group, names=GROUP_B, shard_shapes=shard_shapes)
    w = {**w, **unpack_b(gathered_b)}
    y_mla, = matmul("branch_mla", o_mla, w["w_mla_branch"], "nn", [F32])
    y_fox, = matmul("branch_fox", o_fox, w["w_fox_branch"], "nn", [F32])
    mix = gate_mix("gate_mix", graw, y_mla, y_fox)
    h1, = matmul("out_proj", mix, w["w_out"], "nn", [F32], epilogue=lambda acc, res: (res + acc,), extras=[x])
    hn = rms_fwd("rms_mlp", h1, g_mlp)

    def relu2(acc):
        r = jnp.maximum(acc, 0.0)
        return r * r, r
    u, relu_up = matmul("mlp_up", hn, w["w_up"], "nn", [BF16, BF16], epilogue=relu2)
    h2, = matmul("mlp_down", u, w["w_down"], "nn", [F32], epilogue=lambda acc, res: (res + acc,), extras=[h1])
    dh2, d_final, loss = loss_head("loss_head", h2, target, g_final)

    grads = {}
    dup, = matmul("d_mlp_down", dh2, w["w_down"], "nt", [BF16],
                  epilogue=lambda acc, r: (acc * (2.0 * r.astype(F32)),), extras=[relu_up])
    grads["w_down"], = matmul("gw_down", u, dh2, "tn", [BF16], tm=2048, tn=1024, tk=512)
    dhn, = matmul("d_mlp_up", dup, w["w_up"], "nt", [F32])
    grads["w_up"], = matmul("gw_up", hn, dup, "tn", [BF16], tm=2048, tn=1024, tk=512)
    dh1, d_mlp = rms_bwd("rms_mlp_bwd", h1, dhn, g_mlp, dres=dh2)
    dmix, = matmul("d_out_proj", dh1, w["w_out"], "nt", [F32])
    grads["w_out"], = matmul("gw_out", mix, dh1, "tn", [BF16], tm=2048, tn=1024, tk=512)
    dgraw, dy_mla, dy_fox = gate_mix_bwd("gate_mix_bwd", graw, y_mla, y_fox, dmix)
    do_mla, = matmul("d_branch_mla", dy_mla, w["w_mla_branch"], "nt", [F32])
    grads["w_mla_branch"], = matmul("gw_branch_mla", o_mla, dy_mla, "tn", [BF16], tk=512)
    do_fox, = matmul("d_branch_fox", dy_fox, w["w_fox_branch"], "nt", [F32])
    grads["w_fox_branch"], = matmul("gw_branch_fox", o_fox, dy_fox, "tn", [BF16], tk=512)

    to_packed_b = jax.linear_transpose(unpack_b, jax.ShapeDtypeStruct(gathered_b.shape, BF16))
    parts_b, = to_packed_b({n: grads.pop(n) for n in GROUP_B})
    dq, delta_mla, from_sibling = attn_dq("mla_dq", q, k, kvn, o_mla, do_mla, lse_mla, mla_offs, QK_PAD, CHUNK,
                                          mla_scale, F32, comm=hook_swap_sibling(parts_b))

    def chips_behind_dkv(pair):
        dk, dv, from_chips = attn_dkv("mla_dkv", q, k, kvn, do_mla, lse_mla, delta_mla, mla_offs, QK_PAD, CHUNK,
                                      mla_scale, F32, comm=hook_swap_chips(pair))
        return from_chips, (dk, dv)
    g_packed_b, (dk, dv) = reduce_scatter_tail(parts_b, from_sibling, chips_behind_dkv, "b")
    dq_r =rope_heads("rope_q_bwd", dq, inv_tabs, BF16)
    dkvn, dkr = dk_split("dk_split", dk, dv, inv_tabs)
    dcq, = matmul("d_up_q", dq_r, w["w_uq"], "nt", [F32])
    grads["w_uq"], = matmul("gw_uq", cq, dq_r, "tn", [BF16], tk=512)
    dckv, = matmul("d_up_kv", dkvn, w["w_kv"], "nt", [F32])
    grads["w_kv"], = matmul("gw_kv", ckv, dkvn, "tn", [BF16], tk=512)
    dcq_raw, d_qn = rms_bwd("rms_q_bwd", (lat, Q_LORA, 0), dcq, g_q, out_dtype=BF16)
    dckv_raw, d_kvn = rms_bwd("rms_kv_bwd", (lat, KV_LORA, Q_LORA // KV_LORA), dckv, g_kv, out_dtype=BF16)

    dfq, delta_fox = attn_dq("fox_dq", fox, fox, fox, o_fox, do_fox, lse_fox, fox_offs, HEAD_V, 1, fox_scale, BF16, cum=cum)
    dfk, dfv, dcum = attn_dkv("fox_dkv", fox, fox, fox, do_fox, lse_fox, delta_fox, fox_offs, HEAD_V, 1, fox_scale, BF16,
                              cum=cum)
    dff_t, d_bias = fox_cum_bwd("fox_cum_bwd", ff_t, f_bias, dcum.reshape(HEADS, s_len))

    pad = jnp.zeros((s_len, LAT_W - OFF_FQ - HEADS), BF16)
    dproj = jnp.concatenate([dcq_raw, dckv_raw, dkr[:, :ROPE].astype(BF16), dff_t.T.astype(BF16), pad,
                             dfq, dfk, dfv, dgraw], axis=1)
    w_in_p = jnp.concatenate([w["w_lat"], w["w_fox"], w["w_gate"]], axis=1)
    dxn, = matmul("d_proj", dproj, w_in_p, "nt", [F32], tk=1152)
    gw_in, = matmul("gw_in", xn, dproj, "tn", [BF16], tm=2048, tn=1152, tk=512)
    grads["w_lat"], grads["w_fox"], grads["w_gate"] = (gw_in[:, :LAT_W], gw_in[:, LAT_W:LAT_W + FOX_W],
                                                      gw_in[:, LAT_W + FOX_W:])
    dx, d_attn = rms_bwd("rms_attn_bwd", x, dxn, g_attn, dres=dh1)

    small_grads = {"attn_norm": d_attn, "fox_f_bias": d_bias, "q_norm": d_qn, "kv_norm": d_kvn,
                   "mlp_norm": d_mlp, "final_norm": d_final}
    return loss, dx, grads, g_packed_b, small_grads


def kernel(x, attn_norm, w_in, fox_f_bias, q_norm, w_uq, kv_norm, w_ukv, w_mla_branch, w_fox_branch, w_out, mlp_norm, w_up, w_down, final_norm, loss_target, m_attn_norm, m_w_in, m_fox_f_bias, m_q_norm, m_w_uq, m_kv_norm, m_w_ukv, m_w_mla_branch, m_w_fox_branch, m_w_out, m_mlp_norm, m_w_up, m_w_down, m_final_norm, v_attn_norm, v_w_in, v_fox_f_bias, v_q_norm, v_w_uq, v_kv_norm, v_w_ukv, v_w_mla_branch, v_w_fox_branch, v_w_out, v_mlp_norm, v_w_up, v_w_down, v_final_norm):
    given = dict(locals())
    big = {n: given[n][0] for n in BIG}
    small = {n: given[n] for n in SMALL}
    shard_shapes = {n: tuple(big[n].shape) for n in BIG}

    packed_a = pack_local({n: big[n].astype(BF16) for n in GROUP_A}, GROUP_A)
    packed_b = pack_local({n: big[n].astype(BF16) for n in GROUP_B}, GROUP_B)
    gathered_a = comm_allgather("comm_allgather_a", packed_a)
    unpack = functools.partial(unpack_a, shard_shapes=shard_shapes)
    w_a = unpack(gathered_a)

    loss_part, dx, grads_a, g_packed_b, small_grads = local_step(x[0], loss_target[0], w_a, small, packed_b,
                                                                 shard_shapes)

    to_packed_a = jax.linear_transpose(unpack, jax.ShapeDtypeStruct(gathered_a.shape, BF16))
    parts_a, = to_packed_a({n: grads_a[n] for n in w_a})
    from_sibling = comm_swap_sibling("comm_rs_sibling_a", parts_a)
    g_packed_a, _ = reduce_scatter_tail(parts_a, from_sibling,
                                        lambda pair: (comm_swap_chips("comm_rs_chips_a", pair), None), "a")

    small_sum = comm_allreduce_small("comm_allreduce_small", pack_small(small_grads, loss_part[0, 0]))
    g_small, loss = unpack_small(small_sum)

    grad_w, delta_w, new_m, new_v = {}, {}, {}, {}
    for names, g_packed in ((GROUP_A, g_packed_a), (GROUP_B, g_packed_b)):
        r0 = 0
        for n in names:
            rs, cs = shard_shapes[n]
            nr = rs * cs // PACK_C
            g = g_packed[r0:r0 + nr].reshape(rs, cs)
            r0 += nr
            d, m_new, v_new = adamw("adamw_" + n, big[n], g, given["m_" + n][0], given["v_" + n][0])
            grad_w[n], delta_w[n], new_m[n], new_v[n] = g[None], d[None], m_new[None], v_new[None]
    zero = jnp.zeros((), F32)
    d_s, m_s, v_s = adamw("adamw_small", pack_small(small, zero), small_sum * _small_mask(),
                          pack_small({n: given["m_" + n] for n in SMALL}, zero),
                          pack_small({n: given["v_" + n] for n in SMALL}, zero), tr=SMALL_ROWS)
    d_small, _ = unpack_small(d_s)
    m_small, _ = unpack_small(m_s)
    v_small, _ = unpack_small(v_s)
    for n in SMALL:
        shape = given[n].shape
        grad_w[n], delta_w[n] = g_small[n].reshape(shape), d_small[n].reshape(shape)
        new_m[n], new_v[n] = m_small[n].reshape(shape), v_small[n].reshape(shape)

    order = ["attn_norm", "w_in", "fox_f_bias", "q_norm", "w_uq", "kv_norm", "w_ukv", "w_mla_branch", "w_fox_branch",
             "w_out", "mlp_norm", "w_up", "w_down", "final_norm"]
    return (loss, dx[None], *[grad_w[n] for n in order], *[delta_w[n] for n in order],
            *[new_m[n] for n in order], *[new_v[n] for n in order])


def _small_mask():
    used = sum(SMALL_N[n] for n in SMALL)
    return (jnp.arange(SMALL_ROWS * PACK_C) < used).astype(F32).reshape(SMALL_ROWS, PACK_C)
```

```python
import functools
import math

import jax
import jax.numpy as jnp
from jax import lax
from jax.experimental import pallas as pl
from jax.experimental.pallas import tpu as pltpu

F32 = jnp.float32
BF16 = jnp.bfloat16
MESH = pl.DeviceIdType.MESH

D_MODEL = 2048
HEADS = 8
Q_LORA = 512
KV_LORA = 256
NOPE = 128
ROPE = 64
HEAD_V = 128
D_FF = 4 * D_MODEL
CHUNK = 64
EPS = 1e-6
ROPE_THETA = 10000.0
OFF_KR = Q_LORA + KV_LORA
OFF_FQ = OFF_KR + ROPE
OFF_FF = OFF_FQ + 3 * HEADS * HEAD_V
OFF_G = OFF_FF + HEADS
D_IN = OFF_G + 2 * D_MODEL

LAT_W = 896
FOX_W = 3 * HEADS * HEAD_V
GATE_W = 2 * D_MODEL
PROJ_W = LAT_W + FOX_W + GATE_W
QK_PAD = 256

ADAM_LR = 0.001
ADAM_B1 = 0.9
ADAM_B2 = 0.999
ADAM_EPS = 1e-08
ADAM_WD = 0.01
ADAM_STEP = 10

N_DEV = 8
PACK_C = 1024
NEG = -1e30

VMEM_LIMIT = 56 * 1024 * 1024

BIG = ("w_in", "w_uq", "w_ukv", "w_mla_branch", "w_fox_branch", "w_out", "w_up", "w_down")
GROUP_A = ("w_in", "w_uq", "w_ukv")
GROUP_B = ("w_mla_branch", "w_fox_branch", "w_out", "w_up", "w_down")
ROW_SHARDED = ("w_out", "w_down")
SMALL = ("attn_norm", "fox_f_bias", "q_norm", "kv_norm", "mlp_norm", "final_norm")
SMALL_N = {"attn_norm": D_MODEL, "fox_f_bias": HEADS, "q_norm": Q_LORA, "kv_norm": KV_LORA,
           "mlp_norm": D_MODEL, "final_norm": D_MODEL}
SMALL_ROWS = 8


def _params(sem):
    return pltpu.CompilerParams(dimension_semantics=sem, vmem_limit_bytes=VMEM_LIMIT)


def _rows(name, fn, row_ins, const_ins, outs, reds=(), tr=256):
    norm = [(a, a.shape[1], 0) if not isinstance(a, tuple) else a for a in row_ins]
    n_rows = norm[0][0].shape[0]
    tr = min(tr, n_rows)
    assert n_rows % tr == 0, (name, n_rows, tr)
    n_in, n_out, n_red = len(norm) + len(const_ins), len(outs), len(reds)

    def body(*refs):
        vals = [r[...] for r in refs[:n_in]]
        out_refs = refs[n_in:n_in + n_out]
        red_refs = refs[n_in + n_out:]
        out_vals, red_vals = fn(*vals)
        for r, v in zip(out_refs, out_vals):
            r[...] = v.astype(r.dtype)
        if n_red:
            @pl.when(pl.program_id(0) == 0)
            def _():
                for r in red_refs:
                    r[...] = jnp.zeros_like(r)
            for r, v in zip(red_refs, red_vals):
                r[...] += v

    in_specs = [pl.BlockSpec((tr, w), functools.partial(lambda i, cb: (i, cb), cb=cb)) for _, w, cb in norm]
    in_specs += [pl.BlockSpec(a.shape, lambda i: (0, 0)) for a in const_ins]
    out_specs = [pl.BlockSpec((tr, c), lambda i: (i, 0)) for c, _ in outs]
    out_specs += [pl.BlockSpec((1, c), lambda i: (0, 0)) for c in reds]
    out_shape = [jax.ShapeDtypeStruct((n_rows, c), dt) for c, dt in outs]
    out_shape += [jax.ShapeDtypeStruct((1, c), F32) for c in reds]
    res = pl.pallas_call(
        body, name=name, grid=(n_rows // tr,), in_specs=in_specs, out_specs=out_specs, out_shape=out_shape,
        compiler_params=_params(("arbitrary",)),
    )(*[a for a, _, _ in norm], *const_ins)
    return res


def _rstd(x):
    return lax.rsqrt(jnp.mean(x * x, axis=-1, keepdims=True) + EPS)


def rms_fwd(name, x, gain, tr=256):
    width = x[1] if isinstance(x, tuple) else x.shape[1]

    def fn(xv, g):
        return (xv * _rstd(xv) * g,), ()
    return _rows(name, fn, [x], [gain], [(width, BF16)], tr=tr)[0]


def rms_bwd(name, x, dy, gain, dres=None, out_dtype=F32, tr=256):
    width = x[1] if isinstance(x, tuple) else x.shape[1]

    def fn(xv, dyv, *rest):
        g = rest[-1]
        r = _rstd(xv)
        n = xv * r
        dyv = dyv.astype(F32)
        dn = dyv * g
        dx = r * (dn - n * jnp.mean(dn * n, axis=-1, keepdims=True))
        if dres is not None:
            dx = dx + rest[0]
        return (dx,), (jnp.sum(dyv * n, axis=0, keepdims=True),)

    ins = [x, dy] + ([dres] if dres is not None else [])
    return _rows(name, fn, ins, [gain], [(width, out_dtype)], [width], tr=tr)


def _rope_lanes(t, c, s1, s2):
    return t * c + pltpu.roll(t, 96, 1) * s1 + pltpu.roll(t, 32, 1) * s2


def rope_heads(name, x, tabs, out_dtype):
    def fn(xv, c, s1, s2):
        xv = xv.astype(F32)
        parts = []
        for h in range(HEADS):
            parts.append(xv[:, h * QK_PAD:h * QK_PAD + NOPE])
            parts.append(_rope_lanes(xv[:, h * QK_PAD + NOPE:(h + 1) * QK_PAD], c, s1, s2))
        return (jnp.concatenate(parts, axis=1),), ()
    return _rows(name, fn, [x, *tabs], [], [(HEADS * QK_PAD, out_dtype)])[0]


def rope_block(name, x, col_block, tabs, out_dtype):
    def fn(xv, c, s1, s2):
        return (_rope_lanes(xv.astype(F32), c, s1, s2),), ()
    return _rows(name, fn, [(x, 128, col_block), *tabs], [], [(128, out_dtype)])[0]


def k_assemble(name, kvn, kr):
    def fn(knp, krv):
        parts = []
        for h in range(HEADS):
            parts.append(knp[:, h * NOPE:(h + 1) * NOPE])
            parts.append(krv)
        return (jnp.concatenate(parts, axis=1),), ()
    return _rows(name, fn, [(kvn, HEADS * NOPE, 0), kr], [], [(HEADS * QK_PAD, BF16)])[0]


def dk_split(name, dk, dv, inv_tabs):
    def fn(dkv, dvv, c, s1, s2):
        parts = []
        acc = None
        for h in range(HEADS):
            parts.append(dkv[:, h * QK_PAD:h * QK_PAD + NOPE].astype(BF16))
            t = dkv[:, h * QK_PAD + NOPE:(h + 1) * QK_PAD]
            acc = t if acc is None else acc + t
        parts.append(dvv)
        return (jnp.concatenate(parts, axis=1), _rope_lanes(acc, c, s1, s2)), ()
    return _rows(name, fn, [dk, dv, *inv_tabs], [], [(2 * HEADS * NOPE, BF16), (128, F32)])


def gate_mix(name, graw, y_mla, y_fox):
    def fn(g, ya, yb):
        ga = jax.nn.sigmoid(g[:, :D_MODEL])
        gb = jax.nn.sigmoid(g[:, D_MODEL:])
        return (ga * ya + gb * yb,), ()
    return _rows(name, fn, [graw, y_mla, y_fox], [], [(D_MODEL, BF16)])[0]


def gate_mix_bwd(name, graw, y_mla, y_fox, dmix):
    def fn(g, ya, yb, dm):
        ga = jax.nn.sigmoid(g[:, :D_MODEL])
        gb = jax.nn.sigmoid(g[:, D_MODEL:])
        dgraw = jnp.concatenate([dm * ya * ga * (1.0 - ga), dm * yb * gb * (1.0 - gb)], axis=1)
        return (dgraw, dm * ga, dm * gb), ()
    return _rows(name, fn, [graw, y_mla, y_fox, dmix], [], [(GATE_W, BF16), (D_MODEL, BF16), (D_MODEL, BF16)], tr=128)


def loss_head(name, h2, target, gain):
    inv_d = 1.0 / D_MODEL

    def fn(h, t, g):
        r = _rstd(h)
        n = h * r
        err = n * g - t
        dy = err * inv_d
        dn = dy * g
        dh = r * (dn - n * jnp.mean(dn * n, axis=-1, keepdims=True))
        part = 0.5 * inv_d * jnp.sum(jnp.sum(err * err, axis=1, keepdims=True), axis=0, keepdims=True)
        return (dh,), (jnp.sum(dy * n, axis=0, keepdims=True), jnp.broadcast_to(part, (1, 128)))
    return _rows(name, fn, [h2, target], [gain], [(D_MODEL, F32)], [D_MODEL, 128])


def adamw(name, w, g, m, v, tr=256):
    c1 = 1.0 - ADAM_B1 ** ADAM_STEP
    c2 = 1.0 - ADAM_B2 ** ADAM_STEP

    def fn(wv, gv, mv, vv):
        m_new = ADAM_B1 * mv + (1.0 - ADAM_B1) * gv
        v_new = ADAM_B2 * vv + (1.0 - ADAM_B2) * (gv * gv)
        delta = -ADAM_LR * ((m_new / c1) / (jnp.sqrt(v_new / c2) + ADAM_EPS) + ADAM_WD * wv)
        return (delta, m_new, v_new), ()
    cols = w.shape[1]
    return _rows(name, fn, [w, g, m, v], [], [(cols, F32)] * 3, tr=tr)


def _row_tile(n_rows, cap=640):
    return max(t for t in range(16, cap + 1, 16) if n_rows % t == 0)


def add_pairs(name, a, b):
    def fn(av, bv):
        return (av.astype(F32) + bv.astype(F32),), ()
    return _rows(name, fn, [a, b], [], [(a.shape[1], BF16)], tr=_row_tile(a.shape[0]))[0]


def add_final(name, a, b, r0, r1, r2):
    def fn(av, bv, r0v, r1v, r2v):
        return (((av.astype(F32) + bv.astype(F32)) + r0v.astype(F32)) + r1v.astype(F32) + r2v.astype(F32),), ()
    return _rows(name, fn, [a, b, r0, r1, r2], [], [(a.shape[1], F32)], tr=_row_tile(a.shape[0]))[0]


TN_TILES = dict(tm=1024, tn=1024, tk=2048)
_DIMS = {"nn": (((1,), (0,)), ((), ())), "nt": (((1,), (1,)), ((), ())), "tn": (((0,), (0,)), ((), ()))}


def matmul(name, a, b, mode, outs, epilogue=None, extras=(), tm=1024, tn=1024, tk=2048):
    if mode == "tn":
        kdim, m = a.shape
    else:
        m, kdim = a.shape
    n = b.shape[0] if mode == "nt" else b.shape[1]
    tm, tn, tk = min(tm, m), min(tn, n), min(tk, kdim)
    assert m % tm == 0 and n % tn == 0 and kdim % tk == 0, (name, a.shape, b.shape)
    nk = kdim // tk
    n_ex, n_out = len(extras), len(outs)
    dims = _DIMS[mode]

    def body(a_ref, b_ref, *rest):
        ex_refs = rest[:n_ex]
        out_refs = rest[n_ex:n_ex + n_out]

        def finish(acc):
            vals = (acc,) if epilogue is None else epilogue(acc, *[r[...] for r in ex_refs])
            for r, v in zip(out_refs, vals):
                r[...] = v.astype(r.dtype)

        part = lax.dot_general(a_ref[...].astype(BF16), b_ref[...].astype(BF16), dims, preferred_element_type=F32)
        if nk == 1:
            finish(part)
        else:
            acc_ref = rest[-1]
            k = pl.program_id(2)

            @pl.when(k == 0)
            def _():
                acc_ref[...] = part

            @pl.when(k > 0)
            def _():
                acc_ref[...] += part

            @pl.when(k == nk - 1)
            def _():
                finish(acc_ref[...])

    a_spec = pl.BlockSpec((tk, tm), lambda i, j, k: (k, i)) if mode == "tn" else pl.BlockSpec((tm, tk), lambda i, j, k: (i, k))
    b_spec = pl.BlockSpec((tn, tk), lambda i, j, k: (j, k)) if mode == "nt" else pl.BlockSpec((tk, tn), lambda i, j, k: (k, j))
    tile = pl.BlockSpec((tm, tn), lambda i, j, k: (i, j))
    res = pl.pallas_call(
        body, name=name, grid=(m // tm, n // tn, nk),
        in_specs=[a_spec, b_spec] + [tile] * n_ex,
        out_specs=[tile] * n_out,
        out_shape=[jax.ShapeDtypeStruct((m, n), dt) for dt in outs],
        scratch_shapes=[pltpu.VMEM((tm, tn), F32)] if nk > 1 else [],
        compiler_params=_params(("parallel", "parallel", "arbitrary")),
    )(a, b, *extras)
    return res


_NT = (((1,), (1,)), ((), ()))
_NN = (((1,), (0,)), ((), ()))


def _mask(bq, chunk, transposed, row0=0, shape=None):
    shape = (bq, bq) if shape is None else shape
    row = lax.broadcasted_iota(jnp.int32, shape, 0) + row0
    col = lax.broadcasted_iota(jnp.int32, shape, 1)
    if chunk > 1:
        row, col = row // chunk, col // chunk
    return (row <= col) if transposed else (col <= row)


def _row_layout(a, bq):
    h, s, _ = a.shape
    return a.reshape(h, s // bq, 1, bq)


def _split_refs(refs, n_in, comm, n_out, n_scr):
    n_ci = len(comm.args) if comm else 0
    n_co = len(comm.out_shapes) if comm else 0
    cuts = [n_in, n_ci, n_out, n_co, n_scr, 3 if comm else 0]
    parts, at = [], 0
    for n in cuts:
        parts.append(list(refs[at:at + n]))
        at += n
    assert at == len(refs), (at, len(refs))
    return parts


def _comm_start(comm, c_in, c_out, c_sems, n0, n1):
    if comm is not None:
        @pl.when(jnp.logical_and(pl.program_id(0) == 0, pl.program_id(1) == 0))
        def _():
            comm.start(c_in, c_out, c_sems)


def _comm_wait(comm, c_in, c_out, c_sems, n0, n1):
    if comm is not None:
        @pl.when(jnp.logical_and(pl.program_id(0) == n0 - 1, pl.program_id(1) == n1 - 1))
        def _():
            comm.wait(c_in, c_out, c_sems)


def attn_fwd(name, q, k, v, offs, dqk, chunk, scale, cum=None, bq=512, comm=None):
    s_len = q.shape[0]
    nq = s_len // bq
    qoff, koff, voff = offs
    has_bias = cum is not None

    n_in = 5 if has_bias else 3

    def body(*refs):
        ins, c_in, outs, c_out, scr, c_sems = _split_refs(refs, n_in, comm, 2, 3)
        q_ref, k_ref, v_ref = ins[:3]
        if has_bias:
            cc_ref, cr_ref = ins[3:]
        o_ref, lse_ref = outs
        m_s, l_s, acc_s = scr
        _comm_start(comm, c_in, c_out, c_sems, HEADS, nq)
        i = pl.program_id(1)
        qv = q_ref[...]
        m_s[...] = jnp.full_like(m_s, NEG)
        l_s[...] = jnp.zeros_like(l_s)
        acc_s[...] = jnp.zeros_like(acc_s)

        def step(j, masked):
            off = pl.multiple_of(j * bq, bq)
            kj = k_ref[pl.ds(off, bq), :]
            vj = v_ref[pl.ds(off, bq), :]
            st = lax.dot_general(kj, qv, _NT, preferred_element_type=F32) * scale
            if has_bias:
                st = st + cr_ref[...] - cc_ref[pl.ds(off, bq), :]
            if masked:
                st = jnp.where(_mask(bq, chunk, True), st, NEG)
            m_prev = m_s[...]
            m_new = jnp.maximum(m_prev, jnp.max(st, axis=0, keepdims=True))
            alpha = jnp.exp(m_prev - m_new)
            pt = jnp.exp(st - m_new)
            l_s[...] = alpha * l_s[...] + jnp.sum(pt, axis=0, keepdims=True)
            acc_s[...] = alpha * acc_s[...] + lax.dot_general(vj, pt.astype(BF16), _DIMS["tn"],
                                                              preferred_element_type=F32)
            m_s[...] = m_new

        def pair_body(jj, carry):
            step(2 * jj, False)
            step(2 * jj + 1, False)
            return carry

        lax.fori_loop(0, i // 2, pair_body, 0)

        @pl.when(i % 2 == 1)
        def _():
            step(i - 1, False)

        step(i, True)
        o_ref[...] = (acc_s[...] / l_s[...]).T.astype(o_ref.dtype)
        lse_ref[...] = m_s[...] + jnp.log(l_s[...])
        _comm_wait(comm, c_in, c_out, c_sems, HEADS, nq)

    in_specs = [
        pl.BlockSpec((bq, dqk), lambda h, i: (i, qoff + h)),
        pl.BlockSpec((s_len, dqk), lambda h, i: (0, koff + h)),
        pl.BlockSpec((s_len, HEAD_V), lambda h, i: (0, voff + h)),
    ]
    args = [q, k, v]
    if has_bias:
        in_specs += [pl.BlockSpec((None, s_len, 1), lambda h, i: (h, 0, 0)),
                     pl.BlockSpec((None, None, 1, bq), lambda h, i: (h, i, 0, 0))]
        args += [cum, _row_layout(cum, bq)]
    c_ins, c_outs, c_shapes, c_scratch, aliases = _with_comm(comm, len(args), 2)
    o, lse_rows, *comm_out = pl.pallas_call(
        body, name=name, grid=(HEADS, nq), in_specs=in_specs + c_ins,
        out_specs=[pl.BlockSpec((bq, HEAD_V), lambda h, i: (i, h)),
                   pl.BlockSpec((None, None, 1, bq), lambda h, i: (h, i, 0, 0))] + c_outs,
        out_shape=[jax.ShapeDtypeStruct((s_len, HEADS * HEAD_V), F32),
                   jax.ShapeDtypeStruct((HEADS, nq, 1, bq), F32)] + c_shapes,
        scratch_shapes=[pltpu.VMEM((1, bq), F32), pltpu.VMEM((1, bq), F32), pltpu.VMEM((HEAD_V, bq), F32)] + c_scratch,
        input_output_aliases=aliases,
        compiler_params=_params(("arbitrary", "arbitrary")),
    )(*args, *(comm.args if comm else []))
    return (o, lse_rows.reshape(HEADS, s_len, 1), *comm_out)


def attn_dq(name, q, k, v, o, do, lse, offs, dqk, chunk, scale, out_dtype, cum=None, bq=512, comm=None):
    s_len = q.shape[0]
    nq = s_len // bq
    qoff, koff, voff = offs
    has_bias = cum is not None

    n_in, n_scr = (8, 3) if has_bias else (6, 1)

    def body(*refs):
        ins, c_in, outs, c_out, scr, c_sems = _split_refs(refs, n_in, comm, 2, n_scr)
        q_ref, k_ref, v_ref, o_ref, do_ref, lse_ref = ins[:6]
        dq_ref, delta_ref = outs
        acc_s = scr[0]
        if has_bias:
            cc_ref, cr_ref = ins[6:]
            pk_s, dl_s = scr[1:]
        _comm_start(comm, c_in, c_out, c_sems, HEADS, nq)
        i = pl.program_id(1)
        qv = q_ref[...]
        dov = do_ref[...].astype(F32)
        do_b = dov.astype(BF16)
        lse_v = lse_ref[...]
        acc_s[...] = jnp.zeros_like(acc_s)
        if has_bias:
            pk_s[...] = jnp.zeros_like(pk_s)
            dl_s[...] = jnp.zeros_like(dl_s)
        else:
            delta = jnp.sum(dov * o_ref[...].astype(F32), axis=-1, keepdims=True)
            delta_ref[...] = delta

        def step(j, masked):
            off = pl.multiple_of(j * bq, bq)
            kj = k_ref[pl.ds(off, bq), :]
            vj = v_ref[pl.ds(off, bq), :]
            s = lax.dot_general(qv, kj, _NT, preferred_element_type=F32) * scale
            if has_bias:
                s = s + cc_ref[...] - cr_ref[j]
            if masked:
                s = jnp.where(_mask(bq, chunk, False), s, NEG)
            p = jnp.exp(s - lse_v)
            dp = lax.dot_general(do_b, vj, _NT, preferred_element_type=F32)
            if has_bias:
                pd = p * dp
                dl_s[...] += jnp.sum(pd, axis=-1, keepdims=True)
                acc_s[...] += lax.dot_general(pd.astype(BF16), kj, _NN, preferred_element_type=F32)
                pk_s[...] += lax.dot_general(p.astype(BF16), kj, _NN, preferred_element_type=F32)
            else:
                ds = p * (dp - delta)
                acc_s[...] += lax.dot_general(ds.astype(BF16), kj, _NN, preferred_element_type=F32)

        def loop_body(j, carry):
            step(j, False)
            return carry

        lax.fori_loop(0, i, loop_body, 0)
        step(i, True)
        if has_bias:
            delta_ref[...] = dl_s[...]
            dq_ref[...] = ((acc_s[...] - dl_s[...] * pk_s[...]) * scale).astype(dq_ref.dtype)
        else:
            dq_ref[...] = (acc_s[...] * scale).astype(dq_ref.dtype)
        _comm_wait(comm, c_in, c_out, c_sems, HEADS, nq)

    in_specs = [
        pl.BlockSpec((bq, dqk), lambda h, i: (i, qoff + h)),
        pl.BlockSpec((s_len, dqk), lambda h, i: (0, koff + h)),
        pl.BlockSpec((s_len, HEAD_V), lambda h, i: (0, voff + h)),
        pl.BlockSpec((bq, HEAD_V), lambda h, i: (i, h)),
        pl.BlockSpec((bq, HEAD_V), lambda h, i: (i, h)),
        pl.BlockSpec((None, bq, 1), lambda h, i: (h, i, 0)),
    ]
    args = [q, k, v, o, do, lse]
    if has_bias:
        in_specs += [pl.BlockSpec((None, bq, 1), lambda h, i: (h, i, 0)),
                     pl.BlockSpec((None, nq, 1, bq), lambda h, i: (h, 0, 0, 0))]
        args += [cum, _row_layout(cum, bq)]
    c_ins, c_outs, c_shapes, c_scratch, aliases = _with_comm(comm, len(args), 2)
    return pl.pallas_call(
        body, name=name, grid=(HEADS, nq), in_specs=in_specs + c_ins,
        out_specs=[pl.BlockSpec((bq, dqk), lambda h, i: (i, h)),
                   pl.BlockSpec((None, bq, 1), lambda h, i: (h, i, 0))] + c_outs,
        out_shape=[jax.ShapeDtypeStruct((s_len, HEADS * dqk), out_dtype),
                   jax.ShapeDtypeStruct((HEADS, s_len, 1), F32)] + c_shapes,
        scratch_shapes=[pltpu.VMEM((bq, dqk), F32)] + (
            [pltpu.VMEM((bq, dqk), F32), pltpu.VMEM((bq, 1), F32)] if has_bias else []) + c_scratch,
        input_output_aliases=aliases,
        compiler_params=_params(("arbitrary", "arbitrary")),
    )(*args, *(comm.args if comm else []))


def attn_delta(name, do, o, bq=512):
    s_len = do.shape[0]
    nq = s_len // bq

    def body(do_ref, o_ref, out_ref):
        prod = do_ref[...].astype(F32) * o_ref[...].astype(F32)
        out_ref[...] = jnp.sum(prod.T, axis=0, keepdims=True)

    blk = pl.BlockSpec((bq, HEAD_V), lambda h, i: (i, h))
    return pl.pallas_call(
        body, name=name, grid=(HEADS, nq), in_specs=[blk, blk],
        out_specs=pl.BlockSpec((None, None, 1, bq), lambda h, i: (h, i, 0, 0)),
        out_shape=jax.ShapeDtypeStruct((HEADS, nq, 1, bq), F32),
        compiler_params=_params(("parallel", "parallel")),
    )(do, o)


def attn_bwd(name, q, k, v, do, lse, delta_rows, offs, dqk, chunk, scale, bq=512, comm=None):
    s_len = q.shape[0]
    nq = s_len // bq
    qoff, koff, voff = offs

    def body(*refs):
        ins, c_in, outs, c_out, scr, c_sems = _split_refs(refs, 6, comm, 3, 2)
        k_ref, v_ref, q_ref, do_ref, lse_ref, delta_ref = ins
        dk_ref, dv_ref, dq_ref = outs
        dk_s, dv_s = scr
        _comm_start(comm, c_in, c_out, c_sems, HEADS, nq)
        j = pl.program_id(1)
        kv = k_ref[...]
        vv = v_ref[...]
        dk_s[...] = jnp.zeros_like(dk_s)
        dv_s[...] = jnp.zeros_like(dv_s)

        @pl.when(j == 0)
        def _():
            dq_ref[...] = jnp.zeros_like(dq_ref)

        def step(i, masked):
            off = pl.multiple_of(i * bq, bq)
            qi = q_ref[pl.ds(off, bq), :]
            doi = do_ref[pl.ds(off, bq), :].astype(BF16)
            st = lax.dot_general(kv, qi, _NT, preferred_element_type=F32) * scale
            if masked:
                st = jnp.where(_mask(bq, chunk, True), st, NEG)
            pt = jnp.exp(st - lse_ref[i])
            dv_s[...] += lax.dot_general(pt.astype(BF16), doi, _NN, preferred_element_type=F32)
            dpt = lax.dot_general(vv, doi, _NT, preferred_element_type=F32)
            dst = (pt * (dpt - delta_ref[i])).astype(BF16)
            dk_s[...] += lax.dot_general(dst, qi, _NN, preferred_element_type=F32)
            dq_ref[pl.ds(off, bq), :] += lax.dot_general(dst, kv, _DIMS["tn"], preferred_element_type=F32) * scale

        step(j, True)

        def loop_body(i, carry):
            step(i, False)
            return carry

        lax.fori_loop(j + 1, nq, loop_body, 0)
        dk_ref[...] = (dk_s[...] * scale).astype(dk_ref.dtype)
        dv_ref[...] = dv_s[...].astype(dv_ref.dtype)
        _comm_wait(comm, c_in, c_out, c_sems, HEADS, nq)

    in_specs = [
        pl.BlockSpec((bq, dqk), lambda h, j: (j, koff + h)),
        pl.BlockSpec((bq, HEAD_V), lambda h, j: (j, voff + h)),
        pl.BlockSpec((s_len, dqk), lambda h, j: (0, qoff + h)),
        pl.BlockSpec((s_len, HEAD_V), lambda h, j: (0, h)),
        pl.BlockSpec((None, nq, 1, bq), lambda h, j: (h, 0, 0, 0)),
        pl.BlockSpec((None, nq, 1, bq), lambda h, j: (h, 0, 0, 0)),
    ]
    args = [k, v, q, do, _row_layout(lse, bq), delta_rows]
    c_ins, c_outs, c_shapes, c_scratch, aliases = _with_comm(comm, len(args), 3)
    return pl.pallas_call(
        body, name=name, grid=(HEADS, nq), in_specs=in_specs + c_ins,
        out_specs=[pl.BlockSpec((bq, dqk), lambda h, j: (j, h)), pl.BlockSpec((bq, HEAD_V), lambda h, j: (j, h)),
                   pl.BlockSpec((s_len, dqk), lambda h, j: (0, h))] + c_outs,
        out_shape=[jax.ShapeDtypeStruct((s_len, HEADS * dqk), F32), jax.ShapeDtypeStruct((s_len, HEADS * HEAD_V), BF16),
                   jax.ShapeDtypeStruct((s_len, HEADS * dqk), F32)] + c_shapes,
        scratch_shapes=[pltpu.VMEM((bq, dqk), F32), pltpu.VMEM((bq, HEAD_V), F32)] + c_scratch,
        input_output_aliases=aliases,
        compiler_params=_params(("arbitrary", "arbitrary")),
    )(*args, *(comm.args if comm else []))


def attn_dkv(name, q, k, v, do, lse, delta, offs, dqk, chunk, scale, dk_dtype, cum=None, bq=512, comm=None):
    s_len = q.shape[0]
    nq = s_len // bq
    qoff, koff, voff = offs
    has_bias = cum is not None

    n_in, n_out = (8, 3) if has_bias else (6, 2)

    def body(*refs):
        ins, c_in, outs, c_out, scr, c_sems = _split_refs(refs, n_in, comm, n_out, n_out)
        k_ref, v_ref, q_ref, do_ref, lse_ref, delta_ref = ins[:6]
        dk_ref, dv_ref = outs[:2]
        dk_s, dv_s = scr[:2]
        if has_bias:
            cc_ref, cr_ref = ins[6:]
            dc_ref, dc_s = outs[2], scr[2]
        _comm_start(comm, c_in, c_out, c_sems, HEADS, nq)
        j = pl.program_id(1)
        kv = k_ref[...]
        vv = v_ref[...]
        dk_s[...] = jnp.zeros_like(dk_s)
        dv_s[...] = jnp.zeros_like(dv_s)
        if has_bias:
            dc_s[...] = jnp.zeros_like(dc_s)

        def step(i, masked):
            off = pl.multiple_of(i * bq, bq)
            qi = q_ref[pl.ds(off, bq), :]
            doi = do_ref[pl.ds(off, bq), :].astype(BF16)
            st = lax.dot_general(kv, qi, _NT, preferred_element_type=F32) * scale
            if has_bias:
                st = st + cr_ref[i] - cc_ref[...]
            if masked:
                st = jnp.where(_mask(bq, chunk, True), st, NEG)
            pt = jnp.exp(st - lse_ref[i])
            dv_s[...] += lax.dot_general(pt.astype(BF16), doi, _NN, preferred_element_type=F32)
            dpt = lax.dot_general(vv, doi, _NT, preferred_element_type=F32)
            dst = pt * (dpt - delta_ref[i])
            dk_s[...] += lax.dot_general(dst.astype(BF16), qi, _NN, preferred_element_type=F32)
            if has_bias:
                dc_s[...] -= jnp.sum(dst, axis=-1, keepdims=True)

        step(j, True)

        def loop_body(i, carry):
            step(i, False)
            return carry

        lax.fori_loop(j + 1, nq, loop_body, 0)
        dk_ref[...] = (dk_s[...] * scale).astype(dk_ref.dtype)
        dv_ref[...] = dv_s[...].astype(dv_ref.dtype)
        if has_bias:
            dc_ref[...] = dc_s[...]
        _comm_wait(comm, c_in, c_out, c_sems, HEADS, nq)

    in_specs = [
        pl.BlockSpec((bq, dqk), lambda h, j: (j, koff + h)),
        pl.BlockSpec((bq, HEAD_V), lambda h, j: (j, voff + h)),
        pl.BlockSpec((s_len, dqk), lambda h, j: (0, qoff + h)),
        pl.BlockSpec((s_len, HEAD_V), lambda h, j: (0, h)),
        pl.BlockSpec((None, nq, 1, bq), lambda h, j: (h, 0, 0, 0)),
        pl.BlockSpec((None, nq, 1, bq), lambda h, j: (h, 0, 0, 0)),
    ]
    args = [k, v, q, do, _row_layout(lse, bq), _row_layout(delta, bq)]
    out_specs = [pl.BlockSpec((bq, dqk), lambda h, j: (j, h)), pl.BlockSpec((bq, HEAD_V), lambda h, j: (j, h))]
    out_shape = [jax.ShapeDtypeStruct((s_len, HEADS * dqk), dk_dtype), jax.ShapeDtypeStruct((s_len, HEADS * HEAD_V), BF16)]
    scratch = [pltpu.VMEM((bq, dqk), F32), pltpu.VMEM((bq, HEAD_V), F32)]
    if has_bias:
        in_specs += [pl.BlockSpec((None, bq, 1), lambda h, j: (h, j, 0)),
                     pl.BlockSpec((None, nq, 1, bq), lambda h, j: (h, 0, 0, 0))]
        args += [cum, _row_layout(cum, bq)]
        out_specs.append(pl.BlockSpec((None, bq, 1), lambda h, j: (h, j, 0)))
        out_shape.append(jax.ShapeDtypeStruct((HEADS, s_len, 1), F32))
        scratch.append(pltpu.VMEM((bq, 1), F32))
    c_ins, c_outs, c_shapes, c_scratch, aliases = _with_comm(comm, len(args), n_out)
    return pl.pallas_call(
        body, name=name, grid=(HEADS, nq), in_specs=in_specs + c_ins, out_specs=out_specs + c_outs,
        out_shape=out_shape + c_shapes, scratch_shapes=scratch + c_scratch, input_output_aliases=aliases,
        compiler_params=_params(("arbitrary", "arbitrary")),
    )(*args, *(comm.args if comm else []))


_CUM_BLK = 512


def _split3(x):
    hi = x.astype(BF16)
    r1 = x - hi.astype(F32)
    mid = r1.astype(BF16)
    lo = (r1 - mid.astype(F32)).astype(BF16)
    return hi, mid, lo


def _tri_dot(x, tri):
    hi, mid, lo = _split3(x)
    out = lax.dot_general(lo, tri, _NN, preferred_element_type=F32)
    out = out + lax.dot_general(mid, tri, _NN, preferred_element_type=F32)
    return out + lax.dot_general(hi, tri, _NN, preferred_element_type=F32)


def fox_cum_fwd(name, ff_t, bias):
    s_len = ff_t.shape[1]
    nb = s_len // _CUM_BLK

    def body(ff_ref, b_ref, cum_ref):
        row = lax.broadcasted_iota(jnp.int32, (_CUM_BLK, _CUM_BLK), 0)
        col = lax.broadcasted_iota(jnp.int32, (_CUM_BLK, _CUM_BLK), 1)
        tri = (row <= col).astype(BF16)
        carry = jnp.zeros((HEADS, 1), F32)
        for b in range(nb):
            z = ff_ref[:, b * _CUM_BLK:(b + 1) * _CUM_BLK] + b_ref[...]
            logf = jnp.minimum(z, 0.0) - jnp.log1p(jnp.exp(-jnp.abs(z)))
            blk = _tri_dot(logf, tri) + carry
            cum_ref[:, b * _CUM_BLK:(b + 1) * _CUM_BLK] = blk
            carry = blk[:, _CUM_BLK - 1:_CUM_BLK]

    return pl.pallas_call(
        body, name=name, out_shape=jax.ShapeDtypeStruct((HEADS, s_len), F32),
        compiler_params=pltpu.CompilerParams(vmem_limit_bytes=VMEM_LIMIT),
    )(ff_t, bias)


def fox_cum_bwd(name, ff_t, bias, dcum):
    s_len = ff_t.shape[1]
    nb = s_len // _CUM_BLK

    def body(ff_ref, b_ref, dc_ref, dff_ref, db_ref):
        row = lax.broadcasted_iota(jnp.int32, (_CUM_BLK, _CUM_BLK), 0)
        col = lax.broadcasted_iota(jnp.int32, (_CUM_BLK, _CUM_BLK), 1)
        tri = (row >= col).astype(BF16)
        carry = jnp.zeros((HEADS, 1), F32)
        dbias = jnp.zeros((HEADS, 1), F32)
        for b in reversed(range(nb)):
            sl = slice(b * _CUM_BLK, (b + 1) * _CUM_BLK)
            dlogf = _tri_dot(dc_ref[:, sl], tri) + carry
            carry = dlogf[:, 0:1]
            z = ff_ref[:, sl] + b_ref[...]
            dz = dlogf / (1.0 + jnp.exp(z))
            dff_ref[:, sl] = dz
            dbias = dbias + jnp.sum(dz, axis=-1, keepdims=True)
        db_ref[...] = dbias

    return pl.pallas_call(
        body, name=name,
        out_shape=[jax.ShapeDtypeStruct((HEADS, s_len), F32), jax.ShapeDtypeStruct((HEADS, 1), F32)],
        compiler_params=pltpu.CompilerParams(vmem_limit_bytes=VMEM_LIMIT),
    )(ff_t, bias, dcum)


_ANY = pl.BlockSpec(memory_space=pl.ANY)


def _me():
    return lax.axis_index("x"), lax.axis_index("y"), lax.axis_index("c")


def comm_allgather(name, mine):
    n_rows, n_cols = mine.shape

    def body(x_ref, out_ref, send_sems, recv_sems, local_sem):
        x, y, c = _me()
        sibling = (x, y, 1 - c)
        chips = [(1 - x, y), (x, 1 - y), (1 - x, 1 - y)]

        def blk(px, py, pc):
            return out_ref.at[pc * 4 + px * 2 + py]

        def copy(k, block, to, src=None):
            return pltpu.make_async_remote_copy(
                src_ref=blk(*block) if src is None else src, dst_ref=blk(*block),
                send_sem=send_sems.at[k], recv_sem=recv_sems.at[k], device_id=to, device_id_type=MESH)

        own = pltpu.make_async_copy(x_ref, blk(x, y, c), local_sem)
        own.start()
        first = [copy(0, (x, y, c), sibling, src=x_ref)]
        first += [copy(1 + j, (x, y, c), (*chip, c), src=x_ref) for j, chip in enumerate(chips)]
        for cp in first:
            cp.start()
        passed = [copy(4 + j, (*chip, c), sibling) for j, chip in enumerate(chips)]
        for j, chip in enumerate(chips):
            copy(1 + j, (*chip, c), (x, y, c)).wait_recv()
            passed[j].start()
        copy(0, sibling, (x, y, c)).wait_recv()
        for j, chip in enumerate(chips):
            copy(4 + j, (*chip, 1 - c), (x, y, c)).wait_recv()
        for cp in first + passed:
            cp.wait_send()
        own.wait()

    return pl.pallas_call(
        body, name=name, out_shape=jax.ShapeDtypeStruct((N_DEV, n_rows, n_cols), mine.dtype),
        in_specs=[_ANY], out_specs=_ANY,
        scratch_shapes=[pltpu.SemaphoreType.DMA((7,)), pltpu.SemaphoreType.DMA((7,)), pltpu.SemaphoreType.DMA],
    )(mine)


def comm_swap_sibling(name, parts):
    _, n_rows, n_cols = parts.shape

    def body(p_ref, got_ref, send_sem, recv_sem):
        x, y, c = _me()
        cp = pltpu.make_async_remote_copy(
            src_ref=p_ref.at[pl.ds((1 - c) * 4, 4)], dst_ref=got_ref, send_sem=send_sem, recv_sem=recv_sem,
            device_id=(x, y, 1 - c), device_id_type=MESH)
        cp.start()
        cp.wait()

    return pl.pallas_call(
        body, name=name, out_shape=jax.ShapeDtypeStruct((4, n_rows, n_cols), parts.dtype),
        in_specs=[_ANY], out_specs=_ANY,
        scratch_shapes=[pltpu.SemaphoreType.DMA, pltpu.SemaphoreType.DMA],
    )(parts)


def comm_swap_chips(name, parts):
    _, n_rows, n_cols = parts.shape

    def body(p_ref, got_ref, send_sems, recv_sems):
        x, y, c = _me()
        chips = [(1 - x, y), (x, 1 - y), (1 - x, 1 - y)]
        cps = [pltpu.make_async_remote_copy(
            src_ref=p_ref.at[2 * px + py], dst_ref=got_ref.at[k], send_sem=send_sems.at[k], recv_sem=recv_sems.at[k],
            device_id=(px, py, c), device_id_type=MESH) for k, (px, py) in enumerate(chips)]
        for cp in cps:
            cp.start()
        for cp in cps:
            cp.wait()

    return pl.pallas_call(
        body, name=name, out_shape=jax.ShapeDtypeStruct((3, n_rows, n_cols), parts.dtype),
        in_specs=[_ANY], out_specs=_ANY,
        scratch_shapes=[pltpu.SemaphoreType.DMA((3,)), pltpu.SemaphoreType.DMA((3,))],
    )(parts)


class CommHook:
    def __init__(self, args, out_shapes, n_copies, copies, aliases=None):
        self.args, self.out_shapes, self.n_copies, self.copies = list(args), list(out_shapes), n_copies, copies
        self.aliases = aliases or {}

    def scratch(self):
        return [pltpu.SemaphoreType.DMA((self.n_copies,)), pltpu.SemaphoreType.DMA((self.n_copies,)),
                pltpu.SemaphoreType.DMA((1,))]

    def start(self, in_refs, out_refs, sems):
        sends, _, locs = self.copies(in_refs, out_refs, *sems)
        for cp in locs() + sends():
            cp.start()

    def wait(self, in_refs, out_refs, sems):
        sends, recvs, locs = self.copies(in_refs, out_refs, *sems)
        for cp in sends():
            cp.wait_send()
        for cp in recvs():
            cp.wait_recv()
        for cp in locs():
            cp.wait()


def _remote(src, dst, send_sem, recv_sem, to):
    return pltpu.make_async_remote_copy(src_ref=src, dst_ref=dst, send_sem=send_sem, recv_sem=recv_sem,
                                        device_id=to, device_id_type=MESH)


def hook_gather_first(mine):
    n_rows, n_cols = mine.shape

    def copies(ins, outs, send, recv, local):
        (x_ref,), (out_ref,) = ins, outs
        x, y, c = _me()
        me = c * 4 + x * 2 + y
        peers = [(x, y, 1 - c), (1 - x, y, c), (x, 1 - y, c), (1 - x, 1 - y, c)]

        def sends():
            return [_remote(x_ref, out_ref.at[me], send.at[k], recv.at[k], p) for k, p in enumerate(peers)]

        def recvs():
            return [_remote(x_ref, out_ref.at[pc * 4 + px * 2 + py], send.at[k], recv.at[k], (px, py, pc))
                    for k, (px, py, pc) in enumerate(peers)]

        return sends, recvs, lambda: [pltpu.make_async_copy(x_ref, out_ref.at[me], local.at[0])]

    return CommHook([mine], [jax.ShapeDtypeStruct((N_DEV, n_rows, n_cols), mine.dtype)], 4, copies)


def hook_gather_second(gathered):
    def copies(ins, outs, send, recv, local):
        (g_in,), (g_out,) = ins, outs
        x, y, c = _me()
        chips = [(1 - x, y), (x, 1 - y), (1 - x, 1 - y)]

        def sends():
            return [_remote(g_in.at[c * 4 + px * 2 + py], g_out.at[c * 4 + px * 2 + py], send.at[k], recv.at[k],
                            (x, y, 1 - c)) for k, (px, py) in enumerate(chips)]

        def recvs():
            return [_remote(g_in.at[(1 - c) * 4 + px * 2 + py], g_out.at[(1 - c) * 4 + px * 2 + py], send.at[k],
                            recv.at[k], (x, y, 1 - c)) for k, (px, py) in enumerate(chips)]

        return sends, recvs, lambda: []

    return CommHook([gathered], [jax.ShapeDtypeStruct(gathered.shape, gathered.dtype)], 3, copies, aliases={0: 0})


def hook_swap_sibling(parts):
    _, n_rows, n_cols = parts.shape

    def copies(ins, outs, send, recv, local):
        (p_ref,), (got_ref,) = ins, outs
        x, y, c = _me()

        def swap():
            return [_remote(p_ref.at[pl.ds((1 - c) * 4, 4)], got_ref, send.at[0], recv.at[0], (x, y, 1 - c))]

        return swap, swap, lambda: []

    return CommHook([parts], [jax.ShapeDtypeStruct((4, n_rows, n_cols), parts.dtype)], 1, copies)


def hook_swap_chips(parts):
    _, n_rows, n_cols = parts.shape

    def copies(ins, outs, send, recv, local):
        (p_ref,), (got_ref,) = ins, outs
        x, y, c = _me()
        chips = [(1 - x, y), (x, 1 - y), (1 - x, 1 - y)]

        def swaps():
            return [_remote(p_ref.at[2 * px + py], got_ref.at[k], send.at[k], recv.at[k], (px, py, c))
                    for k, (px, py) in enumerate(chips)]

        return swaps, swaps, lambda: []

    return CommHook([parts], [jax.ShapeDtypeStruct((3, n_rows, n_cols), parts.dtype)], 3, copies)


def _with_comm(comm, n_args, n_outs):
    if comm is None:
        return [], [], [], [], {}
    aliases = {n_args + a: n_outs + o for a, o in comm.aliases.items()}
    return [_ANY] * len(comm.args), [_ANY] * len(comm.out_shapes), comm.out_shapes, comm.scratch(), aliases


def comm_allreduce_small(name, mine):
    shape = mine.shape

    def body(x_ref, out_ref, buf, send_sems, recv_sems):
        x, y, c = _me()
        my_slot = c * 4 + x * 2 + y
        buf[my_slot] = x_ref[...]
        cps = []
        for k in range(1, N_DEV):
            dx, dy, dc = (k >> 2) & 1, (k >> 1) & 1, k & 1
            px, py, pc = x ^ dx, y ^ dy, c ^ dc
            send = pltpu.make_async_remote_copy(
                src_ref=x_ref, dst_ref=buf.at[my_slot], send_sem=send_sems.at[k - 1], recv_sem=recv_sems.at[k - 1],
                device_id=(px, py, pc), device_id_type=MESH)
            send.start()
            recv = pltpu.make_async_remote_copy(
                src_ref=x_ref, dst_ref=buf.at[pc * 4 + px * 2 + py], send_sem=send_sems.at[k - 1],
                recv_sem=recv_sems.at[k - 1], device_id=(px, py, pc), device_id_type=MESH)
            cps.append((send, recv))
        for send, recv in cps:
            send.wait_send()
            recv.wait_recv()
        total = buf[0]
        for s in range(1, N_DEV):
            total = total + buf[s]
        out_ref[...] = total

    vmem = pl.BlockSpec(memory_space=pltpu.VMEM)
    return pl.pallas_call(
        body, name=name, out_shape=jax.ShapeDtypeStruct(shape, F32), in_specs=[vmem], out_specs=vmem,
        scratch_shapes=[pltpu.VMEM((N_DEV,) + shape, F32), pltpu.SemaphoreType.DMA((7,)), pltpu.SemaphoreType.DMA((7,))],
    )(mine)


def pack_local(shards, names):
    flat = [shards[n].reshape(-1, PACK_C) for n in names]
    rows = sum(f.shape[0] for f in flat)
    pad = (-rows) % 128
    return jnp.concatenate(flat + [jnp.zeros((pad, PACK_C), flat[0].dtype)], axis=0)


def unpack_group(gathered, names, shard_shapes):
    _, n_rows, _ = gathered.shape
    by_block = gathered.reshape(2, 4, n_rows, PACK_C).transpose(1, 0, 2, 3).reshape(N_DEV, n_rows, PACK_C)
    full = {}
    r0 = 0
    for name in names:
        rs, cs = shard_shapes[name]
        nr = rs * cs // PACK_C
        piece = by_block[:, r0:r0 + nr, :].reshape(N_DEV, rs, cs)
        r0 += nr
        if name in ROW_SHARDED:
            full[name] = piece.reshape(N_DEV * rs, cs)
        else:
            full[name] = piece.transpose(1, 0, 2).reshape(rs, N_DEV * cs)
    return full


def unpack_a(gathered, shard_shapes):
    full = unpack_group(gathered, GROUP_A, shard_shapes)
    w_in = full.pop("w_in")
    zeros = jnp.zeros((D_MODEL, LAT_W - OFF_FQ - HEADS), w_in.dtype)
    full["w_lat"] = jnp.concatenate([w_in[:, :OFF_FQ], w_in[:, OFF_FF:OFF_G], zeros], axis=1)
    full["w_fox"] = w_in[:, OFF_FQ:OFF_FF]
    full["w_gate"] = w_in[:, OFF_G:]
    w_uq = full.pop("w_uq").reshape(Q_LORA, HEADS, NOPE + ROPE)
    full["w_uq"] = jnp.pad(w_uq, ((0, 0), (0, 0), (0, QK_PAD - NOPE - ROPE))).reshape(Q_LORA, HEADS * QK_PAD)
    w_ukv = full.pop("w_ukv").reshape(KV_LORA, HEADS, 2, NOPE)
    full["w_kv"] = jnp.concatenate([w_ukv[:, :, 0, :].reshape(KV_LORA, HEADS * NOPE),
                                    w_ukv[:, :, 1, :].reshape(KV_LORA, HEADS * HEAD_V)], axis=1)
    return full


def pack_small(vals, loss):
    flat = [vals[n].reshape(-1) for n in SMALL] + [loss.reshape(-1)]
    used = sum(f.shape[0] for f in flat)
    flat.append(jnp.zeros((SMALL_ROWS * PACK_C - used,), F32))
    return jnp.concatenate(flat).reshape(SMALL_ROWS, PACK_C)


def unpack_small(packed):
    flat = packed.reshape(-1)
    out, off = {}, 0
    for n in SMALL:
        out[n] = flat[off:off + SMALL_N[n]]
        off += SMALL_N[n]
    return out, flat[off]


def rope_tables(s_len):
    pos = jnp.arange(s_len, dtype=F32)
    inv = 1.0 / (ROPE_THETA ** (jnp.arange(0, ROPE, 2, dtype=F32) / ROPE))
    ang = pos[:, None] * inv[None, :]
    cos, sin = jnp.cos(ang), jnp.sin(ang)
    zero = jnp.zeros_like(cos)
    c = jnp.concatenate([cos, cos, zero, zero], axis=1)
    s1 = jnp.concatenate([-sin, zero, zero, zero], axis=1)
    s2 = jnp.concatenate([zero, sin, zero, zero], axis=1)
    return (c, s1, s2), (c, -s1, -s2)


def reduce_scatter_tail(parts, from_sibling, from_chips_fn, names):
    cx, cy, cc = _me()
    n_rows = parts.shape[1]
    mine4 = lax.dynamic_slice_in_dim(parts, cc * 4, 4, axis=0)
    pair = add_pairs("rs_pair_sum_" + names, mine4.reshape(4 * n_rows, PACK_C), from_sibling.reshape(4 * n_rows, PACK_C))
    from_chips, extra = from_chips_fn(pair.reshape(4, n_rows, PACK_C))
    own = cx * 2 + cy
    total = add_final("rs_final_sum_" + names, lax.dynamic_index_in_dim(mine4, own, 0, keepdims=False),
                      lax.dynamic_index_in_dim(from_sibling, own, 0, keepdims=False),
                      from_chips[0], from_chips[1], from_chips[2])
    return total, extra


def local_step(x, target, w, small, packed_b, shard_shapes):
    s_len = x.shape[0]
    tabs, inv_tabs = rope_tables(s_len)
    g_attn = small["attn_norm"].reshape(1, D_MODEL)
    g_q = small["q_norm"].reshape(1, Q_LORA)
    g_kv = small["kv_norm"].reshape(1, KV_LORA)
    g_mlp = small["mlp_norm"].reshape(1, D_MODEL)
    g_final = small["final_norm"].reshape(1, D_MODEL)
    f_bias = small["fox_f_bias"].reshape(HEADS, 1)
    mla_scale = 1.0 / math.sqrt(NOPE + ROPE)
    fox_scale = 1.0 / math.sqrt(HEAD_V)
    mla_offs = (0, 0, HEADS)
    fox_offs = (0, HEADS, 2 * HEADS)

    xn = rms_fwd("rms_attn", x, g_attn)
    lat, = matmul("proj_lat", xn, w["w_lat"], "nn", [F32])
    fox, = matmul("proj_fox", xn, w["w_fox"], "nn", [BF16])
    graw, = matmul("proj_gate", xn, w["w_gate"], "nn", [F32])
    cq = rms_fwd("rms_q", (lat, Q_LORA, 0), g_q)
    ckv = rms_fwd("rms_kv", (lat, KV_LORA, Q_LORA // KV_LORA), g_kv)
    qraw, = matmul("up_q", cq, w["w_uq"], "nn", [F32])
    q = rope_heads("rope_q", qraw, tabs, BF16)
    kvn, = matmul("up_kv", ckv, w["w_kv"], "nn", [BF16])
    kr = rope_block("rope_k", lat, OFF_KR // 128, tabs, BF16)
    k = k_assemble("k_assemble", kvn, kr)
    o_mla, lse_mla, gathered_b = attn_fwd("mla_fwd", q, k, kvn, mla_offs, QK_PAD, CHUNK, mla_scale,
                                          comm=hook_gather_first(packed_b))
    ff_t = lat[:, OFF_FQ:OFF_FQ + HEADS].T
    cum = fox_cum_fwd("fox_cum", ff_t, f_bias).reshape(HEADS, s_len, 1)
    o_fox, lse_fox, gathered_b = attn_fwd("fox_fwd", fox, fox, fox, fox_offs, HEAD_V, 1, fox_scale, cum=cum,
                                          comm=hook_gather_second(gathered_b))
    unpack_b = functools.partial(unpack_group, names=GROUP_B, shard_shapes=shard_shapes)
    w = {**w, **unpack_b(gathered_b)}
    y_mla, = matmul("branch_mla", o_mla, w["w_mla_branch"], "nn", [F32])
    y_fox, = matmul("branch_fox", o_fox, w["w_fox_branch"], "nn", [F32])
    mix = gate_mix("gate_mix", graw, y_mla, y_fox)
    h1, = matmul("out_proj", mix, w["w_out"], "nn", [F32], epilogue=lambda acc, res: (res + acc,), extras=[x])
    hn = rms_fwd("rms_mlp", h1, g_mlp)

    def relu2(acc):
        r = jnp.maximum(acc, 0.0)
        return r * r, r
    u, relu_up = matmul("mlp_up", hn, w["w_up"], "nn", [BF16, BF16], epilogue=relu2)
    h2, = matmul("mlp_down", u, w["w_down"], "nn", [F32], epilogue=lambda acc, res: (res + acc,), extras=[h1])
    dh2, d_final, loss = loss_head("loss_head", h2, target, g_final)

    grads = {}
    dup, = matmul("d_mlp_down", dh2, w["w_down"], "nt", [BF16],
                  epilogue=lambda acc, r: (acc * (2.0 * r.astype(F32)),), extras=[relu_up])
    grads["w_down"], = matmul("gw_down", u, dh2, "tn", [BF16], **TN_TILES)
    dhn, = matmul("d_mlp_up", dup, w["w_up"], "nt", [F32])
    grads["w_up"], = matmul("gw_up", hn, dup, "tn", [BF16], **TN_TILES)
    dh1, d_mlp = rms_bwd("rms_mlp_bwd", h1, dhn, g_mlp, dres=dh2)
    dmix, = matmul("d_out_proj", dh1, w["w_out"], "nt", [F32])
    grads["w_out"], = matmul("gw_out", mix, dh1, "tn", [BF16], **TN_TILES)
    dgraw, dy_mla, dy_fox = gate_mix_bwd("gate_mix_bwd", graw, y_mla, y_fox, dmix)
    do_mla, = matmul("d_branch_mla", dy_mla, w["w_mla_branch"], "nt", [F32])
    grads["w_mla_branch"], = matmul("gw_branch_mla", o_mla, dy_mla, "tn", [BF16], **TN_TILES)
    do_fox, = matmul("d_branch_fox", dy_fox, w["w_fox_branch"], "nt", [F32])
    grads["w_fox_branch"], = matmul("gw_branch_fox", o_fox, dy_fox, "tn", [BF16], **TN_TILES)

    to_packed_b = jax.linear_transpose(unpack_b, jax.ShapeDtypeStruct(gathered_b.shape, BF16))
    parts_b, = to_packed_b({n: grads.pop(n) for n in GROUP_B})
    delta_mla = attn_delta("mla_delta", do_mla, o_mla)
    dk, dv, dq, from_sibling = attn_bwd("mla_bwd", q, k, kvn, do_mla, lse_mla, delta_mla, mla_offs, QK_PAD, CHUNK,
                                        mla_scale, comm=hook_swap_sibling(parts_b))

    def chips_behind_fox_dq(pair):
        dfq, delta_fox, from_chips = attn_dq("fox_dq", fox, fox, fox, o_fox, do_fox, lse_fox, fox_offs, HEAD_V, 1,
                                             fox_scale, BF16, cum=cum, comm=hook_swap_chips(pair))
        return from_chips, (dfq, delta_fox)
    g_packed_b, (dfq, delta_fox) = reduce_scatter_tail(parts_b, from_sibling, chips_behind_fox_dq, "b")
    dq_r =rope_heads("rope_q_bwd", dq, inv_tabs, BF16)
    dkvn, dkr = dk_split("dk_split", dk, dv, inv_tabs)
    dcq, = matmul("d_up_q", dq_r, w["w_uq"], "nt", [F32])
    grads["w_uq"], = matmul("gw_uq", cq, dq_r, "tn", [BF16], tm=512, tn=2048, tk=2048)
    dckv, = matmul("d_up_kv", dkvn, w["w_kv"], "nt", [F32])
    grads["w_kv"], = matmul("gw_kv", ckv, dkvn, "tn", [BF16], tm=256, tn=2048, tk=2048)
    dcq_raw, d_qn = rms_bwd("rms_q_bwd", (lat, Q_LORA, 0), dcq, g_q, out_dtype=BF16)
    dckv_raw, d_kvn = rms_bwd("rms_kv_bwd", (lat, KV_LORA, Q_LORA // KV_LORA), dckv, g_kv, out_dtype=BF16)

    dfk, dfv, dcum = attn_dkv("fox_dkv", fox, fox, fox, do_fox, lse_fox, delta_fox, fox_offs, HEAD_V, 1, fox_scale, BF16,
                              cum=cum)
    dff_t, d_bias = fox_cum_bwd("fox_cum_bwd", ff_t, f_bias, dcum.reshape(HEADS, s_len))

    pad = jnp.zeros((s_len, LAT_W - OFF_FQ - HEADS), BF16)
    dproj = jnp.concatenate([dcq_raw, dckv_raw, dkr[:, :ROPE].astype(BF16), dff_t.T.astype(BF16), pad,
                             dfq, dfk, dfv, dgraw], axis=1)
    w_in_p = jnp.concatenate([w["w_lat"], w["w_fox"], w["w_gate"]], axis=1)
    dxn, = matmul("d_proj", dproj, w_in_p, "nt", [F32], tk=2688)
    gw_in, = matmul("gw_in", xn, dproj, "tn", [BF16], tm=1024, tn=1152, tk=2048)
    grads["w_lat"], grads["w_fox"], grads["w_gate"] = (gw_in[:, :LAT_W], gw_in[:, LAT_W:LAT_W + FOX_W],
                                                      gw_in[:, LAT_W + FOX_W:])
    dx, d_attn = rms_bwd("rms_attn_bwd", x, dxn, g_attn, dres=dh1)

    small_grads = {"attn_norm": d_attn, "fox_f_bias": d_bias, "q_norm": d_qn, "kv_norm": d_kvn,
                   "mlp_norm": d_mlp, "final_norm": d_final}
    return loss, dx, grads, g_packed_b, small_grads


def kernel(x, attn_norm, w_in, fox_f_bias, q_norm, w_uq, kv_norm, w_ukv, w_mla_branch, w_fox_branch, w_out, mlp_norm, w_up, w_down, final_norm, loss_target, m_attn_norm, m_w_in, m_fox_f_bias, m_q_norm, m_w_uq, m_kv_norm, m_w_ukv, m_w_mla_branch, m_w_fox_branch, m_w_out, m_mlp_norm, m_w_up, m_w_down, m_final_norm, v_attn_norm, v_w_in, v_fox_f_bias, v_q_norm, v_w_uq, v_kv_norm, v_w_ukv, v_w_mla_branch, v_w_fox_branch, v_w_out, v_mlp_norm, v_w_up, v_w_down, v_final_norm):
    given = dict(locals())
    big = {n: given[n][0] for n in BIG}
    small = {n: given[n] for n in SMALL}
    shard_shapes = {n: tuple(big[n].shape) for n in BIG}

    packed_a = pack_local({n: big[n].astype(BF16) for n in GROUP_A}, GROUP_A)
    packed_b = pack_local({n: big[n].astype(BF16) for n in GROUP_B}, GROUP_B)
    gathered_a = comm_allgather("comm_allgather_a", packed_a)
    unpack = functools.partial(unpack_a, shard_shapes=shard_shapes)
    w_a = unpack(gathered_a)

    loss_part, dx, grads_a, g_packed_b, small_grads = local_step(x[0], loss_target[0], w_a, small, packed_b,
                                                                 shard_shapes)

    to_packed_a = jax.linear_transpose(unpack, jax.ShapeDtypeStruct(gathered_a.shape, BF16))
    parts_a, = to_packed_a({n: grads_a[n] for n in w_a})
    from_sibling = comm_swap_sibling("comm_rs_sibling_a", parts_a)
    g_packed_a, _ = reduce_scatter_tail(parts_a, from_sibling,
                                        lambda pair: (comm_swap_chips("comm_rs_chips_a", pair), None), "a")

    small_sum = comm_allreduce_small("comm_allreduce_small", pack_small(small_grads, loss_part[0, 0]))
    g_small, loss = unpack_small(small_sum)

    grad_w, delta_w, new_m, new_v = {}, {}, {}, {}
    for names, g_packed in ((GROUP_A, g_packed_a), (GROUP_B, g_packed_b)):
        r0 = 0
        for n in names:
            rs, cs = shard_shapes[n]
            nr = rs * cs // PACK_C
            g = g_packed[r0:r0 + nr].reshape(rs, cs)
            r0 += nr
            d, m_new, v_new = adamw("adamw_" + n, big[n], g, given["m_" + n][0], given["v_" + n][0])
            grad_w[n], delta_w[n], new_m[n], new_v[n] = g[None], d[None], m_new[None], v_new[None]
    zero = jnp.zeros((), F32)
    d_s, m_s, v_s = adamw("adamw_small", pack_small(small, zero), small_sum * _small_mask(),
                          pack_small({n: given["m_" + n] for n in SMALL}, zero),
                          pack_small({n: given["v_" + n] for n in SMALL}, zero), tr=SMALL_ROWS)
    d_small, _ = unpack_small(d_s)
    m_small, _ = unpack_small(m_s)
    v_small, _ = unpack_small(v_s)
    for n in SMALL:
        shape = given[n].shape
        grad_w[n], delta_w[n] = g_small[n].reshape(shape), d_small[n].reshape(shape)
        new_m[n], new_v[n] = m_small[n].reshape(shape), v_small[n].reshape(shape)

    order = ["attn_norm", "w_in", "fox_f_bias", "q_norm", "w_uq", "kv_norm", "w_ukv", "w_mla_branch", "w_fox_branch",
             "w_out", "mlp_norm", "w_up", "w_down", "final_norm"]
    return (loss, dx[None], *[grad_w[n] for n in order], *[delta_w[n] for n in order],
            *[new_m[n] for n in order], *[new_v[n] for n in order])


def _small_mask():
    used = sum(SMALL_N[n] for n in SMALL)
    return (jnp.arange(SMALL_ROWS * PACK_C) < used).astype(F32).reshape(SMALL_ROWS, PACK_C)
```

```python
import functools
import math

import jax
import jax.numpy as jnp
from jax import lax
from jax.experimental import pallas as pl
from jax.experimental.pallas import tpu as pltpu

F32 = jnp.float32
BF16 = jnp.bfloat16
MESH = pl.DeviceIdType.MESH

D_MODEL = 2048
HEADS = 8
Q_LORA = 512
KV_LORA = 256
NOPE = 128
ROPE = 64
HEAD_V = 128
D_FF = 4 * D_MODEL
CHUNK = 64
EPS = 1e-6
ROPE_THETA = 10000.0
OFF_KR = Q_LORA + KV_LORA
OFF_FQ = OFF_KR + ROPE
OFF_FF = OFF_FQ + 3 * HEADS * HEAD_V
OFF_G = OFF_FF + HEADS
D_IN = OFF_G + 2 * D_MODEL

LAT_W = 896
FOX_W = 3 * HEADS * HEAD_V
GATE_W = 2 * D_MODEL
PROJ_W = LAT_W + FOX_W + GATE_W
QK_PAD = 256

ADAM_LR = 0.001
ADAM_B1 = 0.9
ADAM_B2 = 0.999
ADAM_EPS = 1e-08
ADAM_WD = 0.01
ADAM_STEP = 10

N_DEV = 8
PACK_C = 1024
NEG = -1e30
LOG2E = math.log2(math.e)

VMEM_LIMIT = 56 * 1024 * 1024

BIG = ("w_in", "w_uq", "w_ukv", "w_mla_branch", "w_fox_branch", "w_out", "w_up", "w_down")
GROUP_A = ("w_in", "w_uq", "w_ukv")
GROUP_B = ("w_mla_branch", "w_fox_branch", "w_out", "w_up", "w_down")
ROW_SHARDED = ("w_out", "w_down")
SMALL = ("attn_norm", "fox_f_bias", "q_norm", "kv_norm", "mlp_norm", "final_norm")
SMALL_N = {"attn_norm": D_MODEL, "fox_f_bias": HEADS, "q_norm": Q_LORA, "kv_norm": KV_LORA,
           "mlp_norm": D_MODEL, "final_norm": D_MODEL}
SMALL_ROWS = 8


def _params(sem):
    return pltpu.CompilerParams(dimension_semantics=sem, vmem_limit_bytes=VMEM_LIMIT)


def _rows(name, fn, row_ins, const_ins, outs, reds=(), tr=256):
    norm = [(a, a.shape[1], 0) if not isinstance(a, tuple) else a for a in row_ins]
    n_rows = norm[0][0].shape[0]
    tr = min(tr, n_rows)
    assert n_rows % tr == 0, (name, n_rows, tr)
    n_in, n_out, n_red = len(norm) + len(const_ins), len(outs), len(reds)

    def body(*refs):
        vals = [r[...] for r in refs[:n_in]]
        out_refs = refs[n_in:n_in + n_out]
        red_refs = refs[n_in + n_out:]
        out_vals, red_vals = fn(*vals)
        for r, v in zip(out_refs, out_vals):
            r[...] = v.astype(r.dtype)
        if n_red:
            @pl.when(pl.program_id(0) == 0)
            def _():
                for r in red_refs:
                    r[...] = jnp.zeros_like(r)
            for r, v in zip(red_refs, red_vals):
                r[...] += v

    in_specs = [pl.BlockSpec((tr, w), functools.partial(lambda i, cb: (i, cb), cb=cb)) for _, w, cb in norm]
    in_specs += [pl.BlockSpec(a.shape, lambda i: (0, 0)) for a in const_ins]
    out_specs = [pl.BlockSpec((tr, c), lambda i: (i, 0)) for c, _ in outs]
    out_specs += [pl.BlockSpec((1, c), lambda i: (0, 0)) for c in reds]
    out_shape = [jax.ShapeDtypeStruct((n_rows, c), dt) for c, dt in outs]
    out_shape += [jax.ShapeDtypeStruct((1, c), F32) for c in reds]
    res = pl.pallas_call(
        body, name=name, grid=(n_rows // tr,), in_specs=in_specs, out_specs=out_specs, out_shape=out_shape,
        compiler_params=_params(("arbitrary",)),
    )(*[a for a, _, _ in norm], *const_ins)
    return res


def _rstd(x):
    return lax.rsqrt(jnp.mean(x * x, axis=-1, keepdims=True) + EPS)


def rms_fwd(name, x, gain, tr=256):
    width = x[1] if isinstance(x, tuple) else x.shape[1]

    def fn(xv, g):
        return (xv * _rstd(xv) * g,), ()
    return _rows(name, fn, [x], [gain], [(width, BF16)], tr=tr)[0]


def rms_bwd(name, x, dy, gain, dres=None, out_dtype=F32, tr=256):
    width = x[1] if isinstance(x, tuple) else x.shape[1]

    def fn(xv, dyv, *rest):
        g = rest[-1]
        r = _rstd(xv)
        n = xv * r
        dyv = dyv.astype(F32)
        dn = dyv * g
        dx = r * (dn - n * jnp.mean(dn * n, axis=-1, keepdims=True))
        if dres is not None:
            dx = dx + rest[0]
        return (dx,), (jnp.sum(dyv * n, axis=0, keepdims=True),)

    ins = [x, dy] + ([dres] if dres is not None else [])
    return _rows(name, fn, ins, [gain], [(width, out_dtype)], [width], tr=tr)


def _rope_lanes(t, c, s1, s2):
    return t * c + pltpu.roll(t, 96, 1) * s1 + pltpu.roll(t, 32, 1) * s2


def rope_heads(name, x, tabs, out_dtype):
    def fn(xv, c, s1, s2):
        xv = xv.astype(F32)
        parts = []
        for h in range(HEADS):
            parts.append(xv[:, h * QK_PAD:h * QK_PAD + NOPE])
            parts.append(_rope_lanes(xv[:, h * QK_PAD + NOPE:(h + 1) * QK_PAD], c, s1, s2))
        return (jnp.concatenate(parts, axis=1),), ()
    return _rows(name, fn, [x, *tabs], [], [(HEADS * QK_PAD, out_dtype)])[0]


def rope_block(name, x, col_block, tabs, out_dtype):
    def fn(xv, c, s1, s2):
        return (_rope_lanes(xv.astype(F32), c, s1, s2),), ()
    return _rows(name, fn, [(x, 128, col_block), *tabs], [], [(128, out_dtype)])[0]


def k_assemble(name, kvn, kr):
    def fn(knp, krv):
        parts = []
        for h in range(HEADS):
            parts.append(knp[:, h * NOPE:(h + 1) * NOPE])
            parts.append(krv)
        return (jnp.concatenate(parts, axis=1),), ()
    return _rows(name, fn, [(kvn, HEADS * NOPE, 0), kr], [], [(HEADS * QK_PAD, BF16)])[0]


def dk_split(name, dk, dv, inv_tabs):
    def fn(dkv, dvv, c, s1, s2):
        parts = []
        acc = None
        for h in range(HEADS):
            parts.append(dkv[:, h * QK_PAD:h * QK_PAD + NOPE].astype(BF16))
            t = dkv[:, h * QK_PAD + NOPE:(h + 1) * QK_PAD]
            acc = t if acc is None else acc + t
        parts.append(dvv)
        return (jnp.concatenate(parts, axis=1), _rope_lanes(acc, c, s1, s2)), ()
    return _rows(name, fn, [dk, dv, *inv_tabs], [], [(2 * HEADS * NOPE, BF16), (128, F32)])


def gate_mix(name, graw, y_mla, y_fox):
    def fn(g, ya, yb):
        ga = jax.nn.sigmoid(g[:, :D_MODEL])
        gb = jax.nn.sigmoid(g[:, D_MODEL:])
        return (ga * ya + gb * yb,), ()
    return _rows(name, fn, [graw, y_mla, y_fox], [], [(D_MODEL, BF16)])[0]


def gate_mix_bwd(name, graw, y_mla, y_fox, dmix):
    def fn(g, ya, yb, dm):
        ga = jax.nn.sigmoid(g[:, :D_MODEL])
        gb = jax.nn.sigmoid(g[:, D_MODEL:])
        dgraw = jnp.concatenate([dm * ya * ga * (1.0 - ga), dm * yb * gb * (1.0 - gb)], axis=1)
        return (dgraw, dm * ga, dm * gb), ()
    return _rows(name, fn, [graw, y_mla, y_fox, dmix], [], [(GATE_W, BF16), (D_MODEL, BF16), (D_MODEL, BF16)], tr=128)


def loss_head(name, h2, target, gain):
    inv_d = 1.0 / D_MODEL

    def fn(h, t, g):
        r = _rstd(h)
        n = h * r
        err = n * g - t
        dy = err * inv_d
        dn = dy * g
        dh = r * (dn - n * jnp.mean(dn * n, axis=-1, keepdims=True))
        part = 0.5 * inv_d * jnp.sum(jnp.sum(err * err, axis=1, keepdims=True), axis=0, keepdims=True)
        return (dh,), (jnp.sum(dy * n, axis=0, keepdims=True), jnp.broadcast_to(part, (1, 128)))
    return _rows(name, fn, [h2, target], [gain], [(D_MODEL, F32)], [D_MODEL, 128])


def adamw(name, w, g, m, v, tr=256):
    c1 = 1.0 - ADAM_B1 ** ADAM_STEP
    c2 = 1.0 - ADAM_B2 ** ADAM_STEP

    def fn(wv, gv, mv, vv):
        m_new = ADAM_B1 * mv + (1.0 - ADAM_B1) * gv
        v_new = ADAM_B2 * vv + (1.0 - ADAM_B2) * (gv * gv)
        delta = -ADAM_LR * ((m_new / c1) / (jnp.sqrt(v_new / c2) + ADAM_EPS) + ADAM_WD * wv)
        return (delta, m_new, v_new), ()
    cols = w.shape[1]
    return _rows(name, fn, [w, g, m, v], [], [(cols, F32)] * 3, tr=tr)


def _row_tile(n_rows, cap=640):
    return max(t for t in range(16, cap + 1, 16) if n_rows % t == 0)


def add_pairs(name, a, b):
    def fn(av, bv):
        return (av.astype(F32) + bv.astype(F32),), ()
    return _rows(name, fn, [a, b], [], [(a.shape[1], BF16)], tr=_row_tile(a.shape[0]))[0]


def add_final(name, a, b, r0, r1, r2):
    def fn(av, bv, r0v, r1v, r2v):
        return (((av.astype(F32) + bv.astype(F32)) + r0v.astype(F32)) + r1v.astype(F32) + r2v.astype(F32),), ()
    return _rows(name, fn, [a, b, r0, r1, r2], [], [(a.shape[1], F32)], tr=_row_tile(a.shape[0]))[0]


TN_TILES = dict(tm=1024, tn=1024, tk=2048)
_DIMS = {"nn": (((1,), (0,)), ((), ())), "nt": (((1,), (1,)), ((), ())), "tn": (((0,), (0,)), ((), ()))}


def matmul(name, a, b, mode, outs, epilogue=None, extras=(), tm=1024, tn=1024, tk=2048):
    if mode == "tn":
        kdim, m = a.shape
    else:
        m, kdim = a.shape
    n = b.shape[0] if mode == "nt" else b.shape[1]
    tm, tn, tk = min(tm, m), min(tn, n), min(tk, kdim)
    assert m % tm == 0 and n % tn == 0 and kdim % tk == 0, (name, a.shape, b.shape)
    nk = kdim // tk
    n_ex, n_out = len(extras), len(outs)
    dims = _DIMS[mode]

    def body(a_ref, b_ref, *rest):
        ex_refs = rest[:n_ex]
        out_refs = rest[n_ex:n_ex + n_out]

        def finish(acc):
            vals = (acc,) if epilogue is None else epilogue(acc, *[r[...] for r in ex_refs])
            for r, v in zip(out_refs, vals):
                r[...] = v.astype(r.dtype)

        part = lax.dot_general(a_ref[...].astype(BF16), b_ref[...].astype(BF16), dims, preferred_element_type=F32)
        if nk == 1:
            finish(part)
        else:
            acc_ref = rest[-1]
            k = pl.program_id(2)

            @pl.when(k == 0)
            def _():
                acc_ref[...] = part

            @pl.when(k > 0)
            def _():
                acc_ref[...] += part

            @pl.when(k == nk - 1)
            def _():
                finish(acc_ref[...])

    a_spec = pl.BlockSpec((tk, tm), lambda i, j, k: (k, i)) if mode == "tn" else pl.BlockSpec((tm, tk), lambda i, j, k: (i, k))
    b_spec = pl.BlockSpec((tn, tk), lambda i, j, k: (j, k)) if mode == "nt" else pl.BlockSpec((tk, tn), lambda i, j, k: (k, j))
    tile = pl.BlockSpec((tm, tn), lambda i, j, k: (i, j))
    res = pl.pallas_call(
        body, name=name, grid=(m // tm, n // tn, nk),
        in_specs=[a_spec, b_spec] + [tile] * n_ex,
        out_specs=[tile] * n_out,
        out_shape=[jax.ShapeDtypeStruct((m, n), dt) for dt in outs],
        scratch_shapes=[pltpu.VMEM((tm, tn), F32)] if nk > 1 else [],
        compiler_params=_params(("parallel", "parallel", "arbitrary")),
    )(a, b, *extras)
    return res


_NT = (((1,), (1,)), ((), ()))
_NN = (((1,), (0,)), ((), ()))


def _mask(bq, chunk, transposed, row0=0, shape=None):
    shape = (bq, bq) if shape is None else shape
    row = lax.broadcasted_iota(jnp.int32, shape, 0) + row0
    col = lax.broadcasted_iota(jnp.int32, shape, 1)
    if chunk > 1:
        row, col = row // chunk, col // chunk
    return (row <= col) if transposed else (col <= row)


def _row_layout(a, bq):
    h, s, _ = a.shape
    return a.reshape(h, s // bq, 1, bq)


def _split_refs(refs, n_in, comm, n_out, n_scr):
    n_ci = len(comm.args) if comm else 0
    n_co = len(comm.out_shapes) if comm else 0
    cuts = [n_in, n_ci, n_out, n_co, n_scr, 3 if comm else 0]
    parts, at = [], 0
    for n in cuts:
        parts.append(list(refs[at:at + n]))
        at += n
    assert at == len(refs), (at, len(refs))
    return parts


def _comm_start(comm, c_in, c_out, c_sems, n0, n1):
    if comm is not None:
        @pl.when(jnp.logical_and(pl.program_id(0) == 0, pl.program_id(1) == 0))
        def _():
            comm.start(c_in, c_out, c_sems)


def _comm_wait(comm, c_in, c_out, c_sems, n0, n1):
    if comm is not None:
        @pl.when(jnp.logical_and(pl.program_id(0) == n0 - 1, pl.program_id(1) == n1 - 1))
        def _():
            comm.wait(c_in, c_out, c_sems)


def attn_fwd(name, q, k, v, offs, dqk, chunk, scale, cum=None, bq=512, comm=None):
    s_len = q.shape[0]
    nq = s_len // bq
    qoff, koff, voff = offs
    has_bias = cum is not None
    scale2 = scale * LOG2E

    n_in = 5 if has_bias else 3

    def body(*refs):
        ins, c_in, outs, c_out, scr, c_sems = _split_refs(refs, n_in, comm, 2, 3)
        q_ref, k_ref, v_ref = ins[:3]
        if has_bias:
            cc_ref, cr_ref = ins[3:]
        o_ref, lse_ref = outs
        m_s, l_s, acc_s = scr
        _comm_start(comm, c_in, c_out, c_sems, HEADS, nq)
        i = pl.program_id(1)
        qv = q_ref[...]
        m_s[...] = jnp.full_like(m_s, NEG)
        l_s[...] = jnp.zeros_like(l_s)
        acc_s[...] = jnp.zeros_like(acc_s)

        def step(j, masked):
            off = pl.multiple_of(j * bq, bq)
            kj = k_ref[pl.ds(off, bq), :]
            vj = v_ref[pl.ds(off, bq), :]
            st = lax.dot_general(kj, qv, _NT, preferred_element_type=F32) * scale2
            if has_bias:
                st = st + cr_ref[...] - cc_ref[pl.ds(off, bq), :]
            if masked:
                st = jnp.where(_mask(bq, chunk, True), st, NEG)
            m_prev = m_s[...]
            m_new = jnp.maximum(m_prev, jnp.max(st, axis=0, keepdims=True))
            alpha = jnp.exp2(m_prev - m_new)
            pt = jnp.exp2(st - m_new)
            l_s[...] = alpha * l_s[...] + jnp.sum(pt, axis=0, keepdims=True)
            acc_s[...] = alpha * acc_s[...] + lax.dot_general(vj, pt.astype(BF16), _DIMS["tn"],
                                                              preferred_element_type=F32)
            m_s[...] = m_new

        def pair_body(jj, carry):
            step(2 * jj, False)
            step(2 * jj + 1, False)
            return carry

        lax.fori_loop(0, i // 2, pair_body, 0)

        @pl.when(i % 2 == 1)
        def _():
            step(i - 1, False)

        step(i, True)
        o_ref[...] = (acc_s[...] / l_s[...]).T.astype(o_ref.dtype)
        lse_ref[...] = m_s[...] + jnp.log2(l_s[...])
        _comm_wait(comm, c_in, c_out, c_sems, HEADS, nq)

    in_specs = [
        pl.BlockSpec((bq, dqk), lambda h, i: (i, qoff + h)),
        pl.BlockSpec((s_len, dqk), lambda h, i: (0, koff + h)),
        pl.BlockSpec((s_len, HEAD_V), lambda h, i: (0, voff + h)),
    ]
    args = [q, k, v]
    if has_bias:
        in_specs += [pl.BlockSpec((None, s_len, 1), lambda h, i: (h, 0, 0)),
                     pl.BlockSpec((None, None, 1, bq), lambda h, i: (h, i, 0, 0))]
        args += [cum, _row_layout(cum, bq)]
    c_ins, c_outs, c_shapes, c_scratch, aliases = _with_comm(comm, len(args), 2)
    o, lse_rows, *comm_out = pl.pallas_call(
        body, name=name, grid=(HEADS, nq), in_specs=in_specs + c_ins,
        out_specs=[pl.BlockSpec((bq, HEAD_V), lambda h, i: (i, h)),
                   pl.BlockSpec((None, None, 1, bq), lambda h, i: (h, i, 0, 0))] + c_outs,
        out_shape=[jax.ShapeDtypeStruct((s_len, HEADS * HEAD_V), F32),
                   jax.ShapeDtypeStruct((HEADS, nq, 1, bq), F32)] + c_shapes,
        scratch_shapes=[pltpu.VMEM((1, bq), F32), pltpu.VMEM((1, bq), F32), pltpu.VMEM((HEAD_V, bq), F32)] + c_scratch,
        input_output_aliases=aliases,
        compiler_params=_params(("arbitrary", "arbitrary")),
    )(*args, *(comm.args if comm else []))
    return (o, lse_rows, *comm_out)


def _bwd_block(kv, vv, qi, doi, lse_row, scale2, bias, masked, bq, chunk):
    st = lax.dot_general(kv, qi, _NT, preferred_element_type=F32) * scale2
    if bias is not None:
        st = st + bias[0] - bias[1]
    if masked:
        st = jnp.where(_mask(bq, chunk, True), st, NEG)
    pt = jnp.exp2(st - lse_row)
    dpt = lax.dot_general(vv, doi, _NT, preferred_element_type=F32)
    return pt, dpt


def attn_delta_pairs(name, q, k, v, do, lse, cum, offs, dqk, chunk, scale, bq=512):
    s_len = q.shape[0]
    nq = s_len // bq
    qoff, koff, voff = offs
    scale2 = scale * LOG2E

    def body(k_ref, v_ref, q_ref, do_ref, lse_ref, cc_ref, cr_ref, delta_ref):
        j = pl.program_id(1)
        kv = k_ref[...]
        vv = v_ref[...]

        @pl.when(j == 0)
        def _():
            delta_ref[...] = jnp.zeros_like(delta_ref)

        def step(i, masked):
            off = pl.multiple_of(i * bq, bq)
            qi = q_ref[pl.ds(off, bq), :]
            doi = do_ref[pl.ds(off, bq), :].astype(BF16)
            pt, dpt = _bwd_block(kv, vv, qi, doi, lse_ref[i], scale2, (cr_ref[i], cc_ref[...]), masked, bq, chunk)
            delta_ref[i] += jnp.sum(pt * dpt, axis=0, keepdims=True)

        step(j, True)

        def loop_body(i, carry):
            step(i, False)
            return carry

        lax.fori_loop(j + 1, nq, loop_body, 0)

    rows = pl.BlockSpec((None, nq, 1, bq), lambda h, j: (h, 0, 0, 0))
    in_specs = [
        pl.BlockSpec((bq, dqk), lambda h, j: (j, koff + h)),
        pl.BlockSpec((bq, HEAD_V), lambda h, j: (j, voff + h)),
        pl.BlockSpec((s_len, dqk), lambda h, j: (0, qoff + h)),
        pl.BlockSpec((s_len, HEAD_V), lambda h, j: (0, h)),
        rows,
        pl.BlockSpec((None, bq, 1), lambda h, j: (h, j, 0)),
        rows,
    ]
    return pl.pallas_call(
        body, name=name, grid=(HEADS, nq), in_specs=in_specs, out_specs=rows,
        out_shape=jax.ShapeDtypeStruct((HEADS, nq, 1, bq), F32),
        compiler_params=_params(("arbitrary", "arbitrary")),
    )(k, v, q, do, lse, cum, _row_layout(cum, bq))


def attn_delta(name, do, o, bq=512):
    s_len = do.shape[0]
    nq = s_len // bq

    def body(do_ref, o_ref, out_ref):
        prod = do_ref[...].astype(F32) * o_ref[...].astype(F32)
        out_ref[...] = jnp.sum(prod.T, axis=0, keepdims=True)

    blk = pl.BlockSpec((bq, HEAD_V), lambda h, i: (i, h))
    return pl.pallas_call(
        body, name=name, grid=(HEADS, nq), in_specs=[blk, blk],
        out_specs=pl.BlockSpec((None, None, 1, bq), lambda h, i: (h, i, 0, 0)),
        out_shape=jax.ShapeDtypeStruct((HEADS, nq, 1, bq), F32),
        compiler_params=_params(("parallel", "parallel")),
    )(do, o)


def attn_bwd(name, q, k, v, do, lse, delta, offs, dqk, chunk, scale, out_dtype, cum=None, bq=512, comm=None):
    s_len = q.shape[0]
    nq = s_len // bq
    qoff, koff, voff = offs
    has_bias = cum is not None
    scale2 = scale * LOG2E
    n_in, n_out = (8, 4) if has_bias else (6, 3)

    def body(*refs):
        ins, c_in, outs, c_out, scr, c_sems = _split_refs(refs, n_in, comm, n_out, n_out - 1)
        k_ref, v_ref, q_ref, do_ref, lse_ref, delta_ref = ins[:6]
        dk_ref, dv_ref, dq_ref = outs[:3]
        dk_s, dv_s = scr[:2]
        if has_bias:
            cc_ref, cr_ref = ins[6:]
            dc_ref, dc_s = outs[3], scr[2]
        _comm_start(comm, c_in, c_out, c_sems, HEADS, nq)
        j = pl.program_id(1)
        kv = k_ref[...]
        vv = v_ref[...]
        dk_s[...] = jnp.zeros_like(dk_s)
        dv_s[...] = jnp.zeros_like(dv_s)
        if has_bias:
            dc_s[...] = jnp.zeros_like(dc_s)

        @pl.when(j == 0)
        def _():
            dq_ref[...] = jnp.zeros_like(dq_ref)

        def step(i, masked):
            off = pl.multiple_of(i * bq, bq)
            qi = q_ref[pl.ds(off, bq), :]
            doi = do_ref[pl.ds(off, bq), :].astype(BF16)
            bias = (cr_ref[i], cc_ref[...]) if has_bias else None
            pt, dpt = _bwd_block(kv, vv, qi, doi, lse_ref[i], scale2, bias, masked, bq, chunk)
            dv_s[...] += lax.dot_general(pt.astype(BF16), doi, _NN, preferred_element_type=F32)
            dst = pt * (dpt - delta_ref[i])
            if has_bias:
                dc_s[...] -= jnp.sum(dst, axis=-1, keepdims=True)
            dst = dst.astype(BF16)
            dk_s[...] += lax.dot_general(dst, qi, _NN, preferred_element_type=F32)
            dq_ref[pl.ds(off, bq), :] += lax.dot_general(dst, kv, _DIMS["tn"], preferred_element_type=F32) * scale

        step(j, True)

        def loop_body(i, carry):
            step(i, False)
            return carry

        lax.fori_loop(j + 1, nq, loop_body, 0)
        dk_ref[...] = (dk_s[...] * scale).astype(dk_ref.dtype)
        dv_ref[...] = dv_s[...].astype(dv_ref.dtype)
        if has_bias:
            dc_ref[...] = dc_s[...]
        _comm_wait(comm, c_in, c_out, c_sems, HEADS, nq)

    rows = pl.BlockSpec((None, nq, 1, bq), lambda h, j: (h, 0, 0, 0))
    in_specs = [
        pl.BlockSpec((bq, dqk), lambda h, j: (j, koff + h)),
        pl.BlockSpec((bq, HEAD_V), lambda h, j: (j, voff + h)),
        pl.BlockSpec((s_len, dqk), lambda h, j: (0, qoff + h)),
        pl.BlockSpec((s_len, HEAD_V), lambda h, j: (0, h)),
        rows,
        rows,
    ]
    args = [k, v, q, do, lse, delta]
    out_specs = [pl.BlockSpec((bq, dqk), lambda h, j: (j, h)), pl.BlockSpec((bq, HEAD_V), lambda h, j: (j, h)),
                 pl.BlockSpec((s_len, dqk), lambda h, j: (0, h))]
    out_shape = [jax.ShapeDtypeStruct((s_len, HEADS * dqk), out_dtype),
                 jax.ShapeDtypeStruct((s_len, HEADS * HEAD_V), BF16),
                 jax.ShapeDtypeStruct((s_len, HEADS * dqk), F32)]
    scratch = [pltpu.VMEM((bq, dqk), F32), pltpu.VMEM((bq, HEAD_V), F32)]
    if has_bias:
        in_specs += [pl.BlockSpec((None, bq, 1), lambda h, j: (h, j, 0)), rows]
        args += [cum, _row_layout(cum, bq)]
        out_specs.append(pl.BlockSpec((None, bq, 1), lambda h, j: (h, j, 0)))
        out_shape.append(jax.ShapeDtypeStruct((HEADS, s_len, 1), F32))
        scratch.append(pltpu.VMEM((bq, 1), F32))
    c_ins, c_outs, c_shapes, c_scratch, aliases = _with_comm(comm, len(args), n_out)
    return pl.pallas_call(
        body, name=name, grid=(HEADS, nq), in_specs=in_specs + c_ins, out_specs=out_specs + c_outs,
        out_shape=out_shape + c_shapes, scratch_shapes=scratch + c_scratch, input_output_aliases=aliases,
        compiler_params=_params(("arbitrary", "arbitrary")),
    )(*args, *(comm.args if comm else []))


_CUM_BLK = 512


def _split3(x):
    hi = x.astype(BF16)
    r1 = x - hi.astype(F32)
    mid = r1.astype(BF16)
    lo = (r1 - mid.astype(F32)).astype(BF16)
    return hi, mid, lo


def _tri_dot(x, tri):
    hi, mid, lo = _split3(x)
    out = lax.dot_general(lo, tri, _NN, preferred_element_type=F32)
    out = out + lax.dot_general(mid, tri, _NN, preferred_element_type=F32)
    return out + lax.dot_general(hi, tri, _NN, preferred_element_type=F32)


def fox_cum_fwd(name, ff_t, bias):
    s_len = ff_t.shape[1]
    nb = s_len // _CUM_BLK

    def body(ff_ref, b_ref, cum_ref):
        row = lax.broadcasted_iota(jnp.int32, (_CUM_BLK, _CUM_BLK), 0)
        col = lax.broadcasted_iota(jnp.int32, (_CUM_BLK, _CUM_BLK), 1)
        tri = (row <= col).astype(BF16)
        carry = jnp.zeros((HEADS, 1), F32)
        for b in range(nb):
            z = ff_ref[:, b * _CUM_BLK:(b + 1) * _CUM_BLK] + b_ref[...]
            logf = jnp.minimum(z, 0.0) - jnp.log1p(jnp.exp(-jnp.abs(z)))
            blk = _tri_dot(logf, tri) + carry
            cum_ref[:, b * _CUM_BLK:(b + 1) * _CUM_BLK] = blk
            carry = blk[:, _CUM_BLK - 1:_CUM_BLK]

    return pl.pallas_call(
        body, name=name, out_shape=jax.ShapeDtypeStruct((HEADS, s_len), F32),
        compiler_params=pltpu.CompilerParams(vmem_limit_bytes=VMEM_LIMIT),
    )(ff_t, bias)


def fox_cum_bwd(name, ff_t, bias, dcum):
    s_len = ff_t.shape[1]
    nb = s_len // _CUM_BLK

    def body(ff_ref, b_ref, dc_ref, dff_ref, db_ref):
        row = lax.broadcasted_iota(jnp.int32, (_CUM_BLK, _CUM_BLK), 0)
        col = lax.broadcasted_iota(jnp.int32, (_CUM_BLK, _CUM_BLK), 1)
        tri = (row >= col).astype(BF16)
        carry = jnp.zeros((HEADS, 1), F32)
        dbias = jnp.zeros((HEADS, 1), F32)
        for b in reversed(range(nb)):
            sl = slice(b * _CUM_BLK, (b + 1) * _CUM_BLK)
            dlogf = _tri_dot(dc_ref[:, sl], tri) + carry
            carry = dlogf[:, 0:1]
            z = ff_ref[:, sl] + b_ref[...]
            dz = dlogf / (1.0 + jnp.exp(z))
            dff_ref[:, sl] = dz
            dbias = dbias + jnp.sum(dz, axis=-1, keepdims=True)
        db_ref[...] = dbias

    return pl.pallas_call(
        body, name=name,
        out_shape=[jax.ShapeDtypeStruct((HEADS, s_len), F32), jax.ShapeDtypeStruct((HEADS, 1), F32)],
        compiler_params=pltpu.CompilerParams(vmem_limit_bytes=VMEM_LIMIT),
    )(ff_t, bias, dcum)


_ANY = pl.BlockSpec(memory_space=pl.ANY)


def _me():
    return lax.axis_index("x"), lax.axis_index("y"), lax.axis_index("c")


def comm_allgather(name, mine):
    n_rows, n_cols = mine.shape

    def body(x_ref, out_ref, send_sems, recv_sems, local_sem):
        x, y, c = _me()
        sibling = (x, y, 1 - c)
        chips = [(1 - x, y), (x, 1 - y), (1 - x, 1 - y)]

        def blk(px, py, pc):
            return out_ref.at[pc * 4 + px * 2 + py]

        def copy(k, block, to, src=None):
            return pltpu.make_async_remote_copy(
                src_ref=blk(*block) if src is None else src, dst_ref=blk(*block),
                send_sem=send_sems.at[k], recv_sem=recv_sems.at[k], device_id=to, device_id_type=MESH)

        own = pltpu.make_async_copy(x_ref, blk(x, y, c), local_sem)
        own.start()
        first = [copy(0, (x, y, c), sibling, src=x_ref)]
        first += [copy(1 + j, (x, y, c), (*chip, c), src=x_ref) for j, chip in enumerate(chips)]
        for cp in first:
            cp.start()
        passed = [copy(4 + j, (*chip, c), sibling) for j, chip in enumerate(chips)]
        for j, chip in enumerate(chips):
            copy(1 + j, (*chip, c), (x, y, c)).wait_recv()
            passed[j].start()
        copy(0, sibling, (x, y, c)).wait_recv()
        for j, chip in enumerate(chips):
            copy(4 + j, (*chip, 1 - c), (x, y, c)).wait_recv()
        for cp in first + passed:
            cp.wait_send()
        own.wait()

    return pl.pallas_call(
        body, name=name, out_shape=jax.ShapeDtypeStruct((N_DEV, n_rows, n_cols), mine.dtype),
        in_specs=[_ANY], out_specs=_ANY,
        scratch_shapes=[pltpu.SemaphoreType.DMA((7,)), pltpu.SemaphoreType.DMA((7,)), pltpu.SemaphoreType.DMA],
    )(mine)


def comm_swap_sibling(name, parts):
    _, n_rows, n_cols = parts.shape

    def body(p_ref, got_ref, send_sem, recv_sem):
        x, y, c = _me()
        cp = pltpu.make_async_remote_copy(
            src_ref=p_ref.at[pl.ds((1 - c) * 4, 4)], dst_ref=got_ref, send_sem=send_sem, recv_sem=recv_sem,
            device_id=(x, y, 1 - c), device_id_type=MESH)
        cp.start()
        cp.wait()

    return pl.pallas_call(
        body, name=name, out_shape=jax.ShapeDtypeStruct((4, n_rows, n_cols), parts.dtype),
        in_specs=[_ANY], out_specs=_ANY,
        scratch_shapes=[pltpu.SemaphoreType.DMA, pltpu.SemaphoreType.DMA],
    )(parts)


def comm_swap_chips(name, parts):
    _, n_rows, n_cols = parts.shape

    def body(p_ref, got_ref, send_sems, recv_sems):
        x, y, c = _me()
        chips = [(1 - x, y), (x, 1 - y), (1 - x, 1 - y)]
        cps = [pltpu.make_async_remote_copy(
            src_ref=p_ref.at[2 * px + py], dst_ref=got_ref.at[k], send_sem=send_sems.at[k], recv_sem=recv_sems.at[k],
            device_id=(px, py, c), device_id_type=MESH) for k, (px, py) in enumerate(chips)]
        for cp in cps:
            cp.start()
        for cp in cps:
            cp.wait()

    return pl.pallas_call(
        body, name=name, out_shape=jax.ShapeDtypeStruct((3, n_rows, n_cols), parts.dtype),
        in_specs=[_ANY], out_specs=_ANY,
        scratch_shapes=[pltpu.SemaphoreType.DMA((3,)), pltpu.SemaphoreType.DMA((3,))],
    )(parts)


class CommHook:
    def __init__(self, args, out_shapes, n_copies, copies, aliases=None):
        self.args, self.out_shapes, self.n_copies, self.copies = list(args), list(out_shapes), n_copies, copies
        self.aliases = aliases or {}

    def scratch(self):
        return [pltpu.SemaphoreType.DMA((self.n_copies,)), pltpu.SemaphoreType.DMA((self.n_copies,)),
                pltpu.SemaphoreType.DMA((1,))]

    def start(self, in_refs, out_refs, sems):
        sends, _, locs = self.copies(in_refs, out_refs, *sems)
        for cp in locs() + sends():
            cp.start()

    def wait(self, in_refs, out_refs, sems):
        sends, recvs, locs = self.copies(in_refs, out_refs, *sems)
        for cp in sends():
            cp.wait_send()
        for cp in recvs():
            cp.wait_recv()
        for cp in locs():
            cp.wait()


def _remote(src, dst, send_sem, recv_sem, to):
    return pltpu.make_async_remote_copy(src_ref=src, dst_ref=dst, send_sem=send_sem, recv_sem=recv_sem,
                                        device_id=to, device_id_type=MESH)


def hook_gather_first(mine):
    n_rows, n_cols = mine.shape

    def copies(ins, outs, send, recv, local):
        (x_ref,), (out_ref,) = ins, outs
        x, y, c = _me()
        me = c * 4 + x * 2 + y
        peers = [(x, y, 1 - c), (1 - x, y, c), (x, 1 - y, c), (1 - x, 1 - y, c)]

        def sends():
            return [_remote(x_ref, out_ref.at[me], send.at[k], recv.at[k], p) for k, p in enumerate(peers)]

        def recvs():
            return [_remote(x_ref, out_ref.at[pc * 4 + px * 2 + py], send.at[k], recv.at[k], (px, py, pc))
                    for k, (px, py, pc) in enumerate(peers)]

        return sends, recvs, lambda: [pltpu.make_async_copy(x_ref, out_ref.at[me], local.at[0])]

    return CommHook([mine], [jax.ShapeDtypeStruct((N_DEV, n_rows, n_cols), mine.dtype)], 4, copies)


def hook_gather_second(gathered):
    def copies(ins, outs, send, recv, local):
        (g_in,), (g_out,) = ins, outs
        x, y, c = _me()
        chips = [(1 - x, y), (x, 1 - y), (1 - x, 1 - y)]

        def sends():
            return [_remote(g_in.at[c * 4 + px * 2 + py], g_out.at[c * 4 + px * 2 + py], send.at[k], recv.at[k],
                            (x, y, 1 - c)) for k, (px, py) in enumerate(chips)]

        def recvs():
            return [_remote(g_in.at[(1 - c) * 4 + px * 2 + py], g_out.at[(1 - c) * 4 + px * 2 + py], send.at[k],
                            recv.at[k], (x, y, 1 - c)) for k, (px, py) in enumerate(chips)]

        return sends, recvs, lambda: []

    return CommHook([gathered], [jax.ShapeDtypeStruct(gathered.shape, gathered.dtype)], 3, copies, aliases={0: 0})


def hook_swap_sibling(parts):
    _, n_rows, n_cols = parts.shape

    def copies(ins, outs, send, recv, local):
        (p_ref,), (got_ref,) = ins, outs
        x, y, c = _me()

        def swap():
            return [_remote(p_ref.at[pl.ds((1 - c) * 4, 4)], got_ref, send.at[0], recv.at[0], (x, y, 1 - c))]

        return swap, swap, lambda: []

    return CommHook([parts], [jax.ShapeDtypeStruct((4, n_rows, n_cols), parts.dtype)], 1, copies)


def hook_swap_chips(parts):
    _, n_rows, n_cols = parts.shape

    def copies(ins, outs, send, recv, local):
        (p_ref,), (got_ref,) = ins, outs
        x, y, c = _me()
        chips = [(1 - x, y), (x, 1 - y), (1 - x, 1 - y)]

        def swaps():
            return [_remote(p_ref.at[2 * px + py], got_ref.at[k], send.at[k], recv.at[k], (px, py, c))
                    for k, (px, py) in enumerate(chips)]

        return swaps, swaps, lambda: []

    return CommHook([parts], [jax.ShapeDtypeStruct((3, n_rows, n_cols), parts.dtype)], 3, copies)


def _with_comm(comm, n_args, n_outs):
    if comm is None:
        return [], [], [], [], {}
    aliases = {n_args + a: n_outs + o for a, o in comm.aliases.items()}
    return [_ANY] * len(comm.args), [_ANY] * len(comm.out_shapes), comm.out_shapes, comm.scratch(), aliases


def comm_allreduce_small(name, mine):
    shape = mine.shape

    def body(x_ref, out_ref, buf, send_sems, recv_sems):
        x, y, c = _me()
        my_slot = c * 4 + x * 2 + y
        buf[my_slot] = x_ref[...]
        cps = []
        for k in range(1, N_DEV):
            dx, dy, dc = (k >> 2) & 1, (k >> 1) & 1, k & 1
            px, py, pc = x ^ dx, y ^ dy, c ^ dc
            send = pltpu.make_async_remote_copy(
                src_ref=x_ref, dst_ref=buf.at[my_slot], send_sem=send_sems.at[k - 1], recv_sem=recv_sems.at[k - 1],
                device_id=(px, py, pc), device_id_type=MESH)
            send.start()
            recv = pltpu.make_async_remote_copy(
                src_ref=x_ref, dst_ref=buf.at[pc * 4 + px * 2 + py], send_sem=send_sems.at[k - 1],
                recv_sem=recv_sems.at[k - 1], device_id=(px, py, pc), device_id_type=MESH)
            cps.append((send, recv))
        for send, recv in cps:
            send.wait_send()
            recv.wait_recv()
        total = buf[0]
        for s in range(1, N_DEV):
            total = total + buf[s]
        out_ref[...] = total

    vmem = pl.BlockSpec(memory_space=pltpu.VMEM)
    return pl.pallas_call(
        body, name=name, out_shape=jax.ShapeDtypeStruct(shape, F32), in_specs=[vmem], out_specs=vmem,
        scratch_shapes=[pltpu.VMEM((N_DEV,) + shape, F32), pltpu.SemaphoreType.DMA((7,)), pltpu.SemaphoreType.DMA((7,))],
    )(mine)


def pack_local(shards, names):
    flat = [shards[n].reshape(-1, PACK_C) for n in names]
    rows = sum(f.shape[0] for f in flat)
    pad = (-rows) % 128
    return jnp.concatenate(flat + [jnp.zeros((pad, PACK_C), flat[0].dtype)], axis=0)


def unpack_group(gathered, names, shard_shapes):
    _, n_rows, _ = gathered.shape
    by_block = gathered.reshape(2, 4, n_rows, PACK_C).transpose(1, 0, 2, 3).reshape(N_DEV, n_rows, PACK_C)
    full = {}
    r0 = 0
    for name in names:
        rs, cs = shard_shapes[name]
        nr = rs * cs // PACK_C
        piece = by_block[:, r0:r0 + nr, :].reshape(N_DEV, rs, cs)
        r0 += nr
        if name in ROW_SHARDED:
            full[name] = piece.reshape(N_DEV * rs, cs)
        else:
            full[name] = piece.transpose(1, 0, 2).reshape(rs, N_DEV * cs)
    return full


def unpack_a(gathered, shard_shapes):
    full = unpack_group(gathered, GROUP_A, shard_shapes)
    w_in = full.pop("w_in")
    zeros = jnp.zeros((D_MODEL, LAT_W - OFF_FQ - HEADS), w_in.dtype)
    full["w_lat"] = jnp.concatenate([w_in[:, :OFF_FQ], w_in[:, OFF_FF:OFF_G], zeros], axis=1)
    full["w_fox"] = w_in[:, OFF_FQ:OFF_FF]
    full["w_gate"] = w_in[:, OFF_G:]
    w_uq = full.pop("w_uq").reshape(Q_LORA, HEADS, NOPE + ROPE)
    full["w_uq"] = jnp.pad(w_uq, ((0, 0), (0, 0), (0, QK_PAD - NOPE - ROPE))).reshape(Q_LORA, HEADS * QK_PAD)
    w_ukv = full.pop("w_ukv").reshape(KV_LORA, HEADS, 2, NOPE)
    full["w_kv"] = jnp.concatenate([w_ukv[:, :, 0, :].reshape(KV_LORA, HEADS * NOPE),
                                    w_ukv[:, :, 1, :].reshape(KV_LORA, HEADS * HEAD_V)], axis=1)
    return full


def pack_small(vals, loss):
    flat = [vals[n].reshape(-1) for n in SMALL] + [loss.reshape(-1)]
    used = sum(f.shape[0] for f in flat)
    flat.append(jnp.zeros((SMALL_ROWS * PACK_C - used,), F32))
    return jnp.concatenate(flat).reshape(SMALL_ROWS, PACK_C)


def unpack_small(packed):
    flat = packed.reshape(-1)
    out, off = {}, 0
    for n in SMALL:
        out[n] = flat[off:off + SMALL_N[n]]
        off += SMALL_N[n]
    return out, flat[off]


def rope_tables(s_len):
    pos = jnp.arange(s_len, dtype=F32)
    inv = 1.0 / (ROPE_THETA ** (jnp.arange(0, ROPE, 2, dtype=F32) / ROPE))
    ang = pos[:, None] * inv[None, :]
    cos, sin = jnp.cos(ang), jnp.sin(ang)
    zero = jnp.zeros_like(cos)
    c = jnp.concatenate([cos, cos, zero, zero], axis=1)
    s1 = jnp.concatenate([-sin, zero, zero, zero], axis=1)
    s2 = jnp.concatenate([zero, sin, zero, zero], axis=1)
    return (c, s1, s2), (c, -s1, -s2)


def reduce_scatter_tail(parts, from_sibling, from_chips_fn, names):
    cx, cy, cc = _me()
    n_rows = parts.shape[1]
    mine4 = lax.dynamic_slice_in_dim(parts, cc * 4, 4, axis=0)
    pair = add_pairs("rs_pair_sum_" + names, mine4.reshape(4 * n_rows, PACK_C), from_sibling.reshape(4 * n_rows, PACK_C))
    from_chips, extra = from_chips_fn(pair.reshape(4, n_rows, PACK_C))
    own = cx * 2 + cy
    total = add_final("rs_final_sum_" + names, lax.dynamic_index_in_dim(mine4, own, 0, keepdims=False),
                      lax.dynamic_index_in_dim(from_sibling, own, 0, keepdims=False),
                      from_chips[0], from_chips[1], from_chips[2])
    return total, extra


def local_step(x, target, w, small, packed_b, shard_shapes):
    s_len = x.shape[0]
    tabs, inv_tabs = rope_tables(s_len)
    g_attn = small["attn_norm"].reshape(1, D_MODEL)
    g_q = small["q_norm"].reshape(1, Q_LORA)
    g_kv = small["kv_norm"].reshape(1, KV_LORA)
    g_mlp = small["mlp_norm"].reshape(1, D_MODEL)
    g_final = small["final_norm"].reshape(1, D_MODEL)
    f_bias = small["fox_f_bias"].reshape(HEADS, 1)
    mla_scale = 1.0 / math.sqrt(NOPE + ROPE)
    fox_scale = 1.0 / math.sqrt(HEAD_V)
    mla_offs = (0, 0, HEADS)
    fox_offs = (0, HEADS, 2 * HEADS)

    xn = rms_fwd("rms_attn", x, g_attn)
    lat, = matmul("proj_lat", xn, w["w_lat"], "nn", [F32])
    fox, = matmul("proj_fox", xn, w["w_fox"], "nn", [BF16])
    graw, = matmul("proj_gate", xn, w["w_gate"], "nn", [F32])
    cq = rms_fwd("rms_q", (lat, Q_LORA, 0), g_q)
    ckv = rms_fwd("rms_kv", (lat, KV_LORA, Q_LORA // KV_LORA), g_kv)
    qraw, = matmul("up_q", cq, w["w_uq"], "nn", [F32])
    q = rope_heads("rope_q", qraw, tabs, BF16)
    kvn, = matmul("up_kv", ckv, w["w_kv"], "nn", [BF16])
    kr = rope_block("rope_k", lat, OFF_KR // 128, tabs, BF16)
    k = k_assemble("k_assemble", kvn, kr)
    o_mla, lse_mla, gathered_b = attn_fwd("mla_fwd", q, k, kvn, mla_offs, QK_PAD, CHUNK, mla_scale,
                                          comm=hook_gather_first(packed_b))
    ff_t = lat[:, OFF_FQ:OFF_FQ + HEADS].T
    cum = fox_cum_fwd("fox_cum", ff_t, f_bias).reshape(HEADS, s_len, 1) * LOG2E
    o_fox, lse_fox, gathered_b = attn_fwd("fox_fwd", fox, fox, fox, fox_offs, HEAD_V, 1, fox_scale, cum=cum,
                                          comm=hook_gather_second(gathered_b))
    unpack_b = functools.partial(unpack_group, names=GROUP_B, shard_shapes=shard_shapes)
    w = {**w, **unpack_b(gathered_b)}
    y_mla, = matmul("branch_mla", o_mla, w["w_mla_branch"], "nn", [F32])
    y_fox, = matmul("branch_fox", o_fox, w["w_fox_branch"], "nn", [F32])
    mix = gate_mix("gate_mix", graw, y_mla, y_fox)
    h1, = matmul("out_proj", mix, w["w_out"], "nn", [F32], epilogue=lambda acc, res: (res + acc,), extras=[x])
    hn = rms_fwd("rms_mlp", h1, g_mlp)

    def relu2(acc):
        r = jnp.maximum(acc, 0.0)
        return r * r, r
    u, relu_up = matmul("mlp_up", hn, w["w_up"], "nn", [BF16, BF16], epilogue=relu2)
    h2, = matmul("mlp_down", u, w["w_down"], "nn", [F32], epilogue=lambda acc, res: (res + acc,), extras=[h1])
    dh2, d_final, loss = loss_head("loss_head", h2, target, g_final)

    grads = {}
    dup, = matmul("d_mlp_down", dh2, w["w_down"], "nt", [BF16],
                  epilogue=lambda acc, r: (acc * (2.0 * r.astype(F32)),), extras=[relu_up])
    grads["w_down"], = matmul("gw_down", u, dh2, "tn", [BF16], **TN_TILES)
    dhn, = matmul("d_mlp_up", dup, w["w_up"], "nt", [F32])
    grads["w_up"], = matmul("gw_up", hn, dup, "tn", [BF16], **TN_TILES)
    dh1, d_mlp = rms_bwd("rms_mlp_bwd", h1, dhn, g_mlp, dres=dh2)
    dmix, = matmul("d_out_proj", dh1, w["w_out"], "nt", [F32])
    grads["w_out"], = matmul("gw_out", mix, dh1, "tn", [BF16], **TN_TILES)
    dgraw, dy_mla, dy_fox = gate_mix_bwd("gate_mix_bwd", graw, y_mla, y_fox, dmix)
    do_mla, = matmul("d_branch_mla", dy_mla, w["w_mla_branch"], "nt", [F32])
    grads["w_mla_branch"], = matmul("gw_branch_mla", o_mla, dy_mla, "tn", [BF16], **TN_TILES)
    do_fox, = matmul("d_branch_fox", dy_fox, w["w_fox_branch"], "nt", [F32])
    grads["w_fox_branch"], = matmul("gw_branch_fox", o_fox, dy_fox, "tn", [BF16], **TN_TILES)

    to_packed_b = jax.linear_transpose(unpack_b, jax.ShapeDtypeStruct(gathered_b.shape, BF16))
    parts_b, = to_packed_b({n: grads.pop(n) for n in GROUP_B})
    delta_mla = attn_delta("mla_delta", do_mla, o_mla)
    dk, dv, dq, from_sibling = attn_bwd("mla_bwd", q, k, kvn, do_mla, lse_mla, delta_mla, mla_offs, QK_PAD, CHUNK,
                                        mla_scale, F32, comm=hook_swap_sibling(parts_b))

    delta_fox = attn_delta_pairs("fox_delta", fox, fox, fox, do_fox, lse_fox, cum, fox_offs, HEAD_V, 1, fox_scale)

    def chips_behind_fox_bwd(pair):
        dfk, dfv, dfq, dcum, from_chips = attn_bwd("fox_bwd", fox, fox, fox, do_fox, lse_fox, delta_fox, fox_offs,
                                                   HEAD_V, 1, fox_scale, BF16, cum=cum, comm=hook_swap_chips(pair))
        return from_chips, (dfq.astype(BF16), dfk, dfv, dcum)
    g_packed_b, (dfq, dfk, dfv, dcum) = reduce_scatter_tail(parts_b, from_sibling, chips_behind_fox_bwd, "b")
    dff_t, d_bias = fox_cum_bwd("fox_cum_bwd", ff_t, f_bias, dcum.reshape(HEADS, s_len))
    dq_r = rope_heads("rope_q_bwd", dq, inv_tabs, BF16)
    dkvn, dkr = dk_split("dk_split", dk, dv, inv_tabs)
    dcq, = matmul("d_up_q", dq_r, w["w_uq"], "nt", [F32])
    grads["w_uq"], = matmul("gw_uq", cq, dq_r, "tn", [BF16], tm=512, tn=2048, tk=2048)
    dckv, = matmul("d_up_kv", dkvn, w["w_kv"], "nt", [F32])
    grads["w_kv"], = matmul("gw_kv", ckv, dkvn, "tn", [BF16], tm=256, tn=2048, tk=2048)
    dcq_raw, d_qn = rms_bwd("rms_q_bwd", (lat, Q_LORA, 0), dcq, g_q, out_dtype=BF16)
    dckv_raw, d_kvn = rms_bwd("rms_kv_bwd", (lat, KV_LORA, Q_LORA // KV_LORA), dckv, g_kv, out_dtype=BF16)

    pad = jnp.zeros((s_len, LAT_W - OFF_FQ - HEADS), BF16)
    dproj = jnp.concatenate([dcq_raw, dckv_raw, dkr[:, :ROPE].astype(BF16), dff_t.T.astype(BF16), pad,
                             dfq, dfk, dfv, dgraw], axis=1)
    w_in_p = jnp.concatenate([w["w_lat"], w["w_fox"], w["w_gate"]], axis=1)
    dxn, = matmul("d_proj", dproj, w_in_p, "nt", [F32], tk=2688)
    gw_in, = matmul("gw_in", xn, dproj, "tn", [BF16], tm=1024, tn=1152, tk=2048)
    grads["w_lat"], grads["w_fox"], grads["w_gate"] = (gw_in[:, :LAT_W], gw_in[:, LAT_W:LAT_W + FOX_W],
                                                      gw_in[:, LAT_W + FOX_W:])
    dx, d_attn = rms_bwd("rms_attn_bwd", x, dxn, g_attn, dres=dh1)

    small_grads = {"attn_norm": d_attn, "fox_f_bias": d_bias, "q_norm": d_qn, "kv_norm": d_kvn,
                   "mlp_norm": d_mlp, "final_norm": d_final}
    return loss, dx, grads, g_packed_b, small_grads


def kernel(x, attn_norm, w_in, fox_f_bias, q_norm, w_uq, kv_norm, w_ukv, w_mla_branch, w_fox_branch, w_out, mlp_norm, w_up, w_down, final_norm, loss_target, m_attn_norm, m_w_in, m_fox_f_bias, m_q_norm, m_w_uq, m_kv_norm, m_w_ukv, m_w_mla_branch, m_w_fox_branch, m_w_out, m_mlp_norm, m_w_up, m_w_down, m_final_norm, v_attn_norm, v_w_in, v_fox_f_bias, v_q_norm, v_w_uq, v_kv_norm, v_w_ukv, v_w_mla_branch, v_w_fox_branch, v_w_out, v_mlp_norm, v_w_up, v_w_down, v_final_norm):
    given = dict(locals())
    big = {n: given[n][0] for n in BIG}
    small = {n: given[n] for n in SMALL}
    shard_shapes = {n: tuple(big[n].shape) for n in BIG}

    packed_a = pack_local({n: big[n].astype(BF16) for n in GROUP_A}, GROUP_A)
    packed_b = pack_local({n: big[n].astype(BF16) for n in GROUP_B}, GROUP_B)
    gathered_a = comm_allgather("comm_allgather_a", packed_a)
    unpack = functools.partial(unpack_a, shard_shapes=shard_shapes)
    w_a = unpack(gathered_a)

    loss_part, dx, grads_a, g_packed_b, small_grads = local_step(x[0], loss_target[0], w_a, small, packed_b,
                                                                 shard_shapes)

    to_packed_a = jax.linear_transpose(unpack, jax.ShapeDtypeStruct(gathered_a.shape, BF16))
    parts_a, = to_packed_a({n: grads_a[n] for n in w_a})
    from_sibling = comm_swap_sibling("comm_rs_sibling_a", parts_a)
    g_packed_a, _ = reduce_scatter_tail(parts_a, from_sibling,
                                        lambda pair: (comm_swap_chips("comm_rs_chips_a", pair), None), "a")

    small_sum = comm_allreduce_small("comm_allreduce_small", pack_small(small_grads, loss_part[0, 0]))
    g_small, loss = unpack_small(small_sum)

    grad_w, delta_w, new_m, new_v = {}, {}, {}, {}
    for names, g_packed in ((GROUP_A, g_packed_a), (GROUP_B, g_packed_b)):
        r0 = 0
        for n in names:
            rs, cs = shard_shapes[n]
            nr = rs * cs // PACK_C
            g = g_packed[r0:r0 + nr].reshape(rs, cs)
            r0 += nr
            d, m_new, v_new = adamw("adamw_" + n, big[n], g, given["m_" + n][0], given["v_" + n][0])
            grad_w[n], delta_w[n], new_m[n], new_v[n] = g[None], d[None], m_new[None], v_new[None]
    zero = jnp.zeros((), F32)
    d_s, m_s, v_s = adamw("adamw_small", pack_small(small, zero), small_sum * _small_mask(),
                          pack_small({n: given["m_" + n] for n in SMALL}, zero),
                          pack_small({n: given["v_" + n] for n in SMALL}, zero), tr=SMALL_ROWS)
    d_small, _ = unpack_small(d_s)
    m_small, _ = unpack_small(m_s)
    v_small, _ = unpack_small(v_s)
    for n in SMALL:
        shape = given[n].shape
        grad_w[n], delta_w[n] = g_small[n].reshape(shape), d_small[n].reshape(shape)
        new_m[n], new_v[n] = m_small[n].reshape(shape), v_small[n].reshape(shape)

    order = ["attn_norm", "w_in", "fox_f_bias", "q_norm", "w_uq", "kv_norm", "w_ukv", "w_mla_branch", "w_fox_branch",
             "w_out", "mlp_norm", "w_up", "w_down", "final_norm"]
    return (loss, dx[None], *[grad_w[n] for n in order], *[delta_w[n] for n in order],
            *[new_m[n] for n in order], *[new_v[n] for n in order])


def _small_mask():
    used = sum(SMALL_N[n] for n in SMALL)
    return (jnp.arange(SMALL_ROWS * PACK_C) < used).astype(F32).reshape(SMALL_ROWS, PACK_C)
```

```python
import functools
import math

import jax
import jax.numpy as jnp
from jax import lax
from jax.experimental import pallas as pl
from jax.experimental.pallas import tpu as pltpu

F32 = jnp.float32
BF16 = jnp.bfloat16
MESH = pl.DeviceIdType.MESH

D_MODEL = 2048
HEADS = 8
Q_LORA = 512
KV_LORA = 256
NOPE = 128
ROPE = 64
HEAD_V = 128
D_FF = 4 * D_MODEL
CHUNK = 64
EPS = 1e-6
ROPE_THETA = 10000.0
OFF_KR = Q_LORA + KV_LORA
OFF_FQ = OFF_KR + ROPE
OFF_FF = OFF_FQ + 3 * HEADS * HEAD_V
OFF_G = OFF_FF + HEADS
D_IN = OFF_G + 2 * D_MODEL

LAT_W = 896
FOX_W = 3 * HEADS * HEAD_V
GATE_W = 2 * D_MODEL
PROJ_W = LAT_W + FOX_W + GATE_W
QK_PAD = 256

ADAM_LR = 0.001
ADAM_B1 = 0.9
ADAM_B2 = 0.999
ADAM_EPS = 1e-08
ADAM_WD = 0.01
ADAM_STEP = 10

N_DEV = 8
PACK_C = 1024
NEG = -1e30
LOG2E = math.log2(math.e)

VMEM_LIMIT = 56 * 1024 * 1024

BIG = ("w_in", "w_uq", "w_ukv", "w_mla_branch", "w_fox_branch", "w_out", "w_up", "w_down")
GROUP_A = ("w_in", "w_uq", "w_ukv")
GROUP_B = ("w_mla_branch", "w_fox_branch", "w_out", "w_up", "w_down")
ROW_SHARDED = ("w_out", "w_down")
SMALL = ("attn_norm", "fox_f_bias", "q_norm", "kv_norm", "mlp_norm", "final_norm")
SMALL_N = {"attn_norm": D_MODEL, "fox_f_bias": HEADS, "q_norm": Q_LORA, "kv_norm": KV_LORA,
           "mlp_norm": D_MODEL, "final_norm": D_MODEL}
SMALL_ROWS = 8


def _params(sem):
    return pltpu.CompilerParams(dimension_semantics=sem, vmem_limit_bytes=VMEM_LIMIT)


def _rows(name, fn, row_ins, const_ins, outs, reds=(), tr=256):
    norm = [(a, a.shape[1], 0) if not isinstance(a, tuple) else a for a in row_ins]
    n_rows = norm[0][0].shape[0]
    tr = min(tr, n_rows)
    assert n_rows % tr == 0, (name, n_rows, tr)
    n_in, n_out, n_red = len(norm) + len(const_ins), len(outs), len(reds)

    def body(*refs):
        vals = [r[...] for r in refs[:n_in]]
        out_refs = refs[n_in:n_in + n_out]
        red_refs = refs[n_in + n_out:]
        out_vals, red_vals = fn(*vals)
        for r, v in zip(out_refs, out_vals):
            r[...] = v.astype(r.dtype)
        if n_red:
            @pl.when(pl.program_id(0) == 0)
            def _():
                for r in red_refs:
                    r[...] = jnp.zeros_like(r)
            for r, v in zip(red_refs, red_vals):
                r[...] += v

    in_specs = [pl.BlockSpec((tr, w), functools.partial(lambda i, cb: (i, cb), cb=cb)) for _, w, cb in norm]
    in_specs += [pl.BlockSpec(a.shape, lambda i: (0, 0)) for a in const_ins]
    out_specs = [pl.BlockSpec((tr, c), lambda i: (i, 0)) for c, _ in outs]
    out_specs += [pl.BlockSpec((1, c), lambda i: (0, 0)) for c in reds]
    out_shape = [jax.ShapeDtypeStruct((n_rows, c), dt) for c, dt in outs]
    out_shape += [jax.ShapeDtypeStruct((1, c), F32) for c in reds]
    res = pl.pallas_call(
        body, name=name, grid=(n_rows // tr,), in_specs=in_specs, out_specs=out_specs, out_shape=out_shape,
        compiler_params=_params(("arbitrary",)),
    )(*[a for a, _, _ in norm], *const_ins)
    return res


def _rstd(x):
    return lax.rsqrt(jnp.mean(x * x, axis=-1, keepdims=True) + EPS)


def rms_fwd(name, x, gain, tr=256):
    width = x[1] if isinstance(x, tuple) else x.shape[1]

    def fn(xv, g):
        return (xv * _rstd(xv) * g,), ()
    return _rows(name, fn, [x], [gain], [(width, BF16)], tr=tr)[0]


def rms_bwd(name, x, dy, gain, dres=None, out_dtype=F32, tr=256):
    width = x[1] if isinstance(x, tuple) else x.shape[1]

    def fn(xv, dyv, *rest):
        g = rest[-1]
        r = _rstd(xv)
        n = xv * r
        dyv = dyv.astype(F32)
        dn = dyv * g
        dx = r * (dn - n * jnp.mean(dn * n, axis=-1, keepdims=True))
        if dres is not None:
            dx = dx + rest[0]
        return (dx,), (jnp.sum(dyv * n, axis=0, keepdims=True),)

    ins = [x, dy] + ([dres] if dres is not None else [])
    return _rows(name, fn, ins, [gain], [(width, out_dtype)], [width], tr=tr)


def _rope_lanes(t, c, s1, s2):
    return t * c + pltpu.roll(t, 96, 1) * s1 + pltpu.roll(t, 32, 1) * s2


def rope_heads(name, x, tabs, out_dtype):
    def fn(xv, c, s1, s2):
        xv = xv.astype(F32)
        parts = []
        for h in range(HEADS):
            parts.append(xv[:, h * QK_PAD:h * QK_PAD + NOPE])
            parts.append(_rope_lanes(xv[:, h * QK_PAD + NOPE:(h + 1) * QK_PAD], c, s1, s2))
        return (jnp.concatenate(parts, axis=1),), ()
    return _rows(name, fn, [x, *tabs], [], [(HEADS * QK_PAD, out_dtype)])[0]


def rope_block(name, x, col_block, tabs, out_dtype):
    def fn(xv, c, s1, s2):
        return (_rope_lanes(xv.astype(F32), c, s1, s2),), ()
    return _rows(name, fn, [(x, 128, col_block), *tabs], [], [(128, out_dtype)])[0]


def k_assemble(name, kvn, kr):
    def fn(knp, krv):
        parts = []
        for h in range(HEADS):
            parts.append(knp[:, h * NOPE:(h + 1) * NOPE])
            parts.append(krv)
        return (jnp.concatenate(parts, axis=1),), ()
    return _rows(name, fn, [(kvn, HEADS * NOPE, 0), kr], [], [(HEADS * QK_PAD, BF16)])[0]


def dk_split(name, dk, dv, inv_tabs):
    def fn(dkv, dvv, c, s1, s2):
        parts = []
        acc = None
        for h in range(HEADS):
            parts.append(dkv[:, h * QK_PAD:h * QK_PAD + NOPE].astype(BF16))
            t = dkv[:, h * QK_PAD + NOPE:(h + 1) * QK_PAD]
            acc = t if acc is None else acc + t
        parts.append(dvv)
        return (jnp.concatenate(parts, axis=1), _rope_lanes(acc, c, s1, s2)), ()
    return _rows(name, fn, [dk, dv, *inv_tabs], [], [(2 * HEADS * NOPE, BF16), (128, F32)])


def gate_mix(name, graw, y_mla, y_fox):
    def fn(g, ya, yb):
        ga = jax.nn.sigmoid(g[:, :D_MODEL])
        gb = jax.nn.sigmoid(g[:, D_MODEL:])
        return (ga * ya + gb * yb,), ()
    return _rows(name, fn, [graw, y_mla, y_fox], [], [(D_MODEL, BF16)])[0]


def gate_mix_bwd(name, graw, y_mla, y_fox, dmix):
    def fn(g, ya, yb, dm):
        ga = jax.nn.sigmoid(g[:, :D_MODEL])
        gb = jax.nn.sigmoid(g[:, D_MODEL:])
        dgraw = jnp.concatenate([dm * ya * ga * (1.0 - ga), dm * yb * gb * (1.0 - gb)], axis=1)
        return (dgraw, dm * ga, dm * gb), ()
    return _rows(name, fn, [graw, y_mla, y_fox, dmix], [], [(GATE_W, BF16), (D_MODEL, BF16), (D_MODEL, BF16)], tr=128)


def loss_head(name, h2, target, gain):
    inv_d = 1.0 / D_MODEL

    def fn(h, t, g):
        r = _rstd(h)
        n = h * r
        err = n * g - t
        dy = err * inv_d
        dn = dy * g
        dh = r * (dn - n * jnp.mean(dn * n, axis=-1, keepdims=True))
        part = 0.5 * inv_d * jnp.sum(jnp.sum(err * err, axis=1, keepdims=True), axis=0, keepdims=True)
        return (dh,), (jnp.sum(dy * n, axis=0, keepdims=True), jnp.broadcast_to(part, (1, 128)))
    return _rows(name, fn, [h2, target], [gain], [(D_MODEL, F32)], [D_MODEL, 128])


def adamw(name, w, g, m, v, tr=256):
    c1 = 1.0 - ADAM_B1 ** ADAM_STEP
    c2 = 1.0 - ADAM_B2 ** ADAM_STEP

    def fn(wv, gv, mv, vv):
        m_new = ADAM_B1 * mv + (1.0 - ADAM_B1) * gv
        v_new = ADAM_B2 * vv + (1.0 - ADAM_B2) * (gv * gv)
        delta = -ADAM_LR * ((m_new / c1) / (jnp.sqrt(v_new / c2) + ADAM_EPS) + ADAM_WD * wv)
        return (delta, m_new, v_new), ()
    cols = w.shape[1]
    return _rows(name, fn, [w, g, m, v], [], [(cols, F32)] * 3, tr=tr)


def _row_tile(n_rows, cap=640):
    return max(t for t in range(16, cap + 1, 16) if n_rows % t == 0)


def add_pairs(name, a, b):
    def fn(av, bv):
        return (av.astype(F32) + bv.astype(F32),), ()
    return _rows(name, fn, [a, b], [], [(a.shape[1], BF16)], tr=_row_tile(a.shape[0]))[0]


def add_final(name, a, b, r0, r1, r2):
    def fn(av, bv, r0v, r1v, r2v):
        return (((av.astype(F32) + bv.astype(F32)) + r0v.astype(F32)) + r1v.astype(F32) + r2v.astype(F32),), ()
    return _rows(name, fn, [a, b, r0, r1, r2], [], [(a.shape[1], F32)], tr=_row_tile(a.shape[0]))[0]


TN_TILES = dict(tm=1024, tn=1024, tk=2048)
_DIMS = {"nn": (((1,), (0,)), ((), ())), "nt": (((1,), (1,)), ((), ())), "tn": (((0,), (0,)), ((), ()))}


def matmul(name, a, b, mode, outs, epilogue=None, extras=(), tm=1024, tn=1024, tk=2048):
    if mode == "tn":
        kdim, m = a.shape
    else:
        m, kdim = a.shape
    n = b.shape[0] if mode == "nt" else b.shape[1]
    tm, tn, tk = min(tm, m), min(tn, n), min(tk, kdim)
    assert m % tm == 0 and n % tn == 0 and kdim % tk == 0, (name, a.shape, b.shape)
    nk = kdim // tk
    n_ex, n_out = len(extras), len(outs)
    dims = _DIMS[mode]

    def body(a_ref, b_ref, *rest):
        ex_refs = rest[:n_ex]
        out_refs = rest[n_ex:n_ex + n_out]

        def finish(acc):
            vals = (acc,) if epilogue is None else epilogue(acc, *[r[...] for r in ex_refs])
            for r, v in zip(out_refs, vals):
                r[...] = v.astype(r.dtype)

        part = lax.dot_general(a_ref[...].astype(BF16), b_ref[...].astype(BF16), dims, preferred_element_type=F32)
        if nk == 1:
            finish(part)
        else:
            acc_ref = rest[-1]
            k = pl.program_id(2)

            @pl.when(k == 0)
            def _():
                acc_ref[...] = part

            @pl.when(k > 0)
            def _():
                acc_ref[...] += part

            @pl.when(k == nk - 1)
            def _():
                finish(acc_ref[...])

    a_spec = pl.BlockSpec((tk, tm), lambda i, j, k: (k, i)) if mode == "tn" else pl.BlockSpec((tm, tk), lambda i, j, k: (i, k))
    b_spec = pl.BlockSpec((tn, tk), lambda i, j, k: (j, k)) if mode == "nt" else pl.BlockSpec((tk, tn), lambda i, j, k: (k, j))
    tile = pl.BlockSpec((tm, tn), lambda i, j, k: (i, j))
    res = pl.pallas_call(
        body, name=name, grid=(m // tm, n // tn, nk),
        in_specs=[a_spec, b_spec] + [tile] * n_ex,
        out_specs=[tile] * n_out,
        out_shape=[jax.ShapeDtypeStruct((m, n), dt) for dt in outs],
        scratch_shapes=[pltpu.VMEM((tm, tn), F32)] if nk > 1 else [],
        compiler_params=_params(("parallel", "parallel", "arbitrary")),
    )(a, b, *extras)
    return res


_NT = (((1,), (1,)), ((), ()))
_NN = (((1,), (0,)), ((), ()))


def _mask(bq, chunk, transposed, row0=0, shape=None):
    shape = (bq, bq) if shape is None else shape
    row = lax.broadcasted_iota(jnp.int32, shape, 0) + row0
    col = lax.broadcasted_iota(jnp.int32, shape, 1)
    if chunk > 1:
        row, col = row // chunk, col // chunk
    return (row <= col) if transposed else (col <= row)


def _row_layout(a, bq):
    h, s, _ = a.shape
    return a.reshape(h, s // bq, 1, bq)


def _split_refs(refs, n_in, comm, n_out, n_scr):
    n_ci = len(comm.args) if comm else 0
    n_co = len(comm.out_shapes) if comm else 0
    cuts = [n_in, n_ci, n_out, n_co, n_scr, 3 if comm else 0]
    parts, at = [], 0
    for n in cuts:
        parts.append(list(refs[at:at + n]))
        at += n
    assert at == len(refs), (at, len(refs))
    return parts


def _comm_start(comm, c_in, c_out, c_sems, n0, n1):
    if comm is not None:
        @pl.when(jnp.logical_and(pl.program_id(0) == 0, pl.program_id(1) == 0))
        def _():
            comm.start(c_in, c_out, c_sems)


def _comm_wait(comm, c_in, c_out, c_sems, n0, n1):
    if comm is not None:
        @pl.when(jnp.logical_and(pl.program_id(0) == n0 - 1, pl.program_id(1) == n1 - 1))
        def _():
            comm.wait(c_in, c_out, c_sems)


def attn_fwd(name, q, k, v, offs, dqk, chunk, scale, cum=None, bq=512, comm=None, split_p=False):
    s_len = q.shape[0]
    nq = s_len // bq
    qoff, koff, voff = offs
    has_bias = cum is not None
    scale2 = scale * LOG2E

    n_in = 5 if has_bias else 3

    def body(*refs):
        ins, c_in, outs, c_out, scr, c_sems = _split_refs(refs, n_in, comm, 2, 3)
        q_ref, k_ref, v_ref = ins[:3]
        if has_bias:
            cc_ref, cr_ref = ins[3:]
        o_ref, lse_ref = outs
        m_s, l_s, acc_s = scr
        _comm_start(comm, c_in, c_out, c_sems, HEADS, nq)
        i = pl.program_id(1)
        qv = q_ref[...]
        m_s[...] = jnp.full_like(m_s, NEG)
        l_s[...] = jnp.zeros_like(l_s)
        acc_s[...] = jnp.zeros_like(acc_s)

        def step(j, masked):
            off = pl.multiple_of(j * bq, bq)
            kj = k_ref[pl.ds(off, bq), :]
            vj = v_ref[pl.ds(off, bq), :]
            st = lax.dot_general(kj, qv, _NT, preferred_element_type=F32) * scale2
            if has_bias:
                st = st + cr_ref[...] - cc_ref[pl.ds(off, bq), :]
            if masked:
                st = jnp.where(_mask(bq, chunk, True), st, NEG)
            m_prev = m_s[...]
            m_new = jnp.maximum(m_prev, jnp.max(st, axis=0, keepdims=True))
            alpha = jnp.exp2(m_prev - m_new)
            pt = jnp.exp2(st - m_new)
            l_s[...] = alpha * l_s[...] + jnp.sum(pt, axis=0, keepdims=True)
            p_hi = pt.astype(BF16)
            pv = lax.dot_general(vj, p_hi, _DIMS["tn"], preferred_element_type=F32)
            if split_p:
                p_lo = (pt - p_hi.astype(F32)).astype(BF16)
                pv = pv + lax.dot_general(vj, p_lo, _DIMS["tn"], preferred_element_type=F32)
            acc_s[...] = alpha * acc_s[...] + pv
            m_s[...] = m_new

        def pair_body(jj, carry):
            step(2 * jj, False)
            step(2 * jj + 1, False)
            return carry

        lax.fori_loop(0, i // 2, pair_body, 0)

        @pl.when(i % 2 == 1)
        def _():
            step(i - 1, False)

        step(i, True)
        o_ref[...] = (acc_s[...] / l_s[...]).T.astype(o_ref.dtype)
        lse_ref[...] = m_s[...] + jnp.log2(l_s[...])
        _comm_wait(comm, c_in, c_out, c_sems, HEADS, nq)

    in_specs = [
        pl.BlockSpec((bq, dqk), lambda h, i: (i, qoff + h)),
        pl.BlockSpec((s_len, dqk), lambda h, i: (0, koff + h)),
        pl.BlockSpec((s_len, HEAD_V), lambda h, i: (0, voff + h)),
    ]
    args = [q, k, v]
    if has_bias:
        in_specs += [pl.BlockSpec((None, s_len, 1), lambda h, i: (h, 0, 0)),
                     pl.BlockSpec((None, None, 1, bq), lambda h, i: (h, i, 0, 0))]
        args += [cum, _row_layout(cum, bq)]
    c_ins, c_outs, c_shapes, c_scratch, aliases = _with_comm(comm, len(args), 2)
    o, lse_rows, *comm_out = pl.pallas_call(
        body, name=name, grid=(HEADS, nq), in_specs=in_specs + c_ins,
        out_specs=[pl.BlockSpec((bq, HEAD_V), lambda h, i: (i, h)),
                   pl.BlockSpec((None, None, 1, bq), lambda h, i: (h, i, 0, 0))] + c_outs,
        out_shape=[jax.ShapeDtypeStruct((s_len, HEADS * HEAD_V), F32),
                   jax.ShapeDtypeStruct((HEADS, nq, 1, bq), F32)] + c_shapes,
        scratch_shapes=[pltpu.VMEM((1, bq), F32), pltpu.VMEM((1, bq), F32), pltpu.VMEM((HEAD_V, bq), F32)] + c_scratch,
        input_output_aliases=aliases,
        compiler_params=_params(("arbitrary", "arbitrary")),
    )(*args, *(comm.args if comm else []))
    return (o, lse_rows, *comm_out)


def _bwd_block(kv, vv, qi, doi, lse_row, scale2, bias, masked, bq, chunk):
    st = lax.dot_general(kv, qi, _NT, preferred_element_type=F32) * scale2
    if bias is not None:
        st = st + bias[0] - bias[1]
    if masked:
        st = jnp.where(_mask(bq, chunk, True), st, NEG)
    pt = jnp.exp2(st - lse_row)
    dpt = lax.dot_general(vv, doi, _NT, preferred_element_type=F32)
    return pt, dpt


def attn_delta(name, do, o, bq=512):
    s_len = do.shape[0]
    nq = s_len // bq

    def body(do_ref, o_ref, out_ref):
        prod = do_ref[...].astype(F32) * o_ref[...].astype(F32)
        ones = jnp.ones((8, HEAD_V), BF16)
        total = None
        for part in reversed(_split3(prod)):
            term = lax.dot_general(ones, part, _NT, preferred_element_type=F32)
            total = term if total is None else total + term
        out_ref[...] = total[0:1, :]

    blk = pl.BlockSpec((bq, HEAD_V), lambda h, i: (i, h))
    return pl.pallas_call(
        body, name=name, grid=(HEADS, nq), in_specs=[blk, blk],
        out_specs=pl.BlockSpec((None, None, 1, bq), lambda h, i: (h, i, 0, 0)),
        out_shape=jax.ShapeDtypeStruct((HEADS, nq, 1, bq), F32),
        compiler_params=_params(("parallel", "parallel")),
    )(do, o)


def attn_bwd(name, q, k, v, do, lse, delta, offs, dqk, chunk, scale, out_dtype, cum=None, bq=512, comm=None):
    s_len = q.shape[0]
    nq = s_len // bq
    qoff, koff, voff = offs
    has_bias = cum is not None
    scale2 = scale * LOG2E
    n_in, n_out = (8, 4) if has_bias else (6, 3)

    def body(*refs):
        ins, c_in, outs, c_out, scr, c_sems = _split_refs(refs, n_in, comm, n_out, n_out - 1)
        k_ref, v_ref, q_ref, do_ref, lse_ref, delta_ref = ins[:6]
        dk_ref, dv_ref, dq_ref = outs[:3]
        dk_s, dv_s = scr[:2]
        if has_bias:
            cc_ref, cr_ref = ins[6:]
            dc_ref, dc_s = outs[3], scr[2]
        _comm_start(comm, c_in, c_out, c_sems, HEADS, nq)
        j = pl.program_id(1)
        kv = k_ref[...]
        vv = v_ref[...]
        dk_s[...] = jnp.zeros_like(dk_s)
        dv_s[...] = jnp.zeros_like(dv_s)
        if has_bias:
            dc_s[...] = jnp.zeros_like(dc_s)

        @pl.when(j == 0)
        def _():
            dq_ref[...] = jnp.zeros_like(dq_ref)

        def step(i, masked):
            off = pl.multiple_of(i * bq, bq)
            qi = q_ref[pl.ds(off, bq), :]
            doi = do_ref[pl.ds(off, bq), :].astype(BF16)
            bias = (cr_ref[i], cc_ref[...]) if has_bias else None
            pt, dpt = _bwd_block(kv, vv, qi, doi, lse_ref[i], scale2, bias, masked, bq, chunk)
            dv_s[...] += lax.dot_general(pt.astype(BF16), doi, _NN, preferred_element_type=F32)
            dst = pt * (dpt - delta_ref[i])
            if has_bias:
                dc_s[...] -= jnp.sum(dst, axis=-1, keepdims=True)
            dst = dst.astype(BF16)
            dk_s[...] += lax.dot_general(dst, qi, _NN, preferred_element_type=F32)
            dq_ref[pl.ds(off, bq), :] += lax.dot_general(dst, kv, _DIMS["tn"], preferred_element_type=F32) * scale

        step(j, True)

        def loop_body(i, carry):
            step(i, False)
            return carry

        lax.fori_loop(j + 1, nq, loop_body, 0)
        dk_ref[...] = (dk_s[...] * scale).astype(dk_ref.dtype)
        dv_ref[...] = dv_s[...].astype(dv_ref.dtype)
        if has_bias:
            dc_ref[...] = dc_s[...]
        _comm_wait(comm, c_in, c_out, c_sems, HEADS, nq)

    rows = pl.BlockSpec((None, nq, 1, bq), lambda h, j: (h, 0, 0, 0))
    in_specs = [
        pl.BlockSpec((bq, dqk), lambda h, j: (j, koff + h)),
        pl.BlockSpec((bq, HEAD_V), lambda h, j: (j, voff + h)),
        pl.BlockSpec((s_len, dqk), lambda h, j: (0, qoff + h)),
        pl.BlockSpec((s_len, HEAD_V), lambda h, j: (0, h)),
        rows,
        rows,
    ]
    args = [k, v, q, do, lse, delta]
    out_specs = [pl.BlockSpec((bq, dqk), lambda h, j: (j, h)), pl.BlockSpec((bq, HEAD_V), lambda h, j: (j, h)),
                 pl.BlockSpec((s_len, dqk), lambda h, j: (0, h))]
    out_shape = [jax.ShapeDtypeStruct((s_len, HEADS * dqk), out_dtype),
                 jax.ShapeDtypeStruct((s_len, HEADS * HEAD_V), BF16),
                 jax.ShapeDtypeStruct((s_len, HEADS * dqk), F32)]
    scratch = [pltpu.VMEM((bq, dqk), F32), pltpu.VMEM((bq, HEAD_V), F32)]
    if has_bias:
        in_specs += [pl.BlockSpec((None, bq, 1), lambda h, j: (h, j, 0)), rows]
        args += [cum, _row_layout(cum, bq)]
        out_specs.append(pl.BlockSpec((None, bq, 1), lambda h, j: (h, j, 0)))
        out_shape.append(jax.ShapeDtypeStruct((HEADS, s_len, 1), F32))
        scratch.append(pltpu.VMEM((bq, 1), F32))
    c_ins, c_outs, c_shapes, c_scratch, aliases = _with_comm(comm, len(args), n_out)
    return pl.pallas_call(
        body, name=name, grid=(HEADS, nq), in_specs=in_specs + c_ins, out_specs=out_specs + c_outs,
        out_shape=out_shape + c_shapes, scratch_shapes=scratch + c_scratch, input_output_aliases=aliases,
        compiler_params=_params(("arbitrary", "arbitrary")),
    )(*args, *(comm.args if comm else []))


_CUM_BLK = 512


def _split3(x):
    hi = x.astype(BF16)
    r1 = x - hi.astype(F32)
    mid = r1.astype(BF16)
    lo = (r1 - mid.astype(F32)).astype(BF16)
    return hi, mid, lo


def _tri_dot(x, tri):
    hi, mid, lo = _split3(x)
    out = lax.dot_general(lo, tri, _NN, preferred_element_type=F32)
    out = out + lax.dot_general(mid, tri, _NN, preferred_element_type=F32)
    return out + lax.dot_general(hi, tri, _NN, preferred_element_type=F32)


def fox_cum_fwd(name, ff_t, bias):
    s_len = ff_t.shape[1]
    nb = s_len // _CUM_BLK

    def body(ff_ref, b_ref, cum_ref):
        row = lax.broadcasted_iota(jnp.int32, (_CUM_BLK, _CUM_BLK), 0)
        col = lax.broadcasted_iota(jnp.int32, (_CUM_BLK, _CUM_BLK), 1)
        tri = (row <= col).astype(BF16)
        carry = jnp.zeros((HEADS, 1), F32)
        for b in range(nb):
            z = ff_ref[:, b * _CUM_BLK:(b + 1) * _CUM_BLK] + b_ref[...]
            logf = jnp.minimum(z, 0.0) - jnp.log1p(jnp.exp(-jnp.abs(z)))
            blk = _tri_dot(logf, tri) + carry
            cum_ref[:, b * _CUM_BLK:(b + 1) * _CUM_BLK] = blk
            carry = blk[:, _CUM_BLK - 1:_CUM_BLK]

    return pl.pallas_call(
        body, name=name, out_shape=jax.ShapeDtypeStruct((HEADS, s_len), F32),
        compiler_params=pltpu.CompilerParams(vmem_limit_bytes=VMEM_LIMIT),
    )(ff_t, bias)


def fox_cum_bwd(name, ff_t, bias, dcum):
    s_len = ff_t.shape[1]
    nb = s_len // _CUM_BLK

    def body(ff_ref, b_ref, dc_ref, dff_ref, db_ref):
        row = lax.broadcasted_iota(jnp.int32, (_CUM_BLK, _CUM_BLK), 0)
        col = lax.broadcasted_iota(jnp.int32, (_CUM_BLK, _CUM_BLK), 1)
        tri = (row >= col).astype(BF16)
        carry = jnp.zeros((HEADS, 1), F32)
        dbias = jnp.zeros((HEADS, 1), F32)
        for b in reversed(range(nb)):
            sl = slice(b * _CUM_BLK, (b + 1) * _CUM_BLK)
            dlogf = _tri_dot(dc_ref[:, sl], tri) + carry
            carry = dlogf[:, 0:1]
            z = ff_ref[:, sl] + b_ref[...]
            dz = dlogf / (1.0 + jnp.exp(z))
            dff_ref[:, sl] = dz
            dbias = dbias + jnp.sum(dz, axis=-1, keepdims=True)
        db_ref[...] = dbias

    return pl.pallas_call(
        body, name=name,
        out_shape=[jax.ShapeDtypeStruct((HEADS, s_len), F32), jax.ShapeDtypeStruct((HEADS, 1), F32)],
        compiler_params=pltpu.CompilerParams(vmem_limit_bytes=VMEM_LIMIT),
    )(ff_t, bias, dcum)


_ANY = pl.BlockSpec(memory_space=pl.ANY)


def _me():
    return lax.axis_index("x"), lax.axis_index("y"), lax.axis_index("c")


def comm_allgather(name, mine):
    n_rows, n_cols = mine.shape

    def body(x_ref, out_ref, send_sems, recv_sems, local_sem):
        x, y, c = _me()
        sibling = (x, y, 1 - c)
        chips = [(1 - x, y), (x, 1 - y), (1 - x, 1 - y)]

        def blk(px, py, pc):
            return out_ref.at[pc * 4 + px * 2 + py]

        def copy(k, block, to, src=None):
            return pltpu.make_async_remote_copy(
                src_ref=blk(*block) if src is None else src, dst_ref=blk(*block),
                send_sem=send_sems.at[k], recv_sem=recv_sems.at[k], device_id=to, device_id_type=MESH)

        own = pltpu.make_async_copy(x_ref, blk(x, y, c), local_sem)
        own.start()
        first = [copy(0, (x, y, c), sibling, src=x_ref)]
        first += [copy(1 + j, (x, y, c), (*chip, c), src=x_ref) for j, chip in enumerate(chips)]
        for cp in first:
            cp.start()
        passed = [copy(4 + j, (*chip, c), sibling) for j, chip in enumerate(chips)]
        for j, chip in enumerate(chips):
            copy(1 + j, (*chip, c), (x, y, c)).wait_recv()
            passed[j].start()
        copy(0, sibling, (x, y, c)).wait_recv()
        for j, chip in enumerate(chips):
            copy(4 + j, (*chip, 1 - c), (x, y, c)).wait_recv()
        for cp in first + passed:
            cp.wait_send()
        own.wait()

    return pl.pallas_call(
        body, name=name, out_shape=jax.ShapeDtypeStruct((N_DEV, n_rows, n_cols), mine.dtype),
        in_specs=[_ANY], out_specs=_ANY,
        scratch_shapes=[pltpu.SemaphoreType.DMA((7,)), pltpu.SemaphoreType.DMA((7,)), pltpu.SemaphoreType.DMA],
    )(mine)


def comm_swap_sibling(name, parts):
    _, n_rows, n_cols = parts.shape

    def body(p_ref, got_ref, send_sem, recv_sem):
        x, y, c = _me()
        cp = pltpu.make_async_remote_copy(
            src_ref=p_ref.at[pl.ds((1 - c) * 4, 4)], dst_ref=got_ref, send_sem=send_sem, recv_sem=recv_sem,
            device_id=(x, y, 1 - c), device_id_type=MESH)
        cp.start()
        cp.wait()

    return pl.pallas_call(
        body, name=name, out_shape=jax.ShapeDtypeStruct((4, n_rows, n_cols), parts.dtype),
        in_specs=[_ANY], out_specs=_ANY,
        scratch_shapes=[pltpu.SemaphoreType.DMA, pltpu.SemaphoreType.DMA],
    )(parts)


def comm_swap_chips(name, parts):
    _, n_rows, n_cols = parts.shape

    def body(p_ref, got_ref, send_sems, recv_sems):
        x, y, c = _me()
        chips = [(1 - x, y), (x, 1 - y), (1 - x, 1 - y)]
        cps = [pltpu.make_async_remote_copy(
            src_ref=p_ref.at[2 * px + py], dst_ref=got_ref.at[k], send_sem=send_sems.at[k], recv_sem=recv_sems.at[k],
            device_id=(px, py, c), device_id_type=MESH) for k, (px, py) in enumerate(chips)]
        for cp in cps:
            cp.start()
        for cp in cps:
            cp.wait()

    return pl.pallas_call(
        body, name=name, out_shape=jax.ShapeDtypeStruct((3, n_rows, n_cols), parts.dtype),
        in_specs=[_ANY], out_specs=_ANY,
        scratch_shapes=[pltpu.SemaphoreType.DMA((3,)), pltpu.SemaphoreType.DMA((3,))],
    )(parts)


class CommHook:
    def __init__(self, args, out_shapes, n_copies, copies, aliases=None):
        self.args, self.out_shapes, self.n_copies, self.copies = list(args), list(out_shapes), n_copies, copies
        self.aliases = aliases or {}

    def scratch(self):
        return [pltpu.SemaphoreType.DMA((self.n_copies,)), pltpu.SemaphoreType.DMA((self.n_copies,)),
                pltpu.SemaphoreType.DMA((1,))]

    def start(self, in_refs, out_refs, sems):
        sends, _, locs = self.copies(in_refs, out_refs, *sems)
        for cp in locs() + sends():
            cp.start()

    def wait(self, in_refs, out_refs, sems):
        sends, recvs, locs = self.copies(in_refs, out_refs, *sems)
        for cp in sends():
            cp.wait_send()
        for cp in recvs():
            cp.wait_recv()
        for cp in locs():
            cp.wait()


def _remote(src, dst, send_sem, recv_sem, to):
    return pltpu.make_async_remote_copy(src_ref=src, dst_ref=dst, send_sem=send_sem, recv_sem=recv_sem,
                                        device_id=to, device_id_type=MESH)


def hook_gather_first(mine):
    n_rows, n_cols = mine.shape

    def copies(ins, outs, send, recv, local):
        (x_ref,), (out_ref,) = ins, outs
        x, y, c = _me()
        me = c * 4 + x * 2 + y
        peers = [(x, y, 1 - c), (1 - x, y, c), (x, 1 - y, c), (1 - x, 1 - y, c)]

        def sends():
            return [_remote(x_ref, out_ref.at[me], send.at[k], recv.at[k], p) for k, p in enumerate(peers)]

        def recvs():
            return [_remote(x_ref, out_ref.at[pc * 4 + px * 2 + py], send.at[k], recv.at[k], (px, py, pc))
                    for k, (px, py, pc) in enumerate(peers)]

        return sends, recvs, lambda: [pltpu.make_async_copy(x_ref, out_ref.at[me], local.at[0])]

    return CommHook([mine], [jax.ShapeDtypeStruct((N_DEV, n_rows, n_cols), mine.dtype)], 4, copies)


def hook_gather_second(gathered):
    def copies(ins, outs, send, recv, local):
        (g_in,), (g_out,) = ins, outs
        x, y, c = _me()
        chips = [(1 - x, y), (x, 1 - y), (1 - x, 1 - y)]

        def sends():
            return [_remote(g_in.at[c * 4 + px * 2 + py], g_out.at[c * 4 + px * 2 + py], send.at[k], recv.at[k],
                            (x, y, 1 - c)) for k, (px, py) in enumerate(chips)]

        def recvs():
            return [_remote(g_in.at[(1 - c) * 4 + px * 2 + py], g_out.at[(1 - c) * 4 + px * 2 + py], send.at[k],
                            recv.at[k], (x, y, 1 - c)) for k, (px, py) in enumerate(chips)]

        return sends, recvs, lambda: []

    return CommHook([gathered], [jax.ShapeDtypeStruct(gathered.shape, gathered.dtype)], 3, copies, aliases={0: 0})


def hook_swap_sibling(parts):
    _, n_rows, n_cols = parts.shape

    def copies(ins, outs, send, recv, local):
        (p_ref,), (got_ref,) = ins, outs
        x, y, c = _me()

        def swap():
            return [_remote(p_ref.at[pl.ds((1 - c) * 4, 4)], got_ref, send.at[0], recv.at[0], (x, y, 1 - c))]

        return swap, swap, lambda: []

    return CommHook([parts], [jax.ShapeDtypeStruct((4, n_rows, n_cols), parts.dtype)], 1, copies)


def hook_swap_chips(parts):
    _, n_rows, n_cols = parts.shape

    def copies(ins, outs, send, recv, local):
        (p_ref,), (got_ref,) = ins, outs
        x, y, c = _me()
        chips = [(1 - x, y), (x, 1 - y), (1 - x, 1 - y)]

        def swaps():
            return [_remote(p_ref.at[2 * px + py], got_ref.at[k], send.at[k], recv.at[k], (px, py, c))
                    for k, (px, py) in enumerate(chips)]

        return swaps, swaps, lambda: []

    return CommHook([parts], [jax.ShapeDtypeStruct((3, n_rows, n_cols), parts.dtype)], 3, copies)


def _with_comm(comm, n_args, n_outs):
    if comm is None:
        return [], [], [], [], {}
    aliases = {n_args + a: n_outs + o for a, o in comm.aliases.items()}
    return [_ANY] * len(comm.args), [_ANY] * len(comm.out_shapes), comm.out_shapes, comm.scratch(), aliases


def comm_allreduce_small(name, mine):
    shape = mine.shape

    def body(x_ref, out_ref, buf, send_sems, recv_sems):
        x, y, c = _me()
        my_slot = c * 4 + x * 2 + y
        buf[my_slot] = x_ref[...]
        cps = []
        for k in range(1, N_DEV):
            dx, dy, dc = (k >> 2) & 1, (k >> 1) & 1, k & 1
            px, py, pc = x ^ dx, y ^ dy, c ^ dc
            send = pltpu.make_async_remote_copy(
                src_ref=x_ref, dst_ref=buf.at[my_slot], send_sem=send_sems.at[k - 1], recv_sem=recv_sems.at[k - 1],
                device_id=(px, py, pc), device_id_type=MESH)
            send.start()
            recv = pltpu.make_async_remote_copy(
                src_ref=x_ref, dst_ref=buf.at[pc * 4 + px * 2 + py], send_sem=send_sems.at[k - 1],
                recv_sem=recv_sems.at[k - 1], device_id=(px, py, pc), device_id_type=MESH)
            cps.append((send, recv))
        for send, recv in cps:
            send.wait_send()
            recv.wait_recv()
        total = buf[0]
        for s in range(1, N_DEV):
            total = total + buf[s]
        out_ref[...] = total

    vmem = pl.BlockSpec(memory_space=pltpu.VMEM)
    return pl.pallas_call(
        body, name=name, out_shape=jax.ShapeDtypeStruct(shape, F32), in_specs=[vmem], out_specs=vmem,
        scratch_shapes=[pltpu.VMEM((N_DEV,) + shape, F32), pltpu.SemaphoreType.DMA((7,)), pltpu.SemaphoreType.DMA((7,))],
    )(mine)


def pack_local(shards, names):
    flat = [shards[n].reshape(-1, PACK_C) for n in names]
    rows = sum(f.shape[0] for f in flat)
    pad = (-rows) % 128
    return jnp.concatenate(flat + [jnp.zeros((pad, PACK_C), flat[0].dtype)], axis=0)


def unpack_group(gathered, names, shard_shapes):
    _, n_rows, _ = gathered.shape
    by_block = gathered.reshape(2, 4, n_rows, PACK_C).transpose(1, 0, 2, 3).reshape(N_DEV, n_rows, PACK_C)
    full = {}
    r0 = 0
    for name in names:
        rs, cs = shard_shapes[name]
        nr = rs * cs // PACK_C
        piece = by_block[:, r0:r0 + nr, :].reshape(N_DEV, rs, cs)
        r0 += nr
        if name in ROW_SHARDED:
            full[name] = piece.reshape(N_DEV * rs, cs)
        else:
            full[name] = piece.transpose(1, 0, 2).reshape(rs, N_DEV * cs)
    return full


def unpack_a(gathered, shard_shapes):
    full = unpack_group(gathered, GROUP_A, shard_shapes)
    w_in = full.pop("w_in")
    zeros = jnp.zeros((D_MODEL, LAT_W - OFF_FQ - HEADS), w_in.dtype)
    full["w_lat"] = jnp.concatenate([w_in[:, :OFF_FQ], w_in[:, OFF_FF:OFF_G], zeros], axis=1)
    full["w_fox"] = w_in[:, OFF_FQ:OFF_FF]
    full["w_gate"] = w_in[:, OFF_G:]
    w_uq = full.pop("w_uq").reshape(Q_LORA, HEADS, NOPE + ROPE)
    full["w_uq"] = jnp.pad(w_uq, ((0, 0), (0, 0), (0, QK_PAD - NOPE - ROPE))).reshape(Q_LORA, HEADS * QK_PAD)
    w_ukv = full.pop("w_ukv").reshape(KV_LORA, HEADS, 2, NOPE)
    full["w_kv"] = jnp.concatenate([w_ukv[:, :, 0, :].reshape(KV_LORA, HEADS * NOPE),
                                    w_ukv[:, :, 1, :].reshape(KV_LORA, HEADS * HEAD_V)], axis=1)
    return full


def pack_small(vals, loss):
    flat = [vals[n].reshape(-1) for n in SMALL] + [loss.reshape(-1)]
    used = sum(f.shape[0] for f in flat)
    flat.append(jnp.zeros((SMALL_ROWS * PACK_C - used,), F32))
    return jnp.concatenate(flat).reshape(SMALL_ROWS, PACK_C)


def unpack_small(packed):
    flat = packed.reshape(-1)
    out, off = {}, 0
    for n in SMALL:
        out[n] = flat[off:off + SMALL_N[n]]
        off += SMALL_N[n]
    return out, flat[off]


def rope_tables(s_len):
    pos = jnp.arange(s_len, dtype=F32)
    inv = 1.0 / (ROPE_THETA ** (jnp.arange(0, ROPE, 2, dtype=F32) / ROPE))
    ang = pos[:, None] * inv[None, :]
    cos, sin = jnp.cos(ang), jnp.sin(ang)
    zero = jnp.zeros_like(cos)
    c = jnp.concatenate([cos, cos, zero, zero], axis=1)
    s1 = jnp.concatenate([-sin, zero, zero, zero], axis=1)
    s2 = jnp.concatenate([zero, sin, zero, zero], axis=1)
    return (c, s1, s2), (c, -s1, -s2)


def reduce_scatter_tail(parts, from_sibling, from_chips_fn, names):
    cx, cy, cc = _me()
    n_rows = parts.shape[1]
    mine4 = lax.dynamic_slice_in_dim(parts, cc * 4, 4, axis=0)
    pair = add_pairs("rs_pair_sum_" + names, mine4.reshape(4 * n_rows, PACK_C), from_sibling.reshape(4 * n_rows, PACK_C))
    from_chips, extra = from_chips_fn(pair.reshape(4, n_rows, PACK_C))
    own = cx * 2 + cy
    total = add_final("rs_final_sum_" + names, lax.dynamic_index_in_dim(mine4, own, 0, keepdims=False),
                      lax.dynamic_index_in_dim(from_sibling, own, 0, keepdims=False),
                      from_chips[0], from_chips[1], from_chips[2])
    return total, extra


def local_step(x, target, w, small, packed_b, shard_shapes):
    s_len = x.shape[0]
    tabs, inv_tabs = rope_tables(s_len)
    g_attn = small["attn_norm"].reshape(1, D_MODEL)
    g_q = small["q_norm"].reshape(1, Q_LORA)
    g_kv = small["kv_norm"].reshape(1, KV_LORA)
    g_mlp = small["mlp_norm"].reshape(1, D_MODEL)
    g_final = small["final_norm"].reshape(1, D_MODEL)
    f_bias = small["fox_f_bias"].reshape(HEADS, 1)
    mla_scale = 1.0 / math.sqrt(NOPE + ROPE)
    fox_scale = 1.0 / math.sqrt(HEAD_V)
    mla_offs = (0, 0, HEADS)
    fox_offs = (0, HEADS, 2 * HEADS)

    xn = rms_fwd("rms_attn", x, g_attn)
    lat, = matmul("proj_lat", xn, w["w_lat"], "nn", [F32])
    fox, = matmul("proj_fox", xn, w["w_fox"], "nn", [BF16])
    graw, = matmul("proj_gate", xn, w["w_gate"], "nn", [F32])
    cq = rms_fwd("rms_q", (lat, Q_LORA, 0), g_q)
    ckv = rms_fwd("rms_kv", (lat, KV_LORA, Q_LORA // KV_LORA), g_kv)
    qraw, = matmul("up_q", cq, w["w_uq"], "nn", [F32])
    q = rope_heads("rope_q", qraw, tabs, BF16)
    kvn, = matmul("up_kv", ckv, w["w_kv"], "nn", [BF16])
    kr = rope_block("rope_k", lat, OFF_KR // 128, tabs, BF16)
    k = k_assemble("k_assemble", kvn, kr)
    o_mla, lse_mla, gathered_b = attn_fwd("mla_fwd", q, k, kvn, mla_offs, QK_PAD, CHUNK, mla_scale,
                                          comm=hook_gather_first(packed_b))
    ff_t = lat[:, OFF_FQ:OFF_FQ + HEADS].T
    cum = fox_cum_fwd("fox_cum", ff_t, f_bias).reshape(HEADS, s_len, 1) * LOG2E
    o_fox, lse_fox, gathered_b = attn_fwd("fox_fwd", fox, fox, fox, fox_offs, HEAD_V, 1, fox_scale, cum=cum,
                                          comm=hook_gather_second(gathered_b), split_p=True)
    unpack_b = functools.partial(unpack_group, names=GROUP_B, shard_shapes=shard_shapes)
    w = {**w, **unpack_b(gathered_b)}
    y_mla, = matmul("branch_mla", o_mla, w["w_mla_branch"], "nn", [F32])
    y_fox, = matmul("branch_fox", o_fox, w["w_fox_branch"], "nn", [F32])
    mix = gate_mix("gate_mix", graw, y_mla, y_fox)
    h1, = matmul("out_proj", mix, w["w_out"], "nn", [F32], epilogue=lambda acc, res: (res + acc,), extras=[x])
    hn = rms_fwd("rms_mlp", h1, g_mlp)

    def relu2(acc):
        r = jnp.maximum(acc, 0.0)
        return r * r, r
    u, relu_up = matmul("mlp_up", hn, w["w_up"], "nn", [BF16, BF16], epilogue=relu2)
    h2, = matmul("mlp_down", u, w["w_down"], "nn", [F32], epilogue=lambda acc, res: (res + acc,), extras=[h1])
    dh2, d_final, loss = loss_head("loss_head", h2, target, g_final)

    grads = {}
    dup, = matmul("d_mlp_down", dh2, w["w_down"], "nt", [BF16],
                  epilogue=lambda acc, r: (acc * (2.0 * r.astype(F32)),), extras=[relu_up])
    grads["w_down"], = matmul("gw_down", u, dh2, "tn", [BF16], **TN_TILES)
    dhn, = matmul("d_mlp_up", dup, w["w_up"], "nt", [F32])
    grads["w_up"], = matmul("gw_up", hn, dup, "tn", [BF16], **TN_TILES)
    dh1, d_mlp = rms_bwd("rms_mlp_bwd", h1, dhn, g_mlp, dres=dh2)
    dmix, = matmul("d_out_proj", dh1, w["w_out"], "nt", [F32])
    grads["w_out"], = matmul("gw_out", mix, dh1, "tn", [BF16], **TN_TILES)
    dgraw, dy_mla, dy_fox = gate_mix_bwd("gate_mix_bwd", graw, y_mla, y_fox, dmix)
    do_mla, = matmul("d_branch_mla", dy_mla, w["w_mla_branch"], "nt", [BF16])
    grads["w_mla_branch"], = matmul("gw_branch_mla", o_mla, dy_mla, "tn", [BF16], **TN_TILES)
    do_fox, = matmul("d_branch_fox", dy_fox, w["w_fox_branch"], "nt", [BF16])
    grads["w_fox_branch"], = matmul("gw_branch_fox", o_fox, dy_fox, "tn", [BF16], **TN_TILES)

    to_packed_b = jax.linear_transpose(unpack_b, jax.ShapeDtypeStruct(gathered_b.shape, BF16))
    parts_b, = to_packed_b({n: grads.pop(n) for n in GROUP_B})
    delta_mla = attn_delta("mla_delta", do_mla, o_mla)
    dk, dv, dq, from_sibling = attn_bwd("mla_bwd", q, k, kvn, do_mla, lse_mla, delta_mla, mla_offs, QK_PAD, CHUNK,
                                        mla_scale, F32, comm=hook_swap_sibling(parts_b))

    delta_fox = attn_delta("fox_delta", do_fox, o_fox)

    def chips_behind_fox_bwd(pair):
        dfk, dfv, dfq, dcum, from_chips = attn_bwd("fox_bwd", fox, fox, fox, do_fox, lse_fox, delta_fox, fox_offs,
                                                   HEAD_V, 1, fox_scale, BF16, cum=cum, comm=hook_swap_chips(pair))
        return from_chips, (dfq.astype(BF16), dfk, dfv, dcum)
    g_packed_b, (dfq, dfk, dfv, dcum) = reduce_scatter_tail(parts_b, from_sibling, chips_behind_fox_bwd, "b")
    dff_t, d_bias = fox_cum_bwd("fox_cum_bwd", ff_t, f_bias, dcum.reshape(HEADS, s_len))
    dq_r = rope_heads("rope_q_bwd", dq, inv_tabs, BF16)
    dkvn, dkr = dk_split("dk_split", dk, dv, inv_tabs)
    dcq, = matmul("d_up_q", dq_r, w["w_uq"], "nt", [F32])
    grads["w_uq"], = matmul("gw_uq", cq, dq_r, "tn", [BF16], tm=512, tn=2048, tk=2048)
    dckv, = matmul("d_up_kv", dkvn, w["w_kv"], "nt", [F32])
    grads["w_kv"], = matmul("gw_kv", ckv, dkvn, "tn", [BF16], tm=256, tn=2048, tk=2048)
    dcq_raw, d_qn = rms_bwd("rms_q_bwd", (lat, Q_LORA, 0), dcq, g_q, out_dtype=BF16)
    dckv_raw, d_kvn = rms_bwd("rms_kv_bwd", (lat, KV_LORA, Q_LORA // KV_LORA), dckv, g_kv, out_dtype=BF16)

    pad = jnp.zeros((s_len, LAT_W - OFF_FQ - HEADS), BF16)
    dproj = jnp.concatenate([dcq_raw, dckv_raw, dkr[:, :ROPE].astype(BF16), dff_t.T.astype(BF16), pad,
                             dfq, dfk, dfv, dgraw], axis=1)
    w_in_p = jnp.concatenate([w["w_lat"], w["w_fox"], w["w_gate"]], axis=1)
    dxn, = matmul("d_proj", dproj, w_in_p, "nt", [F32], tk=2688)
    gw_in, = matmul("gw_in", xn, dproj, "tn", [BF16], tm=1024, tn=1152, tk=2048)
    grads["w_lat"], grads["w_fox"], grads["w_gate"] = (gw_in[:, :LAT_W], gw_in[:, LAT_W:LAT_W + FOX_W],
                                                      gw_in[:, LAT_W + FOX_W:])
    dx, d_attn = rms_bwd("rms_attn_bwd", x, dxn, g_attn, dres=dh1)

    small_grads = {"attn_norm": d_attn, "fox_f_bias": d_bias, "q_norm": d_qn, "kv_norm": d_kvn,
                   "mlp_norm": d_mlp, "final_norm": d_final}
    return loss, dx, grads, g_packed_b, small_grads


def kernel(x, attn_norm, w_in, fox_f_bias, q_norm, w_uq, kv_norm, w_ukv, w_mla_branch, w_fox_branch, w_out, mlp_norm, w_up, w_down, final_norm, loss_target, m_attn_norm, m_w_in, m_fox_f_bias, m_q_norm, m_w_uq, m_kv_norm, m_w_ukv, m_w_mla_branch, m_w_fox_branch, m_w_out, m_mlp_norm, m_w_up, m_w_down, m_final_norm, v_attn_norm, v_w_in, v_fox_f_bias, v_q_norm, v_w_uq, v_kv_norm, v_w_ukv, v_w_mla_branch, v_w_fox_branch, v_w_out, v_mlp_norm, v_w_up, v_w_down, v_final_norm):
    given = dict(locals())
    big = {n: given[n][0] for n in BIG}
    small = {n: given[n] for n in SMALL}
    shard_shapes = {n: tuple(big[n].shape) for n in BIG}

    packed_a = pack_local({n: big[n].astype(BF16) for n in GROUP_A}, GROUP_A)
    packed_b = pack_local({n: big[n].astype(BF16) for n in GROUP_B}, GROUP_B)
    gathered_a = comm_allgather("comm_allgather_a", packed_a)
    unpack = functools.partial(unpack_a, shard_shapes=shard_shapes)
    w_a = unpack(gathered_a)

    loss_part, dx, grads_a, g_packed_b, small_grads = local_step(x[0], loss_target[0], w_a, small, packed_b,
                                                                 shard_shapes)

    to_packed_a = jax.linear_transpose(unpack, jax.ShapeDtypeStruct(gathered_a.shape, BF16))
    parts_a, = to_packed_a({n: grads_a[n] for n in w_a})
    from_sibling = comm_swap_sibling("comm_rs_sibling_a", parts_a)
    g_packed_a, _ = reduce_scatter_tail(parts_a, from_sibling,
                                        lambda pair: (comm_swap_chips("comm_rs_chips_a", pair), None), "a")

    small_sum = comm_allreduce_small("comm_allreduce_small", pack_small(small_grads, loss_part[0, 0]))
    g_small, loss = unpack_small(small_sum)

    grad_w, delta_w, new_m, new_v = {}, {}, {}, {}
    for names, g_packed in ((GROUP_A, g_packed_a), (GROUP_B, g_packed_b)):
        r0 = 0
        for n in names:
            rs, cs = shard_shapes[n]
            nr = rs * cs // PACK_C
            g = g_packed[r0:r0 + nr].reshape(rs, cs)
            r0 += nr
            d, m_new, v_new = adamw("adamw_" + n, big[n], g, given["m_" + n][0], given["v_" + n][0])
            grad_w[n], delta_w[n], new_m[n], new_v[n] = g[None], d[None], m_new[None], v_new[None]
    zero = jnp.zeros((), F32)
    d_s, m_s, v_s = adamw("adamw_small", pack_small(small, zero), small_sum * _small_mask(),
                          pack_small({n: given["m_" + n] for n in SMALL}, zero),
                          pack_small({n: given["v_" + n] for n in SMALL}, zero), tr=SMALL_ROWS)
    d_small, _ = unpack_small(d_s)
    m_small, _ = unpack_small(m_s)
    v_small, _ = unpack_small(v_s)
    for n in SMALL:
        shape = given[n].shape
        grad_w[n], delta_w[n] = g_small[n].reshape(shape), d_small[n].reshape(shape)
        new_m[n], new_v[n] = m_small[n].reshape(shape), v_small[n].reshape(shape)

    order = ["attn_norm", "w_in", "fox_f_bias", "q_norm", "w_uq", "kv_norm", "w_ukv", "w_mla_branch", "w_fox_branch",
             "w_out", "mlp_norm", "w_up", "w_down", "final_norm"]
    return (loss, dx[None], *[grad_w[n] for n in order], *[delta_w[n] for n in order],
            *[new_m[n] for n in order], *[new_v[n] for n in order])


def _small_mask():
    used = sum(SMALL_N[n] for n in SMALL)
    return (jnp.arange(SMALL_ROWS * PACK_C) < used).astype(F32).reshape(SMALL_ROWS, PACK_C)
```

```python
import functools
import math

import jax
import jax.numpy as jnp
from jax import lax
from jax.experimental import pallas as pl
from jax.experimental.pallas import tpu as pltpu

F32 = jnp.float32
BF16 = jnp.bfloat16
MESH = pl.DeviceIdType.MESH

D_MODEL = 2048
HEADS = 8
Q_LORA = 512
KV_LORA = 256
NOPE = 128
ROPE = 64
HEAD_V = 128
D_FF = 4 * D_MODEL
CHUNK = 64
EPS = 1e-6
ROPE_THETA = 10000.0
OFF_KR = Q_LORA + KV_LORA
OFF_FQ = OFF_KR + ROPE
OFF_FF = OFF_FQ + 3 * HEADS * HEAD_V
OFF_G = OFF_FF + HEADS
D_IN = OFF_G + 2 * D_MODEL

LAT_W = 896
FOX_W = 3 * HEADS * HEAD_V
GATE_W = 2 * D_MODEL
PROJ_W = LAT_W + FOX_W + GATE_W
QK_PAD = 256

ADAM_LR = 0.001
ADAM_B1 = 0.9
ADAM_B2 = 0.999
ADAM_EPS = 1e-08
ADAM_WD = 0.01
ADAM_STEP = 10

N_DEV = 8
PACK_C = 1024
NEG = -1e30
LOG2E = math.log2(math.e)

VMEM_LIMIT = 56 * 1024 * 1024

BIG = ("w_in", "w_uq", "w_ukv", "w_mla_branch", "w_fox_branch", "w_out", "w_up", "w_down")
GROUP_A = ("w_in", "w_uq", "w_ukv")
GROUP_B = ("w_mla_branch", "w_fox_branch", "w_out", "w_up", "w_down")
ROW_SHARDED = ("w_out", "w_down")
SMALL = ("attn_norm", "fox_f_bias", "q_norm", "kv_norm", "mlp_norm", "final_norm")
SMALL_N = {"attn_norm": D_MODEL, "fox_f_bias": HEADS, "q_norm": Q_LORA, "kv_norm": KV_LORA,
           "mlp_norm": D_MODEL, "final_norm": D_MODEL}
SMALL_ROWS = 8


def _params(sem):
    return pltpu.CompilerParams(dimension_semantics=sem, vmem_limit_bytes=VMEM_LIMIT)


def _rows(name, fn, row_ins, const_ins, outs, reds=(), tr=256):
    norm = [(a, a.shape[1], 0) if not isinstance(a, tuple) else a for a in row_ins]
    n_rows = norm[0][0].shape[0]
    tr = min(tr, n_rows)
    assert n_rows % tr == 0, (name, n_rows, tr)
    n_in, n_out, n_red = len(norm) + len(const_ins), len(outs), len(reds)

    def body(*refs):
        vals = [r[...] for r in refs[:n_in]]
        out_refs = refs[n_in:n_in + n_out]
        red_refs = refs[n_in + n_out:]
        out_vals, red_vals = fn(*vals)
        for r, v in zip(out_refs, out_vals):
            r[...] = v.astype(r.dtype)
        if n_red:
            @pl.when(pl.program_id(0) == 0)
            def _():
                for r in red_refs:
                    r[...] = jnp.zeros_like(r)
            for r, v in zip(red_refs, red_vals):
                r[...] += v

    in_specs = [pl.BlockSpec((tr, w), functools.partial(lambda i, cb: (i, cb), cb=cb)) for _, w, cb in norm]
    in_specs += [pl.BlockSpec(a.shape, lambda i: (0, 0)) for a in const_ins]
    out_specs = [pl.BlockSpec((tr, c), lambda i: (i, 0)) for c, _ in outs]
    out_specs += [pl.BlockSpec((1, c), lambda i: (0, 0)) for c in reds]
    out_shape = [jax.ShapeDtypeStruct((n_rows, c), dt) for c, dt in outs]
    out_shape += [jax.ShapeDtypeStruct((1, c), F32) for c in reds]
    res = pl.pallas_call(
        body, name=name, grid=(n_rows // tr,), in_specs=in_specs, out_specs=out_specs, out_shape=out_shape,
        compiler_params=_params(("arbitrary",)),
    )(*[a for a, _, _ in norm], *const_ins)
    return res


def _rstd(x):
    return lax.rsqrt(jnp.mean(x * x, axis=-1, keepdims=True) + EPS)


def rms_fwd(name, x, gain, tr=256):
    width = x[1] if isinstance(x, tuple) else x.shape[1]

    def fn(xv, g):
        return (xv * _rstd(xv) * g,), ()
    return _rows(name, fn, [x], [gain], [(width, BF16)], tr=tr)[0]


def rms_bwd(name, x, dy, gain, dres=None, out_dtype=F32, tr=256):
    width = x[1] if isinstance(x, tuple) else x.shape[1]

    def fn(xv, dyv, *rest):
        g = rest[-1]
        r = _rstd(xv)
        n = xv * r
        dyv = dyv.astype(F32)
        dn = dyv * g
        dx = r * (dn - n * jnp.mean(dn * n, axis=-1, keepdims=True))
        if dres is not None:
            dx = dx + rest[0]
        return (dx,), (jnp.sum(dyv * n, axis=0, keepdims=True),)

    ins = [x, dy] + ([dres] if dres is not None else [])
    return _rows(name, fn, ins, [gain], [(width, out_dtype)], [width], tr=tr)


def _rope_lanes(t, c, s1, s2):
    return t * c + pltpu.roll(t, 96, 1) * s1 + pltpu.roll(t, 32, 1) * s2


def rope_heads(name, x, tabs, out_dtype):
    def fn(xv, c, s1, s2):
        xv = xv.astype(F32)
        parts = []
        for h in range(HEADS):
            parts.append(xv[:, h * QK_PAD:h * QK_PAD + NOPE])
            parts.append(_rope_lanes(xv[:, h * QK_PAD + NOPE:(h + 1) * QK_PAD], c, s1, s2))
        return (jnp.concatenate(parts, axis=1),), ()
    return _rows(name, fn, [x, *tabs], [], [(HEADS * QK_PAD, out_dtype)])[0]


def rope_block(name, x, col_block, tabs, out_dtype):
    def fn(xv, c, s1, s2):
        return (_rope_lanes(xv.astype(F32), c, s1, s2),), ()
    return _rows(name, fn, [(x, 128, col_block), *tabs], [], [(128, out_dtype)])[0]


def k_assemble(name, kvn, kr):
    def fn(knp, krv):
        parts = []
        for h in range(HEADS):
            parts.append(knp[:, h * NOPE:(h + 1) * NOPE])
            parts.append(krv)
        return (jnp.concatenate(parts, axis=1),), ()
    return _rows(name, fn, [(kvn, HEADS * NOPE, 0), kr], [], [(HEADS * QK_PAD, BF16)])[0]


def dk_split(name, dk, dv, inv_tabs):
    def fn(dkv, dvv, c, s1, s2):
        parts = []
        acc = None
        for h in range(HEADS):
            parts.append(dkv[:, h * QK_PAD:h * QK_PAD + NOPE].astype(BF16))
            t = dkv[:, h * QK_PAD + NOPE:(h + 1) * QK_PAD]
            acc = t if acc is None else acc + t
        parts.append(dvv)
        return (jnp.concatenate(parts, axis=1), _rope_lanes(acc, c, s1, s2)), ()
    return _rows(name, fn, [dk, dv, *inv_tabs], [], [(2 * HEADS * NOPE, BF16), (128, F32)])


def gate_mix(name, graw, y_mla, y_fox):
    def fn(g, ya, yb):
        ga = jax.nn.sigmoid(g[:, :D_MODEL])
        gb = jax.nn.sigmoid(g[:, D_MODEL:])
        return (ga * ya + gb * yb,), ()
    return _rows(name, fn, [graw, y_mla, y_fox], [], [(D_MODEL, BF16)])[0]


def gate_mix_bwd(name, graw, y_mla, y_fox, dmix):
    def fn(g, ya, yb, dm):
        ga = jax.nn.sigmoid(g[:, :D_MODEL])
        gb = jax.nn.sigmoid(g[:, D_MODEL:])
        dgraw = jnp.concatenate([dm * ya * ga * (1.0 - ga), dm * yb * gb * (1.0 - gb)], axis=1)
        return (dgraw, dm * ga, dm * gb), ()
    return _rows(name, fn, [graw, y_mla, y_fox, dmix], [], [(GATE_W, BF16), (D_MODEL, BF16), (D_MODEL, BF16)], tr=128)


def loss_head(name, h2, target, gain):
    inv_d = 1.0 / D_MODEL

    def fn(h, t, g):
        r = _rstd(h)
        n = h * r
        err = n * g - t
        dy = err * inv_d
        dn = dy * g
        dh = r * (dn - n * jnp.mean(dn * n, axis=-1, keepdims=True))
        part = 0.5 * inv_d * jnp.sum(jnp.sum(err * err, axis=1, keepdims=True), axis=0, keepdims=True)
        return (dh,), (jnp.sum(dy * n, axis=0, keepdims=True), jnp.broadcast_to(part, (1, 128)))
    return _rows(name, fn, [h2, target], [gain], [(D_MODEL, F32)], [D_MODEL, 128])


def adamw(name, w, g, m, v, tr=256):
    c1 = 1.0 - ADAM_B1 ** ADAM_STEP
    c2 = 1.0 - ADAM_B2 ** ADAM_STEP

    def fn(wv, gv, mv, vv):
        m_new = ADAM_B1 * mv + (1.0 - ADAM_B1) * gv
        v_new = ADAM_B2 * vv + (1.0 - ADAM_B2) * (gv * gv)
        delta = -ADAM_LR * ((m_new / c1) / (jnp.sqrt(v_new / c2) + ADAM_EPS) + ADAM_WD * wv)
        return (delta, m_new, v_new), ()
    cols = w.shape[1]
    return _rows(name, fn, [w, g, m, v], [], [(cols, F32)] * 3, tr=tr)


def _row_tile(n_rows, cap=640):
    return max(t for t in range(16, cap + 1, 16) if n_rows % t == 0)


def add_pairs(name, a, b):
    def fn(av, bv):
        return (av.astype(F32) + bv.astype(F32),), ()
    return _rows(name, fn, [a, b], [], [(a.shape[1], BF16)], tr=_row_tile(a.shape[0]))[0]


def add_final(name, a, b, r0, r1, r2):
    def fn(av, bv, r0v, r1v, r2v):
        return (((av.astype(F32) + bv.astype(F32)) + r0v.astype(F32)) + r1v.astype(F32) + r2v.astype(F32),), ()
    return _rows(name, fn, [a, b, r0, r1, r2], [], [(a.shape[1], F32)], tr=_row_tile(a.shape[0]))[0]


TN_TILES = dict(tm=1024, tn=1024, tk=2048)
_DIMS = {"nn": (((1,), (0,)), ((), ())), "nt": (((1,), (1,)), ((), ())), "tn": (((0,), (0,)), ((), ()))}


def matmul(name, a, b, mode, outs, epilogue=None, extras=(), tm=1024, tn=1024, tk=2048, comm=None):
    if mode == "tn":
        kdim, m = a.shape
    else:
        m, kdim = a.shape
    n = b.shape[0] if mode == "nt" else b.shape[1]
    tm, tn, tk = min(tm, m), min(tn, n), min(tk, kdim)
    assert m % tm == 0 and n % tn == 0 and kdim % tk == 0, (name, a.shape, b.shape)
    nk = kdim // tk
    n_ex, n_out = len(extras), len(outs)
    dims = _DIMS[mode]
    grid = (m // tm, n // tn, nk)

    def body(*refs):
        ins, c_in, out_refs, c_out, scr, c_sems = _split_refs(refs, 2 + n_ex, comm, n_out, 1 if nk > 1 else 0)
        a_ref, b_ref = ins[:2]
        ex_refs = ins[2:]
        _comm_start(comm, c_in, c_out, c_sems, *grid)

        def finish(acc):
            vals = (acc,) if epilogue is None else epilogue(acc, *[r[...] for r in ex_refs])
            for r, v in zip(out_refs, vals):
                r[...] = v.astype(r.dtype)

        part = lax.dot_general(a_ref[...].astype(BF16), b_ref[...].astype(BF16), dims, preferred_element_type=F32)
        if nk == 1:
            finish(part)
        else:
            acc_ref = scr[0]
            k = pl.program_id(2)

            @pl.when(k == 0)
            def _():
                acc_ref[...] = part

            @pl.when(k > 0)
            def _():
                acc_ref[...] += part

            @pl.when(k == nk - 1)
            def _():
                finish(acc_ref[...])
        _comm_wait(comm, c_in, c_out, c_sems, *grid)

    a_spec = pl.BlockSpec((tk, tm), lambda i, j, k: (k, i)) if mode == "tn" else pl.BlockSpec((tm, tk), lambda i, j, k: (i, k))
    b_spec = pl.BlockSpec((tn, tk), lambda i, j, k: (j, k)) if mode == "nt" else pl.BlockSpec((tk, tn), lambda i, j, k: (k, j))
    tile = pl.BlockSpec((tm, tn), lambda i, j, k: (i, j))
    c_ins, c_outs, c_shapes, c_scratch, aliases = _with_comm(comm, 2 + n_ex, n_out)
    sem = ("arbitrary",) * 3 if comm else ("parallel", "parallel", "arbitrary")
    res = pl.pallas_call(
        body, name=name, grid=grid,
        in_specs=[a_spec, b_spec] + [tile] * n_ex + c_ins,
        out_specs=[tile] * n_out + c_outs,
        out_shape=[jax.ShapeDtypeStruct((m, n), dt) for dt in outs] + c_shapes,
        scratch_shapes=([pltpu.VMEM((tm, tn), F32)] if nk > 1 else []) + c_scratch,
        input_output_aliases=aliases,
        compiler_params=_params(sem),
    )(a, b, *extras, *(comm.args if comm else []))
    return res


_NT = (((1,), (1,)), ((), ()))
_NN = (((1,), (0,)), ((), ()))


def _mask(bq, chunk, transposed, row0=0, shape=None):
    shape = (bq, bq) if shape is None else shape
    row = lax.broadcasted_iota(jnp.int32, shape, 0) + row0
    col = lax.broadcasted_iota(jnp.int32, shape, 1)
    if chunk > 1:
        row, col = row // chunk, col // chunk
    return (row <= col) if transposed else (col <= row)


def _row_layout(a, bq):
    h, s, _ = a.shape
    return a.reshape(h, s // bq, 1, bq)


def _split_refs(refs, n_in, comm, n_out, n_scr):
    n_ci = len(comm.args) if comm else 0
    n_co = len(comm.out_shapes) if comm else 0
    cuts = [n_in, n_ci, n_out, n_co, n_scr, 3 if comm else 0]
    parts, at = [], 0
    for n in cuts:
        parts.append(list(refs[at:at + n]))
        at += n
    assert at == len(refs), (at, len(refs))
    return parts


def _at_step(grid, last):
    hit = None
    for axis, n in enumerate(grid):
        here = pl.program_id(axis) == (n - 1 if last else 0)
        hit = here if hit is None else jnp.logical_and(hit, here)
    return hit


def _comm_start(comm, c_in, c_out, c_sems, *grid):
    if comm is not None:
        @pl.when(_at_step(grid, False))
        def _():
            comm.start(c_in, c_out, c_sems)


def _comm_wait(comm, c_in, c_out, c_sems, *grid):
    if comm is not None:
        @pl.when(_at_step(grid, True))
        def _():
            comm.wait(c_in, c_out, c_sems)


def attn_fwd(name, q, k, v, offs, dqk, chunk, scale, cum=None, bq=512, comm=None, split_p=False):
    s_len = q.shape[0]
    nq = s_len // bq
    qoff, koff, voff = offs
    has_bias = cum is not None
    scale2 = scale * LOG2E

    n_in = 5 if has_bias else 3

    def body(*refs):
        ins, c_in, outs, c_out, scr, c_sems = _split_refs(refs, n_in, comm, 2, 3)
        q_ref, k_ref, v_ref = ins[:3]
        if has_bias:
            cc_ref, cr_ref = ins[3:]
        o_ref, lse_ref = outs
        m_s, l_s, acc_s = scr
        _comm_start(comm, c_in, c_out, c_sems, HEADS, nq)
        i = pl.program_id(1)
        qv = q_ref[...]
        m_s[...] = jnp.full_like(m_s, NEG)
        l_s[...] = jnp.zeros_like(l_s)
        acc_s[...] = jnp.zeros_like(acc_s)

        def step(j, masked):
            off = pl.multiple_of(j * bq, bq)
            kj = k_ref[pl.ds(off, bq), :]
            vj = v_ref[pl.ds(off, bq), :]
            st = lax.dot_general(kj, qv, _NT, preferred_element_type=F32) * scale2
            if has_bias:
                st = st + cr_ref[...] - cc_ref[pl.ds(off, bq), :]
            if masked:
                st = jnp.where(_mask(bq, chunk, True), st, NEG)
            m_prev = m_s[...]
            m_new = jnp.maximum(m_prev, jnp.max(st, axis=0, keepdims=True))
            alpha = jnp.exp2(m_prev - m_new)
            pt = jnp.exp2(st - m_new)
            l_s[...] = alpha * l_s[...] + jnp.sum(pt, axis=0, keepdims=True)
            p_hi = pt.astype(BF16)
            pv = lax.dot_general(vj, p_hi, _DIMS["tn"], preferred_element_type=F32)
            if split_p:
                p_lo = (pt - p_hi.astype(F32)).astype(BF16)
                pv = pv + lax.dot_general(vj, p_lo, _DIMS["tn"], preferred_element_type=F32)
            acc_s[...] = alpha * acc_s[...] + pv
            m_s[...] = m_new

        def pair_body(jj, carry):
            step(2 * jj, False)
            step(2 * jj + 1, False)
            return carry

        lax.fori_loop(0, i // 2, pair_body, 0)

        @pl.when(i % 2 == 1)
        def _():
            step(i - 1, False)

        step(i, True)
        o_ref[...] = (acc_s[...] / l_s[...]).T.astype(o_ref.dtype)
        lse_ref[...] = m_s[...] + jnp.log2(l_s[...])
        _comm_wait(comm, c_in, c_out, c_sems, HEADS, nq)

    in_specs = [
        pl.BlockSpec((bq, dqk), lambda h, i: (i, qoff + h)),
        pl.BlockSpec((s_len, dqk), lambda h, i: (0, koff + h)),
        pl.BlockSpec((s_len, HEAD_V), lambda h, i: (0, voff + h)),
    ]
    args = [q, k, v]
    if has_bias:
        in_specs += [pl.BlockSpec((None, s_len, 1), lambda h, i: (h, 0, 0)),
                     pl.BlockSpec((None, None, 1, bq), lambda h, i: (h, i, 0, 0))]
        args += [cum, _row_layout(cum, bq)]
    c_ins, c_outs, c_shapes, c_scratch, aliases = _with_comm(comm, len(args), 2)
    o, lse_rows, *comm_out = pl.pallas_call(
        body, name=name, grid=(HEADS, nq), in_specs=in_specs + c_ins,
        out_specs=[pl.BlockSpec((bq, HEAD_V), lambda h, i: (i, h)),
                   pl.BlockSpec((None, None, 1, bq), lambda h, i: (h, i, 0, 0))] + c_outs,
        out_shape=[jax.ShapeDtypeStruct((s_len, HEADS * HEAD_V), F32),
                   jax.ShapeDtypeStruct((HEADS, nq, 1, bq), F32)] + c_shapes,
        scratch_shapes=[pltpu.VMEM((1, bq), F32), pltpu.VMEM((1, bq), F32), pltpu.VMEM((HEAD_V, bq), F32)] + c_scratch,
        input_output_aliases=aliases,
        compiler_params=_params(("arbitrary", "arbitrary")),
    )(*args, *(comm.args if comm else []))
    return (o, lse_rows, *comm_out)


def _bwd_block(kv, vv, qi, doi, lse_row, scale2, bias, masked, bq, chunk):
    st = lax.dot_general(kv, qi, _NT, preferred_element_type=F32) * scale2
    if bias is not None:
        st = st + bias[0] - bias[1]
    if masked:
        st = jnp.where(_mask(bq, chunk, True), st, NEG)
    pt = jnp.exp2(st - lse_row)
    dpt = lax.dot_general(vv, doi, _NT, preferred_element_type=F32)
    return pt, dpt


def attn_delta(name, do, o, bq=512):
    s_len = do.shape[0]
    nq = s_len // bq

    def body(do_ref, o_ref, out_ref):
        prod = do_ref[...].astype(F32) * o_ref[...].astype(F32)
        ones = jnp.ones((8, HEAD_V), BF16)
        total = None
        for part in reversed(_split3(prod)):
            term = lax.dot_general(ones, part, _NT, preferred_element_type=F32)
            total = term if total is None else total + term
        out_ref[...] = total[0:1, :]

    blk = pl.BlockSpec((bq, HEAD_V), lambda h, i: (i, h))
    return pl.pallas_call(
        body, name=name, grid=(HEADS, nq), in_specs=[blk, blk],
        out_specs=pl.BlockSpec((None, None, 1, bq), lambda h, i: (h, i, 0, 0)),
        out_shape=jax.ShapeDtypeStruct((HEADS, nq, 1, bq), F32),
        compiler_params=_params(("parallel", "parallel")),
    )(do, o)


def attn_bwd(name, q, k, v, do, lse, delta, offs, dqk, chunk, scale, out_dtype, cum=None, bq=512, comm=None):
    s_len = q.shape[0]
    nq = s_len // bq
    qoff, koff, voff = offs
    has_bias = cum is not None
    scale2 = scale * LOG2E
    n_in, n_out = (8, 4) if has_bias else (6, 3)

    def body(*refs):
        ins, c_in, outs, c_out, scr, c_sems = _split_refs(refs, n_in, comm, n_out, n_out - 1)
        k_ref, v_ref, q_ref, do_ref, lse_ref, delta_ref = ins[:6]
        dk_ref, dv_ref, dq_ref = outs[:3]
        dk_s, dv_s = scr[:2]
        if has_bias:
            cc_ref, cr_ref = ins[6:]
            dc_ref, dc_s = outs[3], scr[2]
        _comm_start(comm, c_in, c_out, c_sems, HEADS, nq)
        j = pl.program_id(1)
        kv = k_ref[...]
        vv = v_ref[...]
        dk_s[...] = jnp.zeros_like(dk_s)
        dv_s[...] = jnp.zeros_like(dv_s)
        if has_bias:
            dc_s[...] = jnp.zeros_like(dc_s)

        @pl.when(j == 0)
        def _():
            dq_ref[...] = jnp.zeros_like(dq_ref)

        def step(i, masked):
            off = pl.multiple_of(i * bq, bq)
            qi = q_ref[pl.ds(off, bq), :]
            doi = do_ref[pl.ds(off, bq), :].astype(BF16)
            bias = (cr_ref[i], cc_ref[...]) if has_bias else None
            pt, dpt = _bwd_block(kv, vv, qi, doi, lse_ref[i], scale2, bias, masked, bq, chunk)
            dv_s[...] += lax.dot_general(pt.astype(BF16), doi, _NN, preferred_element_type=F32)
            dst = pt * (dpt - delta_ref[i])
            if has_bias:
                dc_s[...] -= jnp.sum(dst, axis=-1, keepdims=True)
            dst = dst.astype(BF16)
            dk_s[...] += lax.dot_general(dst, qi, _NN, preferred_element_type=F32)
            dq_ref[pl.ds(off, bq), :] += lax.dot_general(dst, kv, _DIMS["tn"], preferred_element_type=F32) * scale

        step(j, True)

        def loop_body(i, carry):
            step(i, False)
            return carry

        lax.fori_loop(j + 1, nq, loop_body, 0)
        dk_ref[...] = (dk_s[...] * scale).astype(dk_ref.dtype)
        dv_ref[...] = dv_s[...].astype(dv_ref.dtype)
        if has_bias:
            dc_ref[...] = dc_s[...]
        _comm_wait(comm, c_in, c_out, c_sems, HEADS, nq)

    rows = pl.BlockSpec((None, nq, 1, bq), lambda h, j: (h, 0, 0, 0))
    in_specs = [
        pl.BlockSpec((bq, dqk), lambda h, j: (j, koff + h)),
        pl.BlockSpec((bq, HEAD_V), lambda h, j: (j, voff + h)),
        pl.BlockSpec((s_len, dqk), lambda h, j: (0, qoff + h)),
        pl.BlockSpec((s_len, HEAD_V), lambda h, j: (0, h)),
        rows,
        rows,
    ]
    args = [k, v, q, do, lse, delta]
    out_specs = [pl.BlockSpec((bq, dqk), lambda h, j: (j, h)), pl.BlockSpec((bq, HEAD_V), lambda h, j: (j, h)),
                 pl.BlockSpec((s_len, dqk), lambda h, j: (0, h))]
    out_shape = [jax.ShapeDtypeStruct((s_len, HEADS * dqk), out_dtype),
                 jax.ShapeDtypeStruct((s_len, HEADS * HEAD_V), BF16),
                 jax.ShapeDtypeStruct((s_len, HEADS * dqk), F32)]
    scratch = [pltpu.VMEM((bq, dqk), F32), pltpu.VMEM((bq, HEAD_V), F32)]
    if has_bias:
        in_specs += [pl.BlockSpec((None, bq, 1), lambda h, j: (h, j, 0)), rows]
        args += [cum, _row_layout(cum, bq)]
        out_specs.append(pl.BlockSpec((None, bq, 1), lambda h, j: (h, j, 0)))
        out_shape.append(jax.ShapeDtypeStruct((HEADS, s_len, 1), F32))
        scratch.append(pltpu.VMEM((bq, 1), F32))
    c_ins, c_outs, c_shapes, c_scratch, aliases = _with_comm(comm, len(args), n_out)
    return pl.pallas_call(
        body, name=name, grid=(HEADS, nq), in_specs=in_specs + c_ins, out_specs=out_specs + c_outs,
        out_shape=out_shape + c_shapes, scratch_shapes=scratch + c_scratch, input_output_aliases=aliases,
        compiler_params=_params(("arbitrary", "arbitrary")),
    )(*args, *(comm.args if comm else []))


_CUM_BLK = 512


def _split3(x):
    hi = x.astype(BF16)
    r1 = x - hi.astype(F32)
    mid = r1.astype(BF16)
    lo = (r1 - mid.astype(F32)).astype(BF16)
    return hi, mid, lo


def _tri_dot(x, tri):
    hi, mid, lo = _split3(x)
    out = lax.dot_general(lo, tri, _NN, preferred_element_type=F32)
    out = out + lax.dot_general(mid, tri, _NN, preferred_element_type=F32)
    return out + lax.dot_general(hi, tri, _NN, preferred_element_type=F32)


def fox_cum_fwd(name, ff_t, bias):
    s_len = ff_t.shape[1]
    nb = s_len // _CUM_BLK

    def body(ff_ref, b_ref, cum_ref):
        row = lax.broadcasted_iota(jnp.int32, (_CUM_BLK, _CUM_BLK), 0)
        col = lax.broadcasted_iota(jnp.int32, (_CUM_BLK, _CUM_BLK), 1)
        tri = (row <= col).astype(BF16)
        carry = jnp.zeros((HEADS, 1), F32)
        for b in range(nb):
            z = ff_ref[:, b * _CUM_BLK:(b + 1) * _CUM_BLK] + b_ref[...]
            logf = jnp.minimum(z, 0.0) - jnp.log1p(jnp.exp(-jnp.abs(z)))
            blk = _tri_dot(logf, tri) + carry
            cum_ref[:, b * _CUM_BLK:(b + 1) * _CUM_BLK] = blk
            carry = blk[:, _CUM_BLK - 1:_CUM_BLK]

    return pl.pallas_call(
        body, name=name, out_shape=jax.ShapeDtypeStruct((HEADS, s_len), F32),
        compiler_params=pltpu.CompilerParams(vmem_limit_bytes=VMEM_LIMIT),
    )(ff_t, bias)


def fox_cum_bwd(name, ff_t, bias, dcum):
    s_len = ff_t.shape[1]
    nb = s_len // _CUM_BLK

    def body(ff_ref, b_ref, dc_ref, dff_ref, db_ref):
        row = lax.broadcasted_iota(jnp.int32, (_CUM_BLK, _CUM_BLK), 0)
        col = lax.broadcasted_iota(jnp.int32, (_CUM_BLK, _CUM_BLK), 1)
        tri = (row >= col).astype(BF16)
        carry = jnp.zeros((HEADS, 1), F32)
        dbias = jnp.zeros((HEADS, 1), F32)
        for b in reversed(range(nb)):
            sl = slice(b * _CUM_BLK, (b + 1) * _CUM_BLK)
            dlogf = _tri_dot(dc_ref[:, sl], tri) + carry
            carry = dlogf[:, 0:1]
            z = ff_ref[:, sl] + b_ref[...]
            dz = dlogf / (1.0 + jnp.exp(z))
            dff_ref[:, sl] = dz
            dbias = dbias + jnp.sum(dz, axis=-1, keepdims=True)
        db_ref[...] = dbias

    return pl.pallas_call(
        body, name=name,
        out_shape=[jax.ShapeDtypeStruct((HEADS, s_len), F32), jax.ShapeDtypeStruct((HEADS, 1), F32)],
        compiler_params=pltpu.CompilerParams(vmem_limit_bytes=VMEM_LIMIT),
    )(ff_t, bias, dcum)


_ANY = pl.BlockSpec(memory_space=pl.ANY)


def _me():
    return lax.axis_index("x"), lax.axis_index("y"), lax.axis_index("c")


def comm_allgather(name, mine):
    n_rows, n_cols = mine.shape

    def body(x_ref, out_ref, send_sems, recv_sems, local_sem):
        x, y, c = _me()
        sibling = (x, y, 1 - c)
        chips = [(1 - x, y), (x, 1 - y), (1 - x, 1 - y)]

        def blk(px, py, pc):
            return out_ref.at[pc * 4 + px * 2 + py]

        def copy(k, block, to, src=None):
            return pltpu.make_async_remote_copy(
                src_ref=blk(*block) if src is None else src, dst_ref=blk(*block),
                send_sem=send_sems.at[k], recv_sem=recv_sems.at[k], device_id=to, device_id_type=MESH)

        own = pltpu.make_async_copy(x_ref, blk(x, y, c), local_sem)
        own.start()
        first = [copy(0, (x, y, c), sibling, src=x_ref)]
        first += [copy(1 + j, (x, y, c), (*chip, c), src=x_ref) for j, chip in enumerate(chips)]
        for cp in first:
            cp.start()
        passed = [copy(4 + j, (*chip, c), sibling) for j, chip in enumerate(chips)]
        for j, chip in enumerate(chips):
            copy(1 + j, (*chip, c), (x, y, c)).wait_recv()
            passed[j].start()
        copy(0, sibling, (x, y, c)).wait_recv()
        for j, chip in enumerate(chips):
            copy(4 + j, (*chip, 1 - c), (x, y, c)).wait_recv()
        for cp in first + passed:
            cp.wait_send()
        own.wait()

    return pl.pallas_call(
        body, name=name, out_shape=jax.ShapeDtypeStruct((N_DEV, n_rows, n_cols), mine.dtype),
        in_specs=[_ANY], out_specs=_ANY,
        scratch_shapes=[pltpu.SemaphoreType.DMA((7,)), pltpu.SemaphoreType.DMA((7,)), pltpu.SemaphoreType.DMA],
    )(mine)


def comm_swap_sibling(name, parts):
    _, n_rows, n_cols = parts.shape

    def body(p_ref, got_ref, send_sem, recv_sem):
        x, y, c = _me()
        cp = pltpu.make_async_remote_copy(
            src_ref=p_ref.at[pl.ds((1 - c) * 4, 4)], dst_ref=got_ref, send_sem=send_sem, recv_sem=recv_sem,
            device_id=(x, y, 1 - c), device_id_type=MESH)
        cp.start()
        cp.wait()

    return pl.pallas_call(
        body, name=name, out_shape=jax.ShapeDtypeStruct((4, n_rows, n_cols), parts.dtype),
        in_specs=[_ANY], out_specs=_ANY,
        scratch_shapes=[pltpu.SemaphoreType.DMA, pltpu.SemaphoreType.DMA],
    )(parts)


class CommHook:
    def __init__(self, args, out_shapes, n_copies, copies, aliases=None):
        self.args, self.out_shapes, self.n_copies, self.copies = list(args), list(out_shapes), n_copies, copies
        self.aliases = aliases or {}

    def scratch(self):
        return [pltpu.SemaphoreType.DMA((self.n_copies,)), pltpu.SemaphoreType.DMA((self.n_copies,)),
                pltpu.SemaphoreType.DMA((1,))]

    def start(self, in_refs, out_refs, sems):
        sends, _, locs = self.copies(in_refs, out_refs, *sems)
        for cp in locs() + sends():
            cp.start()

    def wait(self, in_refs, out_refs, sems):
        sends, recvs, locs = self.copies(in_refs, out_refs, *sems)
        for cp in sends():
            cp.wait_send()
        for cp in recvs():
            cp.wait_recv()
        for cp in locs():
            cp.wait()


def _remote(src, dst, send_sem, recv_sem, to):
    return pltpu.make_async_remote_copy(src_ref=src, dst_ref=dst, send_sem=send_sem, recv_sem=recv_sem,
                                        device_id=to, device_id_type=MESH)


def hook_gather_first(mine):
    n_rows, n_cols = mine.shape

    def copies(ins, outs, send, recv, local):
        (x_ref,), (out_ref,) = ins, outs
        x, y, c = _me()
        me = c * 4 + x * 2 + y
        peers = [(x, y, 1 - c), (1 - x, y, c), (x, 1 - y, c), (1 - x, 1 - y, c)]

        def sends():
            return [_remote(x_ref, out_ref.at[me], send.at[k], recv.at[k], p) for k, p in enumerate(peers)]

        def recvs():
            return [_remote(x_ref, out_ref.at[pc * 4 + px * 2 + py], send.at[k], recv.at[k], (px, py, pc))
                    for k, (px, py, pc) in enumerate(peers)]

        return sends, recvs, lambda: [pltpu.make_async_copy(x_ref, out_ref.at[me], local.at[0])]

    return CommHook([mine], [jax.ShapeDtypeStruct((N_DEV, n_rows, n_cols), mine.dtype)], 4, copies)


def hook_gather_second(gathered):
    def copies(ins, outs, send, recv, local):
        (g_in,), (g_out,) = ins, outs
        x, y, c = _me()
        chips = [(1 - x, y), (x, 1 - y), (1 - x, 1 - y)]

        def sends():
            return [_remote(g_in.at[c * 4 + px * 2 + py], g_out.at[c * 4 + px * 2 + py], send.at[k], recv.at[k],
                            (x, y, 1 - c)) for k, (px, py) in enumerate(chips)]

        def recvs():
            return [_remote(g_in.at[(1 - c) * 4 + px * 2 + py], g_out.at[(1 - c) * 4 + px * 2 + py], send.at[k],
                            recv.at[k], (x, y, 1 - c)) for k, (px, py) in enumerate(chips)]

        return sends, recvs, lambda: []

    return CommHook([gathered], [jax.ShapeDtypeStruct(gathered.shape, gathered.dtype)], 3, copies, aliases={0: 0})


def hook_swap_sibling(parts):
    _, n_rows, n_cols = parts.shape

    def copies(ins, outs, send, recv, local):
        (p_ref,), (got_ref,) = ins, outs
        x, y, c = _me()

        def swap():
            return [_remote(p_ref.at[pl.ds((1 - c) * 4, 4)], got_ref, send.at[0], recv.at[0], (x, y, 1 - c))]

        return swap, swap, lambda: []

    return CommHook([parts], [jax.ShapeDtypeStruct((4, n_rows, n_cols), parts.dtype)], 1, copies)


def hook_swap_chips(parts):
    _, n_rows, n_cols = parts.shape

    def copies(ins, outs, send, recv, local):
        (p_ref,), (got_ref,) = ins, outs
        x, y, c = _me()
        chips = [(1 - x, y), (x, 1 - y), (1 - x, 1 - y)]

        def swaps():
            return [_remote(p_ref.at[2 * px + py], got_ref.at[k], send.at[k], recv.at[k], (px, py, c))
                    for k, (px, py) in enumerate(chips)]

        return swaps, swaps, lambda: []

    return CommHook([parts], [jax.ShapeDtypeStruct((3, n_rows, n_cols), parts.dtype)], 3, copies)


def _with_comm(comm, n_args, n_outs):
    if comm is None:
        return [], [], [], [], {}
    aliases = {n_args + a: n_outs + o for a, o in comm.aliases.items()}
    return [_ANY] * len(comm.args), [_ANY] * len(comm.out_shapes), comm.out_shapes, comm.scratch(), aliases


def comm_allreduce_small(name, mine):
    shape = mine.shape

    def body(x_ref, out_ref, buf, send_sems, recv_sems):
        x, y, c = _me()
        my_slot = c * 4 + x * 2 + y
        buf[my_slot] = x_ref[...]
        cps = []
        for k in range(1, N_DEV):
            dx, dy, dc = (k >> 2) & 1, (k >> 1) & 1, k & 1
            px, py, pc = x ^ dx, y ^ dy, c ^ dc
            send = pltpu.make_async_remote_copy(
                src_ref=x_ref, dst_ref=buf.at[my_slot], send_sem=send_sems.at[k - 1], recv_sem=recv_sems.at[k - 1],
                device_id=(px, py, pc), device_id_type=MESH)
            send.start()
            recv = pltpu.make_async_remote_copy(
                src_ref=x_ref, dst_ref=buf.at[pc * 4 + px * 2 + py], send_sem=send_sems.at[k - 1],
                recv_sem=recv_sems.at[k - 1], device_id=(px, py, pc), device_id_type=MESH)
            cps.append((send, recv))
        for send, recv in cps:
            send.wait_send()
            recv.wait_recv()
        total = buf[0]
        for s in range(1, N_DEV):
            total = total + buf[s]
        out_ref[...] = total

    vmem = pl.BlockSpec(memory_space=pltpu.VMEM)
    return pl.pallas_call(
        body, name=name, out_shape=jax.ShapeDtypeStruct(shape, F32), in_specs=[vmem], out_specs=vmem,
        scratch_shapes=[pltpu.VMEM((N_DEV,) + shape, F32), pltpu.SemaphoreType.DMA((7,)), pltpu.SemaphoreType.DMA((7,))],
    )(mine)


def pack_local(shards, names):
    flat = [shards[n].reshape(-1, PACK_C) for n in names]
    rows = sum(f.shape[0] for f in flat)
    pad = (-rows) % 128
    return jnp.concatenate(flat + [jnp.zeros((pad, PACK_C), flat[0].dtype)], axis=0)


def unpack_group(gathered, names, shard_shapes):
    _, n_rows, _ = gathered.shape
    by_block = gathered.reshape(2, 4, n_rows, PACK_C).transpose(1, 0, 2, 3).reshape(N_DEV, n_rows, PACK_C)
    full = {}
    r0 = 0
    for name in names:
        rs, cs = shard_shapes[name]
        nr = rs * cs // PACK_C
        piece = by_block[:, r0:r0 + nr, :].reshape(N_DEV, rs, cs)
        r0 += nr
        if name in ROW_SHARDED:
            full[name] = piece.reshape(N_DEV * rs, cs)
        else:
            full[name] = piece.transpose(1, 0, 2).reshape(rs, N_DEV * cs)
    return full


def unpack_a(gathered, shard_shapes):
    full = unpack_group(gathered, GROUP_A, shard_shapes)
    w_in = full.pop("w_in")
    zeros = jnp.zeros((D_MODEL, LAT_W - OFF_FQ - HEADS), w_in.dtype)
    full["w_lat"] = jnp.concatenate([w_in[:, :OFF_FQ], w_in[:, OFF_FF:OFF_G], zeros], axis=1)
    full["w_fox"] = w_in[:, OFF_FQ:OFF_FF]
    full["w_gate"] = w_in[:, OFF_G:]
    w_uq = full.pop("w_uq").reshape(Q_LORA, HEADS, NOPE + ROPE)
    full["w_uq"] = jnp.pad(w_uq, ((0, 0), (0, 0), (0, QK_PAD - NOPE - ROPE))).reshape(Q_LORA, HEADS * QK_PAD)
    w_ukv = full.pop("w_ukv").reshape(KV_LORA, HEADS, 2, NOPE)
    full["w_kv"] = jnp.concatenate([w_ukv[:, :, 0, :].reshape(KV_LORA, HEADS * NOPE),
                                    w_ukv[:, :, 1, :].reshape(KV_LORA, HEADS * HEAD_V)], axis=1)
    return full


def pack_small(vals, loss):
    flat = [vals[n].reshape(-1) for n in SMALL] + [loss.reshape(-1)]
    used = sum(f.shape[0] for f in flat)
    flat.append(jnp.zeros((SMALL_ROWS * PACK_C - used,), F32))
    return jnp.concatenate(flat).reshape(SMALL_ROWS, PACK_C)


def unpack_small(packed):
    flat = packed.reshape(-1)
    out, off = {}, 0
    for n in SMALL:
        out[n] = flat[off:off + SMALL_N[n]]
        off += SMALL_N[n]
    return out, flat[off]


def rope_tables(s_len):
    pos = jnp.arange(s_len, dtype=F32)
    inv = 1.0 / (ROPE_THETA ** (jnp.arange(0, ROPE, 2, dtype=F32) / ROPE))
    ang = pos[:, None] * inv[None, :]
    cos, sin = jnp.cos(ang), jnp.sin(ang)
    zero = jnp.zeros_like(cos)
    c = jnp.concatenate([cos, cos, zero, zero], axis=1)
    s1 = jnp.concatenate([-sin, zero, zero, zero], axis=1)
    s2 = jnp.concatenate([zero, sin, zero, zero], axis=1)
    return (c, s1, s2), (c, -s1, -s2)


def reduce_scatter_tail(parts, from_sibling, from_chips_fn, names):
    cx, cy, cc = _me()
    n_rows = parts.shape[1]
    mine4 = lax.dynamic_slice_in_dim(parts, cc * 4, 4, axis=0)
    pair = add_pairs("rs_pair_sum_" + names, mine4.reshape(4 * n_rows, PACK_C), from_sibling.reshape(4 * n_rows, PACK_C))
    from_chips, extra = from_chips_fn(pair.reshape(4, n_rows, PACK_C))
    own = cx * 2 + cy
    total = add_final("rs_final_sum_" + names, lax.dynamic_index_in_dim(mine4, own, 0, keepdims=False),
                      lax.dynamic_index_in_dim(from_sibling, own, 0, keepdims=False),
                      from_chips[0], from_chips[1], from_chips[2])
    return total, extra


def local_step(x, target, w, small, packed_b, shard_shapes, to_packed_a):
    s_len = x.shape[0]
    tabs, inv_tabs = rope_tables(s_len)
    g_attn = small["attn_norm"].reshape(1, D_MODEL)
    g_q = small["q_norm"].reshape(1, Q_LORA)
    g_kv = small["kv_norm"].reshape(1, KV_LORA)
    g_mlp = small["mlp_norm"].reshape(1, D_MODEL)
    g_final = small["final_norm"].reshape(1, D_MODEL)
    f_bias = small["fox_f_bias"].reshape(HEADS, 1)
    mla_scale = 1.0 / math.sqrt(NOPE + ROPE)
    fox_scale = 1.0 / math.sqrt(HEAD_V)
    mla_offs = (0, 0, HEADS)
    fox_offs = (0, HEADS, 2 * HEADS)

    xn = rms_fwd("rms_attn", x, g_attn)
    lat, = matmul("proj_lat", xn, w["w_lat"], "nn", [F32])
    fox, = matmul("proj_fox", xn, w["w_fox"], "nn", [BF16])
    graw, = matmul("proj_gate", xn, w["w_gate"], "nn", [F32])
    cq = rms_fwd("rms_q", (lat, Q_LORA, 0), g_q)
    ckv = rms_fwd("rms_kv", (lat, KV_LORA, Q_LORA // KV_LORA), g_kv)
    qraw, = matmul("up_q", cq, w["w_uq"], "nn", [F32])
    q = rope_heads("rope_q", qraw, tabs, BF16)
    kvn, = matmul("up_kv", ckv, w["w_kv"], "nn", [BF16])
    kr = rope_block("rope_k", lat, OFF_KR // 128, tabs, BF16)
    k = k_assemble("k_assemble", kvn, kr)
    o_mla, lse_mla, gathered_b = attn_fwd("mla_fwd", q, k, kvn, mla_offs, QK_PAD, CHUNK, mla_scale,
                                          comm=hook_gather_first(packed_b))
    ff_t = lat[:, OFF_FQ:OFF_FQ + HEADS].T
    cum = fox_cum_fwd("fox_cum", ff_t, f_bias).reshape(HEADS, s_len, 1) * LOG2E
    o_fox, lse_fox, gathered_b = attn_fwd("fox_fwd", fox, fox, fox, fox_offs, HEAD_V, 1, fox_scale, cum=cum,
                                          comm=hook_gather_second(gathered_b), split_p=True)
    unpack_b = functools.partial(unpack_group, names=GROUP_B, shard_shapes=shard_shapes)
    w = {**w, **unpack_b(gathered_b)}
    y_mla, = matmul("branch_mla", o_mla, w["w_mla_branch"], "nn", [F32])
    y_fox, = matmul("branch_fox", o_fox, w["w_fox_branch"], "nn", [F32])
    mix = gate_mix("gate_mix", graw, y_mla, y_fox)
    h1, = matmul("out_proj", mix, w["w_out"], "nn", [F32], epilogue=lambda acc, res: (res + acc,), extras=[x])
    hn = rms_fwd("rms_mlp", h1, g_mlp)

    def relu2(acc):
        r = jnp.maximum(acc, 0.0)
        return r * r, r
    u, relu_up = matmul("mlp_up", hn, w["w_up"], "nn", [BF16, BF16], epilogue=relu2)
    h2, = matmul("mlp_down", u, w["w_down"], "nn", [F32], epilogue=lambda acc, res: (res + acc,), extras=[h1])
    dh2, d_final, loss = loss_head("loss_head", h2, target, g_final)

    grads = {}
    dup, = matmul("d_mlp_down", dh2, w["w_down"], "nt", [BF16],
                  epilogue=lambda acc, r: (acc * (2.0 * r.astype(F32)),), extras=[relu_up])
    grads["w_down"], = matmul("gw_down", u, dh2, "tn", [BF16], **TN_TILES)
    dhn, = matmul("d_mlp_up", dup, w["w_up"], "nt", [F32])
    grads["w_up"], = matmul("gw_up", hn, dup, "tn", [BF16], **TN_TILES)
    dh1, d_mlp = rms_bwd("rms_mlp_bwd", h1, dhn, g_mlp, dres=dh2)
    dmix, = matmul("d_out_proj", dh1, w["w_out"], "nt", [F32])
    grads["w_out"], = matmul("gw_out", mix, dh1, "tn", [BF16], **TN_TILES)
    dgraw, dy_mla, dy_fox = gate_mix_bwd("gate_mix_bwd", graw, y_mla, y_fox, dmix)
    do_mla, = matmul("d_branch_mla", dy_mla, w["w_mla_branch"], "nt", [BF16])
    grads["w_mla_branch"], = matmul("gw_branch_mla", o_mla, dy_mla, "tn", [BF16], **TN_TILES)
    do_fox, = matmul("d_branch_fox", dy_fox, w["w_fox_branch"], "nt", [BF16])
    grads["w_fox_branch"], = matmul("gw_branch_fox", o_fox, dy_fox, "tn", [BF16], **TN_TILES)

    to_packed_b = jax.linear_transpose(unpack_b, jax.ShapeDtypeStruct(gathered_b.shape, BF16))
    parts_b, = to_packed_b({n: grads.pop(n) for n in GROUP_B})
    delta_mla = attn_delta("mla_delta", do_mla, o_mla)
    dk, dv, dq, from_sibling = attn_bwd("mla_bwd", q, k, kvn, do_mla, lse_mla, delta_mla, mla_offs, QK_PAD, CHUNK,
                                        mla_scale, F32, comm=hook_swap_sibling(parts_b))

    delta_fox = attn_delta("fox_delta", do_fox, o_fox)

    def chips_behind_fox_bwd(pair):
        dfk, dfv, dfq, dcum, from_chips = attn_bwd("fox_bwd", fox, fox, fox, do_fox, lse_fox, delta_fox, fox_offs,
                                                   HEAD_V, 1, fox_scale, BF16, cum=cum, comm=hook_swap_chips(pair))
        return from_chips, (dfq.astype(BF16), dfk, dfv, dcum)
    g_packed_b, (dfq, dfk, dfv, dcum) = reduce_scatter_tail(parts_b, from_sibling, chips_behind_fox_bwd, "b")
    dff_t, d_bias = fox_cum_bwd("fox_cum_bwd", ff_t, f_bias, dcum.reshape(HEADS, s_len))
    dq_r = rope_heads("rope_q_bwd", dq, inv_tabs, BF16)
    dkvn, dkr = dk_split("dk_split", dk, dv, inv_tabs)
    dcq, = matmul("d_up_q", dq_r, w["w_uq"], "nt", [F32])
    grads["w_uq"], = matmul("gw_uq", cq, dq_r, "tn", [BF16], tm=512, tn=2048, tk=2048)
    dckv, = matmul("d_up_kv", dkvn, w["w_kv"], "nt", [F32])
    grads["w_kv"], = matmul("gw_kv", ckv, dkvn, "tn", [BF16], tm=256, tn=2048, tk=2048)
    dcq_raw, d_qn = rms_bwd("rms_q_bwd", (lat, Q_LORA, 0), dcq, g_q, out_dtype=BF16)
    dckv_raw, d_kvn = rms_bwd("rms_kv_bwd", (lat, KV_LORA, Q_LORA // KV_LORA), dckv, g_kv, out_dtype=BF16)

    pad = jnp.zeros((s_len, LAT_W - OFF_FQ - HEADS), BF16)
    dproj = jnp.concatenate([dcq_raw, dckv_raw, dkr[:, :ROPE].astype(BF16), dff_t.T.astype(BF16), pad,
                             dfq, dfk, dfv, dgraw], axis=1)
    w_in_p = jnp.concatenate([w["w_lat"], w["w_fox"], w["w_gate"]], axis=1)
    gw_in, = matmul("gw_in", xn, dproj, "tn", [BF16], tm=1024, tn=1152, tk=2048)
    grads["w_lat"], grads["w_fox"], grads["w_gate"] = (gw_in[:, :LAT_W], gw_in[:, LAT_W:LAT_W + FOX_W],
                                                      gw_in[:, LAT_W + FOX_W:])

    parts_a = to_packed_a(grads)
    from_sibling_a = comm_swap_sibling("comm_rs_sibling_a", parts_a)

    def chips_behind_d_proj(pair):
        dxn, from_chips = matmul("d_proj", dproj, w_in_p, "nt", [F32], tk=2688, comm=hook_swap_chips(pair))
        return from_chips, dxn
    g_packed_a, dxn = reduce_scatter_tail(parts_a, from_sibling_a, chips_behind_d_proj, "a")
    dx, d_attn = rms_bwd("rms_attn_bwd", x, dxn, g_attn, dres=dh1)

    small_grads = {"attn_norm": d_attn, "fox_f_bias": d_bias, "q_norm": d_qn, "kv_norm": d_kvn,
                   "mlp_norm": d_mlp, "final_norm": d_final}
    return loss, dx, g_packed_a, g_packed_b, small_grads


def kernel(x, attn_norm, w_in, fox_f_bias, q_norm, w_uq, kv_norm, w_ukv, w_mla_branch, w_fox_branch, w_out, mlp_norm, w_up, w_down, final_norm, loss_target, m_attn_norm, m_w_in, m_fox_f_bias, m_q_norm, m_w_uq, m_kv_norm, m_w_ukv, m_w_mla_branch, m_w_fox_branch, m_w_out, m_mlp_norm, m_w_up, m_w_down, m_final_norm, v_attn_norm, v_w_in, v_fox_f_bias, v_q_norm, v_w_uq, v_kv_norm, v_w_ukv, v_w_mla_branch, v_w_fox_branch, v_w_out, v_mlp_norm, v_w_up, v_w_down, v_final_norm):
    given = dict(locals())
    big = {n: given[n][0] for n in BIG}
    small = {n: given[n] for n in SMALL}
    shard_shapes = {n: tuple(big[n].shape) for n in BIG}

    packed_a = pack_local({n: big[n].astype(BF16) for n in GROUP_A}, GROUP_A)
    packed_b = pack_local({n: big[n].astype(BF16) for n in GROUP_B}, GROUP_B)
    gathered_a = comm_allgather("comm_allgather_a", packed_a)
    unpack = functools.partial(unpack_a, shard_shapes=shard_shapes)
    w_a = unpack(gathered_a)

    transpose_a = jax.linear_transpose(unpack, jax.ShapeDtypeStruct(gathered_a.shape, BF16))
    loss_part, dx, g_packed_a, g_packed_b, small_grads = local_step(
        x[0], loss_target[0], w_a, small, packed_b, shard_shapes, lambda grads: transpose_a(grads)[0])

    small_sum = comm_allreduce_small("comm_allreduce_small", pack_small(small_grads, loss_part[0, 0]))
    g_small, loss = unpack_small(small_sum)

    grad_w, delta_w, new_m, new_v = {}, {}, {}, {}
    for names, g_packed in ((GROUP_A, g_packed_a), (GROUP_B, g_packed_b)):
        r0 = 0
        for n in names:
            rs, cs = shard_shapes[n]
            nr = rs * cs // PACK_C
            g = g_packed[r0:r0 + nr].reshape(rs, cs)
            r0 += nr
            d, m_new, v_new = adamw("adamw_" + n, big[n], g, given["m_" + n][0], given["v_" + n][0])
            grad_w[n], delta_w[n], new_m[n], new_v[n] = g[None], d[None], m_new[None], v_new[None]
    zero = jnp.zeros((), F32)
    d_s, m_s, v_s = adamw("adamw_small", pack_small(small, zero), small_sum * _small_mask(),
                          pack_small({n: given["m_" + n] for n in SMALL}, zero),
                          pack_small({n: given["v_" + n] for n in SMALL}, zero), tr=SMALL_ROWS)
    d_small, _ = unpack_small(d_s)
    m_small, _ = unpack_small(m_s)
    v_small, _ = unpack_small(v_s)
    for n in SMALL:
        shape = given[n].shape
        grad_w[n], delta_w[n] = g_small[n].reshape(shape), d_small[n].reshape(shape)
        new_m[n], new_v[n] = m_small[n].reshape(shape), v_small[n].reshape(shape)

    order = ["attn_norm", "w_in", "fox_f_bias", "q_norm", "w_uq", "kv_norm", "w_ukv", "w_mla_branch", "w_fox_branch",
             "w_out", "mlp_norm", "w_up", "w_down", "final_norm"]
    return (loss, dx[None], *[grad_w[n] for n in order], *[delta_w[n] for n in order],
            *[new_m[n] for n in order], *[new_v[n] for n in order])


def _small_mask():
    used = sum(SMALL_N[n] for n in SMALL)
    return (jnp.arange(SMALL_ROWS * PACK_C) < used).astype(F32).reshape(SMALL_ROWS, PACK_C)
```

```python
import functools
import math

import jax
import jax.numpy as jnp
from jax import lax
from jax.experimental import pallas as pl
from jax.experimental.pallas import tpu as pltpu

F32 = jnp.float32
BF16 = jnp.bfloat16
MESH = pl.DeviceIdType.MESH

D_MODEL = 2048
HEADS = 8
Q_LORA = 512
KV_LORA = 256
NOPE = 128
ROPE = 64
HEAD_V = 128
D_FF = 4 * D_MODEL
CHUNK = 64
EPS = 1e-6
ROPE_THETA = 10000.0
OFF_KR = Q_LORA + KV_LORA
OFF_FQ = OFF_KR + ROPE
OFF_FF = OFF_FQ + 3 * HEADS * HEAD_V
OFF_G = OFF_FF + HEADS
D_IN = OFF_G + 2 * D_MODEL

LAT_W = 896
FOX_W = 3 * HEADS * HEAD_V
GATE_W = 2 * D_MODEL
PROJ_W = LAT_W + FOX_W + GATE_W
QK_PAD = 256

ADAM_LR = 0.001
ADAM_B1 = 0.9
ADAM_B2 = 0.999
ADAM_EPS = 1e-08
ADAM_WD = 0.01
ADAM_STEP = 10

N_DEV = 8
PACK_C = 1024
NEG = -1e30
LOG2E = math.log2(math.e)

VMEM_LIMIT = 56 * 1024 * 1024

BIG = ("w_in", "w_uq", "w_ukv", "w_mla_branch", "w_fox_branch", "w_out", "w_up", "w_down")
GROUP_A = ("w_in", "w_uq", "w_ukv")
GROUP_B = ("w_mla_branch", "w_fox_branch", "w_out", "w_up", "w_down")
ROW_SHARDED = ("w_out", "w_down")
SMALL = ("attn_norm", "fox_f_bias", "q_norm", "kv_norm", "mlp_norm", "final_norm")
SMALL_N = {"attn_norm": D_MODEL, "fox_f_bias": HEADS, "q_norm": Q_LORA, "kv_norm": KV_LORA,
           "mlp_norm": D_MODEL, "final_norm": D_MODEL}
SMALL_ROWS = 8


def _params(sem):
    return pltpu.CompilerParams(dimension_semantics=sem, vmem_limit_bytes=VMEM_LIMIT)


def _rows(name, fn, row_ins, const_ins, outs, reds=(), tr=256):
    norm = [(a, a.shape[1], 0) if not isinstance(a, tuple) else a for a in row_ins]
    n_rows = norm[0][0].shape[0]
    tr = min(tr, n_rows)
    assert n_rows % tr == 0, (name, n_rows, tr)
    n_in, n_out, n_red = len(norm) + len(const_ins), len(outs), len(reds)

    def body(*refs):
        vals = [r[...] for r in refs[:n_in]]
        out_refs = refs[n_in:n_in + n_out]
        red_refs = refs[n_in + n_out:]
        out_vals, red_vals = fn(*vals)
        for r, v in zip(out_refs, out_vals):
            r[...] = v.astype(r.dtype)
        if n_red:
            @pl.when(pl.program_id(0) == 0)
            def _():
                for r in red_refs:
                    r[...] = jnp.zeros_like(r)
            for r, v in zip(red_refs, red_vals):
                r[...] += v

    in_specs = [pl.BlockSpec((tr, w), functools.partial(lambda i, cb: (i, cb), cb=cb)) for _, w, cb in norm]
    in_specs += [pl.BlockSpec(a.shape, lambda i: (0, 0)) for a in const_ins]
    out_specs = [pl.BlockSpec((tr, c), lambda i: (i, 0)) for c, _ in outs]
    out_specs += [pl.BlockSpec((1, c), lambda i: (0, 0)) for c in reds]
    out_shape = [jax.ShapeDtypeStruct((n_rows, c), dt) for c, dt in outs]
    out_shape += [jax.ShapeDtypeStruct((1, c), F32) for c in reds]
    res = pl.pallas_call(
        body, name=name, grid=(n_rows // tr,), in_specs=in_specs, out_specs=out_specs, out_shape=out_shape,
        compiler_params=_params(("arbitrary",)),
    )(*[a for a, _, _ in norm], *const_ins)
    return res


def _rstd(x):
    return lax.rsqrt(jnp.mean(x * x, axis=-1, keepdims=True) + EPS)


def rms_fwd(name, x, gain, tr=256):
    width = x[1] if isinstance(x, tuple) else x.shape[1]

    def fn(xv, g):
        return (xv * _rstd(xv) * g,), ()
    return _rows(name, fn, [x], [gain], [(width, BF16)], tr=tr)[0]


def rms_bwd(name, x, dy, gain, dres=None, out_dtype=F32, tr=256):
    width = x[1] if isinstance(x, tuple) else x.shape[1]

    def fn(xv, dyv, *rest):
        g = rest[-1]
        r = _rstd(xv)
        n = xv * r
        dyv = dyv.astype(F32)
        dn = dyv * g
        dx = r * (dn - n * jnp.mean(dn * n, axis=-1, keepdims=True))
        if dres is not None:
            dx = dx + rest[0]
        return (dx,), (jnp.sum(dyv * n, axis=0, keepdims=True),)

    ins = [x, dy] + ([dres] if dres is not None else [])
    return _rows(name, fn, ins, [gain], [(width, out_dtype)], [width], tr=tr)


def _rope_lanes(t, c, s1, s2):
    return t * c + pltpu.roll(t, 96, 1) * s1 + pltpu.roll(t, 32, 1) * s2


def rope_heads(name, x, tabs, out_dtype):
    def fn(xv, c, s1, s2):
        xv = xv.astype(F32)
        parts = []
        for h in range(HEADS):
            parts.append(xv[:, h * QK_PAD:h * QK_PAD + NOPE])
            parts.append(_rope_lanes(xv[:, h * QK_PAD + NOPE:(h + 1) * QK_PAD], c, s1, s2))
        return (jnp.concatenate(parts, axis=1),), ()
    return _rows(name, fn, [x, *tabs], [], [(HEADS * QK_PAD, out_dtype)])[0]


def rope_block(name, x, col_block, tabs, out_dtype):
    def fn(xv, c, s1, s2):
        return (_rope_lanes(xv.astype(F32), c, s1, s2),), ()
    return _rows(name, fn, [(x, 128, col_block), *tabs], [], [(128, out_dtype)])[0]


def k_assemble(name, kvn, kr):
    def fn(knp, krv):
        parts = []
        for h in range(HEADS):
            parts.append(knp[:, h * NOPE:(h + 1) * NOPE])
            parts.append(krv)
        return (jnp.concatenate(parts, axis=1),), ()
    return _rows(name, fn, [(kvn, HEADS * NOPE, 0), kr], [], [(HEADS * QK_PAD, BF16)])[0]


def dk_split(name, dk, dv, inv_tabs):
    def fn(dkv, dvv, c, s1, s2):
        parts = []
        acc = None
        for h in range(HEADS):
            parts.append(dkv[:, h * QK_PAD:h * QK_PAD + NOPE].astype(BF16))
            t = dkv[:, h * QK_PAD + NOPE:(h + 1) * QK_PAD]
            acc = t if acc is None else acc + t
        parts.append(dvv)
        return (jnp.concatenate(parts, axis=1), _rope_lanes(acc, c, s1, s2)), ()
    return _rows(name, fn, [dk, dv, *inv_tabs], [], [(2 * HEADS * NOPE, BF16), (128, F32)])


def gate_mix(name, graw, y_mla, y_fox):
    def fn(g, ya, yb):
        ga = jax.nn.sigmoid(g[:, :D_MODEL])
        gb = jax.nn.sigmoid(g[:, D_MODEL:])
        return (ga * ya + gb * yb,), ()
    return _rows(name, fn, [graw, y_mla, y_fox], [], [(D_MODEL, BF16)])[0]


def gate_mix_bwd(name, graw, y_mla, y_fox, dmix):
    def fn(g, ya, yb, dm):
        ga = jax.nn.sigmoid(g[:, :D_MODEL])
        gb = jax.nn.sigmoid(g[:, D_MODEL:])
        dgraw = jnp.concatenate([dm * ya * ga * (1.0 - ga), dm * yb * gb * (1.0 - gb)], axis=1)
        return (dgraw, dm * ga, dm * gb), ()
    return _rows(name, fn, [graw, y_mla, y_fox, dmix], [], [(GATE_W, BF16), (D_MODEL, BF16), (D_MODEL, BF16)], tr=128)


def loss_head(name, h2, target, gain):
    inv_d = 1.0 / D_MODEL

    def fn(h, t, g):
        r = _rstd(h)
        n = h * r
        err = n * g - t
        dy = err * inv_d
        dn = dy * g
        dh = r * (dn - n * jnp.mean(dn * n, axis=-1, keepdims=True))
        part = 0.5 * inv_d * jnp.sum(jnp.sum(err * err, axis=1, keepdims=True), axis=0, keepdims=True)
        return (dh,), (jnp.sum(dy * n, axis=0, keepdims=True), jnp.broadcast_to(part, (1, 128)))
    return _rows(name, fn, [h2, target], [gain], [(D_MODEL, F32)], [D_MODEL, 128])


def adamw(name, w, g, m, v, tr=256):
    c1 = 1.0 - ADAM_B1 ** ADAM_STEP
    c2 = 1.0 - ADAM_B2 ** ADAM_STEP

    def fn(wv, gv, mv, vv):
        m_new = ADAM_B1 * mv + (1.0 - ADAM_B1) * gv
        v_new = ADAM_B2 * vv + (1.0 - ADAM_B2) * (gv * gv)
        delta = -ADAM_LR * ((m_new / c1) / (jnp.sqrt(v_new / c2) + ADAM_EPS) + ADAM_WD * wv)
        return (delta, m_new, v_new), ()
    cols = w.shape[1]
    return _rows(name, fn, [w, g, m, v], [], [(cols, F32)] * 3, tr=tr)


def _row_tile(n_rows, cap=640):
    return max(t for t in range(16, cap + 1, 16) if n_rows % t == 0)


def add_pairs(name, a, b):
    def fn(av, bv):
        return (av.astype(F32) + bv.astype(F32),), ()
    return _rows(name, fn, [a, b], [], [(a.shape[1], BF16)], tr=_row_tile(a.shape[0]))[0]


def add_final(name, a, b, r0, r1, r2):
    def fn(av, bv, r0v, r1v, r2v):
        return (((av.astype(F32) + bv.astype(F32)) + r0v.astype(F32)) + r1v.astype(F32) + r2v.astype(F32),), ()
    return _rows(name, fn, [a, b, r0, r1, r2], [], [(a.shape[1], F32)], tr=_row_tile(a.shape[0]))[0]


TN_TILES = dict(tm=1024, tn=1024, tk=2048)
_DIMS = {"nn": (((1,), (0,)), ((), ())), "nt": (((1,), (1,)), ((), ())), "tn": (((0,), (0,)), ((), ()))}


def matmul(name, a, b, mode, outs, epilogue=None, extras=(), tm=1024, tn=1024, tk=2048, comm=None):
    if mode == "tn":
        kdim, m = a.shape
    else:
        m, kdim = a.shape
    n = b.shape[0] if mode == "nt" else b.shape[1]
    tm, tn, tk = min(tm, m), min(tn, n), min(tk, kdim)
    assert m % tm == 0 and n % tn == 0 and kdim % tk == 0, (name, a.shape, b.shape)
    nk = kdim // tk
    n_ex, n_out = len(extras), len(outs)
    dims = _DIMS[mode]
    grid = (m // tm, n // tn, nk)

    def body(*refs):
        ins, c_in, out_refs, c_out, scr, c_sems = _split_refs(refs, 2 + n_ex, comm, n_out, 1 if nk > 1 else 0)
        a_ref, b_ref = ins[:2]
        ex_refs = ins[2:]
        _comm_start(comm, c_in, c_out, c_sems, *grid)

        def finish(acc):
            vals = (acc,) if epilogue is None else epilogue(acc, *[r[...] for r in ex_refs])
            for r, v in zip(out_refs, vals):
                r[...] = v.astype(r.dtype)

        part = lax.dot_general(a_ref[...].astype(BF16), b_ref[...].astype(BF16), dims, preferred_element_type=F32)
        if nk == 1:
            finish(part)
        else:
            acc_ref = scr[0]
            k = pl.program_id(2)

            @pl.when(k == 0)
            def _():
                acc_ref[...] = part

            @pl.when(k > 0)
            def _():
                acc_ref[...] += part

            @pl.when(k == nk - 1)
            def _():
                finish(acc_ref[...])
        _comm_wait(comm, c_in, c_out, c_sems, *grid)

    a_spec = pl.BlockSpec((tk, tm), lambda i, j, k: (k, i)) if mode == "tn" else pl.BlockSpec((tm, tk), lambda i, j, k: (i, k))
    b_spec = pl.BlockSpec((tn, tk), lambda i, j, k: (j, k)) if mode == "nt" else pl.BlockSpec((tk, tn), lambda i, j, k: (k, j))
    tile = pl.BlockSpec((tm, tn), lambda i, j, k: (i, j))
    c_ins, c_outs, c_shapes, c_scratch, aliases = _with_comm(comm, 2 + n_ex, n_out)
    sem = ("arbitrary",) * 3 if comm else ("parallel", "parallel", "arbitrary")
    res = pl.pallas_call(
        body, name=name, grid=grid,
        in_specs=[a_spec, b_spec] + [tile] * n_ex + c_ins,
        out_specs=[tile] * n_out + c_outs,
        out_shape=[jax.ShapeDtypeStruct((m, n), dt) for dt in outs] + c_shapes,
        scratch_shapes=([pltpu.VMEM((tm, tn), F32)] if nk > 1 else []) + c_scratch,
        input_output_aliases=aliases,
        compiler_params=_params(sem),
    )(a, b, *extras, *(comm.args if comm else []))
    return res


_NT = (((1,), (1,)), ((), ()))
_NN = (((1,), (0,)), ((), ()))


def _mask(bq, chunk, transposed, row0=0, shape=None):
    shape = (bq, bq) if shape is None else shape
    row = lax.broadcasted_iota(jnp.int32, shape, 0) + row0
    col = lax.broadcasted_iota(jnp.int32, shape, 1)
    if chunk > 1:
        row, col = row // chunk, col // chunk
    return (row <= col) if transposed else (col <= row)


def _row_layout(a, bq):
    h, s, _ = a.shape
    return a.reshape(h, s // bq, 1, bq)


def _split_refs(refs, n_in, comm, n_out, n_scr):
    n_ci = len(comm.args) if comm else 0
    n_co = len(comm.out_shapes) if comm else 0
    cuts = [n_in, n_ci, n_out, n_co, n_scr, 3 if comm else 0]
    parts, at = [], 0
    for n in cuts:
        parts.append(list(refs[at:at + n]))
        at += n
    assert at == len(refs), (at, len(refs))
    return parts


def _at_step(grid, last):
    hit = None
    for axis, n in enumerate(grid):
        here = pl.program_id(axis) == (n - 1 if last else 0)
        hit = here if hit is None else jnp.logical_and(hit, here)
    return hit


def _comm_start(comm, c_in, c_out, c_sems, *grid):
    if comm is not None:
        @pl.when(_at_step(grid, False))
        def _():
            comm.start(c_in, c_out, c_sems)


def _comm_wait(comm, c_in, c_out, c_sems, *grid):
    if comm is not None:
        @pl.when(_at_step(grid, True))
        def _():
            comm.wait(c_in, c_out, c_sems)


def attn_fwd(name, q, k, v, offs, dqk, chunk, scale, cum=None, bq=512, comm=None, split_p=False):
    s_len = q.shape[0]
    nq = s_len // bq
    qoff, koff, voff = offs
    has_bias = cum is not None
    scale2 = scale * LOG2E

    n_in = 5 if has_bias else 3

    def body(*refs):
        ins, c_in, outs, c_out, scr, c_sems = _split_refs(refs, n_in, comm, 2, 3)
        q_ref, k_ref, v_ref = ins[:3]
        if has_bias:
            cc_ref, cr_ref = ins[3:]
        o_ref, lse_ref = outs
        m_s, l_s, acc_s = scr
        _comm_start(comm, c_in, c_out, c_sems, HEADS, nq)
        i = pl.program_id(1)
        qv = q_ref[...]
        m_s[...] = jnp.full_like(m_s, NEG)
        l_s[...] = jnp.zeros_like(l_s)
        acc_s[...] = jnp.zeros_like(acc_s)

        def step(j, masked):
            off = pl.multiple_of(j * bq, bq)
            kj = k_ref[pl.ds(off, bq), :]
            vj = v_ref[pl.ds(off, bq), :]
            st = lax.dot_general(kj, qv, _NT, preferred_element_type=F32) * scale2
            if has_bias:
                st = st + cr_ref[...] - cc_ref[pl.ds(off, bq), :]
            if masked:
                st = jnp.where(_mask(bq, chunk, True), st, NEG)
            m_prev = m_s[...]
            m_new = jnp.maximum(m_prev, jnp.max(st, axis=0, keepdims=True))
            alpha = jnp.exp2(m_prev - m_new)
            pt = jnp.exp2(st - m_new)
            l_s[...] = alpha * l_s[...] + jnp.sum(pt, axis=0, keepdims=True)
            p_hi = pt.astype(BF16)
            pv = lax.dot_general(vj, p_hi, _DIMS["tn"], preferred_element_type=F32)
            if split_p:
                p_lo = (pt - p_hi.astype(F32)).astype(BF16)
                pv = pv + lax.dot_general(vj, p_lo, _DIMS["tn"], preferred_element_type=F32)
            acc_s[...] = alpha * acc_s[...] + pv
            m_s[...] = m_new

        def pair_body(jj, carry):
            step(2 * jj, False)
            step(2 * jj + 1, False)
            return carry

        lax.fori_loop(0, i // 2, pair_body, 0)

        @pl.when(i % 2 == 1)
        def _():
            step(i - 1, False)

        step(i, True)
        o_ref[...] = (acc_s[...] / l_s[...]).T.astype(o_ref.dtype)
        lse_ref[...] = m_s[...] + jnp.log2(l_s[...])
        _comm_wait(comm, c_in, c_out, c_sems, HEADS, nq)

    in_specs = [
        pl.BlockSpec((bq, dqk), lambda h, i: (i, qoff + h)),
        pl.BlockSpec((s_len, dqk), lambda h, i: (0, koff + h)),
        pl.BlockSpec((s_len, HEAD_V), lambda h, i: (0, voff + h)),
    ]
    args = [q, k, v]
    if has_bias:
        in_specs += [pl.BlockSpec((None, s_len, 1), lambda h, i: (h, 0, 0)),
                     pl.BlockSpec((None, None, 1, bq), lambda h, i: (h, i, 0, 0))]
        args += [cum, _row_layout(cum, bq)]
    c_ins, c_outs, c_shapes, c_scratch, aliases = _with_comm(comm, len(args), 2)
    o, lse_rows, *comm_out = pl.pallas_call(
        body, name=name, grid=(HEADS, nq), in_specs=in_specs + c_ins,
        out_specs=[pl.BlockSpec((bq, HEAD_V), lambda h, i: (i, h)),
                   pl.BlockSpec((None, None, 1, bq), lambda h, i: (h, i, 0, 0))] + c_outs,
        out_shape=[jax.ShapeDtypeStruct((s_len, HEADS * HEAD_V), F32),
                   jax.ShapeDtypeStruct((HEADS, nq, 1, bq), F32)] + c_shapes,
        scratch_shapes=[pltpu.VMEM((1, bq), F32), pltpu.VMEM((1, bq), F32), pltpu.VMEM((HEAD_V, bq), F32)] + c_scratch,
        input_output_aliases=aliases,
        compiler_params=_params(("arbitrary", "arbitrary")),
    )(*args, *(comm.args if comm else []))
    return (o, lse_rows, *comm_out)


def _bwd_block(kv, vv, qi, doi, lse_row, scale2, bias, masked, bq, chunk):
    st = lax.dot_general(kv, qi, _NT, preferred_element_type=F32) * scale2
    if bias is not None:
        st = st + bias[0] - bias[1]
    if masked:
        st = jnp.where(_mask(bq, chunk, True), st, NEG)
    pt = jnp.exp2(st - lse_row)
    dpt = lax.dot_general(vv, doi, _NT, preferred_element_type=F32)
    return pt, dpt


def attn_delta(name, do, o, bq=512):
    s_len = do.shape[0]
    nq = s_len // bq

    def body(do_ref, o_ref, out_ref):
        prod = do_ref[...].astype(F32) * o_ref[...].astype(F32)
        ones = jnp.ones((8, HEAD_V), BF16)
        total = None
        for part in reversed(_split3(prod)):
            term = lax.dot_general(ones, part, _NT, preferred_element_type=F32)
            total = term if total is None else total + term
        out_ref[...] = total[0:1, :]

    blk = pl.BlockSpec((bq, HEAD_V), lambda h, i: (i, h))
    return pl.pallas_call(
        body, name=name, grid=(HEADS, nq), in_specs=[blk, blk],
        out_specs=pl.BlockSpec((None, None, 1, bq), lambda h, i: (h, i, 0, 0)),
        out_shape=jax.ShapeDtypeStruct((HEADS, nq, 1, bq), F32),
        compiler_params=_params(("parallel", "parallel")),
    )(do, o)


def attn_bwd(name, q, k, v, do, lse, delta, offs, dqk, chunk, scale, out_dtype, cum=None, bq=512, comm=None):
    s_len = q.shape[0]
    nq = s_len // bq
    qoff, koff, voff = offs
    has_bias = cum is not None
    scale2 = scale * LOG2E
    n_in, n_out = (8, 4) if has_bias else (6, 3)

    def body(*refs):
        ins, c_in, outs, c_out, scr, c_sems = _split_refs(refs, n_in, comm, n_out, n_out - 1)
        k_ref, v_ref, q_ref, do_ref, lse_ref, delta_ref = ins[:6]
        dk_ref, dv_ref, dq_ref = outs[:3]
        dk_s, dv_s = scr[:2]
        if has_bias:
            cc_ref, cr_ref = ins[6:]
            dc_ref, dc_s = outs[3], scr[2]
        _comm_start(comm, c_in, c_out, c_sems, HEADS, nq)
        j = pl.program_id(1)
        kv = k_ref[...]
        vv = v_ref[...]
        dk_s[...] = jnp.zeros_like(dk_s)
        dv_s[...] = jnp.zeros_like(dv_s)
        if has_bias:
            dc_s[...] = jnp.zeros_like(dc_s)

        @pl.when(j == 0)
        def _():
            dq_ref[...] = jnp.zeros_like(dq_ref)

        def step(i, masked):
            off = pl.multiple_of(i * bq, bq)
            qi = q_ref[pl.ds(off, bq), :]
            doi = do_ref[pl.ds(off, bq), :].astype(BF16)
            bias = (cr_ref[i], cc_ref[...]) if has_bias else None
            pt, dpt = _bwd_block(kv, vv, qi, doi, lse_ref[i], scale2, bias, masked, bq, chunk)
            dv_s[...] += lax.dot_general(pt.astype(BF16), doi, _NN, preferred_element_type=F32)
            dst = pt * (dpt - delta_ref[i])
            if has_bias:
                dc_s[...] -= jnp.sum(dst, axis=-1, keepdims=True)
            dst = dst.astype(BF16)
            dk_s[...] += lax.dot_general(dst, qi, _NN, preferred_element_type=F32)
            dq_ref[pl.ds(off, bq), :] += lax.dot_general(dst, kv, _DIMS["tn"], preferred_element_type=F32) * scale

        step(j, True)

        def loop_body(i, carry):
            step(i, False)
            return carry

        lax.fori_loop(j + 1, nq, loop_body, 0)
        dk_ref[...] = (dk_s[...] * scale).astype(dk_ref.dtype)
        dv_ref[...] = dv_s[...].astype(dv_ref.dtype)
        if has_bias:
            dc_ref[...] = dc_s[...]
        _comm_wait(comm, c_in, c_out, c_sems, HEADS, nq)

    rows = pl.BlockSpec((None, nq, 1, bq), lambda h, j: (h, 0, 0, 0))
    in_specs = [
        pl.BlockSpec((bq, dqk), lambda h, j: (j, koff + h)),
        pl.BlockSpec((bq, HEAD_V), lambda h, j: (j, voff + h)),
        pl.BlockSpec((s_len, dqk), lambda h, j: (0, qoff + h)),
        pl.BlockSpec((s_len, HEAD_V), lambda h, j: (0, h)),
        rows,
        rows,
    ]
    args = [k, v, q, do, lse, delta]
    out_specs = [pl.BlockSpec((bq, dqk), lambda h, j: (j, h)), pl.BlockSpec((bq, HEAD_V), lambda h, j: (j, h)),
                 pl.BlockSpec((s_len, dqk), lambda h, j: (0, h))]
    out_shape = [jax.ShapeDtypeStruct((s_len, HEADS * dqk), out_dtype),
                 jax.ShapeDtypeStruct((s_len, HEADS * HEAD_V), BF16),
                 jax.ShapeDtypeStruct((s_len, HEADS * dqk), F32)]
    scratch = [pltpu.VMEM((bq, dqk), F32), pltpu.VMEM((bq, HEAD_V), F32)]
    if has_bias:
        in_specs += [pl.BlockSpec((None, bq, 1), lambda h, j: (h, j, 0)), rows]
        args += [cum, _row_layout(cum, bq)]
        out_specs.append(pl.BlockSpec((None, bq, 1), lambda h, j: (h, j, 0)))
        out_shape.append(jax.ShapeDtypeStruct((HEADS, s_len, 1), F32))
        scratch.append(pltpu.VMEM((bq, 1), F32))
    c_ins, c_outs, c_shapes, c_scratch, aliases = _with_comm(comm, len(args), n_out)
    return pl.pallas_call(
        body, name=name, grid=(HEADS, nq), in_specs=in_specs + c_ins, out_specs=out_specs + c_outs,
        out_shape=out_shape + c_shapes, scratch_shapes=scratch + c_scratch, input_output_aliases=aliases,
        compiler_params=_params(("arbitrary", "arbitrary")),
    )(*args, *(comm.args if comm else []))


_CUM_BLK = 512


def _split3(x):
    hi = x.astype(BF16)
    r1 = x - hi.astype(F32)
    mid = r1.astype(BF16)
    lo = (r1 - mid.astype(F32)).astype(BF16)
    return hi, mid, lo


def _tri_dot(x, tri):
    hi, mid, lo = _split3(x)
    out = lax.dot_general(lo, tri, _NN, preferred_element_type=F32)
    out = out + lax.dot_general(mid, tri, _NN, preferred_element_type=F32)
    return out + lax.dot_general(hi, tri, _NN, preferred_element_type=F32)


def fox_cum_fwd(name, ff_t, bias):
    s_len = ff_t.shape[1]
    nb = s_len // _CUM_BLK

    def body(ff_ref, b_ref, cum_ref):
        row = lax.broadcasted_iota(jnp.int32, (_CUM_BLK, _CUM_BLK), 0)
        col = lax.broadcasted_iota(jnp.int32, (_CUM_BLK, _CUM_BLK), 1)
        tri = (row <= col).astype(BF16)
        carry = jnp.zeros((HEADS, 1), F32)
        for b in range(nb):
            z = ff_ref[:, b * _CUM_BLK:(b + 1) * _CUM_BLK] + b_ref[...]
            logf = jnp.minimum(z, 0.0) - jnp.log1p(jnp.exp(-jnp.abs(z)))
            blk = _tri_dot(logf, tri) + carry
            cum_ref[:, b * _CUM_BLK:(b + 1) * _CUM_BLK] = blk
            carry = blk[:, _CUM_BLK - 1:_CUM_BLK]

    return pl.pallas_call(
        body, name=name, out_shape=jax.ShapeDtypeStruct((HEADS, s_len), F32),
        compiler_params=pltpu.CompilerParams(vmem_limit_bytes=VMEM_LIMIT),
    )(ff_t, bias)


def fox_cum_bwd(name, ff_t, bias, dcum):
    s_len = ff_t.shape[1]
    nb = s_len // _CUM_BLK

    def body(ff_ref, b_ref, dc_ref, dff_ref, db_ref):
        row = lax.broadcasted_iota(jnp.int32, (_CUM_BLK, _CUM_BLK), 0)
        col = lax.broadcasted_iota(jnp.int32, (_CUM_BLK, _CUM_BLK), 1)
        tri = (row >= col).astype(BF16)
        carry = jnp.zeros((HEADS, 1), F32)
        dbias = jnp.zeros((HEADS, 1), F32)
        for b in reversed(range(nb)):
            sl = slice(b * _CUM_BLK, (b + 1) * _CUM_BLK)
            dlogf = _tri_dot(dc_ref[:, sl], tri) + carry
            carry = dlogf[:, 0:1]
            z = ff_ref[:, sl] + b_ref[...]
            dz = dlogf / (1.0 + jnp.exp(z))
            dff_ref[:, sl] = dz
            dbias = dbias + jnp.sum(dz, axis=-1, keepdims=True)
        db_ref[...] = dbias

    return pl.pallas_call(
        body, name=name,
        out_shape=[jax.ShapeDtypeStruct((HEADS, s_len), F32), jax.ShapeDtypeStruct((HEADS, 1), F32)],
        compiler_params=pltpu.CompilerParams(vmem_limit_bytes=VMEM_LIMIT),
    )(ff_t, bias, dcum)


_ANY = pl.BlockSpec(memory_space=pl.ANY)


def _me():
    return lax.axis_index("x"), lax.axis_index("y"), lax.axis_index("c")


def comm_allgather(name, mine):
    n_rows, n_cols = mine.shape

    def body(x_ref, out_ref, send_sems, recv_sems, local_sem):
        x, y, c = _me()
        sibling = (x, y, 1 - c)
        chips = [(1 - x, y), (x, 1 - y), (1 - x, 1 - y)]

        def blk(px, py, pc):
            return out_ref.at[pc * 4 + px * 2 + py]

        def copy(k, block, to, src=None):
            return pltpu.make_async_remote_copy(
                src_ref=blk(*block) if src is None else src, dst_ref=blk(*block),
                send_sem=send_sems.at[k], recv_sem=recv_sems.at[k], device_id=to, device_id_type=MESH)

        own = pltpu.make_async_copy(x_ref, blk(x, y, c), local_sem)
        own.start()
        first = [copy(0, (x, y, c), sibling, src=x_ref)]
        first += [copy(1 + j, (x, y, c), (*chip, c), src=x_ref) for j, chip in enumerate(chips)]
        for cp in first:
            cp.start()
        passed = [copy(4 + j, (*chip, c), sibling) for j, chip in enumerate(chips)]
        for j, chip in enumerate(chips):
            copy(1 + j, (*chip, c), (x, y, c)).wait_recv()
            passed[j].start()
        copy(0, sibling, (x, y, c)).wait_recv()
        for j, chip in enumerate(chips):
            copy(4 + j, (*chip, 1 - c), (x, y, c)).wait_recv()
        for cp in first + passed:
            cp.wait_send()
        own.wait()

    return pl.pallas_call(
        body, name=name, out_shape=jax.ShapeDtypeStruct((N_DEV, n_rows, n_cols), mine.dtype),
        in_specs=[_ANY], out_specs=_ANY,
        scratch_shapes=[pltpu.SemaphoreType.DMA((7,)), pltpu.SemaphoreType.DMA((7,)), pltpu.SemaphoreType.DMA],
    )(mine)


def comm_swap_sibling(name, parts):
    _, n_rows, n_cols = parts.shape

    def body(p_ref, got_ref, send_sem, recv_sem):
        x, y, c = _me()
        cp = pltpu.make_async_remote_copy(
            src_ref=p_ref.at[pl.ds((1 - c) * 4, 4)], dst_ref=got_ref, send_sem=send_sem, recv_sem=recv_sem,
            device_id=(x, y, 1 - c), device_id_type=MESH)
        cp.start()
        cp.wait()

    return pl.pallas_call(
        body, name=name, out_shape=jax.ShapeDtypeStruct((4, n_rows, n_cols), parts.dtype),
        in_specs=[_ANY], out_specs=_ANY,
        scratch_shapes=[pltpu.SemaphoreType.DMA, pltpu.SemaphoreType.DMA],
    )(parts)


class CommHook:
    def __init__(self, args, out_shapes, n_copies, copies, aliases=None):
        self.args, self.out_shapes, self.n_copies, self.copies = list(args), list(out_shapes), n_copies, copies
        self.aliases = aliases or {}

    def scratch(self):
        return [pltpu.SemaphoreType.DMA((self.n_copies,)), pltpu.SemaphoreType.DMA((self.n_copies,)),
                pltpu.SemaphoreType.DMA((1,))]

    def start(self, in_refs, out_refs, sems):
        sends, _, locs = self.copies(in_refs, out_refs, *sems)
        for cp in locs() + sends():
            cp.start()

    def wait(self, in_refs, out_refs, sems):
        sends, recvs, locs = self.copies(in_refs, out_refs, *sems)
        for cp in sends():
            cp.wait_send()
        for cp in recvs():
            cp.wait_recv()
        for cp in locs():
            cp.wait()


def _remote(src, dst, send_sem, recv_sem, to):
    return pltpu.make_async_remote_copy(src_ref=src, dst_ref=dst, send_sem=send_sem, recv_sem=recv_sem,
                                        device_id=to, device_id_type=MESH)


def hook_gather_first(mine):
    n_rows, n_cols = mine.shape

    def copies(ins, outs, send, recv, local):
        (x_ref,), (out_ref,) = ins, outs
        x, y, c = _me()
        me = c * 4 + x * 2 + y
        peers = [(x, y, 1 - c), (1 - x, y, c), (x, 1 - y, c), (1 - x, 1 - y, c)]

        def sends():
            return [_remote(x_ref, out_ref.at[me], send.at[k], recv.at[k], p) for k, p in enumerate(peers)]

        def recvs():
            return [_remote(x_ref, out_ref.at[pc * 4 + px * 2 + py], send.at[k], recv.at[k], (px, py, pc))
                    for k, (px, py, pc) in enumerate(peers)]

        return sends, recvs, lambda: [pltpu.make_async_copy(x_ref, out_ref.at[me], local.at[0])]

    return CommHook([mine], [jax.ShapeDtypeStruct((N_DEV, n_rows, n_cols), mine.dtype)], 4, copies)


def hook_gather_second(gathered):
    def copies(ins, outs, send, recv, local):
        (g_in,), (g_out,) = ins, outs
        x, y, c = _me()
        chips = [(1 - x, y), (x, 1 - y), (1 - x, 1 - y)]

        def sends():
            return [_remote(g_in.at[c * 4 + px * 2 + py], g_out.at[c * 4 + px * 2 + py], send.at[k], recv.at[k],
                            (x, y, 1 - c)) for k, (px, py) in enumerate(chips)]

        def recvs():
            return [_remote(g_in.at[(1 - c) * 4 + px * 2 + py], g_out.at[(1 - c) * 4 + px * 2 + py], send.at[k],
                            recv.at[k], (x, y, 1 - c)) for k, (px, py) in enumerate(chips)]

        return sends, recvs, lambda: []

    return CommHook([gathered], [jax.ShapeDtypeStruct(gathered.shape, gathered.dtype)], 3, copies, aliases={0: 0})


def hook_swap_sibling(parts):
    _, n_rows, n_cols = parts.shape

    def copies(ins, outs, send, recv, local):
        (p_ref,), (got_ref,) = ins, outs
        x, y, c = _me()

        def swap():
            return [_remote(p_ref.at[pl.ds((1 - c) * 4, 4)], got_ref, send.at[0], recv.at[0], (x, y, 1 - c))]

        return swap, swap, lambda: []

    return CommHook([parts], [jax.ShapeDtypeStruct((4, n_rows, n_cols), parts.dtype)], 1, copies)


def hook_swap_chips(parts):
    _, n_rows, n_cols = parts.shape

    def copies(ins, outs, send, recv, local):
        (p_ref,), (got_ref,) = ins, outs
        x, y, c = _me()
        chips = [(1 - x, y), (x, 1 - y), (1 - x, 1 - y)]

        def swaps():
            return [_remote(p_ref.at[2 * px + py], got_ref.at[k], send.at[k], recv.at[k], (px, py, c))
                    for k, (px, py) in enumerate(chips)]

        return swaps, swaps, lambda: []

    return CommHook([parts], [jax.ShapeDtypeStruct((3, n_rows, n_cols), parts.dtype)], 3, copies)


def _with_comm(comm, n_args, n_outs):
    if comm is None:
        return [], [], [], [], {}
    aliases = {n_args + a: n_outs + o for a, o in comm.aliases.items()}
    return [_ANY] * len(comm.args), [_ANY] * len(comm.out_shapes), comm.out_shapes, comm.scratch(), aliases


def comm_allreduce_small(name, mine):
    shape = mine.shape

    def body(x_ref, out_ref, buf, send_sems, recv_sems):
        x, y, c = _me()
        my_slot = c * 4 + x * 2 + y
        buf[my_slot] = x_ref[...]
        cps = []
        for k in range(1, N_DEV):
            dx, dy, dc = (k >> 2) & 1, (k >> 1) & 1, k & 1
            px, py, pc = x ^ dx, y ^ dy, c ^ dc
            send = pltpu.make_async_remote_copy(
                src_ref=x_ref, dst_ref=buf.at[my_slot], send_sem=send_sems.at[k - 1], recv_sem=recv_sems.at[k - 1],
                device_id=(px, py, pc), device_id_type=MESH)
            send.start()
            recv = pltpu.make_async_remote_copy(
                src_ref=x_ref, dst_ref=buf.at[pc * 4 + px * 2 + py], send_sem=send_sems.at[k - 1],
                recv_sem=recv_sems.at[k - 1], device_id=(px, py, pc), device_id_type=MESH)
            cps.append((send, recv))
        for send, recv in cps:
            send.wait_send()
            recv.wait_recv()
        total = buf[0]
        for s in range(1, N_DEV):
            total = total + buf[s]
        out_ref[...] = total

    vmem = pl.BlockSpec(memory_space=pltpu.VMEM)
    return pl.pallas_call(
        body, name=name, out_shape=jax.ShapeDtypeStruct(shape, F32), in_specs=[vmem], out_specs=vmem,
        scratch_shapes=[pltpu.VMEM((N_DEV,) + shape, F32), pltpu.SemaphoreType.DMA((7,)), pltpu.SemaphoreType.DMA((7,))],
    )(mine)


def _packed_rows(shape):
    rs, cs = shape
    if cs % PACK_C == 0:
        return rs * (cs // PACK_C)
    if cs > PACK_C // 2:
        return rs
    return rs * cs // PACK_C


def _to_rows(a):
    rs, cs = a.shape
    if cs % PACK_C == 0:
        return jnp.concatenate([a[:, i * PACK_C:(i + 1) * PACK_C] for i in range(cs // PACK_C)], axis=0)
    if cs > PACK_C // 2:
        return jnp.pad(a, ((0, 0), (0, PACK_C - cs)))
    return a.reshape(-1, PACK_C)


def _from_rows(rows, shape):
    rs, cs = shape
    lead = rows.shape[:-2]
    if cs % PACK_C == 0:
        return jnp.concatenate([rows[..., i * rs:(i + 1) * rs, :] for i in range(cs // PACK_C)], axis=-1)
    if cs > PACK_C // 2:
        return rows[..., :cs]
    return rows.reshape(*lead, rs, cs)


def pack_local(shards, names):
    flat = [_to_rows(shards[n]) for n in names]
    rows = sum(f.shape[0] for f in flat)
    pad = (-rows) % 128
    return jnp.concatenate(flat + [jnp.zeros((pad, PACK_C), flat[0].dtype)], axis=0)


def unpack_group(gathered, names, shard_shapes):
    _, n_rows, _ = gathered.shape
    by_block = gathered.reshape(2, 4, n_rows, PACK_C).transpose(1, 0, 2, 3).reshape(N_DEV, n_rows, PACK_C)
    full = {}
    r0 = 0
    for name in names:
        rs, cs = shard_shapes[name]
        nr = _packed_rows((rs, cs))
        piece = _from_rows(by_block[:, r0:r0 + nr, :], (rs, cs))
        r0 += nr
        if name in ROW_SHARDED:
            full[name] = piece.reshape(N_DEV * rs, cs)
        else:
            full[name] = piece.transpose(1, 0, 2).reshape(rs, N_DEV * cs)
    return full


def unpack_a(gathered, shard_shapes):
    full = unpack_group(gathered, GROUP_A, shard_shapes)
    w_in = full.pop("w_in")
    zeros = jnp.zeros((D_MODEL, LAT_W - OFF_FQ - HEADS), w_in.dtype)
    full["w_lat"] = jnp.concatenate([w_in[:, :OFF_FQ], w_in[:, OFF_FF:OFF_G], zeros], axis=1)
    full["w_fox"] = w_in[:, OFF_FQ:OFF_FF]
    full["w_gate"] = w_in[:, OFF_G:]
    w_uq = full.pop("w_uq").reshape(Q_LORA, HEADS, NOPE + ROPE)
    full["w_uq"] = jnp.pad(w_uq, ((0, 0), (0, 0), (0, QK_PAD - NOPE - ROPE))).reshape(Q_LORA, HEADS * QK_PAD)
    w_ukv = full.pop("w_ukv").reshape(KV_LORA, HEADS, 2, NOPE)
    full["w_kv"] = jnp.concatenate([w_ukv[:, :, 0, :].reshape(KV_LORA, HEADS * NOPE),
                                    w_ukv[:, :, 1, :].reshape(KV_LORA, HEADS * HEAD_V)], axis=1)
    return full


def pack_small(vals, loss):
    flat = [vals[n].reshape(-1) for n in SMALL] + [loss.reshape(-1)]
    used = sum(f.shape[0] for f in flat)
    flat.append(jnp.zeros((SMALL_ROWS * PACK_C - used,), F32))
    return jnp.concatenate(flat).reshape(SMALL_ROWS, PACK_C)


def unpack_small(packed):
    flat = packed.reshape(-1)
    out, off = {}, 0
    for n in SMALL:
        out[n] = flat[off:off + SMALL_N[n]]
        off += SMALL_N[n]
    return out, flat[off]


def rope_tables(s_len):
    pos = jnp.arange(s_len, dtype=F32)
    inv = 1.0 / (ROPE_THETA ** (jnp.arange(0, ROPE, 2, dtype=F32) / ROPE))
    ang = pos[:, None] * inv[None, :]
    cos, sin = jnp.cos(ang), jnp.sin(ang)
    zero = jnp.zeros_like(cos)
    c = jnp.concatenate([cos, cos, zero, zero], axis=1)
    s1 = jnp.concatenate([-sin, zero, zero, zero], axis=1)
    s2 = jnp.concatenate([zero, sin, zero, zero], axis=1)
    return (c, s1, s2), (c, -s1, -s2)


def reduce_scatter_tail(parts, from_sibling, from_chips_fn, names):
    cx, cy, cc = _me()
    n_rows = parts.shape[1]
    mine4 = lax.dynamic_slice_in_dim(parts, cc * 4, 4, axis=0)
    pair = add_pairs("rs_pair_sum_" + names, mine4.reshape(4 * n_rows, PACK_C), from_sibling.reshape(4 * n_rows, PACK_C))
    from_chips, extra = from_chips_fn(pair.reshape(4, n_rows, PACK_C))
    own = cx * 2 + cy
    total = add_final("rs_final_sum_" + names, lax.dynamic_index_in_dim(mine4, own, 0, keepdims=False),
                      lax.dynamic_index_in_dim(from_sibling, own, 0, keepdims=False),
                      from_chips[0], from_chips[1], from_chips[2])
    return total, extra


def local_step(x, target, w, small, packed_b, shard_shapes, to_packed_a):
    s_len = x.shape[0]
    tabs, inv_tabs = rope_tables(s_len)
    g_attn = small["attn_norm"].reshape(1, D_MODEL)
    g_q = small["q_norm"].reshape(1, Q_LORA)
    g_kv = small["kv_norm"].reshape(1, KV_LORA)
    g_mlp = small["mlp_norm"].reshape(1, D_MODEL)
    g_final = small["final_norm"].reshape(1, D_MODEL)
    f_bias = small["fox_f_bias"].reshape(HEADS, 1)
    mla_scale = 1.0 / math.sqrt(NOPE + ROPE)
    fox_scale = 1.0 / math.sqrt(HEAD_V)
    mla_offs = (0, 0, HEADS)
    fox_offs = (0, HEADS, 2 * HEADS)

    xn = rms_fwd("rms_attn", x, g_attn)
    lat, = matmul("proj_lat", xn, w["w_lat"], "nn", [F32])
    fox, = matmul("proj_fox", xn, w["w_fox"], "nn", [BF16])
    graw, = matmul("proj_gate", xn, w["w_gate"], "nn", [F32])
    cq = rms_fwd("rms_q", (lat, Q_LORA, 0), g_q)
    ckv = rms_fwd("rms_kv", (lat, KV_LORA, Q_LORA // KV_LORA), g_kv)
    qraw, = matmul("up_q", cq, w["w_uq"], "nn", [F32])
    q = rope_heads("rope_q", qraw, tabs, BF16)
    kvn, = matmul("up_kv", ckv, w["w_kv"], "nn", [BF16])
    kr = rope_block("rope_k", lat, OFF_KR // 128, tabs, BF16)
    k = k_assemble("k_assemble", kvn, kr)
    o_mla, lse_mla, gathered_b = attn_fwd("mla_fwd", q, k, kvn, mla_offs, QK_PAD, CHUNK, mla_scale,
                                          comm=hook_gather_first(packed_b))
    ff_t = lat[:, OFF_FQ:OFF_FQ + HEADS].T
    cum = fox_cum_fwd("fox_cum", ff_t, f_bias).reshape(HEADS, s_len, 1) * LOG2E
    o_fox, lse_fox, gathered_b = attn_fwd("fox_fwd", fox, fox, fox, fox_offs, HEAD_V, 1, fox_scale, cum=cum,
                                          comm=hook_gather_second(gathered_b), split_p=True)
    unpack_b = functools.partial(unpack_group, names=GROUP_B, shard_shapes=shard_shapes)
    w = {**w, **unpack_b(gathered_b)}
    y_mla, = matmul("branch_mla", o_mla, w["w_mla_branch"], "nn", [F32])
    y_fox, = matmul("branch_fox", o_fox, w["w_fox_branch"], "nn", [F32])
    mix = gate_mix("gate_mix", graw, y_mla, y_fox)
    h1, = matmul("out_proj", mix, w["w_out"], "nn", [F32], epilogue=lambda acc, res: (res + acc,), extras=[x])
    hn = rms_fwd("rms_mlp", h1, g_mlp)

    def relu2(acc):
        r = jnp.maximum(acc, 0.0)
        return r * r, r
    u, relu_up = matmul("mlp_up", hn, w["w_up"], "nn", [BF16, BF16], epilogue=relu2)
    h2, = matmul("mlp_down", u, w["w_down"], "nn", [F32], epilogue=lambda acc, res: (res + acc,), extras=[h1])
    dh2, d_final, loss = loss_head("loss_head", h2, target, g_final)

    grads = {}
    dup, = matmul("d_mlp_down", dh2, w["w_down"], "nt", [BF16],
                  epilogue=lambda acc, r: (acc * (2.0 * r.astype(F32)),), extras=[relu_up])
    grads["w_down"], = matmul("gw_down", u, dh2, "tn", [BF16], **TN_TILES)
    dhn, = matmul("d_mlp_up", dup, w["w_up"], "nt", [F32])
    grads["w_up"], = matmul("gw_up", hn, dup, "tn", [BF16], **TN_TILES)
    dh1, d_mlp = rms_bwd("rms_mlp_bwd", h1, dhn, g_mlp, dres=dh2)
    dmix, = matmul("d_out_proj", dh1, w["w_out"], "nt", [F32])
    grads["w_out"], = matmul("gw_out", mix, dh1, "tn", [BF16], **TN_TILES)
    dgraw, dy_mla, dy_fox = gate_mix_bwd("gate_mix_bwd", graw, y_mla, y_fox, dmix)
    do_mla, = matmul("d_branch_mla", dy_mla, w["w_mla_branch"], "nt", [BF16])
    grads["w_mla_branch"], = matmul("gw_branch_mla", o_mla, dy_mla, "tn", [BF16], **TN_TILES)
    do_fox, = matmul("d_branch_fox", dy_fox, w["w_fox_branch"], "nt", [BF16])
    grads["w_fox_branch"], = matmul("gw_branch_fox", o_fox, dy_fox, "tn", [BF16], **TN_TILES)

    to_packed_b = jax.linear_transpose(unpack_b, jax.ShapeDtypeStruct(gathered_b.shape, BF16))
    parts_b, = to_packed_b({n: grads.pop(n) for n in GROUP_B})
    delta_mla = attn_delta("mla_delta", do_mla, o_mla)
    dk, dv, dq, from_sibling = attn_bwd("mla_bwd", q, k, kvn, do_mla, lse_mla, delta_mla, mla_offs, QK_PAD, CHUNK,
                                        mla_scale, F32, comm=hook_swap_sibling(parts_b))

    delta_fox = attn_delta("fox_delta", do_fox, o_fox)

    def chips_behind_fox_bwd(pair):
        dfk, dfv, dfq, dcum, from_chips = attn_bwd("fox_bwd", fox, fox, fox, do_fox, lse_fox, delta_fox, fox_offs,
                                                   HEAD_V, 1, fox_scale, BF16, cum=cum, comm=hook_swap_chips(pair))
        return from_chips, (dfq.astype(BF16), dfk, dfv, dcum)
    g_packed_b, (dfq, dfk, dfv, dcum) = reduce_scatter_tail(parts_b, from_sibling, chips_behind_fox_bwd, "b")
    dff_t, d_bias = fox_cum_bwd("fox_cum_bwd", ff_t, f_bias, dcum.reshape(HEADS, s_len))
    dq_r = rope_heads("rope_q_bwd", dq, inv_tabs, BF16)
    dkvn, dkr = dk_split("dk_split", dk, dv, inv_tabs)
    dcq, = matmul("d_up_q", dq_r, w["w_uq"], "nt", [F32])
    grads["w_uq"], = matmul("gw_uq", cq, dq_r, "tn", [BF16], tm=512, tn=2048, tk=2048)
    dckv, = matmul("d_up_kv", dkvn, w["w_kv"], "nt", [F32])
    grads["w_kv"], = matmul("gw_kv", ckv, dkvn, "tn", [BF16], tm=256, tn=2048, tk=2048)
    dcq_raw, d_qn = rms_bwd("rms_q_bwd", (lat, Q_LORA, 0), dcq, g_q, out_dtype=BF16)
    dckv_raw, d_kvn = rms_bwd("rms_kv_bwd", (lat, KV_LORA, Q_LORA // KV_LORA), dckv, g_kv, out_dtype=BF16)

    pad = jnp.zeros((s_len, LAT_W - OFF_FQ - HEADS), BF16)
    dproj = jnp.concatenate([dcq_raw, dckv_raw, dkr[:, :ROPE].astype(BF16), dff_t.T.astype(BF16), pad,
                             dfq, dfk, dfv, dgraw], axis=1)
    w_in_p = jnp.concatenate([w["w_lat"], w["w_fox"], w["w_gate"]], axis=1)
    gw_in, = matmul("gw_in", xn, dproj, "tn", [BF16], tm=1024, tn=1152, tk=2048)
    grads["w_lat"], grads["w_fox"], grads["w_gate"] = (gw_in[:, :LAT_W], gw_in[:, LAT_W:LAT_W + FOX_W],
                                                      gw_in[:, LAT_W + FOX_W:])

    parts_a = to_packed_a(grads)
    from_sibling_a = comm_swap_sibling("comm_rs_sibling_a", parts_a)

    def chips_behind_d_proj(pair):
        dxn, from_chips = matmul("d_proj", dproj, w_in_p, "nt", [F32], tk=2688, comm=hook_swap_chips(pair))
        return from_chips, dxn
    g_packed_a, dxn = reduce_scatter_tail(parts_a, from_sibling_a, chips_behind_d_proj, "a")
    dx, d_attn = rms_bwd("rms_attn_bwd", x, dxn, g_attn, dres=dh1)

    small_grads = {"attn_norm": d_attn, "fox_f_bias": d_bias, "q_norm": d_qn, "kv_norm": d_kvn,
                   "mlp_norm": d_mlp, "final_norm": d_final}
    return loss, dx, g_packed_a, g_packed_b, small_grads


def kernel(x, attn_norm, w_in, fox_f_bias, q_norm, w_uq, kv_norm, w_ukv, w_mla_branch, w_fox_branch, w_out, mlp_norm, w_up, w_down, final_norm, loss_target, m_attn_norm, m_w_in, m_fox_f_bias, m_q_norm, m_w_uq, m_kv_norm, m_w_ukv, m_w_mla_branch, m_w_fox_branch, m_w_out, m_mlp_norm, m_w_up, m_w_down, m_final_norm, v_attn_norm, v_w_in, v_fox_f_bias, v_q_norm, v_w_uq, v_kv_norm, v_w_ukv, v_w_mla_branch, v_w_fox_branch, v_w_out, v_mlp_norm, v_w_up, v_w_down, v_final_norm):
    given = dict(locals())
    big = {n: given[n][0] for n in BIG}
    small = {n: given[n] for n in SMALL}
    shard_shapes = {n: tuple(big[n].shape) for n in BIG}

    packed_a = pack_local({n: big[n].astype(BF16) for n in GROUP_A}, GROUP_A)
    packed_b = pack_local({n: big[n].astype(BF16) for n in GROUP_B}, GROUP_B)
    gathered_a = comm_allgather("comm_allgather_a", packed_a)
    unpack = functools.partial(unpack_a, shard_shapes=shard_shapes)
    w_a = unpack(gathered_a)

    transpose_a = jax.linear_transpose(unpack, jax.ShapeDtypeStruct(gathered_a.shape, BF16))
    loss_part, dx, g_packed_a, g_packed_b, small_grads = local_step(
        x[0], loss_target[0], w_a, small, packed_b, shard_shapes, lambda grads: transpose_a(grads)[0])

    small_sum = comm_allreduce_small("comm_allreduce_small", pack_small(small_grads, loss_part[0, 0]))
    g_small, loss = unpack_small(small_sum)

    grad_w, delta_w, new_m, new_v = {}, {}, {}, {}
    for names, g_packed in ((GROUP_A, g_packed_a), (GROUP_B, g_packed_b)):
        r0 = 0
        for n in names:
            rs, cs = shard_shapes[n]
            nr = _packed_rows((rs, cs))
            g = _from_rows(g_packed[r0:r0 + nr], (rs, cs))
            r0 += nr
            d, m_new, v_new = adamw("adamw_" + n, big[n], g, given["m_" + n][0], given["v_" + n][0])
            grad_w[n], delta_w[n], new_m[n], new_v[n] = g[None], d[None], m_new[None], v_new[None]
    zero = jnp.zeros((), F32)
    d_s, m_s, v_s = adamw("adamw_small", pack_small(small, zero), small_sum * _small_mask(),
                          pack_small({n: given["m_" + n] for n in SMALL}, zero),
                          pack_small({n: given["v_" + n] for n in SMALL}, zero), tr=SMALL_ROWS)
    d_small, _ = unpack_small(d_s)
    m_small, _ = unpack_small(m_s)
    v_small, _ = unpack_small(v_s)
    for n in SMALL:
        shape = given[n].shape
        grad_w[n], delta_w[n] = g_small[n].reshape(shape), d_small[n].reshape(shape)
        new_m[n], new_v[n] = m_small[n].reshape(shape), v_small[n].reshape(shape)

    order = ["attn_norm", "w_in", "fox_f_bias", "q_norm", "w_uq", "kv_norm", "w_ukv", "w_mla_branch", "w_fox_branch",
             "w_out", "mlp_norm", "w_up", "w_down", "final_norm"]
    return (loss, dx[None], *[grad_w[n] for n in order], *[delta_w[n] for n in order],
            *[new_m[n] for n in order], *[new_v[n] for n in order])


def _small_mask():
    used = sum(SMALL_N[n] for n in SMALL)
    return (jnp.arange(SMALL_ROWS * PACK_C) < used).astype(F32).reshape(SMALL_ROWS, PACK_C)
```

```python
import functools
import math

import jax
import jax.numpy as jnp
from jax import lax
from jax.experimental import pallas as pl
from jax.experimental.pallas import tpu as pltpu

F32 = jnp.float32
BF16 = jnp.bfloat16
MESH = pl.DeviceIdType.MESH

D_MODEL = 2048
HEADS = 8
Q_LORA = 512
KV_LORA = 256
NOPE = 128
ROPE = 64
HEAD_V = 128
D_FF = 4 * D_MODEL
CHUNK = 64
EPS = 1e-6
ROPE_THETA = 10000.0
OFF_KR = Q_LORA + KV_LORA
OFF_FQ = OFF_KR + ROPE
OFF_FF = OFF_FQ + 3 * HEADS * HEAD_V
OFF_G = OFF_FF + HEADS
D_IN = OFF_G + 2 * D_MODEL

LAT_W = 896
FOX_W = 3 * HEADS * HEAD_V
GATE_W = 2 * D_MODEL
PROJ_W = LAT_W + FOX_W + GATE_W
QK_PAD = 256

ADAM_LR = 0.001
ADAM_B1 = 0.9
ADAM_B2 = 0.999
ADAM_EPS = 1e-08
ADAM_WD = 0.01
ADAM_STEP = 10

N_DEV = 8
PACK_C = 1024
NEG = -1e30
LOG2E = math.log2(math.e)

VMEM_LIMIT = 56 * 1024 * 1024

BIG = ("w_in", "w_uq", "w_ukv", "w_mla_branch", "w_fox_branch", "w_out", "w_up", "w_down")
GROUP_A = ("w_in", "w_uq", "w_ukv")
GROUP_B = ("w_mla_branch", "w_fox_branch", "w_out", "w_up", "w_down")
ROW_SHARDED = ("w_out", "w_down")
SMALL = ("attn_norm", "fox_f_bias", "q_norm", "kv_norm", "mlp_norm", "final_norm")
SMALL_N = {"attn_norm": D_MODEL, "fox_f_bias": HEADS, "q_norm": Q_LORA, "kv_norm": KV_LORA,
           "mlp_norm": D_MODEL, "final_norm": D_MODEL}
SMALL_ROWS = 8


def _params(sem):
    return pltpu.CompilerParams(dimension_semantics=sem, vmem_limit_bytes=VMEM_LIMIT)


def _rows(name, fn, row_ins, const_ins, outs, reds=(), tr=256):
    norm = [(a, a.shape[1], 0) if not isinstance(a, tuple) else a for a in row_ins]
    n_rows = norm[0][0].shape[0]
    tr = min(tr, n_rows)
    assert n_rows % tr == 0, (name, n_rows, tr)
    n_in, n_out, n_red = len(norm) + len(const_ins), len(outs), len(reds)

    def body(*refs):
        vals = [r[...] for r in refs[:n_in]]
        out_refs = refs[n_in:n_in + n_out]
        red_refs = refs[n_in + n_out:]
        out_vals, red_vals = fn(*vals)
        for r, v in zip(out_refs, out_vals):
            r[...] = v.astype(r.dtype)
        if n_red:
            @pl.when(pl.program_id(0) == 0)
            def _():
                for r in red_refs:
                    r[...] = jnp.zeros_like(r)
            for r, v in zip(red_refs, red_vals):
                r[...] += v

    in_specs = [pl.BlockSpec((tr, w), functools.partial(lambda i, cb: (i, cb), cb=cb)) for _, w, cb in norm]
    in_specs += [pl.BlockSpec(a.shape, lambda i: (0, 0)) for a in const_ins]
    out_specs = [pl.BlockSpec((tr, o[0]), lambda i: (i, 0)) for o in outs]
    out_specs += [pl.BlockSpec((1, c), lambda i: (0, 0)) for c in reds]
    out_shape = [jax.ShapeDtypeStruct((n_rows, o[2] if len(o) > 2 else o[0]), o[1]) for o in outs]
    out_shape += [jax.ShapeDtypeStruct((1, c), F32) for c in reds]
    res = pl.pallas_call(
        body, name=name, grid=(n_rows // tr,), in_specs=in_specs, out_specs=out_specs, out_shape=out_shape,
        compiler_params=_params(("arbitrary",)),
    )(*[a for a, _, _ in norm], *const_ins)
    return res


def _rstd(x):
    return lax.rsqrt(jnp.mean(x * x, axis=-1, keepdims=True) + EPS)


def rms_fwd(name, x, gain, tr=256):
    width = x[1] if isinstance(x, tuple) else x.shape[1]

    def fn(xv, g):
        return (xv * _rstd(xv) * g,), ()
    return _rows(name, fn, [x], [gain], [(width, BF16)], tr=tr)[0]


def rms_bwd(name, x, dy, gain, dres=None, out_dtype=F32, tr=256):
    width = x[1] if isinstance(x, tuple) else x.shape[1]

    def fn(xv, dyv, *rest):
        g = rest[-1]
        r = _rstd(xv)
        n = xv * r
        dyv = dyv.astype(F32)
        dn = dyv * g
        dx = r * (dn - n * jnp.mean(dn * n, axis=-1, keepdims=True))
        if dres is not None:
            dx = dx + rest[0]
        return (dx,), (jnp.sum(dyv * n, axis=0, keepdims=True),)

    ins = [x, dy] + ([dres] if dres is not None else [])
    return _rows(name, fn, ins, [gain], [(width, out_dtype)], [width], tr=tr)


def _rope_lanes(t, c, s1, s2):
    return t * c + pltpu.roll(t, 96, 1) * s1 + pltpu.roll(t, 32, 1) * s2


def rope_heads(name, x, tabs, out_dtype):
    def fn(xv, c, s1, s2):
        xv = xv.astype(F32)
        parts = []
        for h in range(HEADS):
            parts.append(xv[:, h * QK_PAD:h * QK_PAD + NOPE])
            parts.append(_rope_lanes(xv[:, h * QK_PAD + NOPE:(h + 1) * QK_PAD], c, s1, s2))
        return (jnp.concatenate(parts, axis=1),), ()
    return _rows(name, fn, [x, *tabs], [], [(HEADS * QK_PAD, out_dtype)])[0]


def rope_block(name, x, col_block, tabs, out_dtype):
    def fn(xv, c, s1, s2):
        return (_rope_lanes(xv.astype(F32), c, s1, s2),), ()
    return _rows(name, fn, [(x, 128, col_block), *tabs], [], [(128, out_dtype)])[0]


def k_assemble(name, kvn, kr):
    def fn(knp, krv):
        parts = []
        for h in range(HEADS):
            parts.append(knp[:, h * NOPE:(h + 1) * NOPE])
            parts.append(krv)
        return (jnp.concatenate(parts, axis=1),), ()
    return _rows(name, fn, [(kvn, HEADS * NOPE, 0), kr], [], [(HEADS * QK_PAD, BF16)])[0]


def dk_split(name, dk, dv, inv_tabs):
    def fn(dkv, dvv, c, s1, s2):
        parts = []
        acc = None
        for h in range(HEADS):
            parts.append(dkv[:, h * QK_PAD:h * QK_PAD + NOPE].astype(BF16))
            t = dkv[:, h * QK_PAD + NOPE:(h + 1) * QK_PAD]
            acc = t if acc is None else acc + t
        parts.append(dvv)
        return (jnp.concatenate(parts, axis=1), _rope_lanes(acc, c, s1, s2)), ()
    return _rows(name, fn, [dk, dv, *inv_tabs], [], [(2 * HEADS * NOPE, BF16), (128, F32)])


def gate_mix(name, graw, y_mla, y_fox):
    def fn(g, ya, yb):
        ga = jax.nn.sigmoid(g[:, :D_MODEL])
        gb = jax.nn.sigmoid(g[:, D_MODEL:])
        return (ga * ya + gb * yb,), ()
    return _rows(name, fn, [graw, y_mla, y_fox], [], [(D_MODEL, BF16)])[0]


def gate_mix_bwd(name, graw, y_mla, y_fox, dmix):
    def fn(g, ya, yb, dm):
        ga = jax.nn.sigmoid(g[:, :D_MODEL])
        gb = jax.nn.sigmoid(g[:, D_MODEL:])
        dgraw = jnp.concatenate([dm * ya * ga * (1.0 - ga), dm * yb * gb * (1.0 - gb)], axis=1)
        return (dgraw, dm * ga, dm * gb), ()
    return _rows(name, fn, [graw, y_mla, y_fox, dmix], [],
                 [(GATE_W, BF16, PROJ_W), (D_MODEL, BF16), (D_MODEL, BF16)], tr=128)


def loss_head(name, h2, target, gain):
    inv_d = 1.0 / D_MODEL

    def fn(h, t, g):
        r = _rstd(h)
        n = h * r
        err = n * g - t
        dy = err * inv_d
        dn = dy * g
        dh = r * (dn - n * jnp.mean(dn * n, axis=-1, keepdims=True))
        part = 0.5 * inv_d * jnp.sum(jnp.sum(err * err, axis=1, keepdims=True), axis=0, keepdims=True)
        return (dh,), (jnp.sum(dy * n, axis=0, keepdims=True), jnp.broadcast_to(part, (1, 128)))
    return _rows(name, fn, [h2, target], [gain], [(D_MODEL, F32)], [D_MODEL, 128])


def adamw(name, w, g, m, v, tr=256):
    c1 = 1.0 - ADAM_B1 ** ADAM_STEP
    c2 = 1.0 - ADAM_B2 ** ADAM_STEP

    def fn(wv, gv, mv, vv):
        m_new = ADAM_B1 * mv + (1.0 - ADAM_B1) * gv
        v_new = ADAM_B2 * vv + (1.0 - ADAM_B2) * (gv * gv)
        delta = -ADAM_LR * ((m_new / c1) / (jnp.sqrt(v_new / c2) + ADAM_EPS) + ADAM_WD * wv)
        return (delta, m_new, v_new), ()
    cols = w.shape[1]
    return _rows(name, fn, [w, g, m, v], [], [(cols, F32)] * 3, tr=tr)


def _row_tile(n_rows, cap=640):
    return max(t for t in range(16, cap + 1, 16) if n_rows % t == 0)


def add_pairs(name, a, b):
    def fn(av, bv):
        return (av.astype(F32) + bv.astype(F32),), ()
    return _rows(name, fn, [a, b], [], [(a.shape[1], BF16)], tr=_row_tile(a.shape[0]))[0]


def add_final(name, a, b, r0, r1, r2):
    def fn(av, bv, r0v, r1v, r2v):
        return (((av.astype(F32) + bv.astype(F32)) + r0v.astype(F32)) + r1v.astype(F32) + r2v.astype(F32),), ()
    return _rows(name, fn, [a, b, r0, r1, r2], [], [(a.shape[1], F32)], tr=_row_tile(a.shape[0]))[0]


TN_TILES = dict(tm=1024, tn=1024, tk=2048)
_DIMS = {"nn": (((1,), (0,)), ((), ())), "nt": (((1,), (1,)), ((), ())), "tn": (((0,), (0,)), ((), ()))}


def matmul(name, a, b, mode, outs, epilogue=None, extras=(), tm=1024, tn=1024, tk=2048, comm=None):
    if mode == "tn":
        kdim, m = a.shape
    else:
        m, kdim = a.shape
    n = b.shape[0] if mode == "nt" else b.shape[1]
    tm, tn, tk = min(tm, m), min(tn, n), min(tk, kdim)
    assert m % tm == 0 and n % tn == 0 and kdim % tk == 0, (name, a.shape, b.shape)
    nk = kdim // tk
    n_ex, n_out = len(extras), len(outs)
    dims = _DIMS[mode]
    grid = (m // tm, n // tn, nk)

    def body(*refs):
        ins, c_in, out_refs, c_out, scr, c_sems = _split_refs(refs, 2 + n_ex, comm, n_out, 1 if nk > 1 else 0)
        a_ref, b_ref = ins[:2]
        ex_refs = ins[2:]
        _comm_start(comm, c_in, c_out, c_sems, *grid)

        def finish(acc):
            vals = (acc,) if epilogue is None else epilogue(acc, *[r[...] for r in ex_refs])
            for r, v in zip(out_refs, vals):
                r[...] = v.astype(r.dtype)

        part = lax.dot_general(a_ref[...].astype(BF16), b_ref[...].astype(BF16), dims, preferred_element_type=F32)
        if nk == 1:
            finish(part)
        else:
            acc_ref = scr[0]
            k = pl.program_id(2)

            @pl.when(k == 0)
            def _():
                acc_ref[...] = part

            @pl.when(k > 0)
            def _():
                acc_ref[...] += part

            @pl.when(k == nk - 1)
            def _():
                finish(acc_ref[...])
        _comm_wait(comm, c_in, c_out, c_sems, *grid)

    a_spec = pl.BlockSpec((tk, tm), lambda i, j, k: (k, i)) if mode == "tn" else pl.BlockSpec((tm, tk), lambda i, j, k: (i, k))
    b_spec = pl.BlockSpec((tn, tk), lambda i, j, k: (j, k)) if mode == "nt" else pl.BlockSpec((tk, tn), lambda i, j, k: (k, j))
    tile = pl.BlockSpec((tm, tn), lambda i, j, k: (i, j))
    c_ins, c_outs, c_shapes, c_scratch, aliases = _with_comm(comm, 2 + n_ex, n_out)
    sem = ("arbitrary",) * 3 if comm else ("parallel", "parallel", "arbitrary")
    res = pl.pallas_call(
        body, name=name, grid=grid,
        in_specs=[a_spec, b_spec] + [tile] * n_ex + c_ins,
        out_specs=[tile] * n_out + c_outs,
        out_shape=[jax.ShapeDtypeStruct((m, n), dt) for dt in outs] + c_shapes,
        scratch_shapes=([pltpu.VMEM((tm, tn), F32)] if nk > 1 else []) + c_scratch,
        input_output_aliases=aliases,
        compiler_params=_params(sem),
    )(a, b, *extras, *(comm.args if comm else []))
    return res


_NT = (((1,), (1,)), ((), ()))
_NN = (((1,), (0,)), ((), ()))


def _mask(bq, chunk, transposed, row0=0, shape=None):
    shape = (bq, bq) if shape is None else shape
    row = lax.broadcasted_iota(jnp.int32, shape, 0) + row0
    col = lax.broadcasted_iota(jnp.int32, shape, 1)
    if chunk > 1:
        row, col = row // chunk, col // chunk
    return (row <= col) if transposed else (col <= row)


def _row_layout(a, bq):
    h, s, _ = a.shape
    return a.reshape(h, s // bq, 1, bq)


def _split_refs(refs, n_in, comm, n_out, n_scr):
    n_ci = len(comm.args) if comm else 0
    n_co = len(comm.out_shapes) if comm else 0
    cuts = [n_in, n_ci, n_out, n_co, n_scr, 3 if comm else 0]
    parts, at = [], 0
    for n in cuts:
        parts.append(list(refs[at:at + n]))
        at += n
    assert at == len(refs), (at, len(refs))
    return parts


def _at_step(grid, last):
    hit = None
    for axis, n in enumerate(grid):
        here = pl.program_id(axis) == (n - 1 if last else 0)
        hit = here if hit is None else jnp.logical_and(hit, here)
    return hit


def _comm_start(comm, c_in, c_out, c_sems, *grid):
    if comm is not None:
        @pl.when(_at_step(grid, False))
        def _():
            comm.start(c_in, c_out, c_sems)


def _comm_wait(comm, c_in, c_out, c_sems, *grid):
    if comm is not None:
        @pl.when(_at_step(grid, True))
        def _():
            comm.wait(c_in, c_out, c_sems)


def attn_fwd(name, q, k, v, offs, dqk, chunk, scale, cum=None, bq=512, comm=None, split_p=False):
    s_len = q.shape[0]
    nq = s_len // bq
    qoff, koff, voff = offs
    has_bias = cum is not None
    scale2 = scale * LOG2E

    n_in = 5 if has_bias else 3

    def body(*refs):
        ins, c_in, outs, c_out, scr, c_sems = _split_refs(refs, n_in, comm, 2, 3)
        q_ref, k_ref, v_ref = ins[:3]
        if has_bias:
            cc_ref, cr_ref = ins[3:]
        o_ref, lse_ref = outs
        m_s, l_s, acc_s = scr
        _comm_start(comm, c_in, c_out, c_sems, HEADS, nq)
        i = pl.program_id(1)
        qv = q_ref[...]
        m_s[...] = jnp.full_like(m_s, NEG)
        l_s[...] = jnp.zeros_like(l_s)
        acc_s[...] = jnp.zeros_like(acc_s)

        def step(j, masked):
            off = pl.multiple_of(j * bq, bq)
            kj = k_ref[pl.ds(off, bq), :]
            vj = v_ref[pl.ds(off, bq), :]
            st = lax.dot_general(kj, qv, _NT, preferred_element_type=F32) * scale2
            if has_bias:
                st = st + cr_ref[...] - cc_ref[pl.ds(off, bq), :]
            if masked:
                st = jnp.where(_mask(bq, chunk, True), st, NEG)
            m_prev = m_s[...]
            m_new = jnp.maximum(m_prev, jnp.max(st, axis=0, keepdims=True))
            alpha = jnp.exp2(m_prev - m_new)
            pt = jnp.exp2(st - m_new)
            l_s[...] = alpha * l_s[...] + jnp.sum(pt, axis=0, keepdims=True)
            p_hi = pt.astype(BF16)
            pv = lax.dot_general(vj, p_hi, _DIMS["tn"], preferred_element_type=F32)
            if split_p:
                p_lo = (pt - p_hi.astype(F32)).astype(BF16)
                pv = pv + lax.dot_general(vj, p_lo, _DIMS["tn"], preferred_element_type=F32)
            acc_s[...] = alpha * acc_s[...] + pv
            m_s[...] = m_new

        def pair_body(jj, carry):
            step(2 * jj, False)
            step(2 * jj + 1, False)
            return carry

        lax.fori_loop(0, i // 2, pair_body, 0)

        @pl.when(i % 2 == 1)
        def _():
            step(i - 1, False)

        step(i, True)
        o_ref[...] = (acc_s[...] / l_s[...]).T.astype(o_ref.dtype)
        lse_ref[...] = m_s[...] + jnp.log2(l_s[...])
        _comm_wait(comm, c_in, c_out, c_sems, HEADS, nq)

    in_specs = [
        pl.BlockSpec((bq, dqk), lambda h, i: (i, qoff + h)),
        pl.BlockSpec((s_len, dqk), lambda h, i: (0, koff + h)),
        pl.BlockSpec((s_len, HEAD_V), lambda h, i: (0, voff + h)),
    ]
    args = [q, k, v]
    if has_bias:
        in_specs += [pl.BlockSpec((None, s_len, 1), lambda h, i: (h, 0, 0)),
                     pl.BlockSpec((None, None, 1, bq), lambda h, i: (h, i, 0, 0))]
        args += [cum, _row_layout(cum, bq)]
    c_ins, c_outs, c_shapes, c_scratch, aliases = _with_comm(comm, len(args), 2)
    o, lse_rows, *comm_out = pl.pallas_call(
        body, name=name, grid=(HEADS, nq), in_specs=in_specs + c_ins,
        out_specs=[pl.BlockSpec((bq, HEAD_V), lambda h, i: (i, h)),
                   pl.BlockSpec((None, None, 1, bq), lambda h, i: (h, i, 0, 0))] + c_outs,
        out_shape=[jax.ShapeDtypeStruct((s_len, HEADS * HEAD_V), F32),
                   jax.ShapeDtypeStruct((HEADS, nq, 1, bq), F32)] + c_shapes,
        scratch_shapes=[pltpu.VMEM((1, bq), F32), pltpu.VMEM((1, bq), F32), pltpu.VMEM((HEAD_V, bq), F32)] + c_scratch,
        input_output_aliases=aliases,
        compiler_params=_params(("arbitrary", "arbitrary")),
    )(*args, *(comm.args if comm else []))
    return (o, lse_rows, *comm_out)


def _bwd_block(kv, vv, qi, doi, lse_row, scale2, bias, masked, bq, chunk):
    st = lax.dot_general(kv, qi, _NT, preferred_element_type=F32) * scale2
    if bias is not None:
        st = st + bias[0] - bias[1]
    if masked:
        st = jnp.where(_mask(bq, chunk, True), st, NEG)
    pt = jnp.exp2(st - lse_row)
    dpt = lax.dot_general(vv, doi, _NT, preferred_element_type=F32)
    return pt, dpt


def attn_delta(name, do, o, bq=512):
    s_len = do.shape[0]
    nq = s_len // bq

    def body(do_ref, o_ref, out_ref):
        ones = jnp.ones((8, HEAD_V), BF16)
        for h in range(HEADS):
            cols = slice(h * HEAD_V, (h + 1) * HEAD_V)
            prod = do_ref[:, cols].astype(F32) * o_ref[:, cols].astype(F32)
            total = None
            for part in reversed(_split3(prod)):
                term = lax.dot_general(ones, part, _NT, preferred_element_type=F32)
                total = term if total is None else total + term
            out_ref[h] = total[0:1, :]

    blk = pl.BlockSpec((bq, HEADS * HEAD_V), lambda i: (i, 0))
    return pl.pallas_call(
        body, name=name, grid=(nq,), in_specs=[blk, blk],
        out_specs=pl.BlockSpec((HEADS, None, 1, bq), lambda i: (0, i, 0, 0)),
        out_shape=jax.ShapeDtypeStruct((HEADS, nq, 1, bq), F32),
        compiler_params=_params(("parallel",)),
    )(do, o)


def attn_bwd(name, q, k, v, do, lse, delta, offs, dqk, chunk, scale, out_dtype, cum=None, bq=512, comm=None):
    s_len = q.shape[0]
    nq = s_len // bq
    qoff, koff, voff = offs
    has_bias = cum is not None
    scale2 = scale * LOG2E
    n_in, n_out = (8, 4) if has_bias else (6, 3)

    def body(*refs):
        ins, c_in, outs, c_out, scr, c_sems = _split_refs(refs, n_in, comm, n_out, n_out - 1)
        k_ref, v_ref, q_ref, do_ref, lse_ref, delta_ref = ins[:6]
        dk_ref, dv_ref, dq_ref = outs[:3]
        dk_s, dv_s = scr[:2]
        if has_bias:
            cc_ref, cr_ref = ins[6:]
            dc_ref, dc_s = outs[3], scr[2]
        _comm_start(comm, c_in, c_out, c_sems, HEADS, nq)
        j = pl.program_id(1)
        kv = k_ref[...]
        vv = v_ref[...]
        dk_s[...] = jnp.zeros_like(dk_s)
        dv_s[...] = jnp.zeros_like(dv_s)
        if has_bias:
            dc_s[...] = jnp.zeros_like(dc_s)

        @pl.when(j == 0)
        def _():
            dq_ref[...] = jnp.zeros_like(dq_ref)

        def step(i, masked):
            off = pl.multiple_of(i * bq, bq)
            qi = q_ref[pl.ds(off, bq), :]
            doi = do_ref[pl.ds(off, bq), :].astype(BF16)
            bias = (cr_ref[i], cc_ref[...]) if has_bias else None
            pt, dpt = _bwd_block(kv, vv, qi, doi, lse_ref[i], scale2, bias, masked, bq, chunk)
            dv_s[...] += lax.dot_general(pt.astype(BF16), doi, _NN, preferred_element_type=F32)
            dst = pt * (dpt - delta_ref[i])
            if has_bias:
                dc_s[...] -= jnp.sum(dst, axis=-1, keepdims=True)
            dst = dst.astype(BF16)
            dk_s[...] += lax.dot_general(dst, qi, _NN, preferred_element_type=F32)
            dq_ref[pl.ds(off, bq), :] += lax.dot_general(dst, kv, _DIMS["tn"], preferred_element_type=F32) * scale

        step(j, True)

        def loop_body(i, carry):
            step(i, False)
            return carry

        lax.fori_loop(j + 1, nq, loop_body, 0)
        dk_ref[...] = (dk_s[...] * scale).astype(dk_ref.dtype)
        dv_ref[...] = dv_s[...].astype(dv_ref.dtype)
        if has_bias:
            dc_ref[...] = dc_s[...]
        _comm_wait(comm, c_in, c_out, c_sems, HEADS, nq)

    rows = pl.BlockSpec((None, nq, 1, bq), lambda h, j: (h, 0, 0, 0))
    in_specs = [
        pl.BlockSpec((bq, dqk), lambda h, j: (j, koff + h)),
        pl.BlockSpec((bq, HEAD_V), lambda h, j: (j, voff + h)),
        pl.BlockSpec((s_len, dqk), lambda h, j: (0, qoff + h)),
        pl.BlockSpec((s_len, HEAD_V), lambda h, j: (0, h)),
        rows,
        rows,
    ]
    args = [k, v, q, do, lse, delta]
    out_specs = [pl.BlockSpec((bq, dqk), lambda h, j: (j, h)), pl.BlockSpec((bq, HEAD_V), lambda h, j: (j, h)),
                 pl.BlockSpec((s_len, dqk), lambda h, j: (0, h))]
    out_shape = [jax.ShapeDtypeStruct((s_len, HEADS * dqk), out_dtype),
                 jax.ShapeDtypeStruct((s_len, HEADS * HEAD_V), BF16),
                 jax.ShapeDtypeStruct((s_len, HEADS * dqk), F32)]
    scratch = [pltpu.VMEM((bq, dqk), F32), pltpu.VMEM((bq, HEAD_V), F32)]
    if has_bias:
        in_specs += [pl.BlockSpec((None, bq, 1), lambda h, j: (h, j, 0)), rows]
        args += [cum, _row_layout(cum, bq)]
        out_specs.append(pl.BlockSpec((None, bq, 1), lambda h, j: (h, j, 0)))
        out_shape.append(jax.ShapeDtypeStruct((HEADS, s_len, 1), F32))
        scratch.append(pltpu.VMEM((bq, 1), F32))
    c_ins, c_outs, c_shapes, c_scratch, aliases = _with_comm(comm, len(args), n_out)
    return pl.pallas_call(
        body, name=name, grid=(HEADS, nq), in_specs=in_specs + c_ins, out_specs=out_specs + c_outs,
        out_shape=out_shape + c_shapes, scratch_shapes=scratch + c_scratch, input_output_aliases=aliases,
        compiler_params=_params(("arbitrary", "arbitrary")),
    )(*args, *(comm.args if comm else []))


_CUM_BLK = 512


def _split3(x):
    hi = x.astype(BF16)
    r1 = x - hi.astype(F32)
    mid = r1.astype(BF16)
    lo = (r1 - mid.astype(F32)).astype(BF16)
    return hi, mid, lo


def _tri_dot(x, tri):
    hi, mid, lo = _split3(x)
    out = lax.dot_general(lo, tri, _NN, preferred_element_type=F32)
    out = out + lax.dot_general(mid, tri, _NN, preferred_element_type=F32)
    return out + lax.dot_general(hi, tri, _NN, preferred_element_type=F32)


def fox_cum_fwd(name, ff_t, bias):
    s_len = ff_t.shape[1]
    nb = s_len // _CUM_BLK

    def body(ff_ref, b_ref, cum_ref):
        row = lax.broadcasted_iota(jnp.int32, (_CUM_BLK, _CUM_BLK), 0)
        col = lax.broadcasted_iota(jnp.int32, (_CUM_BLK, _CUM_BLK), 1)
        tri = (row <= col).astype(BF16)
        carry = jnp.zeros((HEADS, 1), F32)
        for b in range(nb):
            z = ff_ref[:, b * _CUM_BLK:(b + 1) * _CUM_BLK] + b_ref[...]
            logf = jnp.minimum(z, 0.0) - jnp.log1p(jnp.exp(-jnp.abs(z)))
            blk = _tri_dot(logf, tri) + carry
            cum_ref[:, b * _CUM_BLK:(b + 1) * _CUM_BLK] = blk
            carry = blk[:, _CUM_BLK - 1:_CUM_BLK]

    return pl.pallas_call(
        body, name=name, out_shape=jax.ShapeDtypeStruct((HEADS, s_len), F32),
        compiler_params=pltpu.CompilerParams(vmem_limit_bytes=VMEM_LIMIT),
    )(ff_t, bias)


def fox_cum_bwd(name, ff_t, bias, dcum):
    s_len = ff_t.shape[1]
    nb = s_len // _CUM_BLK

    def body(ff_ref, b_ref, dc_ref, dff_ref, db_ref):
        row = lax.broadcasted_iota(jnp.int32, (_CUM_BLK, _CUM_BLK), 0)
        col = lax.broadcasted_iota(jnp.int32, (_CUM_BLK, _CUM_BLK), 1)
        tri = (row >= col).astype(BF16)
        carry = jnp.zeros((HEADS, 1), F32)
        dbias = jnp.zeros((HEADS, 1), F32)
        for b in reversed(range(nb)):
            sl = slice(b * _CUM_BLK, (b + 1) * _CUM_BLK)
            dlogf = _tri_dot(dc_ref[:, sl], tri) + carry
            carry = dlogf[:, 0:1]
            z = ff_ref[:, sl] + b_ref[...]
            dz = dlogf / (1.0 + jnp.exp(z))
            dff_ref[:, sl] = dz
            dbias = dbias + jnp.sum(dz, axis=-1, keepdims=True)
        db_ref[...] = dbias

    return pl.pallas_call(
        body, name=name,
        out_shape=[jax.ShapeDtypeStruct((HEADS, s_len), F32), jax.ShapeDtypeStruct((HEADS, 1), F32)],
        compiler_params=pltpu.CompilerParams(vmem_limit_bytes=VMEM_LIMIT),
    )(ff_t, bias, dcum)


_ANY = pl.BlockSpec(memory_space=pl.ANY)


def _me():
    return lax.axis_index("x"), lax.axis_index("y"), lax.axis_index("c")


def comm_allgather(name, mine):
    n_rows, n_cols = mine.shape

    def body(x_ref, out_ref, send_sems, recv_sems, local_sem):
        x, y, c = _me()
        sibling = (x, y, 1 - c)
        chips = [(1 - x, y), (x, 1 - y), (1 - x, 1 - y)]

        def blk(px, py, pc):
            return out_ref.at[pc * 4 + px * 2 + py]

        def copy(k, block, to, src=None):
            return pltpu.make_async_remote_copy(
                src_ref=blk(*block) if src is None else src, dst_ref=blk(*block),
                send_sem=send_sems.at[k], recv_sem=recv_sems.at[k], device_id=to, device_id_type=MESH)

        own = pltpu.make_async_copy(x_ref, blk(x, y, c), local_sem)
        own.start()
        first = [copy(0, (x, y, c), sibling, src=x_ref)]
        first += [copy(1 + j, (x, y, c), (*chip, c), src=x_ref) for j, chip in enumerate(chips)]
        for cp in first:
            cp.start()
        passed = [copy(4 + j, (*chip, c), sibling) for j, chip in enumerate(chips)]
        for j, chip in enumerate(chips):
            copy(1 + j, (*chip, c), (x, y, c)).wait_recv()
            passed[j].start()
        copy(0, sibling, (x, y, c)).wait_recv()
        for j, chip in enumerate(chips):
            copy(4 + j, (*chip, 1 - c), (x, y, c)).wait_recv()
        for cp in first + passed:
            cp.wait_send()
        own.wait()

    return pl.pallas_call(
        body, name=name, out_shape=jax.ShapeDtypeStruct((N_DEV, n_rows, n_cols), mine.dtype),
        in_specs=[_ANY], out_specs=_ANY,
        scratch_shapes=[pltpu.SemaphoreType.DMA((7,)), pltpu.SemaphoreType.DMA((7,)), pltpu.SemaphoreType.DMA],
    )(mine)


def comm_swap_sibling(name, parts):
    _, n_rows, n_cols = parts.shape

    def body(p_ref, got_ref, send_sem, recv_sem):
        x, y, c = _me()
        cp = pltpu.make_async_remote_copy(
            src_ref=p_ref.at[pl.ds((1 - c) * 4, 4)], dst_ref=got_ref, send_sem=send_sem, recv_sem=recv_sem,
            device_id=(x, y, 1 - c), device_id_type=MESH)
        cp.start()
        cp.wait()

    return pl.pallas_call(
        body, name=name, out_shape=jax.ShapeDtypeStruct((4, n_rows, n_cols), parts.dtype),
        in_specs=[_ANY], out_specs=_ANY,
        scratch_shapes=[pltpu.SemaphoreType.DMA, pltpu.SemaphoreType.DMA],
    )(parts)


class CommHook:
    def __init__(self, args, out_shapes, n_copies, copies, aliases=None):
        self.args, self.out_shapes, self.n_copies, self.copies = list(args), list(out_shapes), n_copies, copies
        self.aliases = aliases or {}

    def scratch(self):
        return [pltpu.SemaphoreType.DMA((self.n_copies,)), pltpu.SemaphoreType.DMA((self.n_copies,)),
                pltpu.SemaphoreType.DMA((1,))]

    def start(self, in_refs, out_refs, sems):
        sends, _, locs = self.copies(in_refs, out_refs, *sems)
        for cp in locs() + sends():
            cp.start()

    def wait(self, in_refs, out_refs, sems):
        sends, recvs, locs = self.copies(in_refs, out_refs, *sems)
        for cp in sends():
            cp.wait_send()
        for cp in recvs():
            cp.wait_recv()
        for cp in locs():
            cp.wait()


def _remote(src, dst, send_sem, recv_sem, to):
    return pltpu.make_async_remote_copy(src_ref=src, dst_ref=dst, send_sem=send_sem, recv_sem=recv_sem,
                                        device_id=to, device_id_type=MESH)


def hook_gather_first(mine):
    n_rows, n_cols = mine.shape

    def copies(ins, outs, send, recv, local):
        (x_ref,), (out_ref,) = ins, outs
        x, y, c = _me()
        me = c * 4 + x * 2 + y
        peers = [(x, y, 1 - c), (1 - x, y, c), (x, 1 - y, c), (1 - x, 1 - y, c)]

        def sends():
            return [_remote(x_ref, out_ref.at[me], send.at[k], recv.at[k], p) for k, p in enumerate(peers)]

        def recvs():
            return [_remote(x_ref, out_ref.at[pc * 4 + px * 2 + py], send.at[k], recv.at[k], (px, py, pc))
                    for k, (px, py, pc) in enumerate(peers)]

        return sends, recvs, lambda: [pltpu.make_async_copy(x_ref, out_ref.at[me], local.at[0])]

    return CommHook([mine], [jax.ShapeDtypeStruct((N_DEV, n_rows, n_cols), mine.dtype)], 4, copies)


def hook_gather_second(gathered):
    def copies(ins, outs, send, recv, local):
        (g_in,), (g_out,) = ins, outs
        x, y, c = _me()
        chips = [(1 - x, y), (x, 1 - y), (1 - x, 1 - y)]

        def sends():
            return [_remote(g_in.at[c * 4 + px * 2 + py], g_out.at[c * 4 + px * 2 + py], send.at[k], recv.at[k],
                            (x, y, 1 - c)) for k, (px, py) in enumerate(chips)]

        def recvs():
            return [_remote(g_in.at[(1 - c) * 4 + px * 2 + py], g_out.at[(1 - c) * 4 + px * 2 + py], send.at[k],
                            recv.at[k], (x, y, 1 - c)) for k, (px, py) in enumerate(chips)]

        return sends, recvs, lambda: []

    return CommHook([gathered], [jax.ShapeDtypeStruct(gathered.shape, gathered.dtype)], 3, copies, aliases={0: 0})


def hook_swap_sibling(parts):
    _, n_rows, n_cols = parts.shape

    def copies(ins, outs, send, recv, local):
        (p_ref,), (got_ref,) = ins, outs
        x, y, c = _me()

        def swap():
            return [_remote(p_ref.at[pl.ds((1 - c) * 4, 4)], got_ref, send.at[0], recv.at[0], (x, y, 1 - c))]

        return swap, swap, lambda: []

    return CommHook([parts], [jax.ShapeDtypeStruct((4, n_rows, n_cols), parts.dtype)], 1, copies)


def hook_swap_chips(parts):
    _, n_rows, n_cols = parts.shape

    def copies(ins, outs, send, recv, local):
        (p_ref,), (got_ref,) = ins, outs
        x, y, c = _me()
        chips = [(1 - x, y), (x, 1 - y), (1 - x, 1 - y)]

        def swaps():
            return [_remote(p_ref.at[2 * px + py], got_ref.at[k], send.at[k], recv.at[k], (px, py, c))
                    for k, (px, py) in enumerate(chips)]

        return swaps, swaps, lambda: []

    return CommHook([parts], [jax.ShapeDtypeStruct((3, n_rows, n_cols), parts.dtype)], 3, copies)


def _with_comm(comm, n_args, n_outs):
    if comm is None:
        return [], [], [], [], {}
    aliases = {n_args + a: n_outs + o for a, o in comm.aliases.items()}
    return [_ANY] * len(comm.args), [_ANY] * len(comm.out_shapes), comm.out_shapes, comm.scratch(), aliases


def comm_allreduce_small(name, mine):
    shape = mine.shape

    def body(x_ref, out_ref, buf, send_sems, recv_sems):
        x, y, c = _me()
        my_slot = c * 4 + x * 2 + y
        buf[my_slot] = x_ref[...]
        cps = []
        for k in range(1, N_DEV):
            dx, dy, dc = (k >> 2) & 1, (k >> 1) & 1, k & 1
            px, py, pc = x ^ dx, y ^ dy, c ^ dc
            send = pltpu.make_async_remote_copy(
                src_ref=x_ref, dst_ref=buf.at[my_slot], send_sem=send_sems.at[k - 1], recv_sem=recv_sems.at[k - 1],
                device_id=(px, py, pc), device_id_type=MESH)
            send.start()
            recv = pltpu.make_async_remote_copy(
                src_ref=x_ref, dst_ref=buf.at[pc * 4 + px * 2 + py], send_sem=send_sems.at[k - 1],
                recv_sem=recv_sems.at[k - 1], device_id=(px, py, pc), device_id_type=MESH)
            cps.append((send, recv))
        for send, recv in cps:
            send.wait_send()
            recv.wait_recv()
        total = buf[0]
        for s in range(1, N_DEV):
            total = total + buf[s]
        out_ref[...] = total

    vmem = pl.BlockSpec(memory_space=pltpu.VMEM)
    return pl.pallas_call(
        body, name=name, out_shape=jax.ShapeDtypeStruct(shape, F32), in_specs=[vmem], out_specs=vmem,
        scratch_shapes=[pltpu.VMEM((N_DEV,) + shape, F32), pltpu.SemaphoreType.DMA((7,)), pltpu.SemaphoreType.DMA((7,))],
    )(mine)


def _packed_rows(shape):
    rs, cs = shape
    if cs % PACK_C == 0:
        return rs * (cs // PACK_C)
    if cs > PACK_C // 2:
        return rs
    return rs * cs // PACK_C


def _to_rows(a):
    rs, cs = a.shape
    if cs % PACK_C == 0:
        return jnp.concatenate([a[:, i * PACK_C:(i + 1) * PACK_C] for i in range(cs // PACK_C)], axis=0)
    if cs > PACK_C // 2:
        return jnp.pad(a, ((0, 0), (0, PACK_C - cs)))
    return a.reshape(-1, PACK_C)


def _from_rows(rows, shape):
    rs, cs = shape
    lead = rows.shape[:-2]
    if cs % PACK_C == 0:
        return jnp.concatenate([rows[..., i * rs:(i + 1) * rs, :] for i in range(cs // PACK_C)], axis=-1)
    if cs > PACK_C // 2:
        return rows[..., :cs]
    return rows.reshape(*lead, rs, cs)


def pack_local(shards, names):
    flat = [_to_rows(shards[n]) for n in names]
    rows = sum(f.shape[0] for f in flat)
    pad = (-rows) % 128
    return jnp.concatenate(flat + [jnp.zeros((pad, PACK_C), flat[0].dtype)], axis=0)


def unpack_group(gathered, names, shard_shapes):
    _, n_rows, _ = gathered.shape
    by_block = gathered.reshape(2, 4, n_rows, PACK_C).transpose(1, 0, 2, 3).reshape(N_DEV, n_rows, PACK_C)
    full = {}
    r0 = 0
    for name in names:
        rs, cs = shard_shapes[name]
        nr = _packed_rows((rs, cs))
        piece = _from_rows(by_block[:, r0:r0 + nr, :], (rs, cs))
        r0 += nr
        if name in ROW_SHARDED:
            full[name] = piece.reshape(N_DEV * rs, cs)
        else:
            full[name] = piece.transpose(1, 0, 2).reshape(rs, N_DEV * cs)
    return full


def unpack_a(gathered, shard_shapes):
    full = unpack_group(gathered, GROUP_A, shard_shapes)
    w_in = full.pop("w_in")
    zeros = jnp.zeros((D_MODEL, LAT_W - OFF_FQ - HEADS), w_in.dtype)
    full["w_lat"] = jnp.concatenate([w_in[:, :OFF_FQ], w_in[:, OFF_FF:OFF_G], zeros], axis=1)
    full["w_fox"] = w_in[:, OFF_FQ:OFF_FF]
    full["w_gate"] = w_in[:, OFF_G:]
    w_uq = full.pop("w_uq").reshape(Q_LORA, HEADS, NOPE + ROPE)
    full["w_uq"] = jnp.pad(w_uq, ((0, 0), (0, 0), (0, QK_PAD - NOPE - ROPE))).reshape(Q_LORA, HEADS * QK_PAD)
    w_ukv = full.pop("w_ukv").reshape(KV_LORA, HEADS, 2, NOPE)
    full["w_kv"] = jnp.concatenate([w_ukv[:, :, 0, :].reshape(KV_LORA, HEADS * NOPE),
                                    w_ukv[:, :, 1, :].reshape(KV_LORA, HEADS * HEAD_V)], axis=1)
    return full


def pack_small(vals, loss):
    flat = [vals[n].reshape(-1) for n in SMALL] + [loss.reshape(-1)]
    used = sum(f.shape[0] for f in flat)
    flat.append(jnp.zeros((SMALL_ROWS * PACK_C - used,), F32))
    return jnp.concatenate(flat).reshape(SMALL_ROWS, PACK_C)


def unpack_small(packed):
    flat = packed.reshape(-1)
    out, off = {}, 0
    for n in SMALL:
        out[n] = flat[off:off + SMALL_N[n]]
        off += SMALL_N[n]
    return out, flat[off]


def rope_tables(s_len):
    pos = jnp.arange(s_len, dtype=F32)
    inv = 1.0 / (ROPE_THETA ** (jnp.arange(0, ROPE, 2, dtype=F32) / ROPE))
    ang = pos[:, None] * inv[None, :]
    cos, sin = jnp.cos(ang), jnp.sin(ang)
    zero = jnp.zeros_like(cos)
    c = jnp.concatenate([cos, cos, zero, zero], axis=1)
    s1 = jnp.concatenate([-sin, zero, zero, zero], axis=1)
    s2 = jnp.concatenate([zero, sin, zero, zero], axis=1)
    return (c, s1, s2), (c, -s1, -s2)


def reduce_scatter_tail(parts, from_sibling, from_chips_fn, names):
    cx, cy, cc = _me()
    n_rows = parts.shape[1]
    mine4 = lax.dynamic_slice_in_dim(parts, cc * 4, 4, axis=0)
    pair = add_pairs("rs_pair_sum_" + names, mine4.reshape(4 * n_rows, PACK_C), from_sibling.reshape(4 * n_rows, PACK_C))
    from_chips, extra = from_chips_fn(pair.reshape(4, n_rows, PACK_C))
    own = cx * 2 + cy
    total = add_final("rs_final_sum_" + names, lax.dynamic_index_in_dim(mine4, own, 0, keepdims=False),
                      lax.dynamic_index_in_dim(from_sibling, own, 0, keepdims=False),
                      from_chips[0], from_chips[1], from_chips[2])
    return total, extra


def local_step(x, target, w, small, packed_b, shard_shapes, to_packed_a):
    s_len = x.shape[0]
    tabs, inv_tabs = rope_tables(s_len)
    g_attn = small["attn_norm"].reshape(1, D_MODEL)
    g_q = small["q_norm"].reshape(1, Q_LORA)
    g_kv = small["kv_norm"].reshape(1, KV_LORA)
    g_mlp = small["mlp_norm"].reshape(1, D_MODEL)
    g_final = small["final_norm"].reshape(1, D_MODEL)
    f_bias = small["fox_f_bias"].reshape(HEADS, 1)
    mla_scale = 1.0 / math.sqrt(NOPE + ROPE)
    fox_scale = 1.0 / math.sqrt(HEAD_V)
    mla_offs = (0, 0, HEADS)
    fox_offs = (0, HEADS, 2 * HEADS)

    xn = rms_fwd("rms_attn", x, g_attn)
    lat, = matmul("proj_lat", xn, w["w_lat"], "nn", [F32])
    fox, = matmul("proj_fox", xn, w["w_fox"], "nn", [BF16])
    graw, = matmul("proj_gate", xn, w["w_gate"], "nn", [F32])
    cq = rms_fwd("rms_q", (lat, Q_LORA, 0), g_q)
    ckv = rms_fwd("rms_kv", (lat, KV_LORA, Q_LORA // KV_LORA), g_kv)
    qraw, = matmul("up_q", cq, w["w_uq"], "nn", [F32])
    q = rope_heads("rope_q", qraw, tabs, BF16)
    kvn, = matmul("up_kv", ckv, w["w_kv"], "nn", [BF16])
    kr = rope_block("rope_k", lat, OFF_KR // 128, tabs, BF16)
    k = k_assemble("k_assemble", kvn, kr)
    o_mla, lse_mla, gathered_b = attn_fwd("mla_fwd", q, k, kvn, mla_offs, QK_PAD, CHUNK, mla_scale,
                                          comm=hook_gather_first(packed_b))
    ff_t = lat[:, OFF_FQ:OFF_FQ + HEADS].T
    cum = fox_cum_fwd("fox_cum", ff_t, f_bias).reshape(HEADS, s_len, 1) * LOG2E
    o_fox, lse_fox, gathered_b = attn_fwd("fox_fwd", fox, fox, fox, fox_offs, HEAD_V, 1, fox_scale, cum=cum,
                                          comm=hook_gather_second(gathered_b), split_p=True)
    unpack_b = functools.partial(unpack_group, names=GROUP_B, shard_shapes=shard_shapes)
    w = {**w, **unpack_b(gathered_b)}
    y_mla, = matmul("branch_mla", o_mla, w["w_mla_branch"], "nn", [F32])
    y_fox, = matmul("branch_fox", o_fox, w["w_fox_branch"], "nn", [F32])
    mix = gate_mix("gate_mix", graw, y_mla, y_fox)
    h1, = matmul("out_proj", mix, w["w_out"], "nn", [F32], epilogue=lambda acc, res: (res + acc,), extras=[x])
    hn = rms_fwd("rms_mlp", h1, g_mlp)

    def relu2(acc):
        r = jnp.maximum(acc, 0.0)
        return r * r, r
    u, relu_up = matmul("mlp_up", hn, w["w_up"], "nn", [BF16, BF16], epilogue=relu2)
    h2, = matmul("mlp_down", u, w["w_down"], "nn", [F32], epilogue=lambda acc, res: (res + acc,), extras=[h1])
    dh2, d_final, loss = loss_head("loss_head", h2, target, g_final)

    grads = {}
    dup, = matmul("d_mlp_down", dh2, w["w_down"], "nt", [BF16],
                  epilogue=lambda acc, r: (acc * (2.0 * r.astype(F32)),), extras=[relu_up])
    grads["w_down"], = matmul("gw_down", u, dh2, "tn", [BF16], **TN_TILES)
    dhn, = matmul("d_mlp_up", dup, w["w_up"], "nt", [F32])
    grads["w_up"], = matmul("gw_up", hn, dup, "tn", [BF16], **TN_TILES)
    dh1, d_mlp = rms_bwd("rms_mlp_bwd", h1, dhn, g_mlp, dres=dh2)
    dmix, = matmul("d_out_proj", dh1, w["w_out"], "nt", [F32])
    grads["w_out"], = matmul("gw_out", mix, dh1, "tn", [BF16], **TN_TILES)
    dproj, dy_mla, dy_fox = gate_mix_bwd("gate_mix_bwd", graw, y_mla, y_fox, dmix)
    do_mla, = matmul("d_branch_mla", dy_mla, w["w_mla_branch"], "nt", [BF16])
    grads["w_mla_branch"], = matmul("gw_branch_mla", o_mla, dy_mla, "tn", [BF16], **TN_TILES)
    do_fox, = matmul("d_branch_fox", dy_fox, w["w_fox_branch"], "nt", [BF16])
    grads["w_fox_branch"], = matmul("gw_branch_fox", o_fox, dy_fox, "tn", [BF16], **TN_TILES)

    to_packed_b = jax.linear_transpose(unpack_b, jax.ShapeDtypeStruct(gathered_b.shape, BF16))
    parts_b, = to_packed_b({n: grads.pop(n) for n in GROUP_B})
    delta_mla = attn_delta("mla_delta", do_mla, o_mla)
    dk, dv, dq, from_sibling = attn_bwd("mla_bwd", q, k, kvn, do_mla, lse_mla, delta_mla, mla_offs, QK_PAD, CHUNK,
                                        mla_scale, F32, comm=hook_swap_sibling(parts_b))

    delta_fox = attn_delta("fox_delta", do_fox, o_fox)

    def chips_behind_fox_bwd(pair):
        dfk, dfv, dfq, dcum, from_chips = attn_bwd("fox_bwd", fox, fox, fox, do_fox, lse_fox, delta_fox, fox_offs,
                                                   HEAD_V, 1, fox_scale, BF16, cum=cum, comm=hook_swap_chips(pair))
        return from_chips, (dfq.astype(BF16), dfk, dfv, dcum)
    g_packed_b, (dfq, dfk, dfv, dcum) = reduce_scatter_tail(parts_b, from_sibling, chips_behind_fox_bwd, "b")
    dff_t, d_bias = fox_cum_bwd("fox_cum_bwd", ff_t, f_bias, dcum.reshape(HEADS, s_len))
    dq_r = rope_heads("rope_q_bwd", dq, inv_tabs, BF16)
    dkvn, dkr = dk_split("dk_split", dk, dv, inv_tabs)
    dcq, = matmul("d_up_q", dq_r, w["w_uq"], "nt", [F32])
    grads["w_uq"], = matmul("gw_uq", cq, dq_r, "tn", [BF16], tm=512, tn=2048, tk=2048)
    dckv, = matmul("d_up_kv", dkvn, w["w_kv"], "nt", [F32])
    grads["w_kv"], = matmul("gw_kv", ckv, dkvn, "tn", [BF16], tm=256, tn=2048, tk=2048)
    dcq_raw, d_qn = rms_bwd("rms_q_bwd", (lat, Q_LORA, 0), dcq, g_q, out_dtype=BF16)
    dckv_raw, d_kvn = rms_bwd("rms_kv_bwd", (lat, KV_LORA, Q_LORA // KV_LORA), dckv, g_kv, out_dtype=BF16)

    pad = jnp.zeros((s_len, LAT_W - OFF_FQ - HEADS), BF16)
    rest = jnp.concatenate([dfq, dfk, dfv, dcq_raw, dckv_raw, dkr[:, :ROPE].astype(BF16), dff_t.T.astype(BF16), pad],
                           axis=1)
    dproj = lax.dynamic_update_slice(dproj, rest, (0, GATE_W))
    w_in_p = jnp.concatenate([w["w_gate"], w["w_fox"], w["w_lat"]], axis=1)
    gw_in, = matmul("gw_in", xn, dproj, "tn", [BF16], tm=1024, tn=1152, tk=2048)
    grads["w_gate"], grads["w_fox"], grads["w_lat"] = (gw_in[:, :GATE_W], gw_in[:, GATE_W:GATE_W + FOX_W],
                                                      gw_in[:, GATE_W + FOX_W:])

    parts_a = to_packed_a(grads)
    from_sibling_a = comm_swap_sibling("comm_rs_sibling_a", parts_a)

    def chips_behind_d_proj(pair):
        dxn, from_chips = matmul("d_proj", dproj, w_in_p, "nt", [F32], tk=2688, comm=hook_swap_chips(pair))
        return from_chips, dxn
    g_packed_a, dxn = reduce_scatter_tail(parts_a, from_sibling_a, chips_behind_d_proj, "a")
    dx, d_attn = rms_bwd("rms_attn_bwd", x, dxn, g_attn, dres=dh1)

    small_grads = {"attn_norm": d_attn, "fox_f_bias": d_bias, "q_norm": d_qn, "kv_norm": d_kvn,
                   "mlp_norm": d_mlp, "final_norm": d_final}
    return loss, dx, g_packed_a, g_packed_b, small_grads


def kernel(x, attn_norm, w_in, fox_f_bias, q_norm, w_uq, kv_norm, w_ukv, w_mla_branch, w_fox_branch, w_out, mlp_norm, w_up, w_down, final_norm, loss_target, m_attn_norm, m_w_in, m_fox_f_bias, m_q_norm, m_w_uq, m_kv_norm, m_w_ukv, m_w_mla_branch, m_w_fox_branch, m_w_out, m_mlp_norm, m_w_up, m_w_down, m_final_norm, v_attn_norm, v_w_in, v_fox_f_bias, v_q_norm, v_w_uq, v_kv_norm, v_w_ukv, v_w_mla_branch, v_w_fox_branch, v_w_out, v_mlp_norm, v_w_up, v_w_down, v_final_norm):
    given = dict(locals())
    big = {n: given[n][0] for n in BIG}
    small = {n: given[n] for n in SMALL}
    shard_shapes = {n: tuple(big[n].shape) for n in BIG}

    packed_a = pack_local({n: big[n].astype(BF16) for n in GROUP_A}, GROUP_A)
    packed_b = pack_local({n: big[n].astype(BF16) for n in GROUP_B}, GROUP_B)
    gathered_a = comm_allgather("comm_allgather_a", packed_a)
    unpack = functools.partial(unpack_a, shard_shapes=shard_shapes)
    w_a = unpack(gathered_a)

    transpose_a = jax.linear_transpose(unpack, jax.ShapeDtypeStruct(gathered_a.shape, BF16))
    loss_part, dx, g_packed_a, g_packed_b, small_grads = local_step(
        x[0], loss_target[0], w_a, small, packed_b, shard_shapes, lambda grads: transpose_a(grads)[0])

    small_sum = comm_allreduce_small("comm_allreduce_small", pack_small(small_grads, loss_part[0, 0]))
    g_small, loss = unpack_small(small_sum)

    grad_w, delta_w, new_m, new_v = {}, {}, {}, {}
    for names, g_packed in ((GROUP_A, g_packed_a), (GROUP_B, g_packed_b)):
        r0 = 0
        for n in names:
            rs, cs = shard_shapes[n]
            nr = _packed_rows((rs, cs))
            g = _from_rows(g_packed[r0:r0 + nr], (rs, cs))
            r0 += nr
            d, m_new, v_new = adamw("adamw_" + n, big[n], g, given["m_" + n][0], given["v_" + n][0])
            grad_w[n], delta_w[n], new_m[n], new_v[n] = g[None], d[None], m_new[None], v_new[None]
    zero = jnp.zeros((), F32)
    d_s, m_s, v_s = adamw("adamw_small", pack_small(small, zero), small_sum * _small_mask(),
                          pack_small({n: given["m_" + n] for n in SMALL}, zero),
                          pack_small({n: given["v_" + n] for n in SMALL}, zero), tr=SMALL_ROWS)
    d_small, _ = unpack_small(d_s)
    m_small, _ = unpack_small(m_s)
    v_small, _ = unpack_small(v_s)
    for n in SMALL:
        shape = given[n].shape
        grad_w[n], delta_w[n] = g_small[n].reshape(shape), d_small[n].reshape(shape)
        new_m[n], new_v[n] = m_small[n].reshape(shape), v_small[n].reshape(shape)

    order = ["attn_norm", "w_in", "fox_f_bias", "q_norm", "w_uq", "kv_norm", "w_ukv", "w_mla_branch", "w_fox_branch",
             "w_out", "mlp_norm", "w_up", "w_down", "final_norm"]
    return (loss, dx[None], *[grad_w[n] for n in order], *[delta_w[n] for n in order],
            *[new_m[n] for n in order], *[new_v[n] for n in order])


def _small_mask():
    used = sum(SMALL_N[n] for n in SMALL)
    return (jnp.arange(SMALL_ROWS * PACK_C) < used).astype(F32).reshape(SMALL_ROWS, PACK_C)
```

```python
import functools
import math

import jax
import jax.numpy as jnp
from jax import lax
from jax.experimental import pallas as pl
from jax.experimental.pallas import tpu as pltpu

F32 = jnp.float32
BF16 = jnp.bfloat16
MESH = pl.DeviceIdType.MESH

D_MODEL = 2048
HEADS = 8
Q_LORA = 512
KV_LORA = 256
NOPE = 128
ROPE = 64
HEAD_V = 128
D_FF = 4 * D_MODEL
CHUNK = 64
EPS = 1e-6
ROPE_THETA = 10000.0
OFF_KR = Q_LORA + KV_LORA
OFF_FQ = OFF_KR + ROPE
OFF_FF = OFF_FQ + 3 * HEADS * HEAD_V
OFF_G = OFF_FF + HEADS
D_IN = OFF_G + 2 * D_MODEL

LAT_W = 896
FOX_W = 3 * HEADS * HEAD_V
GATE_W = 2 * D_MODEL
PROJ_W = LAT_W + FOX_W + GATE_W
QK_PAD = 256

ADAM_LR = 0.001
ADAM_B1 = 0.9
ADAM_B2 = 0.999
ADAM_EPS = 1e-08
ADAM_WD = 0.01
ADAM_STEP = 10

N_DEV = 8
PACK_C = 1024
NEG = -1e30
LOG2E = math.log2(math.e)

VMEM_LIMIT = 56 * 1024 * 1024

BIG = ("w_in", "w_uq", "w_ukv", "w_mla_branch", "w_fox_branch", "w_out", "w_up", "w_down")
GROUP_A = ("w_in", "w_uq", "w_ukv")
GROUP_B = ("w_mla_branch", "w_fox_branch", "w_out", "w_up", "w_down")
ROW_SHARDED = ("w_out", "w_down")
SMALL = ("attn_norm", "fox_f_bias", "q_norm", "kv_norm", "mlp_norm", "final_norm")
SMALL_N = {"attn_norm": D_MODEL, "fox_f_bias": HEADS, "q_norm": Q_LORA, "kv_norm": KV_LORA,
           "mlp_norm": D_MODEL, "final_norm": D_MODEL}
SMALL_ROWS = 8


def _params(sem):
    return pltpu.CompilerParams(dimension_semantics=sem, vmem_limit_bytes=VMEM_LIMIT)


def _rows(name, fn, row_ins, const_ins, outs, reds=(), tr=256):
    norm = [(a, a.shape[1], 0) if not isinstance(a, tuple) else a for a in row_ins]
    n_rows = norm[0][0].shape[0]
    tr = min(tr, n_rows)
    assert n_rows % tr == 0, (name, n_rows, tr)
    n_in, n_out, n_red = len(norm) + len(const_ins), len(outs), len(reds)

    def body(*refs):
        vals = [r[...] for r in refs[:n_in]]
        out_refs = refs[n_in:n_in + n_out]
        red_refs = refs[n_in + n_out:]
        out_vals, red_vals = fn(*vals)
        for r, v in zip(out_refs, out_vals):
            r[...] = v.astype(r.dtype)
        if n_red:
            @pl.when(pl.program_id(0) == 0)
            def _():
                for r in red_refs:
                    r[...] = jnp.zeros_like(r)
            for r, v in zip(red_refs, red_vals):
                r[...] += v

    in_specs = [pl.BlockSpec((tr, w), functools.partial(lambda i, cb: (i, cb), cb=cb)) for _, w, cb in norm]
    in_specs += [pl.BlockSpec(a.shape, lambda i: (0, 0)) for a in const_ins]
    out_specs = [pl.BlockSpec((tr, o[0]), lambda i: (i, 0)) for o in outs]
    out_specs += [pl.BlockSpec((1, c), lambda i: (0, 0)) for c in reds]
    out_shape = [jax.ShapeDtypeStruct((n_rows, o[2] if len(o) > 2 else o[0]), o[1]) for o in outs]
    out_shape += [jax.ShapeDtypeStruct((1, c), F32) for c in reds]
    res = pl.pallas_call(
        body, name=name, grid=(n_rows // tr,), in_specs=in_specs, out_specs=out_specs, out_shape=out_shape,
        compiler_params=_params(("arbitrary",)),
    )(*[a for a, _, _ in norm], *const_ins)
    return res


def _rstd(x):
    return lax.rsqrt(jnp.mean(x * x, axis=-1, keepdims=True) + EPS)


def rms_fwd(name, x, gain, tr=256):
    width = x[1] if isinstance(x, tuple) else x.shape[1]

    def fn(xv, g):
        return (xv * _rstd(xv) * g,), ()
    return _rows(name, fn, [x], [gain], [(width, BF16)], tr=tr)[0]


def rms_bwd(name, x, dy, gain, dres=None, out_dtype=F32, tr=256):
    width = x[1] if isinstance(x, tuple) else x.shape[1]

    def fn(xv, dyv, *rest):
        g = rest[-1]
        r = _rstd(xv)
        n = xv * r
        dyv = dyv.astype(F32)
        dn = dyv * g
        dx = r * (dn - n * jnp.mean(dn * n, axis=-1, keepdims=True))
        if dres is not None:
            dx = dx + rest[0]
        return (dx,), (jnp.sum(dyv * n, axis=0, keepdims=True),)

    ins = [x, dy] + ([dres] if dres is not None else [])
    return _rows(name, fn, ins, [gain], [(width, out_dtype)], [width], tr=tr)


def _rope_lanes(t, c, s1, s2):
    return t * c + pltpu.roll(t, 96, 1) * s1 + pltpu.roll(t, 32, 1) * s2


def rope_heads(name, x, tabs, out_dtype):
    def fn(xv, c, s1, s2):
        xv = xv.astype(F32)
        parts = []
        for h in range(HEADS):
            parts.append(xv[:, h * QK_PAD:h * QK_PAD + NOPE])
            parts.append(_rope_lanes(xv[:, h * QK_PAD + NOPE:(h + 1) * QK_PAD], c, s1, s2))
        return (jnp.concatenate(parts, axis=1),), ()
    return _rows(name, fn, [x, *tabs], [], [(HEADS * QK_PAD, out_dtype)])[0]


def rope_block(name, x, col_block, tabs, out_dtype):
    def fn(xv, c, s1, s2):
        return (_rope_lanes(xv.astype(F32), c, s1, s2),), ()
    return _rows(name, fn, [(x, 128, col_block), *tabs], [], [(128, out_dtype)])[0]


def k_assemble(name, kvn, kr):
    def fn(knp, krv):
        parts = []
        for h in range(HEADS):
            parts.append(knp[:, h * NOPE:(h + 1) * NOPE])
            parts.append(krv)
        return (jnp.concatenate(parts, axis=1),), ()
    return _rows(name, fn, [(kvn, HEADS * NOPE, 0), kr], [], [(HEADS * QK_PAD, BF16)])[0]


def dk_split(name, dk, dv, inv_tabs):
    def fn(dkv, dvv, c, s1, s2):
        parts = []
        acc = None
        for h in range(HEADS):
            parts.append(dkv[:, h * QK_PAD:h * QK_PAD + NOPE].astype(BF16))
            t = dkv[:, h * QK_PAD + NOPE:(h + 1) * QK_PAD]
            acc = t if acc is None else acc + t
        parts.append(dvv)
        return (jnp.concatenate(parts, axis=1), _rope_lanes(acc, c, s1, s2)), ()
    return _rows(name, fn, [dk, dv, *inv_tabs], [], [(2 * HEADS * NOPE, BF16), (128, F32)])


def gate_mix(name, graw, y_mla, y_fox):
    def fn(g, ya, yb):
        ga = jax.nn.sigmoid(g[:, :D_MODEL])
        gb = jax.nn.sigmoid(g[:, D_MODEL:])
        return (ga * ya + gb * yb,), ()
    return _rows(name, fn, [graw, y_mla, y_fox], [], [(D_MODEL, BF16)])[0]


def gate_mix_bwd(name, graw, y_mla, y_fox, dmix):
    def fn(g, ya, yb, dm):
        ga = jax.nn.sigmoid(g[:, :D_MODEL])
        gb = jax.nn.sigmoid(g[:, D_MODEL:])
        dgraw = jnp.concatenate([dm * ya * ga * (1.0 - ga), dm * yb * gb * (1.0 - gb)], axis=1)
        return (dgraw, dm * ga, dm * gb), ()
    return _rows(name, fn, [graw, y_mla, y_fox, dmix], [],
                 [(GATE_W, BF16, PROJ_W), (D_MODEL, BF16), (D_MODEL, BF16)], tr=256)


def loss_head(name, h2, target, gain):
    inv_d = 1.0 / D_MODEL

    def fn(h, t, g):
        r = _rstd(h)
        n = h * r
        err = n * g - t
        dy = err * inv_d
        dn = dy * g
        dh = r * (dn - n * jnp.mean(dn * n, axis=-1, keepdims=True))
        part = 0.5 * inv_d * jnp.sum(jnp.sum(err * err, axis=1, keepdims=True), axis=0, keepdims=True)
        return (dh,), (jnp.sum(dy * n, axis=0, keepdims=True), jnp.broadcast_to(part, (1, 128)))
    return _rows(name, fn, [h2, target], [gain], [(D_MODEL, F32)], [D_MODEL, 128])


def adamw(name, w, g, m, v, tr=256):
    c1 = 1.0 - ADAM_B1 ** ADAM_STEP
    c2 = 1.0 - ADAM_B2 ** ADAM_STEP

    def fn(wv, gv, mv, vv):
        m_new = ADAM_B1 * mv + (1.0 - ADAM_B1) * gv
        v_new = ADAM_B2 * vv + (1.0 - ADAM_B2) * (gv * gv)
        delta = -ADAM_LR * ((m_new / c1) / (jnp.sqrt(v_new / c2) + ADAM_EPS) + ADAM_WD * wv)
        return (delta, m_new, v_new), ()
    cols = w.shape[1]
    return _rows(name, fn, [w, g, m, v], [], [(cols, F32)] * 3, tr=tr)


def _row_tile(n_rows, cap=640):
    return max(t for t in range(16, cap + 1, 16) if n_rows % t == 0)


def add_pairs(name, a, b):
    def fn(av, bv):
        return (av.astype(F32) + bv.astype(F32),), ()
    return _rows(name, fn, [a, b], [], [(a.shape[1], BF16)], tr=_row_tile(a.shape[0]))[0]


def add_final(name, a, b, r0, r1, r2):
    def fn(av, bv, r0v, r1v, r2v):
        return (((av.astype(F32) + bv.astype(F32)) + r0v.astype(F32)) + r1v.astype(F32) + r2v.astype(F32),), ()
    return _rows(name, fn, [a, b, r0, r1, r2], [], [(a.shape[1], F32)], tr=_row_tile(a.shape[0]))[0]


TN_TILES = dict(tm=1024, tn=1024, tk=2048)
_DIMS = {"nn": (((1,), (0,)), ((), ())), "nt": (((1,), (1,)), ((), ())), "tn": (((0,), (0,)), ((), ()))}


def matmul(name, a, b, mode, outs, epilogue=None, extras=(), tm=1024, tn=1024, tk=2048, comm=None):
    if mode == "tn":
        kdim, m = a.shape
    else:
        m, kdim = a.shape
    n = b.shape[0] if mode == "nt" else b.shape[1]
    tm, tn, tk = min(tm, m), min(tn, n), min(tk, kdim)
    assert m % tm == 0 and n % tn == 0 and kdim % tk == 0, (name, a.shape, b.shape)
    nk = kdim // tk
    n_ex, n_out = len(extras), len(outs)
    dims = _DIMS[mode]
    grid = (m // tm, n // tn, nk)

    def body(*refs):
        ins, c_in, out_refs, c_out, scr, c_sems = _split_refs(refs, 2 + n_ex, comm, n_out, 1 if nk > 1 else 0)
        a_ref, b_ref = ins[:2]
        ex_refs = ins[2:]
        _comm_start(comm, c_in, c_out, c_sems, *grid)

        def finish(acc):
            vals = (acc,) if epilogue is None else epilogue(acc, *[r[...] for r in ex_refs])
            for r, v in zip(out_refs, vals):
                r[...] = v.astype(r.dtype)

        part = lax.dot_general(a_ref[...].astype(BF16), b_ref[...].astype(BF16), dims, preferred_element_type=F32)
        if nk == 1:
            finish(part)
        else:
            acc_ref = scr[0]
            k = pl.program_id(2)

            @pl.when(k == 0)
            def _():
                acc_ref[...] = part

            @pl.when(k > 0)
            def _():
                acc_ref[...] += part

            @pl.when(k == nk - 1)
            def _():
                finish(acc_ref[...])
        _comm_wait(comm, c_in, c_out, c_sems, *grid)

    a_spec = pl.BlockSpec((tk, tm), lambda i, j, k: (k, i)) if mode == "tn" else pl.BlockSpec((tm, tk), lambda i, j, k: (i, k))
    b_spec = pl.BlockSpec((tn, tk), lambda i, j, k: (j, k)) if mode == "nt" else pl.BlockSpec((tk, tn), lambda i, j, k: (k, j))
    tile = pl.BlockSpec((tm, tn), lambda i, j, k: (i, j))
    c_ins, c_outs, c_shapes, c_scratch, aliases = _with_comm(comm, 2 + n_ex, n_out)
    sem = ("arbitrary",) * 3 if comm else ("parallel", "parallel", "arbitrary")
    res = pl.pallas_call(
        body, name=name, grid=grid,
        in_specs=[a_spec, b_spec] + [tile] * n_ex + c_ins,
        out_specs=[tile] * n_out + c_outs,
        out_shape=[jax.ShapeDtypeStruct((m, n), dt) for dt in outs] + c_shapes,
        scratch_shapes=([pltpu.VMEM((tm, tn), F32)] if nk > 1 else []) + c_scratch,
        input_output_aliases=aliases,
        compiler_params=_params(sem),
    )(a, b, *extras, *(comm.args if comm else []))
    return res


_NT = (((1,), (1,)), ((), ()))
_NN = (((1,), (0,)), ((), ()))


def _mask(bq, chunk, transposed, row0=0, shape=None):
    shape = (bq, bq) if shape is None else shape
    row = lax.broadcasted_iota(jnp.int32, shape, 0) + row0
    col = lax.broadcasted_iota(jnp.int32, shape, 1)
    if chunk > 1:
        row, col = row // chunk, col // chunk
    return (row <= col) if transposed else (col <= row)


def _row_layout(a, bq):
    h, s, _ = a.shape
    return a.reshape(h, s // bq, 1, bq)


def _split_refs(refs, n_in, comm, n_out, n_scr):
    n_ci = len(comm.args) if comm else 0
    n_co = len(comm.out_shapes) if comm else 0
    cuts = [n_in, n_ci, n_out, n_co, n_scr, 3 if comm else 0]
    parts, at = [], 0
    for n in cuts:
        parts.append(list(refs[at:at + n]))
        at += n
    assert at == len(refs), (at, len(refs))
    return parts


def _at_step(grid, last):
    hit = None
    for axis, n in enumerate(grid):
        here = pl.program_id(axis) == (n - 1 if last else 0)
        hit = here if hit is None else jnp.logical_and(hit, here)
    return hit


def _comm_start(comm, c_in, c_out, c_sems, *grid):
    if comm is not None:
        @pl.when(_at_step(grid, False))
        def _():
            comm.start(c_in, c_out, c_sems)


def _comm_wait(comm, c_in, c_out, c_sems, *grid):
    if comm is not None:
        @pl.when(_at_step(grid, True))
        def _():
            comm.wait(c_in, c_out, c_sems)


def attn_fwd(name, q, k, v, offs, dqk, chunk, scale, cum=None, bq=512, comm=None, split_p=False):
    s_len = q.shape[0]
    nq = s_len // bq
    qoff, koff, voff = offs
    has_bias = cum is not None
    scale2 = scale * LOG2E

    n_in = 5 if has_bias else 3

    def body(*refs):
        ins, c_in, outs, c_out, scr, c_sems = _split_refs(refs, n_in, comm, 2, 3)
        q_ref, k_ref, v_ref = ins[:3]
        if has_bias:
            cc_ref, cr_ref = ins[3:]
        o_ref, lse_ref = outs
        m_s, l_s, acc_s = scr
        _comm_start(comm, c_in, c_out, c_sems, HEADS, nq)
        i = pl.program_id(1)
        qv = q_ref[...]
        m_s[...] = jnp.full_like(m_s, NEG)
        l_s[...] = jnp.zeros_like(l_s)
        acc_s[...] = jnp.zeros_like(acc_s)

        def step(j, masked):
            off = pl.multiple_of(j * bq, bq)
            kj = k_ref[pl.ds(off, bq), :]
            vj = v_ref[pl.ds(off, bq), :]
            st = lax.dot_general(kj, qv, _NT, preferred_element_type=F32) * scale2
            if has_bias:
                st = st + cr_ref[...] - cc_ref[pl.ds(off, bq), :]
            if masked:
                st = jnp.where(_mask(bq, chunk, True), st, NEG)
            m_prev = m_s[...]
            m_new = jnp.maximum(m_prev, jnp.max(st, axis=0, keepdims=True))
            alpha = jnp.exp2(m_prev - m_new)
            pt = jnp.exp2(st - m_new)
            l_s[...] = alpha * l_s[...] + jnp.sum(pt, axis=0, keepdims=True)
            p_hi = pt.astype(BF16)
            pv = lax.dot_general(vj, p_hi, _DIMS["tn"], preferred_element_type=F32)
            if split_p:
                p_lo = (pt - p_hi.astype(F32)).astype(BF16)
                pv = pv + lax.dot_general(vj, p_lo, _DIMS["tn"], preferred_element_type=F32)
            acc_s[...] = alpha * acc_s[...] + pv
            m_s[...] = m_new

        def pair_body(jj, carry):
            step(2 * jj, False)
            step(2 * jj + 1, False)
            return carry

        lax.fori_loop(0, i // 2, pair_body, 0)

        @pl.when(i % 2 == 1)
        def _():
            step(i - 1, False)

        step(i, True)
        o_ref[...] = (acc_s[...] / l_s[...]).T.astype(o_ref.dtype)
        lse_ref[...] = m_s[...] + jnp.log2(l_s[...])
        _comm_wait(comm, c_in, c_out, c_sems, HEADS, nq)

    in_specs = [
        pl.BlockSpec((bq, dqk), lambda h, i: (i, qoff + h)),
        pl.BlockSpec((s_len, dqk), lambda h, i: (0, koff + h)),
        pl.BlockSpec((s_len, HEAD_V), lambda h, i: (0, voff + h)),
    ]
    args = [q, k, v]
    if has_bias:
        in_specs += [pl.BlockSpec((None, s_len, 1), lambda h, i: (h, 0, 0)),
                     pl.BlockSpec((None, None, 1, bq), lambda h, i: (h, i, 0, 0))]
        args += [cum, _row_layout(cum, bq)]
    c_ins, c_outs, c_shapes, c_scratch, aliases = _with_comm(comm, len(args), 2)
    o, lse_rows, *comm_out = pl.pallas_call(
        body, name=name, grid=(HEADS, nq), in_specs=in_specs + c_ins,
        out_specs=[pl.BlockSpec((bq, HEAD_V), lambda h, i: (i, h)),
                   pl.BlockSpec((None, None, 1, bq), lambda h, i: (h, i, 0, 0))] + c_outs,
        out_shape=[jax.ShapeDtypeStruct((s_len, HEADS * HEAD_V), F32),
                   jax.ShapeDtypeStruct((HEADS, nq, 1, bq), F32)] + c_shapes,
        scratch_shapes=[pltpu.VMEM((1, bq), F32), pltpu.VMEM((1, bq), F32), pltpu.VMEM((HEAD_V, bq), F32)] + c_scratch,
        input_output_aliases=aliases,
        compiler_params=_params(("arbitrary", "arbitrary")),
    )(*args, *(comm.args if comm else []))
    return (o, lse_rows, *comm_out)


def _bwd_block(kv, vv, qi, doi, lse_row, scale2, bias, masked, bq, chunk):
    st = lax.dot_general(kv, qi, _NT, preferred_element_type=F32) * scale2
    if bias is not None:
        st = st + bias[0] - bias[1]
    if masked:
        st = jnp.where(_mask(bq, chunk, True), st, NEG)
    pt = jnp.exp2(st - lse_row)
    dpt = lax.dot_general(vv, doi, _NT, preferred_element_type=F32)
    return pt, dpt


def attn_delta(name, do, o, bq=512):
    s_len = do.shape[0]
    nq = s_len // bq

    def body(do_ref, o_ref, out_ref):
        ones = jnp.ones((8, HEAD_V), BF16)
        for h in range(HEADS):
            cols = slice(h * HEAD_V, (h + 1) * HEAD_V)
            prod = do_ref[:, cols].astype(F32) * o_ref[:, cols].astype(F32)
            total = None
            for part in reversed(_split3(prod)):
                term = lax.dot_general(ones, part, _NT, preferred_element_type=F32)
                total = term if total is None else total + term
            out_ref[h] = total[0:1, :]

    blk = pl.BlockSpec((bq, HEADS * HEAD_V), lambda i: (i, 0))
    return pl.pallas_call(
        body, name=name, grid=(nq,), in_specs=[blk, blk],
        out_specs=pl.BlockSpec((HEADS, None, 1, bq), lambda i: (0, i, 0, 0)),
        out_shape=jax.ShapeDtypeStruct((HEADS, nq, 1, bq), F32),
        compiler_params=_params(("parallel",)),
    )(do, o)


def attn_bwd(name, q, k, v, do, lse, delta, offs, dqk, chunk, scale, out_dtype, cum=None, bq=512, comm=None):
    s_len = q.shape[0]
    nq = s_len // bq
    qoff, koff, voff = offs
    has_bias = cum is not None
    scale2 = scale * LOG2E
    n_in, n_out = (8, 4) if has_bias else (6, 3)

    def body(*refs):
        ins, c_in, outs, c_out, scr, c_sems = _split_refs(refs, n_in, comm, n_out, n_out - 1)
        k_ref, v_ref, q_ref, do_ref, lse_ref, delta_ref = ins[:6]
        dk_ref, dv_ref, dq_ref = outs[:3]
        dk_s, dv_s = scr[:2]
        if has_bias:
            cc_ref, cr_ref = ins[6:]
            dc_ref, dc_s = outs[3], scr[2]
        _comm_start(comm, c_in, c_out, c_sems, HEADS, nq)
        j = pl.program_id(1)
        kv = k_ref[...]
        vv = v_ref[...]
        dk_s[...] = jnp.zeros_like(dk_s)
        dv_s[...] = jnp.zeros_like(dv_s)
        if has_bias:
            dc_s[...] = jnp.zeros_like(dc_s)

        @pl.when(j == 0)
        def _():
            dq_ref[...] = jnp.zeros_like(dq_ref)

        def step(i, masked):
            off = pl.multiple_of(i * bq, bq)
            qi = q_ref[pl.ds(off, bq), :]
            doi = do_ref[pl.ds(off, bq), :].astype(BF16)
            bias = (cr_ref[i], cc_ref[...]) if has_bias else None
            pt, dpt = _bwd_block(kv, vv, qi, doi, lse_ref[i], scale2, bias, masked, bq, chunk)
            dv_s[...] += lax.dot_general(pt.astype(BF16), doi, _NN, preferred_element_type=F32)
            dst = pt * (dpt - delta_ref[i])
            if has_bias:
                dc_s[...] -= jnp.sum(dst, axis=-1, keepdims=True)
            dst = dst.astype(BF16)
            dk_s[...] += lax.dot_general(dst, qi, _NN, preferred_element_type=F32)
            dq_ref[pl.ds(off, bq), :] += lax.dot_general(dst, kv, _DIMS["tn"], preferred_element_type=F32) * scale

        step(j, True)

        def loop_body(i, carry):
            step(i, False)
            return carry

        lax.fori_loop(j + 1, nq, loop_body, 0)
        dk_ref[...] = (dk_s[...] * scale).astype(dk_ref.dtype)
        dv_ref[...] = dv_s[...].astype(dv_ref.dtype)
        if has_bias:
            dc_ref[...] = dc_s[...]
        _comm_wait(comm, c_in, c_out, c_sems, HEADS, nq)

    rows = pl.BlockSpec((None, nq, 1, bq), lambda h, j: (h, 0, 0, 0))
    in_specs = [
        pl.BlockSpec((bq, dqk), lambda h, j: (j, koff + h)),
        pl.BlockSpec((bq, HEAD_V), lambda h, j: (j, voff + h)),
        pl.BlockSpec((s_len, dqk), lambda h, j: (0, qoff + h)),
        pl.BlockSpec((s_len, HEAD_V), lambda h, j: (0, h)),
        rows,
        rows,
    ]
    args = [k, v, q, do, lse, delta]
    out_specs = [pl.BlockSpec((bq, dqk), lambda h, j: (j, h)), pl.BlockSpec((bq, HEAD_V), lambda h, j: (j, h)),
                 pl.BlockSpec((s_len, dqk), lambda h, j: (0, h))]
    out_shape = [jax.ShapeDtypeStruct((s_len, HEADS * dqk), out_dtype),
                 jax.ShapeDtypeStruct((s_len, HEADS * HEAD_V), BF16),
                 jax.ShapeDtypeStruct((s_len, HEADS * dqk), F32)]
    scratch = [pltpu.VMEM((bq, dqk), F32), pltpu.VMEM((bq, HEAD_V), F32)]
    if has_bias:
        in_specs += [pl.BlockSpec((None, bq, 1), lambda h, j: (h, j, 0)), rows]
        args += [cum, _row_layout(cum, bq)]
        out_specs.append(pl.BlockSpec((None, bq, 1), lambda h, j: (h, j, 0)))
        out_shape.append(jax.ShapeDtypeStruct((HEADS, s_len, 1), F32))
        scratch.append(pltpu.VMEM((bq, 1), F32))
    c_ins, c_outs, c_shapes, c_scratch, aliases = _with_comm(comm, len(args), n_out)
    return pl.pallas_call(
        body, name=name, grid=(HEADS, nq), in_specs=in_specs + c_ins, out_specs=out_specs + c_outs,
        out_shape=out_shape + c_shapes, scratch_shapes=scratch + c_scratch, input_output_aliases=aliases,
        compiler_params=_params(("arbitrary", "arbitrary")),
    )(*args, *(comm.args if comm else []))


_CUM_BLK = 512


def _split3(x):
    hi = x.astype(BF16)
    r1 = x - hi.astype(F32)
    mid = r1.astype(BF16)
    lo = (r1 - mid.astype(F32)).astype(BF16)
    return hi, mid, lo


def _tri_dot(x, tri):
    hi, mid, lo = _split3(x)
    out = lax.dot_general(lo, tri, _NN, preferred_element_type=F32)
    out = out + lax.dot_general(mid, tri, _NN, preferred_element_type=F32)
    return out + lax.dot_general(hi, tri, _NN, preferred_element_type=F32)


def fox_cum_fwd(name, ff_t, bias):
    s_len = ff_t.shape[1]
    nb = s_len // _CUM_BLK

    def body(ff_ref, b_ref, cum_ref):
        row = lax.broadcasted_iota(jnp.int32, (_CUM_BLK, _CUM_BLK), 0)
        col = lax.broadcasted_iota(jnp.int32, (_CUM_BLK, _CUM_BLK), 1)
        tri = (row <= col).astype(BF16)
        carry = jnp.zeros((HEADS, 1), F32)
        for b in range(nb):
            z = ff_ref[:, b * _CUM_BLK:(b + 1) * _CUM_BLK] + b_ref[...]
            logf = jnp.minimum(z, 0.0) - jnp.log1p(jnp.exp(-jnp.abs(z)))
            blk = _tri_dot(logf, tri) + carry
            cum_ref[:, b * _CUM_BLK:(b + 1) * _CUM_BLK] = blk
            carry = blk[:, _CUM_BLK - 1:_CUM_BLK]

    return pl.pallas_call(
        body, name=name, out_shape=jax.ShapeDtypeStruct((HEADS, s_len), F32),
        compiler_params=pltpu.CompilerParams(vmem_limit_bytes=VMEM_LIMIT),
    )(ff_t, bias)


def fox_cum_bwd(name, ff_t, bias, dcum):
    s_len = ff_t.shape[1]
    nb = s_len // _CUM_BLK

    def body(ff_ref, b_ref, dc_ref, dff_ref, db_ref):
        row = lax.broadcasted_iota(jnp.int32, (_CUM_BLK, _CUM_BLK), 0)
        col = lax.broadcasted_iota(jnp.int32, (_CUM_BLK, _CUM_BLK), 1)
        tri = (row >= col).astype(BF16)
        carry = jnp.zeros((HEADS, 1), F32)
        dbias = jnp.zeros((HEADS, 1), F32)
        for b in reversed(range(nb)):
            sl = slice(b * _CUM_BLK, (b + 1) * _CUM_BLK)
            dlogf = _tri_dot(dc_ref[:, sl], tri) + carry
            carry = dlogf[:, 0:1]
            z = ff_ref[:, sl] + b_ref[...]
            dz = dlogf / (1.0 + jnp.exp(z))
            dff_ref[:, sl] = dz
            dbias = dbias + jnp.sum(dz, axis=-1, keepdims=True)
        db_ref[...] = dbias

    return pl.pallas_call(
        body, name=name,
        out_shape=[jax.ShapeDtypeStruct((HEADS, s_len), F32), jax.ShapeDtypeStruct((HEADS, 1), F32)],
        compiler_params=pltpu.CompilerParams(vmem_limit_bytes=VMEM_LIMIT),
    )(ff_t, bias, dcum)


_ANY = pl.BlockSpec(memory_space=pl.ANY)


def _me():
    return lax.axis_index("x"), lax.axis_index("y"), lax.axis_index("c")


def comm_allgather(name, mine):
    n_rows, n_cols = mine.shape

    def body(x_ref, out_ref, send_sems, recv_sems, local_sem):
        x, y, c = _me()
        sibling = (x, y, 1 - c)
        chips = [(1 - x, y), (x, 1 - y), (1 - x, 1 - y)]

        def blk(px, py, pc):
            return out_ref.at[pc * 4 + px * 2 + py]

        def copy(k, block, to, src=None):
            return pltpu.make_async_remote_copy(
                src_ref=blk(*block) if src is None else src, dst_ref=blk(*block),
                send_sem=send_sems.at[k], recv_sem=recv_sems.at[k], device_id=to, device_id_type=MESH)

        own = pltpu.make_async_copy(x_ref, blk(x, y, c), local_sem)
        own.start()
        first = [copy(0, (x, y, c), sibling, src=x_ref)]
        first += [copy(1 + j, (x, y, c), (*chip, c), src=x_ref) for j, chip in enumerate(chips)]
        for cp in first:
            cp.start()
        passed = [copy(4 + j, (*chip, c), sibling) for j, chip in enumerate(chips)]
        for j, chip in enumerate(chips):
            copy(1 + j, (*chip, c), (x, y, c)).wait_recv()
            passed[j].start()
        copy(0, sibling, (x, y, c)).wait_recv()
        for j, chip in enumerate(chips):
            copy(4 + j, (*chip, 1 - c), (x, y, c)).wait_recv()
        for cp in first + passed:
            cp.wait_send()
        own.wait()

    return pl.pallas_call(
        body, name=name, out_shape=jax.ShapeDtypeStruct((N_DEV, n_rows, n_cols), mine.dtype),
        in_specs=[_ANY], out_specs=_ANY,
        scratch_shapes=[pltpu.SemaphoreType.DMA((7,)), pltpu.SemaphoreType.DMA((7,)), pltpu.SemaphoreType.DMA],
    )(mine)


def comm_swap_sibling(name, parts):
    _, n_rows, n_cols = parts.shape

    def body(p_ref, got_ref, send_sem, recv_sem):
        x, y, c = _me()
        cp = pltpu.make_async_remote_copy(
            src_ref=p_ref.at[pl.ds((1 - c) * 4, 4)], dst_ref=got_ref, send_sem=send_sem, recv_sem=recv_sem,
            device_id=(x, y, 1 - c), device_id_type=MESH)
        cp.start()
        cp.wait()

    return pl.pallas_call(
        body, name=name, out_shape=jax.ShapeDtypeStruct((4, n_rows, n_cols), parts.dtype),
        in_specs=[_ANY], out_specs=_ANY,
        scratch_shapes=[pltpu.SemaphoreType.DMA, pltpu.SemaphoreType.DMA],
    )(parts)


class CommHook:
    def __init__(self, args, out_shapes, n_copies, copies, aliases=None):
        self.args, self.out_shapes, self.n_copies, self.copies = list(args), list(out_shapes), n_copies, copies
        self.aliases = aliases or {}

    def scratch(self):
        return [pltpu.SemaphoreType.DMA((self.n_copies,)), pltpu.SemaphoreType.DMA((self.n_copies,)),
                pltpu.SemaphoreType.DMA((1,))]

    def start(self, in_refs, out_refs, sems):
        sends, _, locs = self.copies(in_refs, out_refs, *sems)
        for cp in locs() + sends():
            cp.start()

    def wait(self, in_refs, out_refs, sems):
        sends, recvs, locs = self.copies(in_refs, out_refs, *sems)
        for cp in sends():
            cp.wait_send()
        for cp in recvs():
            cp.wait_recv()
        for cp in locs():
            cp.wait()


def _remote(src, dst, send_sem, recv_sem, to):
    return pltpu.make_async_remote_copy(src_ref=src, dst_ref=dst, send_sem=send_sem, recv_sem=recv_sem,
                                        device_id=to, device_id_type=MESH)


def hook_gather_first(mine):
    n_rows, n_cols = mine.shape

    def copies(ins, outs, send, recv, local):
        (x_ref,), (out_ref,) = ins, outs
        x, y, c = _me()
        me = c * 4 + x * 2 + y
        peers = [(x, y, 1 - c), (1 - x, y, c), (x, 1 - y, c), (1 - x, 1 - y, c)]

        def sends():
            return [_remote(x_ref, out_ref.at[me], send.at[k], recv.at[k], p) for k, p in enumerate(peers)]

        def recvs():
            return [_remote(x_ref, out_ref.at[pc * 4 + px * 2 + py], send.at[k], recv.at[k], (px, py, pc))
                    for k, (px, py, pc) in enumerate(peers)]

        return sends, recvs, lambda: [pltpu.make_async_copy(x_ref, out_ref.at[me], local.at[0])]

    return CommHook([mine], [jax.ShapeDtypeStruct((N_DEV, n_rows, n_cols), mine.dtype)], 4, copies)


def hook_gather_second(gathered):
    def copies(ins, outs, send, recv, local):
        (g_in,), (g_out,) = ins, outs
        x, y, c = _me()
        chips = [(1 - x, y), (x, 1 - y), (1 - x, 1 - y)]

        def sends():
            return [_remote(g_in.at[c * 4 + px * 2 + py], g_out.at[c * 4 + px * 2 + py], send.at[k], recv.at[k],
                            (x, y, 1 - c)) for k, (px, py) in enumerate(chips)]

        def recvs():
            return [_remote(g_in.at[(1 - c) * 4 + px * 2 + py], g_out.at[(1 - c) * 4 + px * 2 + py], send.at[k],
                            recv.at[k], (x, y, 1 - c)) for k, (px, py) in enumerate(chips)]

        return sends, recvs, lambda: []

    return CommHook([gathered], [jax.ShapeDtypeStruct(gathered.shape, gathered.dtype)], 3, copies, aliases={0: 0})


def hook_swap_sibling(parts):
    _, n_rows, n_cols = parts.shape

    def copies(ins, outs, send, recv, local):
        (p_ref,), (got_ref,) = ins, outs
        x, y, c = _me()

        def swap():
            return [_remote(p_ref.at[pl.ds((1 - c) * 4, 4)], got_ref, send.at[0], recv.at[0], (x, y, 1 - c))]

        return swap, swap, lambda: []

    return CommHook([parts], [jax.ShapeDtypeStruct((4, n_rows, n_cols), parts.dtype)], 1, copies)


def hook_swap_chips(parts):
    _, n_rows, n_cols = parts.shape

    def copies(ins, outs, send, recv, local):
        (p_ref,), (got_ref,) = ins, outs
        x, y, c = _me()
        chips = [(1 - x, y), (x, 1 - y), (1 - x, 1 - y)]

        def swaps():
            return [_remote(p_ref.at[2 * px + py], got_ref.at[k], send.at[k], recv.at[k], (px, py, c))
                    for k, (px, py) in enumerate(chips)]

        return swaps, swaps, lambda: []

    return CommHook([parts], [jax.ShapeDtypeStruct((3, n_rows, n_cols), parts.dtype)], 3, copies)


def _with_comm(comm, n_args, n_outs):
    if comm is None:
        return [], [], [], [], {}
    aliases = {n_args + a: n_outs + o for a, o in comm.aliases.items()}
    return [_ANY] * len(comm.args), [_ANY] * len(comm.out_shapes), comm.out_shapes, comm.scratch(), aliases


def comm_allreduce_small(name, mine):
    shape = mine.shape

    def body(x_ref, out_ref, buf, send_sems, recv_sems):
        x, y, c = _me()
        my_slot = c * 4 + x * 2 + y
        buf[my_slot] = x_ref[...]
        cps = []
        for k in range(1, N_DEV):
            dx, dy, dc = (k >> 2) & 1, (k >> 1) & 1, k & 1
            px, py, pc = x ^ dx, y ^ dy, c ^ dc
            send = pltpu.make_async_remote_copy(
                src_ref=x_ref, dst_ref=buf.at[my_slot], send_sem=send_sems.at[k - 1], recv_sem=recv_sems.at[k - 1],
                device_id=(px, py, pc), device_id_type=MESH)
            send.start()
            recv = pltpu.make_async_remote_copy(
                src_ref=x_ref, dst_ref=buf.at[pc * 4 + px * 2 + py], send_sem=send_sems.at[k - 1],
                recv_sem=recv_sems.at[k - 1], device_id=(px, py, pc), device_id_type=MESH)
            cps.append((send, recv))
        for send, recv in cps:
            send.wait_send()
            recv.wait_recv()
        total = buf[0]
        for s in range(1, N_DEV):
            total = total + buf[s]
        out_ref[...] = total

    vmem = pl.BlockSpec(memory_space=pltpu.VMEM)
    return pl.pallas_call(
        body, name=name, out_shape=jax.ShapeDtypeStruct(shape, F32), in_specs=[vmem], out_specs=vmem,
        scratch_shapes=[pltpu.VMEM((N_DEV,) + shape, F32), pltpu.SemaphoreType.DMA((7,)), pltpu.SemaphoreType.DMA((7,))],
    )(mine)


def _packed_rows(shape):
    rs, cs = shape
    if cs % PACK_C == 0:
        return rs * (cs // PACK_C)
    if cs > PACK_C // 2:
        return rs
    return rs * cs // PACK_C


def _to_rows(a):
    rs, cs = a.shape
    if cs % PACK_C == 0:
        return jnp.concatenate([a[:, i * PACK_C:(i + 1) * PACK_C] for i in range(cs // PACK_C)], axis=0)
    if cs > PACK_C // 2:
        return jnp.pad(a, ((0, 0), (0, PACK_C - cs)))
    return a.reshape(-1, PACK_C)


def _from_rows(rows, shape):
    rs, cs = shape
    lead = rows.shape[:-2]
    if cs % PACK_C == 0:
        return jnp.concatenate([rows[..., i * rs:(i + 1) * rs, :] for i in range(cs // PACK_C)], axis=-1)
    if cs > PACK_C // 2:
        return rows[..., :cs]
    return rows.reshape(*lead, rs, cs)


def pack_local(shards, names):
    flat = [_to_rows(shards[n]) for n in names]
    rows = sum(f.shape[0] for f in flat)
    pad = (-rows) % 128
    return jnp.concatenate(flat + [jnp.zeros((pad, PACK_C), flat[0].dtype)], axis=0)


def unpack_group(gathered, names, shard_shapes):
    _, n_rows, _ = gathered.shape
    by_block = gathered.reshape(2, 4, n_rows, PACK_C).transpose(1, 0, 2, 3).reshape(N_DEV, n_rows, PACK_C)
    full = {}
    r0 = 0
    for name in names:
        rs, cs = shard_shapes[name]
        nr = _packed_rows((rs, cs))
        piece = _from_rows(by_block[:, r0:r0 + nr, :], (rs, cs))
        r0 += nr
        if name in ROW_SHARDED:
            full[name] = piece.reshape(N_DEV * rs, cs)
        else:
            full[name] = piece.transpose(1, 0, 2).reshape(rs, N_DEV * cs)
    return full


def unpack_a(gathered, shard_shapes):
    full = unpack_group(gathered, GROUP_A, shard_shapes)
    w_in = full.pop("w_in")
    zeros = jnp.zeros((D_MODEL, LAT_W - OFF_FQ - HEADS), w_in.dtype)
    full["w_lat"] = jnp.concatenate([w_in[:, :OFF_FQ], w_in[:, OFF_FF:OFF_G], zeros], axis=1)
    full["w_fox"] = w_in[:, OFF_FQ:OFF_FF]
    full["w_gate"] = w_in[:, OFF_G:]
    w_uq = full.pop("w_uq").reshape(Q_LORA, HEADS, NOPE + ROPE)
    full["w_uq"] = jnp.pad(w_uq, ((0, 0), (0, 0), (0, QK_PAD - NOPE - ROPE))).reshape(Q_LORA, HEADS * QK_PAD)
    w_ukv = full.pop("w_ukv").reshape(KV_LORA, HEADS, 2, NOPE)
    full["w_kv"] = jnp.concatenate([w_ukv[:, :, 0, :].reshape(KV_LORA, HEADS * NOPE),
                                    w_ukv[:, :, 1, :].reshape(KV_LORA, HEADS * HEAD_V)], axis=1)
    return full


def pack_small(vals, loss):
    flat = [vals[n].reshape(-1) for n in SMALL] + [loss.reshape(-1)]
    used = sum(f.shape[0] for f in flat)
    flat.append(jnp.zeros((SMALL_ROWS * PACK_C - used,), F32))
    return jnp.concatenate(flat).reshape(SMALL_ROWS, PACK_C)


def unpack_small(packed):
    flat = packed.reshape(-1)
    out, off = {}, 0
    for n in SMALL:
        out[n] = flat[off:off + SMALL_N[n]]
        off += SMALL_N[n]
    return out, flat[off]


def rope_tables(s_len):
    pos = jnp.arange(s_len, dtype=F32)
    inv = 1.0 / (ROPE_THETA ** (jnp.arange(0, ROPE, 2, dtype=F32) / ROPE))
    ang = pos[:, None] * inv[None, :]
    cos, sin = jnp.cos(ang), jnp.sin(ang)
    zero = jnp.zeros_like(cos)
    c = jnp.concatenate([cos, cos, zero, zero], axis=1)
    s1 = jnp.concatenate([-sin, zero, zero, zero], axis=1)
    s2 = jnp.concatenate([zero, sin, zero, zero], axis=1)
    return (c, s1, s2), (c, -s1, -s2)


def reduce_scatter_tail(parts, from_sibling, from_chips_fn, names):
    cx, cy, cc = _me()
    n_rows = parts.shape[1]
    mine4 = lax.dynamic_slice_in_dim(parts, cc * 4, 4, axis=0)
    pair = add_pairs("rs_pair_sum_" + names, mine4.reshape(4 * n_rows, PACK_C), from_sibling.reshape(4 * n_rows, PACK_C))
    from_chips, extra = from_chips_fn(pair.reshape(4, n_rows, PACK_C))
    own = cx * 2 + cy
    total = add_final("rs_final_sum_" + names, lax.dynamic_index_in_dim(mine4, own, 0, keepdims=False),
                      lax.dynamic_index_in_dim(from_sibling, own, 0, keepdims=False),
                      from_chips[0], from_chips[1], from_chips[2])
    return total, extra


def local_step(x, target, w, small, packed_b, shard_shapes, to_packed_a):
    s_len = x.shape[0]
    tabs, inv_tabs = rope_tables(s_len)
    g_attn = small["attn_norm"].reshape(1, D_MODEL)
    g_q = small["q_norm"].reshape(1, Q_LORA)
    g_kv = small["kv_norm"].reshape(1, KV_LORA)
    g_mlp = small["mlp_norm"].reshape(1, D_MODEL)
    g_final = small["final_norm"].reshape(1, D_MODEL)
    f_bias = small["fox_f_bias"].reshape(HEADS, 1)
    mla_scale = 1.0 / math.sqrt(NOPE + ROPE)
    fox_scale = 1.0 / math.sqrt(HEAD_V)
    mla_offs = (0, 0, HEADS)
    fox_offs = (0, HEADS, 2 * HEADS)

    xn = rms_fwd("rms_attn", x, g_attn)
    lat, = matmul("proj_lat", xn, w["w_lat"], "nn", [F32])
    fox, = matmul("proj_fox", xn, w["w_fox"], "nn", [BF16])
    graw, = matmul("proj_gate", xn, w["w_gate"], "nn", [F32])
    cq = rms_fwd("rms_q", (lat, Q_LORA, 0), g_q)
    ckv = rms_fwd("rms_kv", (lat, KV_LORA, Q_LORA // KV_LORA), g_kv)
    qraw, = matmul("up_q", cq, w["w_uq"], "nn", [F32])
    q = rope_heads("rope_q", qraw, tabs, BF16)
    kvn, = matmul("up_kv", ckv, w["w_kv"], "nn", [BF16])
    kr = rope_block("rope_k", lat, OFF_KR // 128, tabs, BF16)
    k = k_assemble("k_assemble", kvn, kr)
    o_mla, lse_mla, gathered_b = attn_fwd("mla_fwd", q, k, kvn, mla_offs, QK_PAD, CHUNK, mla_scale,
                                          comm=hook_gather_first(packed_b))
    ff_t = lat[:, OFF_FQ:OFF_FQ + HEADS].T
    cum = fox_cum_fwd("fox_cum", ff_t, f_bias).reshape(HEADS, s_len, 1) * LOG2E
    o_fox, lse_fox, gathered_b = attn_fwd("fox_fwd", fox, fox, fox, fox_offs, HEAD_V, 1, fox_scale, cum=cum,
                                          comm=hook_gather_second(gathered_b), split_p=True)
    unpack_b = functools.partial(unpack_group, names=GROUP_B, shard_shapes=shard_shapes)
    w = {**w, **unpack_b(gathered_b)}
    y_mla, = matmul("branch_mla", o_mla, w["w_mla_branch"], "nn", [F32])
    y_fox, = matmul("branch_fox", o_fox, w["w_fox_branch"], "nn", [F32])
    mix = gate_mix("gate_mix", graw, y_mla, y_fox)
    h1, = matmul("out_proj", mix, w["w_out"], "nn", [F32], epilogue=lambda acc, res: (res + acc,), extras=[x])
    hn = rms_fwd("rms_mlp", h1, g_mlp)

    def relu2(acc):
        r = jnp.maximum(acc, 0.0)
        return r * r, r
    u, relu_up = matmul("mlp_up", hn, w["w_up"], "nn", [BF16, BF16], epilogue=relu2)
    h2, = matmul("mlp_down", u, w["w_down"], "nn", [F32], epilogue=lambda acc, res: (res + acc,), extras=[h1])
    dh2, d_final, loss = loss_head("loss_head", h2, target, g_final)

    grads = {}
    dup, = matmul("d_mlp_down", dh2, w["w_down"], "nt", [BF16],
                  epilogue=lambda acc, r: (acc * (2.0 * r.astype(F32)),), extras=[relu_up])
    grads["w_down"], = matmul("gw_down", u, dh2, "tn", [BF16], **TN_TILES)
    dhn, = matmul("d_mlp_up", dup, w["w_up"], "nt", [F32])
    grads["w_up"], = matmul("gw_up", hn, dup, "tn", [BF16], **TN_TILES)
    dh1, d_mlp = rms_bwd("rms_mlp_bwd", h1, dhn, g_mlp, dres=dh2)
    dmix, = matmul("d_out_proj", dh1, w["w_out"], "nt", [F32])
    grads["w_out"], = matmul("gw_out", mix, dh1, "tn", [BF16], **TN_TILES)
    dproj, dy_mla, dy_fox = gate_mix_bwd("gate_mix_bwd", graw, y_mla, y_fox, dmix)
    do_mla, = matmul("d_branch_mla", dy_mla, w["w_mla_branch"], "nt", [BF16])
    grads["w_mla_branch"], = matmul("gw_branch_mla", o_mla, dy_mla, "tn", [BF16], **TN_TILES)
    do_fox, = matmul("d_branch_fox", dy_fox, w["w_fox_branch"], "nt", [BF16])
    grads["w_fox_branch"], = matmul("gw_branch_fox", o_fox, dy_fox, "tn", [BF16], **TN_TILES)

    to_packed_b = jax.linear_transpose(unpack_b, jax.ShapeDtypeStruct(gathered_b.shape, BF16))
    parts_b, = to_packed_b({n: grads.pop(n) for n in GROUP_B})
    delta_mla = attn_delta("mla_delta", do_mla, o_mla)
    dk, dv, dq, from_sibling = attn_bwd("mla_bwd", q, k, kvn, do_mla, lse_mla, delta_mla, mla_offs, QK_PAD, CHUNK,
                                        mla_scale, F32, comm=hook_swap_sibling(parts_b))

    delta_fox = attn_delta("fox_delta", do_fox, o_fox)

    def chips_behind_fox_bwd(pair):
        dfk, dfv, dfq, dcum, from_chips = attn_bwd("fox_bwd", fox, fox, fox, do_fox, lse_fox, delta_fox, fox_offs,
                                                   HEAD_V, 1, fox_scale, BF16, cum=cum, comm=hook_swap_chips(pair))
        return from_chips, (dfq.astype(BF16), dfk, dfv, dcum)
    g_packed_b, (dfq, dfk, dfv, dcum) = reduce_scatter_tail(parts_b, from_sibling, chips_behind_fox_bwd, "b")
    dff_t, d_bias = fox_cum_bwd("fox_cum_bwd", ff_t, f_bias, dcum.reshape(HEADS, s_len))
    dq_r = rope_heads("rope_q_bwd", dq, inv_tabs, BF16)
    dkvn, dkr = dk_split("dk_split", dk, dv, inv_tabs)
    dcq, = matmul("d_up_q", dq_r, w["w_uq"], "nt", [F32])
    grads["w_uq"], = matmul("gw_uq", cq, dq_r, "tn", [BF16], tm=512, tn=2048, tk=2048)
    dckv, = matmul("d_up_kv", dkvn, w["w_kv"], "nt", [F32])
    grads["w_kv"], = matmul("gw_kv", ckv, dkvn, "tn", [BF16], tm=256, tn=2048, tk=2048)
    dcq_raw, d_qn = rms_bwd("rms_q_bwd", (lat, Q_LORA, 0), dcq, g_q, out_dtype=BF16)
    dckv_raw, d_kvn = rms_bwd("rms_kv_bwd", (lat, KV_LORA, Q_LORA // KV_LORA), dckv, g_kv, out_dtype=BF16)

    pad = jnp.zeros((s_len, LAT_W - OFF_FQ - HEADS), BF16)
    tail = jnp.concatenate([dkr[:, :ROPE].astype(BF16), dff_t.T.astype(BF16), pad], axis=1)
    col = GATE_W
    for piece in (dfq, dfk, dfv, dcq_raw, dckv_raw, tail):
        dproj = lax.dynamic_update_slice(dproj, piece, (0, col))
        col += piece.shape[1]
    w_in_p = jnp.concatenate([w["w_gate"], w["w_fox"], w["w_lat"]], axis=1)
    gw_in, = matmul("gw_in", xn, dproj, "tn", [BF16], tm=1024, tn=1152, tk=2048)
    grads["w_gate"], grads["w_fox"], grads["w_lat"] = (gw_in[:, :GATE_W], gw_in[:, GATE_W:GATE_W + FOX_W],
                                                      gw_in[:, GATE_W + FOX_W:])

    parts_a = to_packed_a(grads)
    from_sibling_a = comm_swap_sibling("comm_rs_sibling_a", parts_a)

    def chips_behind_d_proj(pair):
        dxn, from_chips = matmul("d_proj", dproj, w_in_p, "nt", [F32], tk=2688, comm=hook_swap_chips(pair))
        return from_chips, dxn
    g_packed_a, dxn = reduce_scatter_tail(parts_a, from_sibling_a, chips_behind_d_proj, "a")
    dx, d_attn = rms_bwd("rms_attn_bwd", x, dxn, g_attn, dres=dh1)

    small_grads = {"attn_norm": d_attn, "fox_f_bias": d_bias, "q_norm": d_qn, "kv_norm": d_kvn,
                   "mlp_norm": d_mlp, "final_norm": d_final}
    return loss, dx, g_packed_a, g_packed_b, small_grads


def kernel(x, attn_norm, w_in, fox_f_bias, q_norm, w_uq, kv_norm, w_ukv, w_mla_branch, w_fox_branch, w_out, mlp_norm, w_up, w_down, final_norm, loss_target, m_attn_norm, m_w_in, m_fox_f_bias, m_q_norm, m_w_uq, m_kv_norm, m_w_ukv, m_w_mla_branch, m_w_fox_branch, m_w_out, m_mlp_norm, m_w_up, m_w_down, m_final_norm, v_attn_norm, v_w_in, v_fox_f_bias, v_q_norm, v_w_uq, v_kv_norm, v_w_ukv, v_w_mla_branch, v_w_fox_branch, v_w_out, v_mlp_norm, v_w_up, v_w_down, v_final_norm):
    given = dict(locals())
    big = {n: given[n][0] for n in BIG}
    small = {n: given[n] for n in SMALL}
    shard_shapes = {n: tuple(big[n].shape) for n in BIG}

    packed_a = pack_local({n: big[n].astype(BF16) for n in GROUP_A}, GROUP_A)
    packed_b = pack_local({n: big[n].astype(BF16) for n in GROUP_B}, GROUP_B)
    gathered_a = comm_allgather("comm_allgather_a", packed_a)
    unpack = functools.partial(unpack_a, shard_shapes=shard_shapes)
    w_a = unpack(gathered_a)

    transpose_a = jax.linear_transpose(unpack, jax.ShapeDtypeStruct(gathered_a.shape, BF16))
    loss_part, dx, g_packed_a, g_packed_b, small_grads = local_step(
        x[0], loss_target[0], w_a, small, packed_b, shard_shapes, lambda grads: transpose_a(grads)[0])

    small_sum = comm_allreduce_small("comm_allreduce_small", pack_small(small_grads, loss_part[0, 0]))
    g_small, loss = unpack_small(small_sum)

    grad_w, delta_w, new_m, new_v = {}, {}, {}, {}
    for names, g_packed in ((GROUP_A, g_packed_a), (GROUP_B, g_packed_b)):
        r0 = 0
        for n in names:
            rs, cs = shard_shapes[n]
            nr = _packed_rows((rs, cs))
            g = _from_rows(g_packed[r0:r0 + nr], (rs, cs))
            r0 += nr
            d, m_new, v_new = adamw("adamw_" + n, big[n], g, given["m_" + n][0], given["v_" + n][0])
            grad_w[n], delta_w[n], new_m[n], new_v[n] = g[None], d[None], m_new[None], v_new[None]
    zero = jnp.zeros((), F32)
    d_s, m_s, v_s = adamw("adamw_small", pack_small(small, zero), small_sum * _small_mask(),
                          pack_small({n: given["m_" + n] for n in SMALL}, zero),
                          pack_small({n: given["v_" + n] for n in SMALL}, zero), tr=SMALL_ROWS)
    d_small, _ = unpack_small(d_s)
    m_small, _ = unpack_small(m_s)
    v_small, _ = unpack_small(v_s)
    for n in SMALL:
        shape = given[n].shape
        grad_w[n], delta_w[n] = g_small[n].reshape(shape), d_small[n].reshape(shape)
        new_m[n], new_v[n] = m_small[n].reshape(shape), v_small[n].reshape(shape)

    order = ["attn_norm", "w_in", "fox_f_bias", "q_norm", "w_uq", "kv_norm", "w_ukv", "w_mla_branch", "w_fox_branch",
             "w_out", "mlp_norm", "w_up", "w_down", "final_norm"]
    return (loss, dx[None], *[grad_w[n] for n in order], *[delta_w[n] for n in order],
            *[new_m[n] for n in order], *[new_v[n] for n in order])


def _small_mask():
    used = sum(SMALL_N[n] for n in SMALL)
    return (jnp.arange(SMALL_ROWS * PACK_C) < used).astype(F32).reshape(SMALL_ROWS, PACK_C)
```

```python
import functools
import math

import jax
import jax.numpy as jnp
from jax import lax
from jax.experimental import pallas as pl
from jax.experimental.pallas import tpu as pltpu

F32 = jnp.float32
BF16 = jnp.bfloat16
MESH = pl.DeviceIdType.MESH

D_MODEL = 2048
HEADS = 8
Q_LORA = 512
KV_LORA = 256
NOPE = 128
ROPE = 64
HEAD_V = 128
D_FF = 4 * D_MODEL
CHUNK = 64
EPS = 1e-6
ROPE_THETA = 10000.0
OFF_KR = Q_LORA + KV_LORA
OFF_FQ = OFF_KR + ROPE
OFF_FF = OFF_FQ + 3 * HEADS * HEAD_V
OFF_G = OFF_FF + HEADS
D_IN = OFF_G + 2 * D_MODEL

LAT_W = 896
FOX_W = 3 * HEADS * HEAD_V
GATE_W = 2 * D_MODEL
PROJ_W = LAT_W + FOX_W + GATE_W
QK_PAD = 256

ADAM_LR = 0.001
ADAM_B1 = 0.9
ADAM_B2 = 0.999
ADAM_EPS = 1e-08
ADAM_WD = 0.01
ADAM_STEP = 10

N_DEV = 8
PACK_C = 1024
NEG = -1e30
LOG2E = math.log2(math.e)

VMEM_LIMIT = 56 * 1024 * 1024

BIG = ("w_in", "w_uq", "w_ukv", "w_mla_branch", "w_fox_branch", "w_out", "w_up", "w_down")
GROUP_A = ("w_in", "w_uq", "w_ukv")
GROUP_B = ("w_mla_branch", "w_fox_branch", "w_out", "w_up", "w_down")
ROW_SHARDED = ("w_out", "w_down")
SMALL = ("attn_norm", "fox_f_bias", "q_norm", "kv_norm", "mlp_norm", "final_norm")
SMALL_N = {"attn_norm": D_MODEL, "fox_f_bias": HEADS, "q_norm": Q_LORA, "kv_norm": KV_LORA,
           "mlp_norm": D_MODEL, "final_norm": D_MODEL}
SMALL_ROWS = 8


def _params(sem):
    return pltpu.CompilerParams(dimension_semantics=sem, vmem_limit_bytes=VMEM_LIMIT)


def _rows(name, fn, row_ins, const_ins, outs, reds=(), tr=256):
    norm = [(a, a.shape[1], 0) if not isinstance(a, tuple) else a for a in row_ins]
    n_rows = norm[0][0].shape[0]
    tr = min(tr, n_rows)
    assert n_rows % tr == 0, (name, n_rows, tr)
    n_in, n_out, n_red = len(norm) + len(const_ins), len(outs), len(reds)

    def body(*refs):
        vals = [r[...] for r in refs[:n_in]]
        out_refs = refs[n_in:n_in + n_out]
        red_refs = refs[n_in + n_out:]
        out_vals, red_vals = fn(*vals)
        for r, v in zip(out_refs, out_vals):
            r[...] = v.astype(r.dtype)
        if n_red:
            @pl.when(pl.program_id(0) == 0)
            def _():
                for r in red_refs:
                    r[...] = jnp.zeros_like(r)
            for r, v in zip(red_refs, red_vals):
                r[...] += v

    in_specs = [pl.BlockSpec((tr, w), functools.partial(lambda i, cb: (i, cb), cb=cb)) for _, w, cb in norm]
    in_specs += [pl.BlockSpec(a.shape, lambda i: (0, 0)) for a in const_ins]
    out_specs = [pl.BlockSpec((tr, o[0]), lambda i: (i, 0)) for o in outs]
    out_specs += [pl.BlockSpec((1, c), lambda i: (0, 0)) for c in reds]
    out_shape = [jax.ShapeDtypeStruct((n_rows, o[2] if len(o) > 2 else o[0]), o[1]) for o in outs]
    out_shape += [jax.ShapeDtypeStruct((1, c), F32) for c in reds]
    res = pl.pallas_call(
        body, name=name, grid=(n_rows // tr,), in_specs=in_specs, out_specs=out_specs, out_shape=out_shape,
        compiler_params=_params(("arbitrary",)),
    )(*[a for a, _, _ in norm], *const_ins)
    return res


def _rstd(x):
    return lax.rsqrt(jnp.mean(x * x, axis=-1, keepdims=True) + EPS)


def rms_fwd(name, x, gain, tr=256):
    width = x[1] if isinstance(x, tuple) else x.shape[1]

    def fn(xv, g):
        return (xv * _rstd(xv) * g,), ()
    return _rows(name, fn, [x], [gain], [(width, BF16)], tr=tr)[0]


def rms_bwd(name, x, dy, gain, dres=None, out_dtype=F32, tr=256):
    width = x[1] if isinstance(x, tuple) else x.shape[1]

    def fn(xv, dyv, *rest):
        g = rest[-1]
        r = _rstd(xv)
        n = xv * r
        dyv = dyv.astype(F32)
        dn = dyv * g
        dx = r * (dn - n * jnp.mean(dn * n, axis=-1, keepdims=True))
        if dres is not None:
            dx = dx + rest[0]
        return (dx,), (jnp.sum(dyv * n, axis=0, keepdims=True),)

    ins = [x, dy] + ([dres] if dres is not None else [])
    return _rows(name, fn, ins, [gain], [(width, out_dtype)], [width], tr=tr)


def _rope_lanes(t, c, s1, s2):
    return t * c + pltpu.roll(t, 96, 1) * s1 + pltpu.roll(t, 32, 1) * s2


def rope_heads(name, x, tabs, out_dtype):
    def fn(xv, c, s1, s2):
        xv = xv.astype(F32)
        parts = []
        for h in range(HEADS):
            parts.append(xv[:, h * QK_PAD:h * QK_PAD + NOPE])
            parts.append(_rope_lanes(xv[:, h * QK_PAD + NOPE:(h + 1) * QK_PAD], c, s1, s2))
        return (jnp.concatenate(parts, axis=1),), ()
    return _rows(name, fn, [x, *tabs], [], [(HEADS * QK_PAD, out_dtype)])[0]


def rope_block(name, x, col_block, tabs, out_dtype):
    def fn(xv, c, s1, s2):
        return (_rope_lanes(xv.astype(F32), c, s1, s2),), ()
    return _rows(name, fn, [(x, 128, col_block), *tabs], [], [(128, out_dtype)])[0]


def k_assemble(name, kvn, kr):
    def fn(knp, krv):
        parts = []
        for h in range(HEADS):
            parts.append(knp[:, h * NOPE:(h + 1) * NOPE])
            parts.append(krv)
        return (jnp.concatenate(parts, axis=1),), ()
    return _rows(name, fn, [(kvn, HEADS * NOPE, 0), kr], [], [(HEADS * QK_PAD, BF16)])[0]


def dk_split(name, dk, dv, inv_tabs):
    def fn(dkv, dvv, c, s1, s2):
        parts = []
        acc = None
        for h in range(HEADS):
            parts.append(dkv[:, h * QK_PAD:h * QK_PAD + NOPE].astype(BF16))
            t = dkv[:, h * QK_PAD + NOPE:(h + 1) * QK_PAD]
            acc = t if acc is None else acc + t
        parts.append(dvv)
        return (jnp.concatenate(parts, axis=1), _rope_lanes(acc, c, s1, s2)), ()
    return _rows(name, fn, [dk, dv, *inv_tabs], [], [(2 * HEADS * NOPE, BF16), (128, F32)])


def gate_mix(name, graw, y_mla, y_fox):
    def fn(g, ya, yb):
        ga = jax.nn.sigmoid(g[:, :D_MODEL])
        gb = jax.nn.sigmoid(g[:, D_MODEL:])
        return (ga * ya + gb * yb,), ()
    return _rows(name, fn, [graw, y_mla, y_fox], [], [(D_MODEL, BF16)])[0]


def gate_mix_bwd(name, graw, y_mla, y_fox, dmix):
    def fn(g, ya, yb, dm):
        ga = jax.nn.sigmoid(g[:, :D_MODEL])
        gb = jax.nn.sigmoid(g[:, D_MODEL:])
        dgraw = jnp.concatenate([dm * ya * ga * (1.0 - ga), dm * yb * gb * (1.0 - gb)], axis=1)
        return (dgraw, dm * ga, dm * gb), ()
    return _rows(name, fn, [graw, y_mla, y_fox, dmix], [],
                 [(GATE_W, BF16, PROJ_W), (D_MODEL, BF16), (D_MODEL, BF16)], tr=256)


def loss_head(name, h2, target, gain):
    inv_d = 1.0 / D_MODEL

    def fn(h, t, g):
        r = _rstd(h)
        n = h * r
        err = n * g - t
        dy = err * inv_d
        dn = dy * g
        dh = r * (dn - n * jnp.mean(dn * n, axis=-1, keepdims=True))
        part = 0.5 * inv_d * jnp.sum(jnp.sum(err * err, axis=1, keepdims=True), axis=0, keepdims=True)
        return (dh,), (jnp.sum(dy * n, axis=0, keepdims=True), jnp.broadcast_to(part, (1, 128)))
    return _rows(name, fn, [h2, target], [gain], [(D_MODEL, F32)], [D_MODEL, 128])


def adamw(name, w, g, m, v, tr=256):
    c1 = 1.0 - ADAM_B1 ** ADAM_STEP
    c2 = 1.0 - ADAM_B2 ** ADAM_STEP

    def fn(wv, gv, mv, vv):
        m_new = ADAM_B1 * mv + (1.0 - ADAM_B1) * gv
        v_new = ADAM_B2 * vv + (1.0 - ADAM_B2) * (gv * gv)
        delta = -ADAM_LR * ((m_new / c1) / (jnp.sqrt(v_new / c2) + ADAM_EPS) + ADAM_WD * wv)
        return (delta, m_new, v_new), ()
    cols = w.shape[1]
    return _rows(name, fn, [w, g, m, v], [], [(cols, F32)] * 3, tr=tr)


def _row_tile(n_rows, cap=640):
    return max(t for t in range(16, cap + 1, 16) if n_rows % t == 0)


def add_pairs(name, a, b):
    def fn(av, bv):
        return (av.astype(F32) + bv.astype(F32),), ()
    return _rows(name, fn, [a, b], [], [(a.shape[1], BF16)], tr=_row_tile(a.shape[0]))[0]


def add_final(name, a, b, r0, r1, r2):
    def fn(av, bv, r0v, r1v, r2v):
        return (((av.astype(F32) + bv.astype(F32)) + r0v.astype(F32)) + r1v.astype(F32) + r2v.astype(F32),), ()
    return _rows(name, fn, [a, b, r0, r1, r2], [], [(a.shape[1], F32)], tr=_row_tile(a.shape[0]))[0]


TN_TILES = dict(tm=1024, tn=1024, tk=2048)
_DIMS = {"nn": (((1,), (0,)), ((), ())), "nt": (((1,), (1,)), ((), ())), "tn": (((0,), (0,)), ((), ()))}


def matmul(name, a, b, mode, outs, epilogue=None, extras=(), tm=1024, tn=1024, tk=2048, comm=None, b_cols=None):
    if mode == "tn":
        kdim, m = a.shape
    else:
        m, kdim = a.shape
    n = b.shape[0] if mode == "nt" else b.shape[1]
    if b_cols is not None:
        assert mode != "nt"
        col0, n = b_cols
        tn = min(tn, n)
        assert col0 % tn == 0, (name, b_cols, tn)
    j0 = 0 if b_cols is None else col0 // tn
    tm, tn, tk = min(tm, m), min(tn, n), min(tk, kdim)
    assert m % tm == 0 and n % tn == 0 and kdim % tk == 0, (name, a.shape, b.shape)
    nk = kdim // tk
    n_ex, n_out = len(extras), len(outs)
    dims = _DIMS[mode]
    grid = (m // tm, n // tn, nk)

    def body(*refs):
        ins, c_in, out_refs, c_out, scr, c_sems = _split_refs(refs, 2 + n_ex, comm, n_out, 1 if nk > 1 else 0)
        a_ref, b_ref = ins[:2]
        ex_refs = ins[2:]
        _comm_start(comm, c_in, c_out, c_sems, *grid)

        def finish(acc):
            vals = (acc,) if epilogue is None else epilogue(acc, *[r[...] for r in ex_refs])
            for r, v in zip(out_refs, vals):
                r[...] = v.astype(r.dtype)

        part = lax.dot_general(a_ref[...].astype(BF16), b_ref[...].astype(BF16), dims, preferred_element_type=F32)
        if nk == 1:
            finish(part)
        else:
            acc_ref = scr[0]
            k = pl.program_id(2)

            @pl.when(k == 0)
            def _():
                acc_ref[...] = part

            @pl.when(k > 0)
            def _():
                acc_ref[...] += part

            @pl.when(k == nk - 1)
            def _():
                finish(acc_ref[...])
        _comm_wait(comm, c_in, c_out, c_sems, *grid)

    a_spec = pl.BlockSpec((tk, tm), lambda i, j, k: (k, i)) if mode == "tn" else pl.BlockSpec((tm, tk), lambda i, j, k: (i, k))
    b_spec = (pl.BlockSpec((tn, tk), lambda i, j, k: (j, k)) if mode == "nt"
              else pl.BlockSpec((tk, tn), lambda i, j, k: (k, j + j0)))
    tile = pl.BlockSpec((tm, tn), lambda i, j, k: (i, j))
    c_ins, c_outs, c_shapes, c_scratch, aliases = _with_comm(comm, 2 + n_ex, n_out)
    sem = ("arbitrary",) * 3 if comm else ("parallel", "parallel", "arbitrary")
    res = pl.pallas_call(
        body, name=name, grid=grid,
        in_specs=[a_spec, b_spec] + [tile] * n_ex + c_ins,
        out_specs=[tile] * n_out + c_outs,
        out_shape=[jax.ShapeDtypeStruct((m, n), dt) for dt in outs] + c_shapes,
        scratch_shapes=([pltpu.VMEM((tm, tn), F32)] if nk > 1 else []) + c_scratch,
        input_output_aliases=aliases,
        compiler_params=_params(sem),
    )(a, b, *extras, *(comm.args if comm else []))
    return res


_NT = (((1,), (1,)), ((), ()))
_NN = (((1,), (0,)), ((), ()))


def _mask(bq, chunk, transposed, row0=0, shape=None):
    shape = (bq, bq) if shape is None else shape
    row = lax.broadcasted_iota(jnp.int32, shape, 0) + row0
    col = lax.broadcasted_iota(jnp.int32, shape, 1)
    if chunk > 1:
        row, col = row // chunk, col // chunk
    return (row <= col) if transposed else (col <= row)


def _row_layout(a, bq):
    h, s, _ = a.shape
    return a.reshape(h, s // bq, 1, bq)


def _split_refs(refs, n_in, comm, n_out, n_scr):
    n_ci = len(comm.args) if comm else 0
    n_co = len(comm.out_shapes) if comm else 0
    cuts = [n_in, n_ci, n_out, n_co, n_scr, 3 if comm else 0]
    parts, at = [], 0
    for n in cuts:
        parts.append(list(refs[at:at + n]))
        at += n
    assert at == len(refs), (at, len(refs))
    return parts


def _at_step(grid, last):
    hit = None
    for axis, n in enumerate(grid):
        here = pl.program_id(axis) == (n - 1 if last else 0)
        hit = here if hit is None else jnp.logical_and(hit, here)
    return hit


def _comm_start(comm, c_in, c_out, c_sems, *grid):
    if comm is not None:
        @pl.when(_at_step(grid, False))
        def _():
            comm.start(c_in, c_out, c_sems)


def _comm_wait(comm, c_in, c_out, c_sems, *grid):
    if comm is not None:
        @pl.when(_at_step(grid, True))
        def _():
            comm.wait(c_in, c_out, c_sems)


def attn_fwd(name, q, k, v, offs, dqk, chunk, scale, cum=None, bq=512, comm=None, split_p=False):
    s_len = q.shape[0]
    nq = s_len // bq
    qoff, koff, voff = offs
    has_bias = cum is not None
    scale2 = scale * LOG2E

    n_in = 5 if has_bias else 3

    def body(*refs):
        ins, c_in, outs, c_out, scr, c_sems = _split_refs(refs, n_in, comm, 2, 3)
        q_ref, k_ref, v_ref = ins[:3]
        if has_bias:
            cc_ref, cr_ref = ins[3:]
        o_ref, lse_ref = outs
        m_s, l_s, acc_s = scr
        _comm_start(comm, c_in, c_out, c_sems, HEADS, nq)
        i = pl.program_id(1)
        qv = q_ref[...]
        m_s[...] = jnp.full_like(m_s, NEG)
        l_s[...] = jnp.zeros_like(l_s)
        acc_s[...] = jnp.zeros_like(acc_s)

        def step(j, masked):
            off = pl.multiple_of(j * bq, bq)
            kj = k_ref[pl.ds(off, bq), :]
            vj = v_ref[pl.ds(off, bq), :]
            st = lax.dot_general(kj, qv, _NT, preferred_element_type=F32) * scale2
            if has_bias:
                st = st + cr_ref[...] - cc_ref[pl.ds(off, bq), :]
            if masked:
                st = jnp.where(_mask(bq, chunk, True), st, NEG)
            m_prev = m_s[...]
            m_new = jnp.maximum(m_prev, jnp.max(st, axis=0, keepdims=True))
            alpha = jnp.exp2(m_prev - m_new)
            pt = jnp.exp2(st - m_new)
            l_s[...] = alpha * l_s[...] + jnp.sum(pt, axis=0, keepdims=True)
            p_hi = pt.astype(BF16)
            pv = lax.dot_general(vj, p_hi, _DIMS["tn"], preferred_element_type=F32)
            if split_p:
                p_lo = (pt - p_hi.astype(F32)).astype(BF16)
                pv = pv + lax.dot_general(vj, p_lo, _DIMS["tn"], preferred_element_type=F32)
            acc_s[...] = alpha * acc_s[...] + pv
            m_s[...] = m_new

        def pair_body(jj, carry):
            step(2 * jj, False)
            step(2 * jj + 1, False)
            return carry

        lax.fori_loop(0, i // 2, pair_body, 0)

        @pl.when(i % 2 == 1)
        def _():
            step(i - 1, False)

        step(i, True)
        o_ref[...] = (acc_s[...] / l_s[...]).T.astype(o_ref.dtype)
        lse_ref[...] = m_s[...] + jnp.log2(l_s[...])
        _comm_wait(comm, c_in, c_out, c_sems, HEADS, nq)

    in_specs = [
        pl.BlockSpec((bq, dqk), lambda h, i: (i, qoff + h)),
        pl.BlockSpec((s_len, dqk), lambda h, i: (0, koff + h)),
        pl.BlockSpec((s_len, HEAD_V), lambda h, i: (0, voff + h)),
    ]
    args = [q, k, v]
    if has_bias:
        in_specs += [pl.BlockSpec((None, s_len, 1), lambda h, i: (h, 0, 0)),
                     pl.BlockSpec((None, None, 1, bq), lambda h, i: (h, i, 0, 0))]
        args += [cum, _row_layout(cum, bq)]
    c_ins, c_outs, c_shapes, c_scratch, aliases = _with_comm(comm, len(args), 2)
    o, lse_rows, *comm_out = pl.pallas_call(
        body, name=name, grid=(HEADS, nq), in_specs=in_specs + c_ins,
        out_specs=[pl.BlockSpec((bq, HEAD_V), lambda h, i: (i, h)),
                   pl.BlockSpec((None, None, 1, bq), lambda h, i: (h, i, 0, 0))] + c_outs,
        out_shape=[jax.ShapeDtypeStruct((s_len, HEADS * HEAD_V), F32),
                   jax.ShapeDtypeStruct((HEADS, nq, 1, bq), F32)] + c_shapes,
        scratch_shapes=[pltpu.VMEM((1, bq), F32), pltpu.VMEM((1, bq), F32), pltpu.VMEM((HEAD_V, bq), F32)] + c_scratch,
        input_output_aliases=aliases,
        compiler_params=_params(("arbitrary", "arbitrary")),
    )(*args, *(comm.args if comm else []))
    return (o, lse_rows, *comm_out)


def _bwd_block(kv, vv, qi, doi, lse_row, scale2, bias, masked, bq, chunk):
    st = lax.dot_general(kv, qi, _NT, preferred_element_type=F32) * scale2
    if bias is not None:
        st = st + bias[0] - bias[1]
    if masked:
        st = jnp.where(_mask(bq, chunk, True), st, NEG)
    pt = jnp.exp2(st - lse_row)
    dpt = lax.dot_general(vv, doi, _NT, preferred_element_type=F32)
    return pt, dpt


def attn_delta(name, do, o, bq=512):
    s_len = do.shape[0]
    nq = s_len // bq

    def body(do_ref, o_ref, out_ref):
        ones = jnp.ones((8, HEAD_V), BF16)
        for h in range(HEADS):
            cols = slice(h * HEAD_V, (h + 1) * HEAD_V)
            prod = do_ref[:, cols].astype(F32) * o_ref[:, cols].astype(F32)
            total = None
            for part in reversed(_split3(prod)):
                term = lax.dot_general(ones, part, _NT, preferred_element_type=F32)
                total = term if total is None else total + term
            out_ref[h] = total[0:1, :]

    blk = pl.BlockSpec((bq, HEADS * HEAD_V), lambda i: (i, 0))
    return pl.pallas_call(
        body, name=name, grid=(nq,), in_specs=[blk, blk],
        out_specs=pl.BlockSpec((HEADS, None, 1, bq), lambda i: (0, i, 0, 0)),
        out_shape=jax.ShapeDtypeStruct((HEADS, nq, 1, bq), F32),
        compiler_params=_params(("parallel",)),
    )(do, o)


def attn_bwd(name, q, k, v, do, lse, delta, offs, dqk, chunk, scale, out_dtype, cum=None, bq=512, comm=None):
    s_len = q.shape[0]
    nq = s_len // bq
    qoff, koff, voff = offs
    has_bias = cum is not None
    scale2 = scale * LOG2E
    n_in, n_out = (8, 4) if has_bias else (6, 3)

    def body(*refs):
        ins, c_in, outs, c_out, scr, c_sems = _split_refs(refs, n_in, comm, n_out, n_out - 1)
        k_ref, v_ref, q_ref, do_ref, lse_ref, delta_ref = ins[:6]
        dk_ref, dv_ref, dq_ref = outs[:3]
        dk_s, dv_s = scr[:2]
        if has_bias:
            cc_ref, cr_ref = ins[6:]
            dc_ref, dc_s = outs[3], scr[2]
        _comm_start(comm, c_in, c_out, c_sems, HEADS, nq)
        j = pl.program_id(1)
        kv = k_ref[...]
        vv = v_ref[...]
        dk_s[...] = jnp.zeros_like(dk_s)
        dv_s[...] = jnp.zeros_like(dv_s)
        if has_bias:
            dc_s[...] = jnp.zeros_like(dc_s)

        @pl.when(j == 0)
        def _():
            dq_ref[...] = jnp.zeros_like(dq_ref)

        def step(i, masked):
            off = pl.multiple_of(i * bq, bq)
            qi = q_ref[pl.ds(off, bq), :]
            doi = do_ref[pl.ds(off, bq), :].astype(BF16)
            bias = (cr_ref[i], cc_ref[...]) if has_bias else None
            pt, dpt = _bwd_block(kv, vv, qi, doi, lse_ref[i], scale2, bias, masked, bq, chunk)
            dv_s[...] += lax.dot_general(pt.astype(BF16), doi, _NN, preferred_element_type=F32)
            dst = pt * (dpt - delta_ref[i])
            if has_bias:
                dc_s[...] -= jnp.sum(dst, axis=-1, keepdims=True)
            dst = dst.astype(BF16)
            dk_s[...] += lax.dot_general(dst, qi, _NN, preferred_element_type=F32)
            dq_ref[pl.ds(off, bq), :] += lax.dot_general(dst, kv, _DIMS["tn"], preferred_element_type=F32) * scale

        step(j, True)

        def loop_body(i, carry):
            step(i, False)
            return carry

        lax.fori_loop(j + 1, nq, loop_body, 0)
        dk_ref[...] = (dk_s[...] * scale).astype(dk_ref.dtype)
        dv_ref[...] = dv_s[...].astype(dv_ref.dtype)
        if has_bias:
            dc_ref[...] = dc_s[...]
        _comm_wait(comm, c_in, c_out, c_sems, HEADS, nq)

    rows = pl.BlockSpec((None, nq, 1, bq), lambda h, j: (h, 0, 0, 0))
    in_specs = [
        pl.BlockSpec((bq, dqk), lambda h, j: (j, koff + h)),
        pl.BlockSpec((bq, HEAD_V), lambda h, j: (j, voff + h)),
        pl.BlockSpec((s_len, dqk), lambda h, j: (0, qoff + h)),
        pl.BlockSpec((s_len, HEAD_V), lambda h, j: (0, h)),
        rows,
        rows,
    ]
    args = [k, v, q, do, lse, delta]
    out_specs = [pl.BlockSpec((bq, dqk), lambda h, j: (j, h)), pl.BlockSpec((bq, HEAD_V), lambda h, j: (j, h)),
                 pl.BlockSpec((s_len, dqk), lambda h, j: (0, h))]
    out_shape = [jax.ShapeDtypeStruct((s_len, HEADS * dqk), out_dtype),
                 jax.ShapeDtypeStruct((s_len, HEADS * HEAD_V), BF16),
                 jax.ShapeDtypeStruct((s_len, HEADS * dqk), F32)]
    scratch = [pltpu.VMEM((bq, dqk), F32), pltpu.VMEM((bq, HEAD_V), F32)]
    if has_bias:
        in_specs += [pl.BlockSpec((None, bq, 1), lambda h, j: (h, j, 0)), rows]
        args += [cum, _row_layout(cum, bq)]
        out_specs.append(pl.BlockSpec((None, bq, 1), lambda h, j: (h, j, 0)))
        out_shape.append(jax.ShapeDtypeStruct((HEADS, s_len, 1), F32))
        scratch.append(pltpu.VMEM((bq, 1), F32))
    c_ins, c_outs, c_shapes, c_scratch, aliases = _with_comm(comm, len(args), n_out)
    return pl.pallas_call(
        body, name=name, grid=(HEADS, nq), in_specs=in_specs + c_ins, out_specs=out_specs + c_outs,
        out_shape=out_shape + c_shapes, scratch_shapes=scratch + c_scratch, input_output_aliases=aliases,
        compiler_params=_params(("arbitrary", "arbitrary")),
    )(*args, *(comm.args if comm else []))


_CUM_BLK = 512


def _split3(x):
    hi = x.astype(BF16)
    r1 = x - hi.astype(F32)
    mid = r1.astype(BF16)
    lo = (r1 - mid.astype(F32)).astype(BF16)
    return hi, mid, lo


def _tri_dot(x, tri):
    hi, mid, lo = _split3(x)
    out = lax.dot_general(lo, tri, _NN, preferred_element_type=F32)
    out = out + lax.dot_general(mid, tri, _NN, preferred_element_type=F32)
    return out + lax.dot_general(hi, tri, _NN, preferred_element_type=F32)


def fox_cum_fwd(name, ff_t, bias):
    s_len = ff_t.shape[1]
    nb = s_len // _CUM_BLK

    def body(ff_ref, b_ref, cum_ref):
        row = lax.broadcasted_iota(jnp.int32, (_CUM_BLK, _CUM_BLK), 0)
        col = lax.broadcasted_iota(jnp.int32, (_CUM_BLK, _CUM_BLK), 1)
        tri = (row <= col).astype(BF16)
        carry = jnp.zeros((HEADS, 1), F32)
        for b in range(nb):
            z = ff_ref[:, b * _CUM_BLK:(b + 1) * _CUM_BLK] + b_ref[...]
            logf = jnp.minimum(z, 0.0) - jnp.log1p(jnp.exp(-jnp.abs(z)))
            blk = _tri_dot(logf, tri) + carry
            cum_ref[:, b * _CUM_BLK:(b + 1) * _CUM_BLK] = blk
            carry = blk[:, _CUM_BLK - 1:_CUM_BLK]

    return pl.pallas_call(
        body, name=name, out_shape=jax.ShapeDtypeStruct((HEADS, s_len), F32),
        compiler_params=pltpu.CompilerParams(vmem_limit_bytes=VMEM_LIMIT),
    )(ff_t, bias)


def fox_cum_bwd(name, ff_t, bias, dcum):
    s_len = ff_t.shape[1]
    nb = s_len // _CUM_BLK

    def body(ff_ref, b_ref, dc_ref, dff_ref, db_ref):
        row = lax.broadcasted_iota(jnp.int32, (_CUM_BLK, _CUM_BLK), 0)
        col = lax.broadcasted_iota(jnp.int32, (_CUM_BLK, _CUM_BLK), 1)
        tri = (row >= col).astype(BF16)
        carry = jnp.zeros((HEADS, 1), F32)
        dbias = jnp.zeros((HEADS, 1), F32)
        for b in reversed(range(nb)):
            sl = slice(b * _CUM_BLK, (b + 1) * _CUM_BLK)
            dlogf = _tri_dot(dc_ref[:, sl], tri) + carry
            carry = dlogf[:, 0:1]
            z = ff_ref[:, sl] + b_ref[...]
            dz = dlogf / (1.0 + jnp.exp(z))
            dff_ref[:, sl] = dz
            dbias = dbias + jnp.sum(dz, axis=-1, keepdims=True)
        db_ref[...] = dbias

    return pl.pallas_call(
        body, name=name,
        out_shape=[jax.ShapeDtypeStruct((HEADS, s_len), F32), jax.ShapeDtypeStruct((HEADS, 1), F32)],
        compiler_params=pltpu.CompilerParams(vmem_limit_bytes=VMEM_LIMIT),
    )(ff_t, bias, dcum)


_ANY = pl.BlockSpec(memory_space=pl.ANY)


def _me():
    return lax.axis_index("x"), lax.axis_index("y"), lax.axis_index("c")


def comm_allgather(name, mine):
    n_rows, n_cols = mine.shape

    def body(x_ref, out_ref, send_sems, recv_sems, local_sem):
        x, y, c = _me()
        sibling = (x, y, 1 - c)
        chips = [(1 - x, y), (x, 1 - y), (1 - x, 1 - y)]

        def blk(px, py, pc):
            return out_ref.at[pc * 4 + px * 2 + py]

        def copy(k, block, to, src=None):
            return pltpu.make_async_remote_copy(
                src_ref=blk(*block) if src is None else src, dst_ref=blk(*block),
                send_sem=send_sems.at[k], recv_sem=recv_sems.at[k], device_id=to, device_id_type=MESH)

        own = pltpu.make_async_copy(x_ref, blk(x, y, c), local_sem)
        own.start()
        first = [copy(0, (x, y, c), sibling, src=x_ref)]
        first += [copy(1 + j, (x, y, c), (*chip, c), src=x_ref) for j, chip in enumerate(chips)]
        for cp in first:
            cp.start()
        passed = [copy(4 + j, (*chip, c), sibling) for j, chip in enumerate(chips)]
        for j, chip in enumerate(chips):
            copy(1 + j, (*chip, c), (x, y, c)).wait_recv()
            passed[j].start()
        copy(0, sibling, (x, y, c)).wait_recv()
        for j, chip in enumerate(chips):
            copy(4 + j, (*chip, 1 - c), (x, y, c)).wait_recv()
        for cp in first + passed:
            cp.wait_send()
        own.wait()

    return pl.pallas_call(
        body, name=name, out_shape=jax.ShapeDtypeStruct((N_DEV, n_rows, n_cols), mine.dtype),
        in_specs=[_ANY], out_specs=_ANY,
        scratch_shapes=[pltpu.SemaphoreType.DMA((7,)), pltpu.SemaphoreType.DMA((7,)), pltpu.SemaphoreType.DMA],
    )(mine)


def comm_swap_sibling(name, parts):
    _, n_rows, n_cols = parts.shape

    def body(p_ref, got_ref, send_sem, recv_sem):
        x, y, c = _me()
        cp = pltpu.make_async_remote_copy(
            src_ref=p_ref.at[pl.ds((1 - c) * 4, 4)], dst_ref=got_ref, send_sem=send_sem, recv_sem=recv_sem,
            device_id=(x, y, 1 - c), device_id_type=MESH)
        cp.start()
        cp.wait()

    return pl.pallas_call(
        body, name=name, out_shape=jax.ShapeDtypeStruct((4, n_rows, n_cols), parts.dtype),
        in_specs=[_ANY], out_specs=_ANY,
        scratch_shapes=[pltpu.SemaphoreType.DMA, pltpu.SemaphoreType.DMA],
    )(parts)


class CommHook:
    def __init__(self, args, out_shapes, n_copies, copies, aliases=None):
        self.args, self.out_shapes, self.n_copies, self.copies = list(args), list(out_shapes), n_copies, copies
        self.aliases = aliases or {}

    def scratch(self):
        return [pltpu.SemaphoreType.DMA((self.n_copies,)), pltpu.SemaphoreType.DMA((self.n_copies,)),
                pltpu.SemaphoreType.DMA((1,))]

    def start(self, in_refs, out_refs, sems):
        sends, _, locs = self.copies(in_refs, out_refs, *sems)
        for cp in locs() + sends():
            cp.start()

    def wait(self, in_refs, out_refs, sems):
        sends, recvs, locs = self.copies(in_refs, out_refs, *sems)
        for cp in sends():
            cp.wait_send()
        for cp in recvs():
            cp.wait_recv()
        for cp in locs():
            cp.wait()


def _remote(src, dst, send_sem, recv_sem, to):
    return pltpu.make_async_remote_copy(src_ref=src, dst_ref=dst, send_sem=send_sem, recv_sem=recv_sem,
                                        device_id=to, device_id_type=MESH)


def hook_gather_first(mine):
    n_rows, n_cols = mine.shape

    def copies(ins, outs, send, recv, local):
        (x_ref,), (out_ref,) = ins, outs
        x, y, c = _me()
        me = c * 4 + x * 2 + y
        peers = [(x, y, 1 - c), (1 - x, y, c), (x, 1 - y, c), (1 - x, 1 - y, c)]

        def sends():
            return [_remote(x_ref, out_ref.at[me], send.at[k], recv.at[k], p) for k, p in enumerate(peers)]

        def recvs():
            return [_remote(x_ref, out_ref.at[pc * 4 + px * 2 + py], send.at[k], recv.at[k], (px, py, pc))
                    for k, (px, py, pc) in enumerate(peers)]

        return sends, recvs, lambda: [pltpu.make_async_copy(x_ref, out_ref.at[me], local.at[0])]

    return CommHook([mine], [jax.ShapeDtypeStruct((N_DEV, n_rows, n_cols), mine.dtype)], 4, copies)


def hook_gather_second(gathered):
    def copies(ins, outs, send, recv, local):
        (g_in,), (g_out,) = ins, outs
        x, y, c = _me()
        chips = [(1 - x, y), (x, 1 - y), (1 - x, 1 - y)]

        def sends():
            return [_remote(g_in.at[c * 4 + px * 2 + py], g_out.at[c * 4 + px * 2 + py], send.at[k], recv.at[k],
                            (x, y, 1 - c)) for k, (px, py) in enumerate(chips)]

        def recvs():
            return [_remote(g_in.at[(1 - c) * 4 + px * 2 + py], g_out.at[(1 - c) * 4 + px * 2 + py], send.at[k],
                            recv.at[k], (x, y, 1 - c)) for k, (px, py) in enumerate(chips)]

        return sends, recvs, lambda: []

    return CommHook([gathered], [jax.ShapeDtypeStruct(gathered.shape, gathered.dtype)], 3, copies, aliases={0: 0})


def hook_swap_sibling(parts):
    _, n_rows, n_cols = parts.shape

    def copies(ins, outs, send, recv, local):
        (p_ref,), (got_ref,) = ins, outs
        x, y, c = _me()

        def swap():
            return [_remote(p_ref.at[pl.ds((1 - c) * 4, 4)], got_ref, send.at[0], recv.at[0], (x, y, 1 - c))]

        return swap, swap, lambda: []

    return CommHook([parts], [jax.ShapeDtypeStruct((4, n_rows, n_cols), parts.dtype)], 1, copies)


def hook_swap_chips(parts):
    _, n_rows, n_cols = parts.shape

    def copies(ins, outs, send, recv, local):
        (p_ref,), (got_ref,) = ins, outs
        x, y, c = _me()
        chips = [(1 - x, y), (x, 1 - y), (1 - x, 1 - y)]

        def swaps():
            return [_remote(p_ref.at[2 * px + py], got_ref.at[k], send.at[k], recv.at[k], (px, py, c))
                    for k, (px, py) in enumerate(chips)]

        return swaps, swaps, lambda: []

    return CommHook([parts], [jax.ShapeDtypeStruct((3, n_rows, n_cols), parts.dtype)], 3, copies)


def _with_comm(comm, n_args, n_outs):
    if comm is None:
        return [], [], [], [], {}
    aliases = {n_args + a: n_outs + o for a, o in comm.aliases.items()}
    return [_ANY] * len(comm.args), [_ANY] * len(comm.out_shapes), comm.out_shapes, comm.scratch(), aliases


def comm_allreduce_small(name, mine):
    shape = mine.shape

    def body(x_ref, out_ref, buf, send_sems, recv_sems):
        x, y, c = _me()
        my_slot = c * 4 + x * 2 + y
        buf[my_slot] = x_ref[...]
        cps = []
        for k in range(1, N_DEV):
            dx, dy, dc = (k >> 2) & 1, (k >> 1) & 1, k & 1
            px, py, pc = x ^ dx, y ^ dy, c ^ dc
            send = pltpu.make_async_remote_copy(
                src_ref=x_ref, dst_ref=buf.at[my_slot], send_sem=send_sems.at[k - 1], recv_sem=recv_sems.at[k - 1],
                device_id=(px, py, pc), device_id_type=MESH)
            send.start()
            recv = pltpu.make_async_remote_copy(
                src_ref=x_ref, dst_ref=buf.at[pc * 4 + px * 2 + py], send_sem=send_sems.at[k - 1],
                recv_sem=recv_sems.at[k - 1], device_id=(px, py, pc), device_id_type=MESH)
            cps.append((send, recv))
        for send, recv in cps:
            send.wait_send()
            recv.wait_recv()
        total = buf[0]
        for s in range(1, N_DEV):
            total = total + buf[s]
        out_ref[...] = total

    vmem = pl.BlockSpec(memory_space=pltpu.VMEM)
    return pl.pallas_call(
        body, name=name, out_shape=jax.ShapeDtypeStruct(shape, F32), in_specs=[vmem], out_specs=vmem,
        scratch_shapes=[pltpu.VMEM((N_DEV,) + shape, F32), pltpu.SemaphoreType.DMA((7,)), pltpu.SemaphoreType.DMA((7,))],
    )(mine)


def _packed_rows(shape):
    rs, cs = shape
    if cs % PACK_C == 0:
        return rs * (cs // PACK_C)
    if cs > PACK_C // 2:
        return rs
    return rs * cs // PACK_C


def _to_rows(a):
    rs, cs = a.shape
    if cs % PACK_C == 0:
        return jnp.concatenate([a[:, i * PACK_C:(i + 1) * PACK_C] for i in range(cs // PACK_C)], axis=0)
    if cs > PACK_C // 2:
        return jnp.pad(a, ((0, 0), (0, PACK_C - cs)))
    return a.reshape(-1, PACK_C)


def _from_rows(rows, shape):
    rs, cs = shape
    lead = rows.shape[:-2]
    if cs % PACK_C == 0:
        return jnp.concatenate([rows[..., i * rs:(i + 1) * rs, :] for i in range(cs // PACK_C)], axis=-1)
    if cs > PACK_C // 2:
        return rows[..., :cs]
    return rows.reshape(*lead, rs, cs)


def pack_local(shards, names):
    flat = [_to_rows(shards[n]) for n in names]
    rows = sum(f.shape[0] for f in flat)
    pad = (-rows) % 128
    return jnp.concatenate(flat + [jnp.zeros((pad, PACK_C), flat[0].dtype)], axis=0)


def unpack_group(gathered, names, shard_shapes):
    _, n_rows, _ = gathered.shape
    by_block = gathered.reshape(2, 4, n_rows, PACK_C).transpose(1, 0, 2, 3).reshape(N_DEV, n_rows, PACK_C)
    full = {}
    r0 = 0
    for name in names:
        rs, cs = shard_shapes[name]
        nr = _packed_rows((rs, cs))
        piece = _from_rows(by_block[:, r0:r0 + nr, :], (rs, cs))
        r0 += nr
        if name in ROW_SHARDED:
            full[name] = piece.reshape(N_DEV * rs, cs)
        else:
            full[name] = piece.transpose(1, 0, 2).reshape(rs, N_DEV * cs)
    return full


def unpack_a(gathered, shard_shapes):
    full = unpack_group(gathered, GROUP_A, shard_shapes)
    w_in = full.pop("w_in")
    zeros = jnp.zeros((D_MODEL, LAT_W - OFF_FQ - HEADS), w_in.dtype)
    full["w_lat"] = jnp.concatenate([w_in[:, :OFF_FQ], w_in[:, OFF_FF:OFF_G], zeros], axis=1)
    full["w_fox"] = w_in[:, OFF_FQ:OFF_FF]
    full["w_gate"] = w_in[:, OFF_G:]
    w_uq = full.pop("w_uq").reshape(Q_LORA, HEADS, NOPE + ROPE)
    full["w_uq"] = jnp.pad(w_uq, ((0, 0), (0, 0), (0, QK_PAD - NOPE - ROPE))).reshape(Q_LORA, HEADS * QK_PAD)
    w_ukv = full.pop("w_ukv").reshape(KV_LORA, HEADS, 2, NOPE)
    full["w_kv"] = jnp.concatenate([w_ukv[:, :, 0, :].reshape(KV_LORA, HEADS * NOPE),
                                    w_ukv[:, :, 1, :].reshape(KV_LORA, HEADS * HEAD_V)], axis=1)
    return full


def pack_small(vals, loss):
    flat = [vals[n].reshape(-1) for n in SMALL] + [loss.reshape(-1)]
    used = sum(f.shape[0] for f in flat)
    flat.append(jnp.zeros((SMALL_ROWS * PACK_C - used,), F32))
    return jnp.concatenate(flat).reshape(SMALL_ROWS, PACK_C)


def unpack_small(packed):
    flat = packed.reshape(-1)
    out, off = {}, 0
    for n in SMALL:
        out[n] = flat[off:off + SMALL_N[n]]
        off += SMALL_N[n]
    return out, flat[off]


def rope_tables(s_len):
    pos = jnp.arange(s_len, dtype=F32)
    inv = 1.0 / (ROPE_THETA ** (jnp.arange(0, ROPE, 2, dtype=F32) / ROPE))
    ang = pos[:, None] * inv[None, :]
    cos, sin = jnp.cos(ang), jnp.sin(ang)
    zero = jnp.zeros_like(cos)
    c = jnp.concatenate([cos, cos, zero, zero], axis=1)
    s1 = jnp.concatenate([-sin, zero, zero, zero], axis=1)
    s2 = jnp.concatenate([zero, sin, zero, zero], axis=1)
    return (c, s1, s2), (c, -s1, -s2)


def reduce_scatter_tail(parts, from_sibling, from_chips_fn, names):
    cx, cy, cc = _me()
    n_rows = parts.shape[1]
    mine4 = lax.dynamic_slice_in_dim(parts, cc * 4, 4, axis=0)
    pair = add_pairs("rs_pair_sum_" + names, mine4.reshape(4 * n_rows, PACK_C), from_sibling.reshape(4 * n_rows, PACK_C))
    from_chips, extra = from_chips_fn(pair.reshape(4, n_rows, PACK_C))
    own = cx * 2 + cy
    total = add_final("rs_final_sum_" + names, lax.dynamic_index_in_dim(mine4, own, 0, keepdims=False),
                      lax.dynamic_index_in_dim(from_sibling, own, 0, keepdims=False),
                      from_chips[0], from_chips[1], from_chips[2])
    return total, extra


def local_step(x, target, w, small, packed_b, shard_shapes, to_packed_a):
    s_len = x.shape[0]
    tabs, inv_tabs = rope_tables(s_len)
    g_attn = small["attn_norm"].reshape(1, D_MODEL)
    g_q = small["q_norm"].reshape(1, Q_LORA)
    g_kv = small["kv_norm"].reshape(1, KV_LORA)
    g_mlp = small["mlp_norm"].reshape(1, D_MODEL)
    g_final = small["final_norm"].reshape(1, D_MODEL)
    f_bias = small["fox_f_bias"].reshape(HEADS, 1)
    mla_scale = 1.0 / math.sqrt(NOPE + ROPE)
    fox_scale = 1.0 / math.sqrt(HEAD_V)
    mla_offs = (0, 0, HEADS)
    fox_offs = (0, HEADS, 2 * HEADS)

    xn = rms_fwd("rms_attn", x, g_attn)
    lat, = matmul("proj_lat", xn, w["w_lat"], "nn", [F32])
    fox, = matmul("proj_fox", xn, w["w_fox"], "nn", [BF16])
    graw, = matmul("proj_gate", xn, w["w_gate"], "nn", [F32])
    cq = rms_fwd("rms_q", (lat, Q_LORA, 0), g_q)
    ckv = rms_fwd("rms_kv", (lat, KV_LORA, Q_LORA // KV_LORA), g_kv)
    qraw, = matmul("up_q", cq, w["w_uq"], "nn", [F32])
    q = rope_heads("rope_q", qraw, tabs, BF16)
    kvn, = matmul("up_kv", ckv, w["w_kv"], "nn", [BF16])
    kr = rope_block("rope_k", lat, OFF_KR // 128, tabs, BF16)
    k = k_assemble("k_assemble", kvn, kr)
    o_mla, lse_mla, gathered_b = attn_fwd("mla_fwd", q, k, kvn, mla_offs, QK_PAD, CHUNK, mla_scale,
                                          comm=hook_gather_first(packed_b))
    ff_t = lat[:, OFF_FQ:OFF_FQ + HEADS].T
    cum = fox_cum_fwd("fox_cum", ff_t, f_bias).reshape(HEADS, s_len, 1) * LOG2E
    o_fox, lse_fox, gathered_b = attn_fwd("fox_fwd", fox, fox, fox, fox_offs, HEAD_V, 1, fox_scale, cum=cum,
                                          comm=hook_gather_second(gathered_b), split_p=True)
    unpack_b = functools.partial(unpack_group, names=GROUP_B, shard_shapes=shard_shapes)
    w = {**w, **unpack_b(gathered_b)}
    y_mla, = matmul("branch_mla", o_mla, w["w_mla_branch"], "nn", [F32])
    y_fox, = matmul("branch_fox", o_fox, w["w_fox_branch"], "nn", [F32])
    mix = gate_mix("gate_mix", graw, y_mla, y_fox)
    h1, = matmul("out_proj", mix, w["w_out"], "nn", [F32], epilogue=lambda acc, res: (res + acc,), extras=[x])
    hn = rms_fwd("rms_mlp", h1, g_mlp)

    def relu2(acc):
        r = jnp.maximum(acc, 0.0)
        return r * r, r
    u, relu_up = matmul("mlp_up", hn, w["w_up"], "nn", [BF16, BF16], epilogue=relu2)
    h2, = matmul("mlp_down", u, w["w_down"], "nn", [F32], epilogue=lambda acc, res: (res + acc,), extras=[h1])
    dh2, d_final, loss = loss_head("loss_head", h2, target, g_final)

    grads = {}
    dup, = matmul("d_mlp_down", dh2, w["w_down"], "nt", [BF16],
                  epilogue=lambda acc, r: (acc * (2.0 * r.astype(F32)),), extras=[relu_up])
    grads["w_down"], = matmul("gw_down", u, dh2, "tn", [BF16], **TN_TILES)
    dhn, = matmul("d_mlp_up", dup, w["w_up"], "nt", [F32])
    grads["w_up"], = matmul("gw_up", hn, dup, "tn", [BF16], **TN_TILES)
    dh1, d_mlp = rms_bwd("rms_mlp_bwd", h1, dhn, g_mlp, dres=dh2)
    dmix, = matmul("d_out_proj", dh1, w["w_out"], "nt", [F32])
    grads["w_out"], = matmul("gw_out", mix, dh1, "tn", [BF16], **TN_TILES)
    dproj, dy_mla, dy_fox = gate_mix_bwd("gate_mix_bwd", graw, y_mla, y_fox, dmix)
    do_mla, = matmul("d_branch_mla", dy_mla, w["w_mla_branch"], "nt", [BF16])
    grads["w_mla_branch"], = matmul("gw_branch_mla", o_mla, dy_mla, "tn", [BF16], **TN_TILES)
    do_fox, = matmul("d_branch_fox", dy_fox, w["w_fox_branch"], "nt", [BF16])
    grads["w_fox_branch"], = matmul("gw_branch_fox", o_fox, dy_fox, "tn", [BF16], **TN_TILES)

    to_packed_b = jax.linear_transpose(unpack_b, jax.ShapeDtypeStruct(gathered_b.shape, BF16))
    parts_b, = to_packed_b({n: grads.pop(n) for n in GROUP_B})
    delta_mla = attn_delta("mla_delta", do_mla, o_mla)
    dk, dv, dq, from_sibling = attn_bwd("mla_bwd", q, k, kvn, do_mla, lse_mla, delta_mla, mla_offs, QK_PAD, CHUNK,
                                        mla_scale, F32, comm=hook_swap_sibling(parts_b))

    delta_fox = attn_delta("fox_delta", do_fox, o_fox)

    def chips_behind_fox_bwd(pair):
        dfk, dfv, dfq, dcum, from_chips = attn_bwd("fox_bwd", fox, fox, fox, do_fox, lse_fox, delta_fox, fox_offs,
                                                   HEAD_V, 1, fox_scale, BF16, cum=cum, comm=hook_swap_chips(pair))
        return from_chips, (dfq.astype(BF16), dfk, dfv, dcum)
    g_packed_b, (dfq, dfk, dfv, dcum) = reduce_scatter_tail(parts_b, from_sibling, chips_behind_fox_bwd, "b")
    dff_t, d_bias = fox_cum_bwd("fox_cum_bwd", ff_t, f_bias, dcum.reshape(HEADS, s_len))
    dq_r = rope_heads("rope_q_bwd", dq, inv_tabs, BF16)
    dkvn, dkr = dk_split("dk_split", dk, dv, inv_tabs)
    dcq, = matmul("d_up_q", dq_r, w["w_uq"], "nt", [F32])
    grads["w_uq"], = matmul("gw_uq", cq, dq_r, "tn", [BF16], tm=512, tn=2048, tk=2048)
    dckv, = matmul("d_up_kv", dkvn, w["w_kv"], "nt", [F32])
    grads["w_kv"], = matmul("gw_kv", ckv, dkvn, "tn", [BF16], tm=256, tn=2048, tk=2048)
    dcq_raw, d_qn = rms_bwd("rms_q_bwd", (lat, Q_LORA, 0), dcq, g_q, out_dtype=BF16)
    dckv_raw, d_kvn = rms_bwd("rms_kv_bwd", (lat, KV_LORA, Q_LORA // KV_LORA), dckv, g_kv, out_dtype=BF16)

    pad = jnp.zeros((s_len, LAT_W - OFF_FQ - HEADS), BF16)
    tail = jnp.concatenate([dkr[:, :ROPE].astype(BF16), dff_t.T.astype(BF16), pad], axis=1)
    col = GATE_W
    for piece in (dfq, dfk, dfv, dcq_raw, dckv_raw, tail):
        dproj = lax.dynamic_update_slice(dproj, piece, (0, col))
        col += piece.shape[1]
    w_in_p = jnp.concatenate([w["w_gate"], w["w_fox"], w["w_lat"]], axis=1)
    for part, col0, width in (("w_gate", 0, GATE_W), ("w_fox", GATE_W, FOX_W), ("w_lat", GATE_W + FOX_W, LAT_W)):
        grads[part], = matmul("gw_in_" + part, xn, dproj, "tn", [BF16], **TN_TILES, b_cols=(col0, width))

    parts_a = to_packed_a(grads)
    from_sibling_a = comm_swap_sibling("comm_rs_sibling_a", parts_a)

    def chips_behind_d_proj(pair):
        dxn, from_chips = matmul("d_proj", dproj, w_in_p, "nt", [F32], tk=2688, comm=hook_swap_chips(pair))
        return from_chips, dxn
    g_packed_a, dxn = reduce_scatter_tail(parts_a, from_sibling_a, chips_behind_d_proj, "a")
    dx, d_attn = rms_bwd("rms_attn_bwd", x, dxn, g_attn, dres=dh1)

    small_grads = {"attn_norm": d_attn, "fox_f_bias": d_bias, "q_norm": d_qn, "kv_norm": d_kvn,
                   "mlp_norm": d_mlp, "final_norm": d_final}
    return loss, dx, g_packed_a, g_packed_b, small_grads


def kernel(x, attn_norm, w_in, fox_f_bias, q_norm, w_uq, kv_norm, w_ukv, w_mla_branch, w_fox_branch, w_out, mlp_norm, w_up, w_down, final_norm, loss_target, m_attn_norm, m_w_in, m_fox_f_bias, m_q_norm, m_w_uq, m_kv_norm, m_w_ukv, m_w_mla_branch, m_w_fox_branch, m_w_out, m_mlp_norm, m_w_up, m_w_down, m_final_norm, v_attn_norm, v_w_in, v_fox_f_bias, v_q_norm, v_w_uq, v_kv_norm, v_w_ukv, v_w_mla_branch, v_w_fox_branch, v_w_out, v_mlp_norm, v_w_up, v_w_down, v_final_norm):
    given = dict(locals())
    big = {n: given[n][0] for n in BIG}
    small = {n: given[n] for n in SMALL}
    shard_shapes = {n: tuple(big[n].shape) for n in BIG}

    packed_a = pack_local({n: big[n].astype(BF16) for n in GROUP_A}, GROUP_A)
    packed_b = pack_local({n: big[n].astype(BF16) for n in GROUP_B}, GROUP_B)
    gathered_a = comm_allgather("comm_allgather_a", packed_a)
    unpack = functools.partial(unpack_a, shard_shapes=shard_shapes)
    w_a = unpack(gathered_a)

    transpose_a = jax.linear_transpose(unpack, jax.ShapeDtypeStruct(gathered_a.shape, BF16))
    loss_part, dx, g_packed_a, g_packed_b, small_grads = local_step(
        x[0], loss_target[0], w_a, small, packed_b, shard_shapes, lambda grads: transpose_a(grads)[0])

    small_sum = comm_allreduce_small("comm_allreduce_small", pack_small(small_grads, loss_part[0, 0]))
    g_small, loss = unpack_small(small_sum)

    grad_w, delta_w, new_m, new_v = {}, {}, {}, {}
    for names, g_packed in ((GROUP_A, g_packed_a), (GROUP_B, g_packed_b)):
        r0 = 0
        for n in names:
            rs, cs = shard_shapes[n]
            nr = _packed_rows((rs, cs))
            g = _from_rows(g_packed[r0:r0 + nr], (rs, cs))
            r0 += nr
            d, m_new, v_new = adamw("adamw_" + n, big[n], g, given["m_" + n][0], given["v_" + n][0])
            grad_w[n], delta_w[n], new_m[n], new_v[n] = g[None], d[None], m_new[None], v_new[None]
    zero = jnp.zeros((), F32)
    d_s, m_s, v_s = adamw("adamw_small", pack_small(small, zero), small_sum * _small_mask(),
                          pack_small({n: given["m_" + n] for n in SMALL}, zero),
                          pack_small({n: given["v_" + n] for n in SMALL}, zero), tr=SMALL_ROWS)
    d_small, _ = unpack_small(d_s)
    m_small, _ = unpack_small(m_s)
    v_small, _ = unpack_small(v_s)
    for n in SMALL:
        shape = given[n].shape
        grad_w[n], delta_w[n] = g_small[n].reshape(shape), d_small[n].reshape(shape)
        new_m[n], new_v[n] = m_small[n].reshape(shape), v_small[n].reshape(shape)

    order = ["attn_norm", "w_in", "fox_f_bias", "q_norm", "w_uq", "kv_norm", "w_ukv", "w_mla_branch", "w_fox_branch",
             "w_out", "mlp_norm", "w_up", "w_down", "final_norm"]
    return (loss, dx[None], *[grad_w[n] for n in order], *[delta_w[n] for n in order],
            *[new_m[n] for n in order], *[new_v[n] for n in order])


def _small_mask():
    used = sum(SMALL_N[n] for n in SMALL)
    return (jnp.arange(SMALL_ROWS * PACK_C) < used).astype(F32).reshape(SMALL_ROWS, PACK_C)
```

```python
import functools
import math

import jax
import jax.numpy as jnp
from jax import lax
from jax.experimental import pallas as pl
from jax.experimental.pallas import tpu as pltpu

F32 = jnp.float32
BF16 = jnp.bfloat16
MESH = pl.DeviceIdType.MESH

D_MODEL = 2048
HEADS = 8
Q_LORA = 512
KV_LORA = 256
NOPE = 128
ROPE = 64
HEAD_V = 128
D_FF = 4 * D_MODEL
CHUNK = 64
EPS = 1e-6
ROPE_THETA = 10000.0
OFF_KR = Q_LORA + KV_LORA
OFF_FQ = OFF_KR + ROPE
OFF_FF = OFF_FQ + 3 * HEADS * HEAD_V
OFF_G = OFF_FF + HEADS
D_IN = OFF_G + 2 * D_MODEL

LAT_W = 896
FOX_W = 3 * HEADS * HEAD_V
GATE_W = 2 * D_MODEL
PROJ_W = LAT_W + FOX_W + GATE_W
QK_PAD = 256

ADAM_LR = 0.001
ADAM_B1 = 0.9
ADAM_B2 = 0.999
ADAM_EPS = 1e-08
ADAM_WD = 0.01
ADAM_STEP = 10

N_DEV = 8
PACK_C = 1024
NEG = -1e30
LOG2E = math.log2(math.e)

VMEM_LIMIT = 56 * 1024 * 1024

BIG = ("w_in", "w_uq", "w_ukv", "w_mla_branch", "w_fox_branch", "w_out", "w_up", "w_down")
GROUP_A = ("w_in", "w_uq", "w_ukv")
GROUP_B = ("w_mla_branch", "w_fox_branch", "w_out", "w_up", "w_down")
ROW_SHARDED = ("w_out", "w_down")
SMALL = ("attn_norm", "fox_f_bias", "q_norm", "kv_norm", "mlp_norm", "final_norm")
SMALL_N = {"attn_norm": D_MODEL, "fox_f_bias": HEADS, "q_norm": Q_LORA, "kv_norm": KV_LORA,
           "mlp_norm": D_MODEL, "final_norm": D_MODEL}
SMALL_ROWS = 8


def _params(sem):
    return pltpu.CompilerParams(dimension_semantics=sem, vmem_limit_bytes=VMEM_LIMIT)


def _rows(name, fn, row_ins, const_ins, outs, reds=(), tr=256):
    norm = [(a, a.shape[1], 0) if not isinstance(a, tuple) else a for a in row_ins]
    n_rows = norm[0][0].shape[0]
    tr = min(tr, n_rows)
    assert n_rows % tr == 0, (name, n_rows, tr)
    n_in, n_out, n_red = len(norm) + len(const_ins), len(outs), len(reds)

    def body(*refs):
        vals = [r[...] for r in refs[:n_in]]
        out_refs = refs[n_in:n_in + n_out]
        red_refs = refs[n_in + n_out:]
        out_vals, red_vals = fn(*vals)
        for r, v in zip(out_refs, out_vals):
            r[...] = v.astype(r.dtype)
        if n_red:
            @pl.when(pl.program_id(0) == 0)
            def _():
                for r in red_refs:
                    r[...] = jnp.zeros_like(r)
            for r, v in zip(red_refs, red_vals):
                r[...] += v

    in_specs = [pl.BlockSpec((tr, w), functools.partial(lambda i, cb: (i, cb), cb=cb)) for _, w, cb in norm]
    in_specs += [pl.BlockSpec(a.shape, lambda i: (0, 0)) for a in const_ins]
    out_specs = [pl.BlockSpec((tr, o[0]), lambda i: (i, 0)) for o in outs]
    out_specs += [pl.BlockSpec((1, c), lambda i: (0, 0)) for c in reds]
    out_shape = [jax.ShapeDtypeStruct((n_rows, o[2] if len(o) > 2 else o[0]), o[1]) for o in outs]
    out_shape += [jax.ShapeDtypeStruct((1, c), F32) for c in reds]
    res = pl.pallas_call(
        body, name=name, grid=(n_rows // tr,), in_specs=in_specs, out_specs=out_specs, out_shape=out_shape,
        compiler_params=_params(("arbitrary",)),
    )(*[a for a, _, _ in norm], *const_ins)
    return res


def _rstd(x):
    return lax.rsqrt(jnp.mean(x * x, axis=-1, keepdims=True) + EPS)


def rms_fwd(name, x, gain, tr=256):
    width = x[1] if isinstance(x, tuple) else x.shape[1]

    def fn(xv, g):
        return (xv * _rstd(xv) * g,), ()
    return _rows(name, fn, [x], [gain], [(width, BF16)], tr=tr)[0]


def rms_bwd(name, x, dy, gain, dres=None, out_dtype=F32, tr=256):
    width = x[1] if isinstance(x, tuple) else x.shape[1]

    def fn(xv, dyv, *rest):
        g = rest[-1]
        r = _rstd(xv)
        n = xv * r
        dyv = dyv.astype(F32)
        dn = dyv * g
        dx = r * (dn - n * jnp.mean(dn * n, axis=-1, keepdims=True))
        if dres is not None:
            dx = dx + rest[0]
        return (dx,), (jnp.sum(dyv * n, axis=0, keepdims=True),)

    ins = [x, dy] + ([dres] if dres is not None else [])
    return _rows(name, fn, ins, [gain], [(width, out_dtype)], [width], tr=tr)


def _rope_lanes(t, c, s1, s2):
    return t * c + pltpu.roll(t, 96, 1) * s1 + pltpu.roll(t, 32, 1) * s2


def rope_heads(name, x, tabs, out_dtype):
    def fn(xv, c, s1, s2):
        xv = xv.astype(F32)
        parts = []
        for h in range(HEADS):
            parts.append(xv[:, h * QK_PAD:h * QK_PAD + NOPE])
            parts.append(_rope_lanes(xv[:, h * QK_PAD + NOPE:(h + 1) * QK_PAD], c, s1, s2))
        return (jnp.concatenate(parts, axis=1),), ()
    return _rows(name, fn, [x, *tabs], [], [(HEADS * QK_PAD, out_dtype)])[0]


def rope_block(name, x, col_block, tabs, out_dtype):
    def fn(xv, c, s1, s2):
        return (_rope_lanes(xv.astype(F32), c, s1, s2),), ()
    return _rows(name, fn, [(x, 128, col_block), *tabs], [], [(128, out_dtype)])[0]


def k_assemble(name, kvn, kr):
    def fn(knp, krv):
        parts = []
        for h in range(HEADS):
            parts.append(knp[:, h * NOPE:(h + 1) * NOPE])
            parts.append(krv)
        return (jnp.concatenate(parts, axis=1),), ()
    return _rows(name, fn, [(kvn, HEADS * NOPE, 0), kr], [], [(HEADS * QK_PAD, BF16)])[0]


def dk_split(name, dk, dv, inv_tabs):
    def fn(dkv, dvv, c, s1, s2):
        parts = []
        acc = None
        for h in range(HEADS):
            parts.append(dkv[:, h * QK_PAD:h * QK_PAD + NOPE].astype(BF16))
            t = dkv[:, h * QK_PAD + NOPE:(h + 1) * QK_PAD]
            acc = t if acc is None else acc + t
        parts.append(dvv)
        return (jnp.concatenate(parts, axis=1), _rope_lanes(acc, c, s1, s2)), ()
    return _rows(name, fn, [dk, dv, *inv_tabs], [], [(2 * HEADS * NOPE, BF16), (128, F32)])


def gate_mix(name, graw, y_mla, y_fox):
    def fn(g, ya, yb):
        ga = jax.nn.sigmoid(g[:, :D_MODEL])
        gb = jax.nn.sigmoid(g[:, D_MODEL:])
        return (ga * ya + gb * yb,), ()
    return _rows(name, fn, [graw, y_mla, y_fox], [], [(D_MODEL, BF16)])[0]


def gate_mix_bwd(name, graw, y_mla, y_fox, dmix):
    def fn(g, ya, yb, dm):
        ga = jax.nn.sigmoid(g[:, :D_MODEL])
        gb = jax.nn.sigmoid(g[:, D_MODEL:])
        dgraw = jnp.concatenate([dm * ya * ga * (1.0 - ga), dm * yb * gb * (1.0 - gb)], axis=1)
        return (dgraw, dm * ga, dm * gb), ()
    return _rows(name, fn, [graw, y_mla, y_fox, dmix], [],
                 [(GATE_W, BF16, PROJ_W), (D_MODEL, BF16), (D_MODEL, BF16)], tr=256)


def loss_head(name, h2, target, gain):
    inv_d = 1.0 / D_MODEL

    def fn(h, t, g):
        r = _rstd(h)
        n = h * r
        err = n * g - t
        dy = err * inv_d
        dn = dy * g
        dh = r * (dn - n * jnp.mean(dn * n, axis=-1, keepdims=True))
        part = 0.5 * inv_d * jnp.sum(jnp.sum(err * err, axis=1, keepdims=True), axis=0, keepdims=True)
        return (dh,), (jnp.sum(dy * n, axis=0, keepdims=True), jnp.broadcast_to(part, (1, 128)))
    return _rows(name, fn, [h2, target], [gain], [(D_MODEL, F32)], [D_MODEL, 128])


def adamw(name, w, g, m, v, tr=256):
    c1 = 1.0 - ADAM_B1 ** ADAM_STEP
    c2 = 1.0 - ADAM_B2 ** ADAM_STEP

    def fn(wv, gv, mv, vv):
        m_new = ADAM_B1 * mv + (1.0 - ADAM_B1) * gv
        v_new = ADAM_B2 * vv + (1.0 - ADAM_B2) * (gv * gv)
        delta = -ADAM_LR * ((m_new / c1) / (jnp.sqrt(v_new / c2) + ADAM_EPS) + ADAM_WD * wv)
        return (delta, m_new, v_new), ()
    cols = w.shape[1]
    return _rows(name, fn, [w, g, m, v], [], [(cols, F32)] * 3, tr=tr)


def _row_tile(n_rows, cap=640):
    return max(t for t in range(16, cap + 1, 16) if n_rows % t == 0)


def add_pairs(name, a, b):
    def fn(av, bv):
        return (av.astype(F32) + bv.astype(F32),), ()
    return _rows(name, fn, [a, b], [], [(a.shape[1], BF16)], tr=_row_tile(a.shape[0]))[0]


def add_final(name, a, b, r0, r1, r2):
    def fn(av, bv, r0v, r1v, r2v):
        return (((av.astype(F32) + bv.astype(F32)) + r0v.astype(F32)) + r1v.astype(F32) + r2v.astype(F32),), ()
    return _rows(name, fn, [a, b, r0, r1, r2], [], [(a.shape[1], F32)], tr=_row_tile(a.shape[0]))[0]


TN_TILES = dict(tm=1024, tn=1024, tk=2048)
LONG_K = 4096
_DIMS = {"nn": (((1,), (0,)), ((), ())), "nt": (((1,), (1,)), ((), ())), "tn": (((0,), (0,)), ((), ()))}


def matmul(name, a, b, mode, outs, epilogue=None, extras=(), tm=1024, tn=1024, tk=2048, comm=None, b_cols=None):
    if mode == "tn":
        kdim, m = a.shape
    else:
        m, kdim = a.shape
    n = b.shape[0] if mode == "nt" else b.shape[1]
    if b_cols is not None:
        assert mode != "nt"
        col0, n = b_cols
        tn = min(tn, n)
        assert col0 % tn == 0, (name, b_cols, tn)
    j0 = 0 if b_cols is None else col0 // tn
    tm, tn, tk = min(tm, m), min(tn, n), min(tk, kdim)
    assert m % tm == 0 and n % tn == 0 and kdim % tk == 0, (name, a.shape, b.shape)
    nk = kdim // tk
    n_ex, n_out = len(extras), len(outs)
    dims = _DIMS[mode]
    grid = (m // tm, n // tn, nk)

    def body(*refs):
        ins, c_in, out_refs, c_out, scr, c_sems = _split_refs(refs, 2 + n_ex, comm, n_out, 1 if nk > 1 else 0)
        a_ref, b_ref = ins[:2]
        ex_refs = ins[2:]
        _comm_start(comm, c_in, c_out, c_sems, *grid)

        def finish(acc):
            vals = (acc,) if epilogue is None else epilogue(acc, *[r[...] for r in ex_refs])
            for r, v in zip(out_refs, vals):
                r[...] = v.astype(r.dtype)

        part = lax.dot_general(a_ref[...].astype(BF16), b_ref[...].astype(BF16), dims, preferred_element_type=F32)
        if nk == 1:
            finish(part)
        else:
            acc_ref = scr[0]
            k = pl.program_id(2)

            @pl.when(k == 0)
            def _():
                acc_ref[...] = part

            @pl.when(k > 0)
            def _():
                acc_ref[...] += part

            @pl.when(k == nk - 1)
            def _():
                finish(acc_ref[...])
        _comm_wait(comm, c_in, c_out, c_sems, *grid)

    a_spec = pl.BlockSpec((tk, tm), lambda i, j, k: (k, i)) if mode == "tn" else pl.BlockSpec((tm, tk), lambda i, j, k: (i, k))
    b_spec = (pl.BlockSpec((tn, tk), lambda i, j, k: (j, k)) if mode == "nt"
              else pl.BlockSpec((tk, tn), lambda i, j, k: (k, j + j0)))
    tile = pl.BlockSpec((tm, tn), lambda i, j, k: (i, j))
    c_ins, c_outs, c_shapes, c_scratch, aliases = _with_comm(comm, 2 + n_ex, n_out)
    sem = ("arbitrary",) * 3 if comm else ("parallel", "parallel", "arbitrary")
    res = pl.pallas_call(
        body, name=name, grid=grid,
        in_specs=[a_spec, b_spec] + [tile] * n_ex + c_ins,
        out_specs=[tile] * n_out + c_outs,
        out_shape=[jax.ShapeDtypeStruct((m, n), dt) for dt in outs] + c_shapes,
        scratch_shapes=([pltpu.VMEM((tm, tn), F32)] if nk > 1 else []) + c_scratch,
        input_output_aliases=aliases,
        compiler_params=_params(sem),
    )(a, b, *extras, *(comm.args if comm else []))
    return res


_NT = (((1,), (1,)), ((), ()))
_NN = (((1,), (0,)), ((), ()))


def _mask(bq, chunk, transposed, row0=0, shape=None):
    shape = (bq, bq) if shape is None else shape
    row = lax.broadcasted_iota(jnp.int32, shape, 0) + row0
    col = lax.broadcasted_iota(jnp.int32, shape, 1)
    if chunk > 1:
        row, col = row // chunk, col // chunk
    return (row <= col) if transposed else (col <= row)


def _row_layout(a, bq):
    h, s, _ = a.shape
    return a.reshape(h, s // bq, 1, bq)


def _split_refs(refs, n_in, comm, n_out, n_scr):
    n_ci = len(comm.args) if comm else 0
    n_co = len(comm.out_shapes) if comm else 0
    cuts = [n_in, n_ci, n_out, n_co, n_scr, 3 if comm else 0]
    parts, at = [], 0
    for n in cuts:
        parts.append(list(refs[at:at + n]))
        at += n
    assert at == len(refs), (at, len(refs))
    return parts


def _at_step(grid, last):
    hit = None
    for axis, n in enumerate(grid):
        here = pl.program_id(axis) == (n - 1 if last else 0)
        hit = here if hit is None else jnp.logical_and(hit, here)
    return hit


def _comm_start(comm, c_in, c_out, c_sems, *grid):
    if comm is not None:
        @pl.when(_at_step(grid, False))
        def _():
            comm.start(c_in, c_out, c_sems)


def _comm_wait(comm, c_in, c_out, c_sems, *grid):
    if comm is not None:
        @pl.when(_at_step(grid, True))
        def _():
            comm.wait(c_in, c_out, c_sems)


def attn_fwd(name, q, k, v, offs, dqk, chunk, scale, cum=None, bq=512, comm=None, split_p=False):
    s_len = q.shape[0]
    nq = s_len // bq
    qoff, koff, voff = offs
    has_bias = cum is not None
    scale2 = scale * LOG2E

    n_in = 5 if has_bias else 3

    def body(*refs):
        ins, c_in, outs, c_out, scr, c_sems = _split_refs(refs, n_in, comm, 2, 3)
        q_ref, k_ref, v_ref = ins[:3]
        if has_bias:
            cc_ref, cr_ref = ins[3:]
        o_ref, lse_ref = outs
        m_s, l_s, acc_s = scr
        _comm_start(comm, c_in, c_out, c_sems, HEADS, nq)
        i = pl.program_id(1)
        qv = q_ref[...]
        m_s[...] = jnp.full_like(m_s, NEG)
        l_s[...] = jnp.zeros_like(l_s)
        acc_s[...] = jnp.zeros_like(acc_s)

        def step(j, masked):
            off = pl.multiple_of(j * bq, bq)
            kj = k_ref[pl.ds(off, bq), :]
            vj = v_ref[pl.ds(off, bq), :]
            st = lax.dot_general(kj, qv, _NT, preferred_element_type=F32) * scale2
            if has_bias:
                st = st + cr_ref[...] - cc_ref[pl.ds(off, bq), :]
            if masked:
                st = jnp.where(_mask(bq, chunk, True), st, NEG)
            m_prev = m_s[...]
            m_new = jnp.maximum(m_prev, jnp.max(st, axis=0, keepdims=True))
            alpha = jnp.exp2(m_prev - m_new)
            pt = jnp.exp2(st - m_new)
            l_s[...] = alpha * l_s[...] + jnp.sum(pt, axis=0, keepdims=True)
            p_hi = pt.astype(BF16)
            pv = lax.dot_general(vj, p_hi, _DIMS["tn"], preferred_element_type=F32)
            if split_p:
                p_lo = (pt - p_hi.astype(F32)).astype(BF16)
                pv = pv + lax.dot_general(vj, p_lo, _DIMS["tn"], preferred_element_type=F32)
            acc_s[...] = alpha * acc_s[...] + pv
            m_s[...] = m_new

        def pair_body(jj, carry):
            step(2 * jj, False)
            step(2 * jj + 1, False)
            return carry

        lax.fori_loop(0, i // 2, pair_body, 0)

        @pl.when(i % 2 == 1)
        def _():
            step(i - 1, False)

        step(i, True)
        o_ref[...] = (acc_s[...] / l_s[...]).T.astype(o_ref.dtype)
        lse_ref[...] = m_s[...] + jnp.log2(l_s[...])
        _comm_wait(comm, c_in, c_out, c_sems, HEADS, nq)

    in_specs = [
        pl.BlockSpec((bq, dqk), lambda h, i: (i, qoff + h)),
        pl.BlockSpec((s_len, dqk), lambda h, i: (0, koff + h)),
        pl.BlockSpec((s_len, HEAD_V), lambda h, i: (0, voff + h)),
    ]
    args = [q, k, v]
    if has_bias:
        in_specs += [pl.BlockSpec((None, s_len, 1), lambda h, i: (h, 0, 0)),
                     pl.BlockSpec((None, None, 1, bq), lambda h, i: (h, i, 0, 0))]
        args += [cum, _row_layout(cum, bq)]
    c_ins, c_outs, c_shapes, c_scratch, aliases = _with_comm(comm, len(args), 2)
    o, lse_rows, *comm_out = pl.pallas_call(
        body, name=name, grid=(HEADS, nq), in_specs=in_specs + c_ins,
        out_specs=[pl.BlockSpec((bq, HEAD_V), lambda h, i: (i, h)),
                   pl.BlockSpec((None, None, 1, bq), lambda h, i: (h, i, 0, 0))] + c_outs,
        out_shape=[jax.ShapeDtypeStruct((s_len, HEADS * HEAD_V), F32),
                   jax.ShapeDtypeStruct((HEADS, nq, 1, bq), F32)] + c_shapes,
        scratch_shapes=[pltpu.VMEM((1, bq), F32), pltpu.VMEM((1, bq), F32), pltpu.VMEM((HEAD_V, bq), F32)] + c_scratch,
        input_output_aliases=aliases,
        compiler_params=_params(("arbitrary", "arbitrary")),
    )(*args, *(comm.args if comm else []))
    return (o, lse_rows, *comm_out)


def _bwd_block(kv, vv, qi, doi, lse_row, scale2, bias, masked, bq, chunk):
    st = lax.dot_general(kv, qi, _NT, preferred_element_type=F32) * scale2
    if bias is not None:
        st = st + bias[0] - bias[1]
    if masked:
        st = jnp.where(_mask(bq, chunk, True), st, NEG)
    pt = jnp.exp2(st - lse_row)
    dpt = lax.dot_general(vv, doi, _NT, preferred_element_type=F32)
    return pt, dpt


def attn_delta(name, do, o, bq=512):
    s_len = do.shape[0]
    nq = s_len // bq

    def body(do_ref, o_ref, out_ref):
        ones = jnp.ones((8, HEAD_V), BF16)
        for h in range(HEADS):
            cols = slice(h * HEAD_V, (h + 1) * HEAD_V)
            prod = do_ref[:, cols].astype(F32) * o_ref[:, cols].astype(F32)
            total = None
            for part in reversed(_split3(prod)):
                term = lax.dot_general(ones, part, _NT, preferred_element_type=F32)
                total = term if total is None else total + term
            out_ref[h] = total[0:1, :]

    blk = pl.BlockSpec((bq, HEADS * HEAD_V), lambda i: (i, 0))
    return pl.pallas_call(
        body, name=name, grid=(nq,), in_specs=[blk, blk],
        out_specs=pl.BlockSpec((HEADS, None, 1, bq), lambda i: (0, i, 0, 0)),
        out_shape=jax.ShapeDtypeStruct((HEADS, nq, 1, bq), F32),
        compiler_params=_params(("parallel",)),
    )(do, o)


def attn_bwd(name, q, k, v, do, lse, delta, offs, dqk, chunk, scale, out_dtype, cum=None, bq=512, comm=None):
    s_len = q.shape[0]
    nq = s_len // bq
    qoff, koff, voff = offs
    has_bias = cum is not None
    scale2 = scale * LOG2E
    n_in, n_out = (8, 4) if has_bias else (6, 3)

    def body(*refs):
        ins, c_in, outs, c_out, scr, c_sems = _split_refs(refs, n_in, comm, n_out, n_out - 1)
        k_ref, v_ref, q_ref, do_ref, lse_ref, delta_ref = ins[:6]
        dk_ref, dv_ref, dq_ref = outs[:3]
        dk_s, dv_s = scr[:2]
        if has_bias:
            cc_ref, cr_ref = ins[6:]
            dc_ref, dc_s = outs[3], scr[2]
        _comm_start(comm, c_in, c_out, c_sems, HEADS, nq)
        j = pl.program_id(1)
        kv = k_ref[...]
        vv = v_ref[...]
        dk_s[...] = jnp.zeros_like(dk_s)
        dv_s[...] = jnp.zeros_like(dv_s)
        if has_bias:
            dc_s[...] = jnp.zeros_like(dc_s)

        @pl.when(j == 0)
        def _():
            dq_ref[...] = jnp.zeros_like(dq_ref)

        def step(i, masked):
            off = pl.multiple_of(i * bq, bq)
            qi = q_ref[pl.ds(off, bq), :]
            doi = do_ref[pl.ds(off, bq), :].astype(BF16)
            bias = (cr_ref[i], cc_ref[...]) if has_bias else None
            pt, dpt = _bwd_block(kv, vv, qi, doi, lse_ref[i], scale2, bias, masked, bq, chunk)
            dv_s[...] += lax.dot_general(pt.astype(BF16), doi, _NN, preferred_element_type=F32)
            dst = pt * (dpt - delta_ref[i])
            if has_bias:
                dc_s[...] -= jnp.sum(dst, axis=-1, keepdims=True)
            dst = dst.astype(BF16)
            dk_s[...] += lax.dot_general(dst, qi, _NN, preferred_element_type=F32)
            dq_ref[pl.ds(off, bq), :] += lax.dot_general(dst, kv, _DIMS["tn"], preferred_element_type=F32) * scale

        step(j, True)

        def loop_body(i, carry):
            step(i, False)
            return carry

        lax.fori_loop(j + 1, nq, loop_body, 0)
        dk_ref[...] = (dk_s[...] * scale).astype(dk_ref.dtype)
        dv_ref[...] = dv_s[...].astype(dv_ref.dtype)
        if has_bias:
            dc_ref[...] = dc_s[...]
        _comm_wait(comm, c_in, c_out, c_sems, HEADS, nq)

    rows = pl.BlockSpec((None, nq, 1, bq), lambda h, j: (h, 0, 0, 0))
    in_specs = [
        pl.BlockSpec((bq, dqk), lambda h, j: (j, koff + h)),
        pl.BlockSpec((bq, HEAD_V), lambda h, j: (j, voff + h)),
        pl.BlockSpec((s_len, dqk), lambda h, j: (0, qoff + h)),
        pl.BlockSpec((s_len, HEAD_V), lambda h, j: (0, h)),
        rows,
        rows,
    ]
    args = [k, v, q, do, lse, delta]
    out_specs = [pl.BlockSpec((bq, dqk), lambda h, j: (j, h)), pl.BlockSpec((bq, HEAD_V), lambda h, j: (j, h)),
                 pl.BlockSpec((s_len, dqk), lambda h, j: (0, h))]
    out_shape = [jax.ShapeDtypeStruct((s_len, HEADS * dqk), out_dtype),
                 jax.ShapeDtypeStruct((s_len, HEADS * HEAD_V), BF16),
                 jax.ShapeDtypeStruct((s_len, HEADS * dqk), F32)]
    scratch = [pltpu.VMEM((bq, dqk), F32), pltpu.VMEM((bq, HEAD_V), F32)]
    if has_bias:
        in_specs += [pl.BlockSpec((None, bq, 1), lambda h, j: (h, j, 0)), rows]
        args += [cum, _row_layout(cum, bq)]
        out_specs.append(pl.BlockSpec((None, bq, 1), lambda h, j: (h, j, 0)))
        out_shape.append(jax.ShapeDtypeStruct((HEADS, s_len, 1), F32))
        scratch.append(pltpu.VMEM((bq, 1), F32))
    c_ins, c_outs, c_shapes, c_scratch, aliases = _with_comm(comm, len(args), n_out)
    return pl.pallas_call(
        body, name=name, grid=(HEADS, nq), in_specs=in_specs + c_ins, out_specs=out_specs + c_outs,
        out_shape=out_shape + c_shapes, scratch_shapes=scratch + c_scratch, input_output_aliases=aliases,
        compiler_params=_params(("arbitrary", "arbitrary")),
    )(*args, *(comm.args if comm else []))


_CUM_BLK = 512


def _split3(x):
    hi = x.astype(BF16)
    r1 = x - hi.astype(F32)
    mid = r1.astype(BF16)
    lo = (r1 - mid.astype(F32)).astype(BF16)
    return hi, mid, lo


def _tri_dot(x, tri):
    hi, mid, lo = _split3(x)
    out = lax.dot_general(lo, tri, _NN, preferred_element_type=F32)
    out = out + lax.dot_general(mid, tri, _NN, preferred_element_type=F32)
    return out + lax.dot_general(hi, tri, _NN, preferred_element_type=F32)


def fox_cum_fwd(name, ff_t, bias):
    s_len = ff_t.shape[1]
    nb = s_len // _CUM_BLK

    def body(ff_ref, b_ref, cum_ref):
        row = lax.broadcasted_iota(jnp.int32, (_CUM_BLK, _CUM_BLK), 0)
        col = lax.broadcasted_iota(jnp.int32, (_CUM_BLK, _CUM_BLK), 1)
        tri = (row <= col).astype(BF16)
        carry = jnp.zeros((HEADS, 1), F32)
        for b in range(nb):
            z = ff_ref[:, b * _CUM_BLK:(b + 1) * _CUM_BLK] + b_ref[...]
            logf = jnp.minimum(z, 0.0) - jnp.log1p(jnp.exp(-jnp.abs(z)))
            blk = _tri_dot(logf, tri) + carry
            cum_ref[:, b * _CUM_BLK:(b + 1) * _CUM_BLK] = blk
            carry = blk[:, _CUM_BLK - 1:_CUM_BLK]

    return pl.pallas_call(
        body, name=name, out_shape=jax.ShapeDtypeStruct((HEADS, s_len), F32),
        compiler_params=pltpu.CompilerParams(vmem_limit_bytes=VMEM_LIMIT),
    )(ff_t, bias)


def fox_cum_bwd(name, ff_t, bias, dcum):
    s_len = ff_t.shape[1]
    nb = s_len // _CUM_BLK

    def body(ff_ref, b_ref, dc_ref, dff_ref, db_ref):
        row = lax.broadcasted_iota(jnp.int32, (_CUM_BLK, _CUM_BLK), 0)
        col = lax.broadcasted_iota(jnp.int32, (_CUM_BLK, _CUM_BLK), 1)
        tri = (row >= col).astype(BF16)
        carry = jnp.zeros((HEADS, 1), F32)
        dbias = jnp.zeros((HEADS, 1), F32)
        for b in reversed(range(nb)):
            sl = slice(b * _CUM_BLK, (b + 1) * _CUM_BLK)
            dlogf = _tri_dot(dc_ref[:, sl], tri) + carry
            carry = dlogf[:, 0:1]
            z = ff_ref[:, sl] + b_ref[...]
            dz = dlogf / (1.0 + jnp.exp(z))
            dff_ref[:, sl] = dz
            dbias = dbias + jnp.sum(dz, axis=-1, keepdims=True)
        db_ref[...] = dbias

    return pl.pallas_call(
        body, name=name,
        out_shape=[jax.ShapeDtypeStruct((HEADS, s_len), F32), jax.ShapeDtypeStruct((HEADS, 1), F32)],
        compiler_params=pltpu.CompilerParams(vmem_limit_bytes=VMEM_LIMIT),
    )(ff_t, bias, dcum)


_ANY = pl.BlockSpec(memory_space=pl.ANY)


def _me():
    return lax.axis_index("x"), lax.axis_index("y"), lax.axis_index("c")


def comm_allgather(name, mine):
    n_rows, n_cols = mine.shape

    def body(x_ref, out_ref, send_sems, recv_sems, local_sem):
        x, y, c = _me()
        sibling = (x, y, 1 - c)
        chips = [(1 - x, y), (x, 1 - y), (1 - x, 1 - y)]

        def blk(px, py, pc):
            return out_ref.at[pc * 4 + px * 2 + py]

        def copy(k, block, to, src=None):
            return pltpu.make_async_remote_copy(
                src_ref=blk(*block) if src is None else src, dst_ref=blk(*block),
                send_sem=send_sems.at[k], recv_sem=recv_sems.at[k], device_id=to, device_id_type=MESH)

        own = pltpu.make_async_copy(x_ref, blk(x, y, c), local_sem)
        own.start()
        first = [copy(0, (x, y, c), sibling, src=x_ref)]
        first += [copy(1 + j, (x, y, c), (*chip, c), src=x_ref) for j, chip in enumerate(chips)]
        for cp in first:
            cp.start()
        passed = [copy(4 + j, (*chip, c), sibling) for j, chip in enumerate(chips)]
        for j, chip in enumerate(chips):
            copy(1 + j, (*chip, c), (x, y, c)).wait_recv()
            passed[j].start()
        copy(0, sibling, (x, y, c)).wait_recv()
        for j, chip in enumerate(chips):
            copy(4 + j, (*chip, 1 - c), (x, y, c)).wait_recv()
        for cp in first + passed:
            cp.wait_send()
        own.wait()

    return pl.pallas_call(
        body, name=name, out_shape=jax.ShapeDtypeStruct((N_DEV, n_rows, n_cols), mine.dtype),
        in_specs=[_ANY], out_specs=_ANY,
        scratch_shapes=[pltpu.SemaphoreType.DMA((7,)), pltpu.SemaphoreType.DMA((7,)), pltpu.SemaphoreType.DMA],
    )(mine)


def comm_swap_sibling(name, parts):
    _, n_rows, n_cols = parts.shape

    def body(p_ref, got_ref, send_sem, recv_sem):
        x, y, c = _me()
        cp = pltpu.make_async_remote_copy(
            src_ref=p_ref.at[pl.ds((1 - c) * 4, 4)], dst_ref=got_ref, send_sem=send_sem, recv_sem=recv_sem,
            device_id=(x, y, 1 - c), device_id_type=MESH)
        cp.start()
        cp.wait()

    return pl.pallas_call(
        body, name=name, out_shape=jax.ShapeDtypeStruct((4, n_rows, n_cols), parts.dtype),
        in_specs=[_ANY], out_specs=_ANY,
        scratch_shapes=[pltpu.SemaphoreType.DMA, pltpu.SemaphoreType.DMA],
    )(parts)


class CommHook:
    def __init__(self, args, out_shapes, n_copies, copies, aliases=None):
        self.args, self.out_shapes, self.n_copies, self.copies = list(args), list(out_shapes), n_copies, copies
        self.aliases = aliases or {}

    def scratch(self):
        return [pltpu.SemaphoreType.DMA((self.n_copies,)), pltpu.SemaphoreType.DMA((self.n_copies,)),
                pltpu.SemaphoreType.DMA((1,))]

    def start(self, in_refs, out_refs, sems):
        sends, _, locs = self.copies(in_refs, out_refs, *sems)
        for cp in locs() + sends():
            cp.start()

    def wait(self, in_refs, out_refs, sems):
        sends, recvs, locs = self.copies(in_refs, out_refs, *sems)
        for cp in sends():
            cp.wait_send()
        for cp in recvs():
            cp.wait_recv()
        for cp in locs():
            cp.wait()


def _remote(src, dst, send_sem, recv_sem, to):
    return pltpu.make_async_remote_copy(src_ref=src, dst_ref=dst, send_sem=send_sem, recv_sem=recv_sem,
                                        device_id=to, device_id_type=MESH)


def hook_gather_first(mine):
    n_rows, n_cols = mine.shape

    def copies(ins, outs, send, recv, local):
        (x_ref,), (out_ref,) = ins, outs
        x, y, c = _me()
        me = c * 4 + x * 2 + y
        peers = [(x, y, 1 - c), (1 - x, y, c), (x, 1 - y, c), (1 - x, 1 - y, c)]

        def sends():
            return [_remote(x_ref, out_ref.at[me], send.at[k], recv.at[k], p) for k, p in enumerate(peers)]

        def recvs():
            return [_remote(x_ref, out_ref.at[pc * 4 + px * 2 + py], send.at[k], recv.at[k], (px, py, pc))
                    for k, (px, py, pc) in enumerate(peers)]

        return sends, recvs, lambda: [pltpu.make_async_copy(x_ref, out_ref.at[me], local.at[0])]

    return CommHook([mine], [jax.ShapeDtypeStruct((N_DEV, n_rows, n_cols), mine.dtype)], 4, copies)


def hook_gather_second(gathered):
    def copies(ins, outs, send, recv, local):
        (g_in,), (g_out,) = ins, outs
        x, y, c = _me()
        chips = [(1 - x, y), (x, 1 - y), (1 - x, 1 - y)]

        def sends():
            return [_remote(g_in.at[c * 4 + px * 2 + py], g_out.at[c * 4 + px * 2 + py], send.at[k], recv.at[k],
                            (x, y, 1 - c)) for k, (px, py) in enumerate(chips)]

        def recvs():
            return [_remote(g_in.at[(1 - c) * 4 + px * 2 + py], g_out.at[(1 - c) * 4 + px * 2 + py], send.at[k],
                            recv.at[k], (x, y, 1 - c)) for k, (px, py) in enumerate(chips)]

        return sends, recvs, lambda: []

    return CommHook([gathered], [jax.ShapeDtypeStruct(gathered.shape, gathered.dtype)], 3, copies, aliases={0: 0})


def hook_swap_sibling(parts):
    _, n_rows, n_cols = parts.shape

    def copies(ins, outs, send, recv, local):
        (p_ref,), (got_ref,) = ins, outs
        x, y, c = _me()

        def swap():
            return [_remote(p_ref.at[pl.ds((1 - c) * 4, 4)], got_ref, send.at[0], recv.at[0], (x, y, 1 - c))]

        return swap, swap, lambda: []

    return CommHook([parts], [jax.ShapeDtypeStruct((4, n_rows, n_cols), parts.dtype)], 1, copies)


def hook_swap_chips(parts):
    _, n_rows, n_cols = parts.shape

    def copies(ins, outs, send, recv, local):
        (p_ref,), (got_ref,) = ins, outs
        x, y, c = _me()
        chips = [(1 - x, y), (x, 1 - y), (1 - x, 1 - y)]

        def swaps():
            return [_remote(p_ref.at[2 * px + py], got_ref.at[k], send.at[k], recv.at[k], (px, py, c))
                    for k, (px, py) in enumerate(chips)]

        return swaps, swaps, lambda: []

    return CommHook([parts], [jax.ShapeDtypeStruct((3, n_rows, n_cols), parts.dtype)], 3, copies)


def _with_comm(comm, n_args, n_outs):
    if comm is None:
        return [], [], [], [], {}
    aliases = {n_args + a: n_outs + o for a, o in comm.aliases.items()}
    return [_ANY] * len(comm.args), [_ANY] * len(comm.out_shapes), comm.out_shapes, comm.scratch(), aliases


def comm_allreduce_small(name, mine):
    shape = mine.shape

    def body(x_ref, out_ref, buf, send_sems, recv_sems):
        x, y, c = _me()
        my_slot = c * 4 + x * 2 + y
        buf[my_slot] = x_ref[...]
        cps = []
        for k in range(1, N_DEV):
            dx, dy, dc = (k >> 2) & 1, (k >> 1) & 1, k & 1
            px, py, pc = x ^ dx, y ^ dy, c ^ dc
            send = pltpu.make_async_remote_copy(
                src_ref=x_ref, dst_ref=buf.at[my_slot], send_sem=send_sems.at[k - 1], recv_sem=recv_sems.at[k - 1],
                device_id=(px, py, pc), device_id_type=MESH)
            send.start()
            recv = pltpu.make_async_remote_copy(
                src_ref=x_ref, dst_ref=buf.at[pc * 4 + px * 2 + py], send_sem=send_sems.at[k - 1],
                recv_sem=recv_sems.at[k - 1], device_id=(px, py, pc), device_id_type=MESH)
            cps.append((send, recv))
        for send, recv in cps:
            send.wait_send()
            recv.wait_recv()
        total = buf[0]
        for s in range(1, N_DEV):
            total = total + buf[s]
        out_ref[...] = total

    vmem = pl.BlockSpec(memory_space=pltpu.VMEM)
    return pl.pallas_call(
        body, name=name, out_shape=jax.ShapeDtypeStruct(shape, F32), in_specs=[vmem], out_specs=vmem,
        scratch_shapes=[pltpu.VMEM((N_DEV,) + shape, F32), pltpu.SemaphoreType.DMA((7,)), pltpu.SemaphoreType.DMA((7,))],
    )(mine)


def _packed_rows(shape):
    rs, cs = shape
    if cs % PACK_C == 0:
        return rs * (cs // PACK_C)
    if cs > PACK_C // 2:
        return rs
    return rs * cs // PACK_C


def _to_rows(a):
    rs, cs = a.shape
    if cs % PACK_C == 0:
        return jnp.concatenate([a[:, i * PACK_C:(i + 1) * PACK_C] for i in range(cs // PACK_C)], axis=0)
    if cs > PACK_C // 2:
        return jnp.pad(a, ((0, 0), (0, PACK_C - cs)))
    return a.reshape(-1, PACK_C)


def _from_rows(rows, shape):
    rs, cs = shape
    lead = rows.shape[:-2]
    if cs % PACK_C == 0:
        return jnp.concatenate([rows[..., i * rs:(i + 1) * rs, :] for i in range(cs // PACK_C)], axis=-1)
    if cs > PACK_C // 2:
        return rows[..., :cs]
    return rows.reshape(*lead, rs, cs)


def pack_local(shards, names):
    flat = [_to_rows(shards[n]) for n in names]
    rows = sum(f.shape[0] for f in flat)
    pad = (-rows) % 128
    return jnp.concatenate(flat + [jnp.zeros((pad, PACK_C), flat[0].dtype)], axis=0)


def unpack_group(gathered, names, shard_shapes):
    _, n_rows, _ = gathered.shape
    by_block = gathered.reshape(2, 4, n_rows, PACK_C).transpose(1, 0, 2, 3).reshape(N_DEV, n_rows, PACK_C)
    full = {}
    r0 = 0
    for name in names:
        rs, cs = shard_shapes[name]
        nr = _packed_rows((rs, cs))
        piece = _from_rows(by_block[:, r0:r0 + nr, :], (rs, cs))
        r0 += nr
        if name in ROW_SHARDED:
            full[name] = piece.reshape(N_DEV * rs, cs)
        else:
            full[name] = piece.transpose(1, 0, 2).reshape(rs, N_DEV * cs)
    return full


def unpack_a(gathered, shard_shapes):
    full = unpack_group(gathered, GROUP_A, shard_shapes)
    w_in = full.pop("w_in")
    zeros = jnp.zeros((D_MODEL, LAT_W - OFF_FQ - HEADS), w_in.dtype)
    full["w_lat"] = jnp.concatenate([w_in[:, :OFF_FQ], w_in[:, OFF_FF:OFF_G], zeros], axis=1)
    full["w_fox"] = w_in[:, OFF_FQ:OFF_FF]
    full["w_gate"] = w_in[:, OFF_G:]
    w_uq = full.pop("w_uq").reshape(Q_LORA, HEADS, NOPE + ROPE)
    full["w_uq"] = jnp.pad(w_uq, ((0, 0), (0, 0), (0, QK_PAD - NOPE - ROPE))).reshape(Q_LORA, HEADS * QK_PAD)
    w_ukv = full.pop("w_ukv").reshape(KV_LORA, HEADS, 2, NOPE)
    full["w_kv"] = jnp.concatenate([w_ukv[:, :, 0, :].reshape(KV_LORA, HEADS * NOPE),
                                    w_ukv[:, :, 1, :].reshape(KV_LORA, HEADS * HEAD_V)], axis=1)
    return full


def pack_small(vals, loss):
    flat = [vals[n].reshape(-1) for n in SMALL] + [loss.reshape(-1)]
    used = sum(f.shape[0] for f in flat)
    flat.append(jnp.zeros((SMALL_ROWS * PACK_C - used,), F32))
    return jnp.concatenate(flat).reshape(SMALL_ROWS, PACK_C)


def unpack_small(packed):
    flat = packed.reshape(-1)
    out, off = {}, 0
    for n in SMALL:
        out[n] = flat[off:off + SMALL_N[n]]
        off += SMALL_N[n]
    return out, flat[off]


def rope_tables(s_len):
    pos = jnp.arange(s_len, dtype=F32)
    inv = 1.0 / (ROPE_THETA ** (jnp.arange(0, ROPE, 2, dtype=F32) / ROPE))
    ang = pos[:, None] * inv[None, :]
    cos, sin = jnp.cos(ang), jnp.sin(ang)
    zero = jnp.zeros_like(cos)
    c = jnp.concatenate([cos, cos, zero, zero], axis=1)
    s1 = jnp.concatenate([-sin, zero, zero, zero], axis=1)
    s2 = jnp.concatenate([zero, sin, zero, zero], axis=1)
    return (c, s1, s2), (c, -s1, -s2)


def reduce_scatter_tail(parts, from_sibling, from_chips_fn, names):
    cx, cy, cc = _me()
    n_rows = parts.shape[1]
    mine4 = lax.dynamic_slice_in_dim(parts, cc * 4, 4, axis=0)
    pair = add_pairs("rs_pair_sum_" + names, mine4.reshape(4 * n_rows, PACK_C), from_sibling.reshape(4 * n_rows, PACK_C))
    from_chips, extra = from_chips_fn(pair.reshape(4, n_rows, PACK_C))
    own = cx * 2 + cy
    total = add_final("rs_final_sum_" + names, lax.dynamic_index_in_dim(mine4, own, 0, keepdims=False),
                      lax.dynamic_index_in_dim(from_sibling, own, 0, keepdims=False),
                      from_chips[0], from_chips[1], from_chips[2])
    return total, extra


def local_step(x, target, w, small, packed_b, shard_shapes, to_packed_a):
    s_len = x.shape[0]
    tabs, inv_tabs = rope_tables(s_len)
    g_attn = small["attn_norm"].reshape(1, D_MODEL)
    g_q = small["q_norm"].reshape(1, Q_LORA)
    g_kv = small["kv_norm"].reshape(1, KV_LORA)
    g_mlp = small["mlp_norm"].reshape(1, D_MODEL)
    g_final = small["final_norm"].reshape(1, D_MODEL)
    f_bias = small["fox_f_bias"].reshape(HEADS, 1)
    mla_scale = 1.0 / math.sqrt(NOPE + ROPE)
    fox_scale = 1.0 / math.sqrt(HEAD_V)
    mla_offs = (0, 0, HEADS)
    fox_offs = (0, HEADS, 2 * HEADS)

    xn = rms_fwd("rms_attn", x, g_attn)
    lat, = matmul("proj_lat", xn, w["w_lat"], "nn", [F32])
    fox, = matmul("proj_fox", xn, w["w_fox"], "nn", [BF16])
    graw, = matmul("proj_gate", xn, w["w_gate"], "nn", [F32])
    cq = rms_fwd("rms_q", (lat, Q_LORA, 0), g_q)
    ckv = rms_fwd("rms_kv", (lat, KV_LORA, Q_LORA // KV_LORA), g_kv)
    qraw, = matmul("up_q", cq, w["w_uq"], "nn", [F32])
    q = rope_heads("rope_q", qraw, tabs, BF16)
    kvn, = matmul("up_kv", ckv, w["w_kv"], "nn", [BF16])
    kr = rope_block("rope_k", lat, OFF_KR // 128, tabs, BF16)
    k = k_assemble("k_assemble", kvn, kr)
    o_mla, lse_mla, gathered_b = attn_fwd("mla_fwd", q, k, kvn, mla_offs, QK_PAD, CHUNK, mla_scale,
                                          comm=hook_gather_first(packed_b))
    ff_t = lat[:, OFF_FQ:OFF_FQ + HEADS].T
    cum = fox_cum_fwd("fox_cum", ff_t, f_bias).reshape(HEADS, s_len, 1) * LOG2E
    o_fox, lse_fox, gathered_b = attn_fwd("fox_fwd", fox, fox, fox, fox_offs, HEAD_V, 1, fox_scale, cum=cum,
                                          comm=hook_gather_second(gathered_b), split_p=True)
    unpack_b = functools.partial(unpack_group, names=GROUP_B, shard_shapes=shard_shapes)
    w = {**w, **unpack_b(gathered_b)}
    y_mla, = matmul("branch_mla", o_mla, w["w_mla_branch"], "nn", [F32])
    y_fox, = matmul("branch_fox", o_fox, w["w_fox_branch"], "nn", [F32])
    mix = gate_mix("gate_mix", graw, y_mla, y_fox)
    h1, = matmul("out_proj", mix, w["w_out"], "nn", [F32], epilogue=lambda acc, res: (res + acc,), extras=[x])
    hn = rms_fwd("rms_mlp", h1, g_mlp)

    def relu2(acc):
        r = jnp.maximum(acc, 0.0)
        return r * r, r
    u, relu_up = matmul("mlp_up", hn, w["w_up"], "nn", [BF16, BF16], epilogue=relu2)
    h2, = matmul("mlp_down", u, w["w_down"], "nn", [F32], epilogue=lambda acc, res: (res + acc,), extras=[h1])
    dh2, d_final, loss = loss_head("loss_head", h2, target, g_final)

    grads = {}
    dup, = matmul("d_mlp_down", dh2, w["w_down"], "nt", [BF16],
                  epilogue=lambda acc, r: (acc * (2.0 * r.astype(F32)),), extras=[relu_up])
    grads["w_down"], = matmul("gw_down", u, dh2, "tn", [BF16], **TN_TILES)
    dhn, = matmul("d_mlp_up", dup, w["w_up"], "nt", [F32], tk=LONG_K)
    grads["w_up"], = matmul("gw_up", hn, dup, "tn", [BF16], **{**TN_TILES, "tk": LONG_K})
    dh1, d_mlp = rms_bwd("rms_mlp_bwd", h1, dhn, g_mlp, dres=dh2)
    dmix, = matmul("d_out_proj", dh1, w["w_out"], "nt", [F32])
    grads["w_out"], = matmul("gw_out", mix, dh1, "tn", [BF16], **TN_TILES)
    dproj, dy_mla, dy_fox = gate_mix_bwd("gate_mix_bwd", graw, y_mla, y_fox, dmix)
    do_mla, = matmul("d_branch_mla", dy_mla, w["w_mla_branch"], "nt", [BF16])
    grads["w_mla_branch"], = matmul("gw_branch_mla", o_mla, dy_mla, "tn", [BF16], **TN_TILES)
    do_fox, = matmul("d_branch_fox", dy_fox, w["w_fox_branch"], "nt", [BF16])
    grads["w_fox_branch"], = matmul("gw_branch_fox", o_fox, dy_fox, "tn", [BF16], **TN_TILES)

    to_packed_b = jax.linear_transpose(unpack_b, jax.ShapeDtypeStruct(gathered_b.shape, BF16))
    parts_b, = to_packed_b({n: grads.pop(n) for n in GROUP_B})
    delta_mla = attn_delta("mla_delta", do_mla, o_mla)
    dk, dv, dq, from_sibling = attn_bwd("mla_bwd", q, k, kvn, do_mla, lse_mla, delta_mla, mla_offs, QK_PAD, CHUNK,
                                        mla_scale, F32, comm=hook_swap_sibling(parts_b))

    delta_fox = attn_delta("fox_delta", do_fox, o_fox)

    def chips_behind_fox_bwd(pair):
        dfk, dfv, dfq, dcum, from_chips = attn_bwd("fox_bwd", fox, fox, fox, do_fox, lse_fox, delta_fox, fox_offs,
                                                   HEAD_V, 1, fox_scale, BF16, cum=cum, comm=hook_swap_chips(pair))
        return from_chips, (dfq.astype(BF16), dfk, dfv, dcum)
    g_packed_b, (dfq, dfk, dfv, dcum) = reduce_scatter_tail(parts_b, from_sibling, chips_behind_fox_bwd, "b")
    dff_t, d_bias = fox_cum_bwd("fox_cum_bwd", ff_t, f_bias, dcum.reshape(HEADS, s_len))
    dq_r = rope_heads("rope_q_bwd", dq, inv_tabs, BF16)
    dkvn, dkr = dk_split("dk_split", dk, dv, inv_tabs)
    dcq, = matmul("d_up_q", dq_r, w["w_uq"], "nt", [F32])
    grads["w_uq"], = matmul("gw_uq", cq, dq_r, "tn", [BF16], tm=512, tn=2048, tk=2048)
    dckv, = matmul("d_up_kv", dkvn, w["w_kv"], "nt", [F32])
    grads["w_kv"], = matmul("gw_kv", ckv, dkvn, "tn", [BF16], tm=256, tn=2048, tk=2048)
    dcq_raw, d_qn = rms_bwd("rms_q_bwd", (lat, Q_LORA, 0), dcq, g_q, out_dtype=BF16)
    dckv_raw, d_kvn = rms_bwd("rms_kv_bwd", (lat, KV_LORA, Q_LORA // KV_LORA), dckv, g_kv, out_dtype=BF16)

    pad = jnp.zeros((s_len, LAT_W - OFF_FQ - HEADS), BF16)
    tail = jnp.concatenate([dkr[:, :ROPE].astype(BF16), dff_t.T.astype(BF16), pad], axis=1)
    col = GATE_W
    for piece in (dfq, dfk, dfv, dcq_raw, dckv_raw, tail):
        dproj = lax.dynamic_update_slice(dproj, piece, (0, col))
        col += piece.shape[1]
    w_in_p = jnp.concatenate([w["w_gate"], w["w_fox"], w["w_lat"]], axis=1)
    for part, col0, width in (("w_gate", 0, GATE_W), ("w_fox", GATE_W, FOX_W), ("w_lat", GATE_W + FOX_W, LAT_W)):
        grads[part], = matmul("gw_in_" + part, xn, dproj, "tn", [BF16], **{**TN_TILES, "tk": LONG_K},
                              b_cols=(col0, width))

    parts_a = to_packed_a(grads)
    from_sibling_a = comm_swap_sibling("comm_rs_sibling_a", parts_a)

    def chips_behind_d_proj(pair):
        dxn, from_chips = matmul("d_proj", dproj, w_in_p, "nt", [F32], tk=2688, comm=hook_swap_chips(pair))
        return from_chips, dxn
    g_packed_a, dxn = reduce_scatter_tail(parts_a, from_sibling_a, chips_behind_d_proj, "a")
    dx, d_attn = rms_bwd("rms_attn_bwd", x, dxn, g_attn, dres=dh1)

    small_grads = {"attn_norm": d_attn, "fox_f_bias": d_bias, "q_norm": d_qn, "kv_norm": d_kvn,
                   "mlp_norm": d_mlp, "final_norm": d_final}
    return loss, dx, g_packed_a, g_packed_b, small_grads


def kernel(x, attn_norm, w_in, fox_f_bias, q_norm, w_uq, kv_norm, w_ukv, w_mla_branch, w_fox_branch, w_out, mlp_norm, w_up, w_down, final_norm, loss_target, m_attn_norm, m_w_in, m_fox_f_bias, m_q_norm, m_w_uq, m_kv_norm, m_w_ukv, m_w_mla_branch, m_w_fox_branch, m_w_out, m_mlp_norm, m_w_up, m_w_down, m_final_norm, v_attn_norm, v_w_in, v_fox_f_bias, v_q_norm, v_w_uq, v_kv_norm, v_w_ukv, v_w_mla_branch, v_w_fox_branch, v_w_out, v_mlp_norm, v_w_up, v_w_down, v_final_norm):
    given = dict(locals())
    big = {n: given[n][0] for n in BIG}
    small = {n: given[n] for n in SMALL}
    shard_shapes = {n: tuple(big[n].shape) for n in BIG}

    packed_a = pack_local({n: big[n].astype(BF16) for n in GROUP_A}, GROUP_A)
    packed_b = pack_local({n: big[n].astype(BF16) for n in GROUP_B}, GROUP_B)
    gathered_a = comm_allgather("comm_allgather_a", packed_a)
    unpack = functools.partial(unpack_a, shard_shapes=shard_shapes)
    w_a = unpack(gathered_a)

    transpose_a = jax.linear_transpose(unpack, jax.ShapeDtypeStruct(gathered_a.shape, BF16))
    loss_part, dx, g_packed_a, g_packed_b, small_grads = local_step(
        x[0], loss_target[0], w_a, small, packed_b, shard_shapes, lambda grads: transpose_a(grads)[0])

    small_sum = comm_allreduce_small("comm_allreduce_small", pack_small(small_grads, loss_part[0, 0]))
    g_small, loss = unpack_small(small_sum)

    grad_w, delta_w, new_m, new_v = {}, {}, {}, {}
    for names, g_packed in ((GROUP_A, g_packed_a), (GROUP_B, g_packed_b)):
        r0 = 0
        for n in names:
            rs, cs = shard_shapes[n]
            nr = _packed_rows((rs, cs))
            g = _from_rows(g_packed[r0:r0 + nr], (rs, cs))
            r0 += nr
            d, m_new, v_new = adamw("adamw_" + n, big[n], g, given["m_" + n][0], given["v_" + n][0])
            grad_w[n], delta_w[n], new_m[n], new_v[n] = g[None], d[None], m_new[None], v_new[None]
    zero = jnp.zeros((), F32)
    d_s, m_s, v_s = adamw("adamw_small", pack_small(small, zero), small_sum * _small_mask(),
                          pack_small({n: given["m_" + n] for n in SMALL}, zero),
                          pack_small({n: given["v_" + n] for n in SMALL}, zero), tr=SMALL_ROWS)
    d_small, _ = unpack_small(d_s)
    m_small, _ = unpack_small(m_s)
    v_small, _ = unpack_small(v_s)
    for n in SMALL:
        shape = given[n].shape
        grad_w[n], delta_w[n] = g_small[n].reshape(shape), d_small[n].reshape(shape)
        new_m[n], new_v[n] = m_small[n].reshape(shape), v_small[n].reshape(shape)

    order = ["attn_norm", "w_in", "fox_f_bias", "q_norm", "w_uq", "kv_norm", "w_ukv", "w_mla_branch", "w_fox_branch",
             "w_out", "mlp_norm", "w_up", "w_down", "final_norm"]
    return (loss, dx[None], *[grad_w[n] for n in order], *[delta_w[n] for n in order],
            *[new_m[n] for n in order], *[new_v[n] for n in order])


def _small_mask():
    used = sum(SMALL_N[n] for n in SMALL)
    return (jnp.arange(SMALL_ROWS * PACK_C) < used).astype(F32).reshape(SMALL_ROWS, PACK_C)
```

```python
import functools
import math

import jax
import jax.numpy as jnp
from jax import lax
from jax.experimental import pallas as pl
from jax.experimental.pallas import tpu as pltpu

F32 = jnp.float32
BF16 = jnp.bfloat16
MESH = pl.DeviceIdType.MESH

D_MODEL = 2048
HEADS = 8
Q_LORA = 512
KV_LORA = 256
NOPE = 128
ROPE = 64
HEAD_V = 128
D_FF = 4 * D_MODEL
CHUNK = 64
EPS = 1e-6
ROPE_THETA = 10000.0
OFF_KR = Q_LORA + KV_LORA
OFF_FQ = OFF_KR + ROPE
OFF_FF = OFF_FQ + 3 * HEADS * HEAD_V
OFF_G = OFF_FF + HEADS
D_IN = OFF_G + 2 * D_MODEL

LAT_W = 896
FOX_W = 3 * HEADS * HEAD_V
GATE_W = 2 * D_MODEL
PROJ_W = LAT_W + FOX_W + GATE_W
QK_PAD = 256

ADAM_LR = 0.001
ADAM_B1 = 0.9
ADAM_B2 = 0.999
ADAM_EPS = 1e-08
ADAM_WD = 0.01
ADAM_STEP = 10

N_DEV = 8
PACK_C = 1024
NEG = -1e30
LOG2E = math.log2(math.e)

VMEM_LIMIT = 56 * 1024 * 1024

BIG = ("w_in", "w_uq", "w_ukv", "w_mla_branch", "w_fox_branch", "w_out", "w_up", "w_down")
GROUP_A = ("w_in", "w_uq", "w_ukv")
GROUP_B = ("w_mla_branch", "w_fox_branch", "w_out", "w_up", "w_down")
ROW_SHARDED = ("w_out", "w_down")
SMALL = ("attn_norm", "fox_f_bias", "q_norm", "kv_norm", "mlp_norm", "final_norm")
SMALL_N = {"attn_norm": D_MODEL, "fox_f_bias": HEADS, "q_norm": Q_LORA, "kv_norm": KV_LORA,
           "mlp_norm": D_MODEL, "final_norm": D_MODEL}
SMALL_ROWS = 8


def _params(sem):
    return pltpu.CompilerParams(dimension_semantics=sem, vmem_limit_bytes=VMEM_LIMIT)


def _rows(name, fn, row_ins, const_ins, outs, reds=(), tr=256):
    norm = [(a, a.shape[1], 0) if not isinstance(a, tuple) else a for a in row_ins]
    n_rows = norm[0][0].shape[0]
    tr = min(tr, n_rows)
    assert n_rows % tr == 0, (name, n_rows, tr)
    n_in, n_out, n_red = len(norm) + len(const_ins), len(outs), len(reds)

    def body(*refs):
        vals = [r[...] for r in refs[:n_in]]
        out_refs = refs[n_in:n_in + n_out]
        red_refs = refs[n_in + n_out:]
        out_vals, red_vals = fn(*vals)
        for r, v in zip(out_refs, out_vals):
            r[...] = v.astype(r.dtype)
        if n_red:
            @pl.when(pl.program_id(0) == 0)
            def _():
                for r in red_refs:
                    r[...] = jnp.zeros_like(r)
            for r, v in zip(red_refs, red_vals):
                r[...] += v

    in_specs = [pl.BlockSpec((tr, w), functools.partial(lambda i, cb: (i, cb), cb=cb)) for _, w, cb in norm]
    in_specs += [pl.BlockSpec(a.shape, lambda i: (0, 0)) for a in const_ins]
    out_specs = [pl.BlockSpec((tr, o[0]), lambda i: (i, 0)) for o in outs]
    out_specs += [pl.BlockSpec((1, c), lambda i: (0, 0)) for c in reds]
    out_shape = [jax.ShapeDtypeStruct((n_rows, o[2] if len(o) > 2 else o[0]), o[1]) for o in outs]
    out_shape += [jax.ShapeDtypeStruct((1, c), F32) for c in reds]
    res = pl.pallas_call(
        body, name=name, grid=(n_rows // tr,), in_specs=in_specs, out_specs=out_specs, out_shape=out_shape,
        compiler_params=_params(("arbitrary",)),
    )(*[a for a, _, _ in norm], *const_ins)
    return res


def _rstd(x):
    return lax.rsqrt(jnp.mean(x * x, axis=-1, keepdims=True) + EPS)


def rms_fwd(name, x, gain, tr=256):
    width = x[1] if isinstance(x, tuple) else x.shape[1]

    def fn(xv, g):
        return (xv * _rstd(xv) * g,), ()
    return _rows(name, fn, [x], [gain], [(width, BF16)], tr=tr)[0]


def rms_bwd(name, x, dy, gain, dres=None, out_dtype=F32, tr=256):
    width = x[1] if isinstance(x, tuple) else x.shape[1]

    def fn(xv, dyv, *rest):
        g = rest[-1]
        r = _rstd(xv)
        n = xv * r
        dyv = dyv.astype(F32)
        dn = dyv * g
        dx = r * (dn - n * jnp.mean(dn * n, axis=-1, keepdims=True))
        if dres is not None:
            dx = dx + rest[0]
        return (dx,), (jnp.sum(dyv * n, axis=0, keepdims=True),)

    ins = [x, dy] + ([dres] if dres is not None else [])
    return _rows(name, fn, ins, [gain], [(width, out_dtype)], [width], tr=tr)


def _rope_lanes(t, c, s1, s2):
    return t * c + pltpu.roll(t, 96, 1) * s1 + pltpu.roll(t, 32, 1) * s2


def rope_heads(name, x, tabs, out_dtype):
    def fn(xv, c, s1, s2):
        xv = xv.astype(F32)
        parts = []
        for h in range(HEADS):
            parts.append(xv[:, h * QK_PAD:h * QK_PAD + NOPE])
            parts.append(_rope_lanes(xv[:, h * QK_PAD + NOPE:(h + 1) * QK_PAD], c, s1, s2))
        return (jnp.concatenate(parts, axis=1),), ()
    return _rows(name, fn, [x, *tabs], [], [(HEADS * QK_PAD, out_dtype)])[0]


def rope_block(name, x, col_block, tabs, out_dtype):
    def fn(xv, c, s1, s2):
        return (_rope_lanes(xv.astype(F32), c, s1, s2),), ()
    return _rows(name, fn, [(x, 128, col_block), *tabs], [], [(128, out_dtype)])[0]


def k_assemble(name, kvn, kr):
    def fn(knp, krv):
        parts = []
        for h in range(HEADS):
            parts.append(knp[:, h * NOPE:(h + 1) * NOPE])
            parts.append(krv)
        return (jnp.concatenate(parts, axis=1),), ()
    return _rows(name, fn, [(kvn, HEADS * NOPE, 0), kr], [], [(HEADS * QK_PAD, BF16)])[0]


def dk_split(name, dk, dv, inv_tabs):
    def fn(dkv, dvv, c, s1, s2):
        parts = []
        acc = None
        for h in range(HEADS):
            parts.append(dkv[:, h * QK_PAD:h * QK_PAD + NOPE].astype(BF16))
            t = dkv[:, h * QK_PAD + NOPE:(h + 1) * QK_PAD]
            acc = t if acc is None else acc + t
        parts.append(dvv)
        return (jnp.concatenate(parts, axis=1), _rope_lanes(acc, c, s1, s2)), ()
    return _rows(name, fn, [dk, dv, *inv_tabs], [], [(2 * HEADS * NOPE, BF16), (128, F32)])


def gate_mix(name, graw, y_mla, y_fox):
    def fn(g, ya, yb):
        ga = jax.nn.sigmoid(g[:, :D_MODEL])
        gb = jax.nn.sigmoid(g[:, D_MODEL:])
        return (ga * ya + gb * yb,), ()
    return _rows(name, fn, [graw, y_mla, y_fox], [], [(D_MODEL, BF16)])[0]


def gate_mix_bwd(name, graw, y_mla, y_fox, dmix):
    def fn(g, ya, yb, dm):
        ga = jax.nn.sigmoid(g[:, :D_MODEL])
        gb = jax.nn.sigmoid(g[:, D_MODEL:])
        dgraw = jnp.concatenate([dm * ya * ga * (1.0 - ga), dm * yb * gb * (1.0 - gb)], axis=1)
        return (dgraw, dm * ga, dm * gb), ()
    return _rows(name, fn, [graw, y_mla, y_fox, dmix], [],
                 [(GATE_W, BF16, PROJ_W), (D_MODEL, BF16), (D_MODEL, BF16)], tr=256)


def loss_head(name, h2, target, gain):
    inv_d = 1.0 / D_MODEL

    def fn(h, t, g):
        r = _rstd(h)
        n = h * r
        err = n * g - t
        dy = err * inv_d
        dn = dy * g
        dh = r * (dn - n * jnp.mean(dn * n, axis=-1, keepdims=True))
        part = 0.5 * inv_d * jnp.sum(jnp.sum(err * err, axis=1, keepdims=True), axis=0, keepdims=True)
        return (dh, dh), (jnp.sum(dy * n, axis=0, keepdims=True), jnp.broadcast_to(part, (1, 128)))
    return _rows(name, fn, [h2, target], [gain], [(D_MODEL, F32), (D_MODEL, BF16)], [D_MODEL, 128])


def adamw(name, w, g, m, v, tr=256):
    c1 = 1.0 - ADAM_B1 ** ADAM_STEP
    c2 = 1.0 - ADAM_B2 ** ADAM_STEP

    def fn(wv, gv, mv, vv):
        m_new = ADAM_B1 * mv + (1.0 - ADAM_B1) * gv
        v_new = ADAM_B2 * vv + (1.0 - ADAM_B2) * (gv * gv)
        delta = -ADAM_LR * ((m_new / c1) / (jnp.sqrt(v_new / c2) + ADAM_EPS) + ADAM_WD * wv)
        return (delta, m_new, v_new), ()
    cols = w.shape[1]
    return _rows(name, fn, [w, g, m, v], [], [(cols, F32)] * 3, tr=tr)


def _row_tile(n_rows, cap=640):
    return max(t for t in range(16, cap + 1, 16) if n_rows % t == 0)


def add_pairs(name, a, b):
    def fn(av, bv):
        return (av.astype(F32) + bv.astype(F32),), ()
    return _rows(name, fn, [a, b], [], [(a.shape[1], BF16)], tr=_row_tile(a.shape[0]))[0]


def add_final(name, a, b, r0, r1, r2):
    def fn(av, bv, r0v, r1v, r2v):
        return (((av.astype(F32) + bv.astype(F32)) + r0v.astype(F32)) + r1v.astype(F32) + r2v.astype(F32),), ()
    return _rows(name, fn, [a, b, r0, r1, r2], [], [(a.shape[1], F32)], tr=_row_tile(a.shape[0]))[0]


TN_TILES = dict(tm=1024, tn=1024, tk=2048)
LONG_K = 4096
_DIMS = {"nn": (((1,), (0,)), ((), ())), "nt": (((1,), (1,)), ((), ())), "tn": (((0,), (0,)), ((), ()))}


def matmul(name, a, b, mode, outs, epilogue=None, extras=(), tm=1024, tn=1024, tk=2048, comm=None, b_cols=None):
    if mode == "tn":
        kdim, m = a.shape
    else:
        m, kdim = a.shape
    n = b.shape[0] if mode == "nt" else b.shape[1]
    if b_cols is not None:
        assert mode != "nt"
        col0, n = b_cols
        tn = min(tn, n)
        assert col0 % tn == 0, (name, b_cols, tn)
    j0 = 0 if b_cols is None else col0 // tn
    tm, tn, tk = min(tm, m), min(tn, n), min(tk, kdim)
    assert m % tm == 0 and n % tn == 0 and kdim % tk == 0, (name, a.shape, b.shape)
    nk = kdim // tk
    n_ex, n_out = len(extras), len(outs)
    dims = _DIMS[mode]
    grid = (m // tm, n // tn, nk)

    def body(*refs):
        ins, c_in, out_refs, c_out, scr, c_sems = _split_refs(refs, 2 + n_ex, comm, n_out, 1 if nk > 1 else 0)
        a_ref, b_ref = ins[:2]
        ex_refs = ins[2:]
        _comm_start(comm, c_in, c_out, c_sems, *grid)

        def finish(acc):
            vals = (acc,) if epilogue is None else epilogue(acc, *[r[...] for r in ex_refs])
            for r, v in zip(out_refs, vals):
                r[...] = v.astype(r.dtype)

        part = lax.dot_general(a_ref[...].astype(BF16), b_ref[...].astype(BF16), dims, preferred_element_type=F32)
        if nk == 1:
            finish(part)
        else:
            acc_ref = scr[0]
            k = pl.program_id(2)

            @pl.when(k == 0)
            def _():
                acc_ref[...] = part

            @pl.when(k > 0)
            def _():
                acc_ref[...] += part

            @pl.when(k == nk - 1)
            def _():
                finish(acc_ref[...])
        _comm_wait(comm, c_in, c_out, c_sems, *grid)

    a_spec = pl.BlockSpec((tk, tm), lambda i, j, k: (k, i)) if mode == "tn" else pl.BlockSpec((tm, tk), lambda i, j, k: (i, k))
    b_spec = (pl.BlockSpec((tn, tk), lambda i, j, k: (j, k)) if mode == "nt"
              else pl.BlockSpec((tk, tn), lambda i, j, k: (k, j + j0)))
    tile = pl.BlockSpec((tm, tn), lambda i, j, k: (i, j))
    c_ins, c_outs, c_shapes, c_scratch, aliases = _with_comm(comm, 2 + n_ex, n_out)
    sem = ("arbitrary",) * 3 if comm else ("parallel", "parallel", "arbitrary")
    res = pl.pallas_call(
        body, name=name, grid=grid,
        in_specs=[a_spec, b_spec] + [tile] * n_ex + c_ins,
        out_specs=[tile] * n_out + c_outs,
        out_shape=[jax.ShapeDtypeStruct((m, n), dt) for dt in outs] + c_shapes,
        scratch_shapes=([pltpu.VMEM((tm, tn), F32)] if nk > 1 else []) + c_scratch,
        input_output_aliases=aliases,
        compiler_params=_params(sem),
    )(a, b, *extras, *(comm.args if comm else []))
    return res


_NT = (((1,), (1,)), ((), ()))
_NN = (((1,), (0,)), ((), ()))


def _mask(bq, chunk, transposed, row0=0, shape=None):
    shape = (bq, bq) if shape is None else shape
    row = lax.broadcasted_iota(jnp.int32, shape, 0) + row0
    col = lax.broadcasted_iota(jnp.int32, shape, 1)
    if chunk > 1:
        row, col = row // chunk, col // chunk
    return (row <= col) if transposed else (col <= row)


def _row_layout(a, bq):
    h, s, _ = a.shape
    return a.reshape(h, s // bq, 1, bq)


def _split_refs(refs, n_in, comm, n_out, n_scr):
    n_ci = len(comm.args) if comm else 0
    n_co = len(comm.out_shapes) if comm else 0
    cuts = [n_in, n_ci, n_out, n_co, n_scr, 3 if comm else 0]
    parts, at = [], 0
    for n in cuts:
        parts.append(list(refs[at:at + n]))
        at += n
    assert at == len(refs), (at, len(refs))
    return parts


def _at_step(grid, last):
    hit = None
    for axis, n in enumerate(grid):
        here = pl.program_id(axis) == (n - 1 if last else 0)
        hit = here if hit is None else jnp.logical_and(hit, here)
    return hit


def _comm_start(comm, c_in, c_out, c_sems, *grid):
    if comm is not None:
        @pl.when(_at_step(grid, False))
        def _():
            comm.start(c_in, c_out, c_sems)


def _comm_wait(comm, c_in, c_out, c_sems, *grid):
    if comm is not None:
        @pl.when(_at_step(grid, True))
        def _():
            comm.wait(c_in, c_out, c_sems)


def attn_fwd(name, q, k, v, offs, dqk, chunk, scale, cum=None, bq=512, comm=None, split_p=False):
    s_len = q.shape[0]
    nq = s_len // bq
    qoff, koff, voff = offs
    has_bias = cum is not None
    scale2 = scale * LOG2E

    n_in = 5 if has_bias else 3

    def body(*refs):
        ins, c_in, outs, c_out, scr, c_sems = _split_refs(refs, n_in, comm, 2, 3)
        q_ref, k_ref, v_ref = ins[:3]
        if has_bias:
            cc_ref, cr_ref = ins[3:]
        o_ref, lse_ref = outs
        m_s, l_s, acc_s = scr
        _comm_start(comm, c_in, c_out, c_sems, HEADS, nq)
        i = pl.program_id(1)
        qv = q_ref[...]
        m_s[...] = jnp.full_like(m_s, NEG)
        l_s[...] = jnp.zeros_like(l_s)
        acc_s[...] = jnp.zeros_like(acc_s)

        def step(j, masked):
            off = pl.multiple_of(j * bq, bq)
            kj = k_ref[pl.ds(off, bq), :]
            vj = v_ref[pl.ds(off, bq), :]
            st = lax.dot_general(kj, qv, _NT, preferred_element_type=F32) * scale2
            if has_bias:
                st = st + cr_ref[...] - cc_ref[pl.ds(off, bq), :]
            if masked:
                st = jnp.where(_mask(bq, chunk, True), st, NEG)
            m_prev = m_s[...]
            m_new = jnp.maximum(m_prev, jnp.max(st, axis=0, keepdims=True))
            alpha = jnp.exp2(m_prev - m_new)
            pt = jnp.exp2(st - m_new)
            l_s[...] = alpha * l_s[...] + jnp.sum(pt, axis=0, keepdims=True)
            p_hi = pt.astype(BF16)
            pv = lax.dot_general(vj, p_hi, _DIMS["tn"], preferred_element_type=F32)
            if split_p:
                p_lo = (pt - p_hi.astype(F32)).astype(BF16)
                pv = pv + lax.dot_general(vj, p_lo, _DIMS["tn"], preferred_element_type=F32)
            acc_s[...] = alpha * acc_s[...] + pv
            m_s[...] = m_new

        def pair_body(jj, carry):
            step(2 * jj, False)
            step(2 * jj + 1, False)
            return carry

        lax.fori_loop(0, i // 2, pair_body, 0)

        @pl.when(i % 2 == 1)
        def _():
            step(i - 1, False)

        step(i, True)
        o_ref[...] = (acc_s[...] / l_s[...]).T.astype(o_ref.dtype)
        lse_ref[...] = m_s[...] + jnp.log2(l_s[...])
        _comm_wait(comm, c_in, c_out, c_sems, HEADS, nq)

    in_specs = [
        pl.BlockSpec((bq, dqk), lambda h, i: (i, qoff + h)),
        pl.BlockSpec((s_len, dqk), lambda h, i: (0, koff + h)),
        pl.BlockSpec((s_len, HEAD_V), lambda h, i: (0, voff + h)),
    ]
    args = [q, k, v]
    if has_bias:
        in_specs += [pl.BlockSpec((None, s_len, 1), lambda h, i: (h, 0, 0)),
                     pl.BlockSpec((None, None, 1, bq), lambda h, i: (h, i, 0, 0))]
        args += [cum, _row_layout(cum, bq)]
    c_ins, c_outs, c_shapes, c_scratch, aliases = _with_comm(comm, len(args), 2)
    o, lse_rows, *comm_out = pl.pallas_call(
        body, name=name, grid=(HEADS, nq), in_specs=in_specs + c_ins,
        out_specs=[pl.BlockSpec((bq, HEAD_V), lambda h, i: (i, h)),
                   pl.BlockSpec((None, None, 1, bq), lambda h, i: (h, i, 0, 0))] + c_outs,
        out_shape=[jax.ShapeDtypeStruct((s_len, HEADS * HEAD_V), F32),
                   jax.ShapeDtypeStruct((HEADS, nq, 1, bq), F32)] + c_shapes,
        scratch_shapes=[pltpu.VMEM((1, bq), F32), pltpu.VMEM((1, bq), F32), pltpu.VMEM((HEAD_V, bq), F32)] + c_scratch,
        input_output_aliases=aliases,
        compiler_params=_params(("arbitrary", "arbitrary")),
    )(*args, *(comm.args if comm else []))
    return (o, lse_rows, *comm_out)


def _bwd_block(kv, vv, qi, doi, lse_row, scale2, bias, masked, bq, chunk):
    st = lax.dot_general(kv, qi, _NT, preferred_element_type=F32) * scale2
    if bias is not None:
        st = st + bias[0] - bias[1]
    if masked:
        st = jnp.where(_mask(bq, chunk, True), st, NEG)
    pt = jnp.exp2(st - lse_row)
    dpt = lax.dot_general(vv, doi, _NT, preferred_element_type=F32)
    return pt, dpt


def attn_delta(name, do, o, bq=512):
    s_len = do.shape[0]
    nq = s_len // bq

    def body(do_ref, o_ref, out_ref):
        ones = jnp.ones((8, HEAD_V), BF16)
        for h in range(HEADS):
            cols = slice(h * HEAD_V, (h + 1) * HEAD_V)
            prod = do_ref[:, cols].astype(F32) * o_ref[:, cols].astype(F32)
            total = None
            for part in reversed(_split3(prod)):
                term = lax.dot_general(ones, part, _NT, preferred_element_type=F32)
                total = term if total is None else total + term
            out_ref[h] = total[0:1, :]

    blk = pl.BlockSpec((bq, HEADS * HEAD_V), lambda i: (i, 0))
    return pl.pallas_call(
        body, name=name, grid=(nq,), in_specs=[blk, blk],
        out_specs=pl.BlockSpec((HEADS, None, 1, bq), lambda i: (0, i, 0, 0)),
        out_shape=jax.ShapeDtypeStruct((HEADS, nq, 1, bq), F32),
        compiler_params=_params(("parallel",)),
    )(do, o)


def attn_bwd(name, q, k, v, do, lse, delta, offs, dqk, chunk, scale, out_dtype, cum=None, bq=512, comm=None):
    s_len = q.shape[0]
    nq = s_len // bq
    qoff, koff, voff = offs
    has_bias = cum is not None
    scale2 = scale * LOG2E
    n_in, n_out = (8, 4) if has_bias else (6, 3)

    def body(*refs):
        ins, c_in, outs, c_out, scr, c_sems = _split_refs(refs, n_in, comm, n_out, n_out - 1)
        k_ref, v_ref, q_ref, do_ref, lse_ref, delta_ref = ins[:6]
        dk_ref, dv_ref, dq_ref = outs[:3]
        dk_s, dv_s = scr[:2]
        if has_bias:
            cc_ref, cr_ref = ins[6:]
            dc_ref, dc_s = outs[3], scr[2]
        _comm_start(comm, c_in, c_out, c_sems, HEADS, nq)
        j = pl.program_id(1)
        kv = k_ref[...]
        vv = v_ref[...]
        dk_s[...] = jnp.zeros_like(dk_s)
        dv_s[...] = jnp.zeros_like(dv_s)
        if has_bias:
            dc_s[...] = jnp.zeros_like(dc_s)

        @pl.when(j == 0)
        def _():
            dq_ref[...] = jnp.zeros_like(dq_ref)

        def step(i, masked):
            off = pl.multiple_of(i * bq, bq)
            qi = q_ref[pl.ds(off, bq), :]
            doi = do_ref[pl.ds(off, bq), :].astype(BF16)
            bias = (cr_ref[i], cc_ref[...]) if has_bias else None
            pt, dpt = _bwd_block(kv, vv, qi, doi, lse_ref[i], scale2, bias, masked, bq, chunk)
            dv_s[...] += lax.dot_general(pt.astype(BF16), doi, _NN, preferred_element_type=F32)
            dst = pt * (dpt - delta_ref[i])
            if has_bias:
                dc_s[...] -= jnp.sum(dst, axis=-1, keepdims=True)
            dst = dst.astype(BF16)
            dk_s[...] += lax.dot_general(dst, qi, _NN, preferred_element_type=F32)
            dq_ref[pl.ds(off, bq), :] += lax.dot_general(dst, kv, _DIMS["tn"], preferred_element_type=F32) * scale

        step(j, True)

        def loop_body(i, carry):
            step(i, False)
            return carry

        lax.fori_loop(j + 1, nq, loop_body, 0)
        dk_ref[...] = (dk_s[...] * scale).astype(dk_ref.dtype)
        dv_ref[...] = dv_s[...].astype(dv_ref.dtype)
        if has_bias:
            dc_ref[...] = dc_s[...]
        _comm_wait(comm, c_in, c_out, c_sems, HEADS, nq)

    rows = pl.BlockSpec((None, nq, 1, bq), lambda h, j: (h, 0, 0, 0))
    in_specs = [
        pl.BlockSpec((bq, dqk), lambda h, j: (j, koff + h)),
        pl.BlockSpec((bq, HEAD_V), lambda h, j: (j, voff + h)),
        pl.BlockSpec((s_len, dqk), lambda h, j: (0, qoff + h)),
        pl.BlockSpec((s_len, HEAD_V), lambda h, j: (0, h)),
        rows,
        rows,
    ]
    args = [k, v, q, do, lse, delta]
    out_specs = [pl.BlockSpec((bq, dqk), lambda h, j: (j, h)), pl.BlockSpec((bq, HEAD_V), lambda h, j: (j, h)),
                 pl.BlockSpec((s_len, dqk), lambda h, j: (0, h))]
    out_shape = [jax.ShapeDtypeStruct((s_len, HEADS * dqk), out_dtype),
                 jax.ShapeDtypeStruct((s_len, HEADS * HEAD_V), BF16),
                 jax.ShapeDtypeStruct((s_len, HEADS * dqk), F32)]
    scratch = [pltpu.VMEM((bq, dqk), F32), pltpu.VMEM((bq, HEAD_V), F32)]
    if has_bias:
        in_specs += [pl.BlockSpec((None, bq, 1), lambda h, j: (h, j, 0)), rows]
        args += [cum, _row_layout(cum, bq)]
        out_specs.append(pl.BlockSpec((None, bq, 1), lambda h, j: (h, j, 0)))
        out_shape.append(jax.ShapeDtypeStruct((HEADS, s_len, 1), F32))
        scratch.append(pltpu.VMEM((bq, 1), F32))
    c_ins, c_outs, c_shapes, c_scratch, aliases = _with_comm(comm, len(args), n_out)
    return pl.pallas_call(
        body, name=name, grid=(HEADS, nq), in_specs=in_specs + c_ins, out_specs=out_specs + c_outs,
        out_shape=out_shape + c_shapes, scratch_shapes=scratch + c_scratch, input_output_aliases=aliases,
        compiler_params=_params(("arbitrary", "arbitrary")),
    )(*args, *(comm.args if comm else []))


_CUM_BLK = 512


def _split3(x):
    hi = x.astype(BF16)
    r1 = x - hi.astype(F32)
    mid = r1.astype(BF16)
    lo = (r1 - mid.astype(F32)).astype(BF16)
    return hi, mid, lo


def _tri_dot(x, tri):
    hi, mid, lo = _split3(x)
    out = lax.dot_general(lo, tri, _NN, preferred_element_type=F32)
    out = out + lax.dot_general(mid, tri, _NN, preferred_element_type=F32)
    return out + lax.dot_general(hi, tri, _NN, preferred_element_type=F32)


def fox_cum_fwd(name, ff_t, bias):
    s_len = ff_t.shape[1]
    nb = s_len // _CUM_BLK

    def body(ff_ref, b_ref, cum_ref):
        row = lax.broadcasted_iota(jnp.int32, (_CUM_BLK, _CUM_BLK), 0)
        col = lax.broadcasted_iota(jnp.int32, (_CUM_BLK, _CUM_BLK), 1)
        tri = (row <= col).astype(BF16)
        carry = jnp.zeros((HEADS, 1), F32)
        for b in range(nb):
            z = ff_ref[:, b * _CUM_BLK:(b + 1) * _CUM_BLK] + b_ref[...]
            logf = jnp.minimum(z, 0.0) - jnp.log1p(jnp.exp(-jnp.abs(z)))
            blk = _tri_dot(logf, tri) + carry
            cum_ref[:, b * _CUM_BLK:(b + 1) * _CUM_BLK] = blk
            carry = blk[:, _CUM_BLK - 1:_CUM_BLK]

    return pl.pallas_call(
        body, name=name, out_shape=jax.ShapeDtypeStruct((HEADS, s_len), F32),
        compiler_params=pltpu.CompilerParams(vmem_limit_bytes=VMEM_LIMIT),
    )(ff_t, bias)


def fox_cum_bwd(name, ff_t, bias, dcum):
    s_len = ff_t.shape[1]
    nb = s_len // _CUM_BLK

    def body(ff_ref, b_ref, dc_ref, dff_ref, db_ref):
        row = lax.broadcasted_iota(jnp.int32, (_CUM_BLK, _CUM_BLK), 0)
        col = lax.broadcasted_iota(jnp.int32, (_CUM_BLK, _CUM_BLK), 1)
        tri = (row >= col).astype(BF16)
        carry = jnp.zeros((HEADS, 1), F32)
        dbias = jnp.zeros((HEADS, 1), F32)
        for b in reversed(range(nb)):
            sl = slice(b * _CUM_BLK, (b + 1) * _CUM_BLK)
            dlogf = _tri_dot(dc_ref[:, sl], tri) + carry
            carry = dlogf[:, 0:1]
            z = ff_ref[:, sl] + b_ref[...]
            dz = dlogf / (1.0 + jnp.exp(z))
            dff_ref[:, sl] = dz
            dbias = dbias + jnp.sum(dz, axis=-1, keepdims=True)
        db_ref[...] = dbias

    return pl.pallas_call(
        body, name=name,
        out_shape=[jax.ShapeDtypeStruct((HEADS, s_len), F32), jax.ShapeDtypeStruct((HEADS, 1), F32)],
        compiler_params=pltpu.CompilerParams(vmem_limit_bytes=VMEM_LIMIT),
    )(ff_t, bias, dcum)


_ANY = pl.BlockSpec(memory_space=pl.ANY)


def _me():
    return lax.axis_index("x"), lax.axis_index("y"), lax.axis_index("c")


def comm_allgather(name, mine):
    n_rows, n_cols = mine.shape

    def body(x_ref, out_ref, send_sems, recv_sems, local_sem):
        x, y, c = _me()
        sibling = (x, y, 1 - c)
        chips = [(1 - x, y), (x, 1 - y), (1 - x, 1 - y)]

        def blk(px, py, pc):
            return out_ref.at[pc * 4 + px * 2 + py]

        def copy(k, block, to, src=None):
            return pltpu.make_async_remote_copy(
                src_ref=blk(*block) if src is None else src, dst_ref=blk(*block),
                send_sem=send_sems.at[k], recv_sem=recv_sems.at[k], device_id=to, device_id_type=MESH)

        own = pltpu.make_async_copy(x_ref, blk(x, y, c), local_sem)
        own.start()
        first = [copy(0, (x, y, c), sibling, src=x_ref)]
        first += [copy(1 + j, (x, y, c), (*chip, c), src=x_ref) for j, chip in enumerate(chips)]
        for cp in first:
            cp.start()
        passed = [copy(4 + j, (*chip, c), sibling) for j, chip in enumerate(chips)]
        for j, chip in enumerate(chips):
            copy(1 + j, (*chip, c), (x, y, c)).wait_recv()
            passed[j].start()
        copy(0, sibling, (x, y, c)).wait_recv()
        for j, chip in enumerate(chips):
            copy(4 + j, (*chip, 1 - c), (x, y, c)).wait_recv()
        for cp in first + passed:
            cp.wait_send()
        own.wait()

    return pl.pallas_call(
        body, name=name, out_shape=jax.ShapeDtypeStruct((N_DEV, n_rows, n_cols), mine.dtype),
        in_specs=[_ANY], out_specs=_ANY,
        scratch_shapes=[pltpu.SemaphoreType.DMA((7,)), pltpu.SemaphoreType.DMA((7,)), pltpu.SemaphoreType.DMA],
    )(mine)


def comm_swap_sibling(name, parts):
    _, n_rows, n_cols = parts.shape

    def body(p_ref, got_ref, send_sem, recv_sem):
        x, y, c = _me()
        cp = pltpu.make_async_remote_copy(
            src_ref=p_ref.at[pl.ds((1 - c) * 4, 4)], dst_ref=got_ref, send_sem=send_sem, recv_sem=recv_sem,
            device_id=(x, y, 1 - c), device_id_type=MESH)
        cp.start()
        cp.wait()

    return pl.pallas_call(
        body, name=name, out_shape=jax.ShapeDtypeStruct((4, n_rows, n_cols), parts.dtype),
        in_specs=[_ANY], out_specs=_ANY,
        scratch_shapes=[pltpu.SemaphoreType.DMA, pltpu.SemaphoreType.DMA],
    )(parts)


class CommHook:
    def __init__(self, args, out_shapes, n_copies, copies, aliases=None):
        self.args, self.out_shapes, self.n_copies, self.copies = list(args), list(out_shapes), n_copies, copies
        self.aliases = aliases or {}

    def scratch(self):
        return [pltpu.SemaphoreType.DMA((self.n_copies,)), pltpu.SemaphoreType.DMA((self.n_copies,)),
                pltpu.SemaphoreType.DMA((1,))]

    def start(self, in_refs, out_refs, sems):
        sends, _, locs = self.copies(in_refs, out_refs, *sems)
        for cp in locs() + sends():
            cp.start()

    def wait(self, in_refs, out_refs, sems):
        sends, recvs, locs = self.copies(in_refs, out_refs, *sems)
        for cp in sends():
            cp.wait_send()
        for cp in recvs():
            cp.wait_recv()
        for cp in locs():
            cp.wait()


def _remote(src, dst, send_sem, recv_sem, to):
    return pltpu.make_async_remote_copy(src_ref=src, dst_ref=dst, send_sem=send_sem, recv_sem=recv_sem,
                                        device_id=to, device_id_type=MESH)


def hook_gather_first(mine):
    n_rows, n_cols = mine.shape

    def copies(ins, outs, send, recv, local):
        (x_ref,), (out_ref,) = ins, outs
        x, y, c = _me()
        me = c * 4 + x * 2 + y
        peers = [(x, y, 1 - c), (1 - x, y, c), (x, 1 - y, c), (1 - x, 1 - y, c)]

        def sends():
            return [_remote(x_ref, out_ref.at[me], send.at[k], recv.at[k], p) for k, p in enumerate(peers)]

        def recvs():
            return [_remote(x_ref, out_ref.at[pc * 4 + px * 2 + py], send.at[k], recv.at[k], (px, py, pc))
                    for k, (px, py, pc) in enumerate(peers)]

        return sends, recvs, lambda: [pltpu.make_async_copy(x_ref, out_ref.at[me], local.at[0])]

    return CommHook([mine], [jax.ShapeDtypeStruct((N_DEV, n_rows, n_cols), mine.dtype)], 4, copies)


def hook_gather_second(gathered):
    def copies(ins, outs, send, recv, local):
        (g_in,), (g_out,) = ins, outs
        x, y, c = _me()
        chips = [(1 - x, y), (x, 1 - y), (1 - x, 1 - y)]

        def sends():
            return [_remote(g_in.at[c * 4 + px * 2 + py], g_out.at[c * 4 + px * 2 + py], send.at[k], recv.at[k],
                            (x, y, 1 - c)) for k, (px, py) in enumerate(chips)]

        def recvs():
            return [_remote(g_in.at[(1 - c) * 4 + px * 2 + py], g_out.at[(1 - c) * 4 + px * 2 + py], send.at[k],
                            recv.at[k], (x, y, 1 - c)) for k, (px, py) in enumerate(chips)]

        return sends, recvs, lambda: []

    return CommHook([gathered], [jax.ShapeDtypeStruct(gathered.shape, gathered.dtype)], 3, copies, aliases={0: 0})


def hook_swap_sibling(parts):
    _, n_rows, n_cols = parts.shape

    def copies(ins, outs, send, recv, local):
        (p_ref,), (got_ref,) = ins, outs
        x, y, c = _me()

        def swap():
            return [_remote(p_ref.at[pl.ds((1 - c) * 4, 4)], got_ref, send.at[0], recv.at[0], (x, y, 1 - c))]

        return swap, swap, lambda: []

    return CommHook([parts], [jax.ShapeDtypeStruct((4, n_rows, n_cols), parts.dtype)], 1, copies)


def hook_swap_chips(parts):
    _, n_rows, n_cols = parts.shape

    def copies(ins, outs, send, recv, local):
        (p_ref,), (got_ref,) = ins, outs
        x, y, c = _me()
        chips = [(1 - x, y), (x, 1 - y), (1 - x, 1 - y)]

        def swaps():
            return [_remote(p_ref.at[2 * px + py], got_ref.at[k], send.at[k], recv.at[k], (px, py, c))
                    for k, (px, py) in enumerate(chips)]

        return swaps, swaps, lambda: []

    return CommHook([parts], [jax.ShapeDtypeStruct((3, n_rows, n_cols), parts.dtype)], 3, copies)


def _with_comm(comm, n_args, n_outs):
    if comm is None:
        return [], [], [], [], {}
    aliases = {n_args + a: n_outs + o for a, o in comm.aliases.items()}
    return [_ANY] * len(comm.args), [_ANY] * len(comm.out_shapes), comm.out_shapes, comm.scratch(), aliases


def comm_allreduce_small(name, mine):
    shape = mine.shape

    def body(x_ref, out_ref, buf, send_sems, recv_sems):
        x, y, c = _me()
        my_slot = c * 4 + x * 2 + y
        buf[my_slot] = x_ref[...]
        cps = []
        for k in range(1, N_DEV):
            dx, dy, dc = (k >> 2) & 1, (k >> 1) & 1, k & 1
            px, py, pc = x ^ dx, y ^ dy, c ^ dc
            send = pltpu.make_async_remote_copy(
                src_ref=x_ref, dst_ref=buf.at[my_slot], send_sem=send_sems.at[k - 1], recv_sem=recv_sems.at[k - 1],
                device_id=(px, py, pc), device_id_type=MESH)
            send.start()
            recv = pltpu.make_async_remote_copy(
                src_ref=x_ref, dst_ref=buf.at[pc * 4 + px * 2 + py], send_sem=send_sems.at[k - 1],
                recv_sem=recv_sems.at[k - 1], device_id=(px, py, pc), device_id_type=MESH)
            cps.append((send, recv))
        for send, recv in cps:
            send.wait_send()
            recv.wait_recv()
        total = buf[0]
        for s in range(1, N_DEV):
            total = total + buf[s]
        out_ref[...] = total

    vmem = pl.BlockSpec(memory_space=pltpu.VMEM)
    return pl.pallas_call(
        body, name=name, out_shape=jax.ShapeDtypeStruct(shape, F32), in_specs=[vmem], out_specs=vmem,
        scratch_shapes=[pltpu.VMEM((N_DEV,) + shape, F32), pltpu.SemaphoreType.DMA((7,)), pltpu.SemaphoreType.DMA((7,))],
    )(mine)


def _packed_rows(shape):
    rs, cs = shape
    if cs % PACK_C == 0:
        return rs * (cs // PACK_C)
    if cs > PACK_C // 2:
        return rs
    return rs * cs // PACK_C


def _to_rows(a):
    rs, cs = a.shape
    if cs % PACK_C == 0:
        return jnp.concatenate([a[:, i * PACK_C:(i + 1) * PACK_C] for i in range(cs // PACK_C)], axis=0)
    if cs > PACK_C // 2:
        return jnp.pad(a, ((0, 0), (0, PACK_C - cs)))
    return a.reshape(-1, PACK_C)


def _from_rows(rows, shape):
    rs, cs = shape
    lead = rows.shape[:-2]
    if cs % PACK_C == 0:
        return jnp.concatenate([rows[..., i * rs:(i + 1) * rs, :] for i in range(cs // PACK_C)], axis=-1)
    if cs > PACK_C // 2:
        return rows[..., :cs]
    return rows.reshape(*lead, rs, cs)


def pack_local(shards, names):
    flat = [_to_rows(shards[n]) for n in names]
    rows = sum(f.shape[0] for f in flat)
    pad = (-rows) % 128
    return jnp.concatenate(flat + [jnp.zeros((pad, PACK_C), flat[0].dtype)], axis=0)


def unpack_group(gathered, names, shard_shapes):
    _, n_rows, _ = gathered.shape
    by_block = gathered.reshape(2, 4, n_rows, PACK_C).transpose(1, 0, 2, 3).reshape(N_DEV, n_rows, PACK_C)
    full = {}
    r0 = 0
    for name in names:
        rs, cs = shard_shapes[name]
        nr = _packed_rows((rs, cs))
        piece = _from_rows(by_block[:, r0:r0 + nr, :], (rs, cs))
        r0 += nr
        if name in ROW_SHARDED:
            full[name] = piece.reshape(N_DEV * rs, cs)
        else:
            full[name] = piece.transpose(1, 0, 2).reshape(rs, N_DEV * cs)
    return full


def unpack_a(gathered, shard_shapes):
    full = unpack_group(gathered, GROUP_A, shard_shapes)
    w_in = full.pop("w_in")
    zeros = jnp.zeros((D_MODEL, LAT_W - OFF_FQ - HEADS), w_in.dtype)
    full["w_lat"] = jnp.concatenate([w_in[:, :OFF_FQ], w_in[:, OFF_FF:OFF_G], zeros], axis=1)
    full["w_fox"] = w_in[:, OFF_FQ:OFF_FF]
    full["w_gate"] = w_in[:, OFF_G:]
    w_uq = full.pop("w_uq").reshape(Q_LORA, HEADS, NOPE + ROPE)
    full["w_uq"] = jnp.pad(w_uq, ((0, 0), (0, 0), (0, QK_PAD - NOPE - ROPE))).reshape(Q_LORA, HEADS * QK_PAD)
    w_ukv = full.pop("w_ukv").reshape(KV_LORA, HEADS, 2, NOPE)
    full["w_kv"] = jnp.concatenate([w_ukv[:, :, 0, :].reshape(KV_LORA, HEADS * NOPE),
                                    w_ukv[:, :, 1, :].reshape(KV_LORA, HEADS * HEAD_V)], axis=1)
    return full


def pack_small(vals, loss):
    flat = [vals[n].reshape(-1) for n in SMALL] + [loss.reshape(-1)]
    used = sum(f.shape[0] for f in flat)
    flat.append(jnp.zeros((SMALL_ROWS * PACK_C - used,), F32))
    return jnp.concatenate(flat).reshape(SMALL_ROWS, PACK_C)


def unpack_small(packed):
    flat = packed.reshape(-1)
    out, off = {}, 0
    for n in SMALL:
        out[n] = flat[off:off + SMALL_N[n]]
        off += SMALL_N[n]
    return out, flat[off]


def rope_tables(s_len):
    pos = jnp.arange(s_len, dtype=F32)
    inv = 1.0 / (ROPE_THETA ** (jnp.arange(0, ROPE, 2, dtype=F32) / ROPE))
    ang = pos[:, None] * inv[None, :]
    cos, sin = jnp.cos(ang), jnp.sin(ang)
    zero = jnp.zeros_like(cos)
    c = jnp.concatenate([cos, cos, zero, zero], axis=1)
    s1 = jnp.concatenate([-sin, zero, zero, zero], axis=1)
    s2 = jnp.concatenate([zero, sin, zero, zero], axis=1)
    return (c, s1, s2), (c, -s1, -s2)


def reduce_scatter_tail(parts, from_sibling, from_chips_fn, names):
    cx, cy, cc = _me()
    n_rows = parts.shape[1]
    mine4 = lax.dynamic_slice_in_dim(parts, cc * 4, 4, axis=0)
    pair = add_pairs("rs_pair_sum_" + names, mine4.reshape(4 * n_rows, PACK_C), from_sibling.reshape(4 * n_rows, PACK_C))
    from_chips, extra = from_chips_fn(pair.reshape(4, n_rows, PACK_C))
    own = cx * 2 + cy
    total = add_final("rs_final_sum_" + names, lax.dynamic_index_in_dim(mine4, own, 0, keepdims=False),
                      lax.dynamic_index_in_dim(from_sibling, own, 0, keepdims=False),
                      from_chips[0], from_chips[1], from_chips[2])
    return total, extra


def local_step(x, target, w, small, packed_b, shard_shapes, to_packed_a):
    s_len = x.shape[0]
    tabs, inv_tabs = rope_tables(s_len)
    g_attn = small["attn_norm"].reshape(1, D_MODEL)
    g_q = small["q_norm"].reshape(1, Q_LORA)
    g_kv = small["kv_norm"].reshape(1, KV_LORA)
    g_mlp = small["mlp_norm"].reshape(1, D_MODEL)
    g_final = small["final_norm"].reshape(1, D_MODEL)
    f_bias = small["fox_f_bias"].reshape(HEADS, 1)
    mla_scale = 1.0 / math.sqrt(NOPE + ROPE)
    fox_scale = 1.0 / math.sqrt(HEAD_V)
    mla_offs = (0, 0, HEADS)
    fox_offs = (0, HEADS, 2 * HEADS)

    xn = rms_fwd("rms_attn", x, g_attn)
    lat, = matmul("proj_lat", xn, w["w_lat"], "nn", [F32])
    fox, = matmul("proj_fox", xn, w["w_fox"], "nn", [BF16])
    graw, = matmul("proj_gate", xn, w["w_gate"], "nn", [F32])
    cq = rms_fwd("rms_q", (lat, Q_LORA, 0), g_q)
    ckv = rms_fwd("rms_kv", (lat, KV_LORA, Q_LORA // KV_LORA), g_kv)
    qraw, = matmul("up_q", cq, w["w_uq"], "nn", [F32])
    q = rope_heads("rope_q", qraw, tabs, BF16)
    kvn, = matmul("up_kv", ckv, w["w_kv"], "nn", [BF16])
    kr = rope_block("rope_k", lat, OFF_KR // 128, tabs, BF16)
    k = k_assemble("k_assemble", kvn, kr)
    o_mla, lse_mla, gathered_b = attn_fwd("mla_fwd", q, k, kvn, mla_offs, QK_PAD, CHUNK, mla_scale,
                                          comm=hook_gather_first(packed_b))
    ff_t = lat[:, OFF_FQ:OFF_FQ + HEADS].T
    cum = fox_cum_fwd("fox_cum", ff_t, f_bias).reshape(HEADS, s_len, 1) * LOG2E
    o_fox, lse_fox, gathered_b = attn_fwd("fox_fwd", fox, fox, fox, fox_offs, HEAD_V, 1, fox_scale, cum=cum,
                                          comm=hook_gather_second(gathered_b), split_p=True)
    unpack_b = functools.partial(unpack_group, names=GROUP_B, shard_shapes=shard_shapes)
    w = {**w, **unpack_b(gathered_b)}
    y_mla, = matmul("branch_mla", o_mla, w["w_mla_branch"], "nn", [F32])
    y_fox, = matmul("branch_fox", o_fox, w["w_fox_branch"], "nn", [F32])
    mix = gate_mix("gate_mix", graw, y_mla, y_fox)
    h1, = matmul("out_proj", mix, w["w_out"], "nn", [F32], epilogue=lambda acc, res: (res + acc,), extras=[x])
    hn = rms_fwd("rms_mlp", h1, g_mlp)

    def relu2(acc):
        r = jnp.maximum(acc, 0.0)
        return r * r, r
    u, relu_up = matmul("mlp_up", hn, w["w_up"], "nn", [BF16, BF16], epilogue=relu2)
    h2, = matmul("mlp_down", u, w["w_down"], "nn", [F32], epilogue=lambda acc, res: (res + acc,), extras=[h1])
    dh2, dh2_b, d_final, loss = loss_head("loss_head", h2, target, g_final)

    grads = {}
    dup, = matmul("d_mlp_down", dh2_b, w["w_down"], "nt", [BF16],
                  epilogue=lambda acc, r: (acc * (2.0 * r.astype(F32)),), extras=[relu_up])
    grads["w_down"], = matmul("gw_down", u, dh2_b, "tn", [BF16], **{**TN_TILES, "tk": LONG_K})
    dhn, = matmul("d_mlp_up", dup, w["w_up"], "nt", [F32], tk=LONG_K)
    grads["w_up"], = matmul("gw_up", hn, dup, "tn", [BF16], **{**TN_TILES, "tk": LONG_K})
    dh1, d_mlp = rms_bwd("rms_mlp_bwd", h1, dhn, g_mlp, dres=dh2)
    dmix, = matmul("d_out_proj", dh1, w["w_out"], "nt", [F32])
    grads["w_out"], = matmul("gw_out", mix, dh1, "tn", [BF16], **TN_TILES)
    dproj, dy_mla, dy_fox = gate_mix_bwd("gate_mix_bwd", graw, y_mla, y_fox, dmix)
    do_mla, = matmul("d_branch_mla", dy_mla, w["w_mla_branch"], "nt", [BF16])
    grads["w_mla_branch"], = matmul("gw_branch_mla", o_mla, dy_mla, "tn", [BF16], **TN_TILES)
    do_fox, = matmul("d_branch_fox", dy_fox, w["w_fox_branch"], "nt", [BF16])
    grads["w_fox_branch"], = matmul("gw_branch_fox", o_fox, dy_fox, "tn", [BF16], **TN_TILES)

    to_packed_b = jax.linear_transpose(unpack_b, jax.ShapeDtypeStruct(gathered_b.shape, BF16))
    parts_b, = to_packed_b({n: grads.pop(n) for n in GROUP_B})
    delta_mla = attn_delta("mla_delta", do_mla, o_mla)
    dk, dv, dq, from_sibling = attn_bwd("mla_bwd", q, k, kvn, do_mla, lse_mla, delta_mla, mla_offs, QK_PAD, CHUNK,
                                        mla_scale, F32, comm=hook_swap_sibling(parts_b))

    delta_fox = attn_delta("fox_delta", do_fox, o_fox)

    def chips_behind_fox_bwd(pair):
        dfk, dfv, dfq, dcum, from_chips = attn_bwd("fox_bwd", fox, fox, fox, do_fox, lse_fox, delta_fox, fox_offs,
                                                   HEAD_V, 1, fox_scale, BF16, cum=cum, comm=hook_swap_chips(pair))
        return from_chips, (dfq.astype(BF16), dfk, dfv, dcum)
    g_packed_b, (dfq, dfk, dfv, dcum) = reduce_scatter_tail(parts_b, from_sibling, chips_behind_fox_bwd, "b")
    dff_t, d_bias = fox_cum_bwd("fox_cum_bwd", ff_t, f_bias, dcum.reshape(HEADS, s_len))
    dq_r = rope_heads("rope_q_bwd", dq, inv_tabs, BF16)
    dkvn, dkr = dk_split("dk_split", dk, dv, inv_tabs)
    dcq, = matmul("d_up_q", dq_r, w["w_uq"], "nt", [F32])
    grads["w_uq"], = matmul("gw_uq", cq, dq_r, "tn", [BF16], tm=512, tn=2048, tk=2048)
    dckv, = matmul("d_up_kv", dkvn, w["w_kv"], "nt", [F32])
    grads["w_kv"], = matmul("gw_kv", ckv, dkvn, "tn", [BF16], tm=256, tn=2048, tk=2048)
    dcq_raw, d_qn = rms_bwd("rms_q_bwd", (lat, Q_LORA, 0), dcq, g_q, out_dtype=BF16)
    dckv_raw, d_kvn = rms_bwd("rms_kv_bwd", (lat, KV_LORA, Q_LORA // KV_LORA), dckv, g_kv, out_dtype=BF16)

    pad = jnp.zeros((s_len, LAT_W - OFF_FQ - HEADS), BF16)
    tail = jnp.concatenate([dkr[:, :ROPE].astype(BF16), dff_t.T.astype(BF16), pad], axis=1)
    col = GATE_W
    for piece in (dfq, dfk, dfv, dcq_raw, dckv_raw, tail):
        dproj = lax.dynamic_update_slice(dproj, piece, (0, col))
        col += piece.shape[1]
    w_in_p = jnp.concatenate([w["w_gate"], w["w_fox"], w["w_lat"]], axis=1)
    for part, col0, width in (("w_gate", 0, GATE_W), ("w_fox", GATE_W, FOX_W), ("w_lat", GATE_W + FOX_W, LAT_W)):
        grads[part], = matmul("gw_in_" + part, xn, dproj, "tn", [BF16], **{**TN_TILES, "tk": LONG_K},
                              b_cols=(col0, width))

    parts_a = to_packed_a(grads)
    from_sibling_a = comm_swap_sibling("comm_rs_sibling_a", parts_a)

    def chips_behind_d_proj(pair):
        dxn, from_chips = matmul("d_proj", dproj, w_in_p, "nt", [F32], tk=2688, comm=hook_swap_chips(pair))
        return from_chips, dxn
    g_packed_a, dxn = reduce_scatter_tail(parts_a, from_sibling_a, chips_behind_d_proj, "a")
    dx, d_attn = rms_bwd("rms_attn_bwd", x, dxn, g_attn, dres=dh1)

    small_grads = {"attn_norm": d_attn, "fox_f_bias": d_bias, "q_norm": d_qn, "kv_norm": d_kvn,
                   "mlp_norm": d_mlp, "final_norm": d_final}
    return loss, dx, g_packed_a, g_packed_b, small_grads


def kernel(x, attn_norm, w_in, fox_f_bias, q_norm, w_uq, kv_norm, w_ukv, w_mla_branch, w_fox_branch, w_out, mlp_norm, w_up, w_down, final_norm, loss_target, m_attn_norm, m_w_in, m_fox_f_bias, m_q_norm, m_w_uq, m_kv_norm, m_w_ukv, m_w_mla_branch, m_w_fox_branch, m_w_out, m_mlp_norm, m_w_up, m_w_down, m_final_norm, v_attn_norm, v_w_in, v_fox_f_bias, v_q_norm, v_w_uq, v_kv_norm, v_w_ukv, v_w_mla_branch, v_w_fox_branch, v_w_out, v_mlp_norm, v_w_up, v_w_down, v_final_norm):
    given = dict(locals())
    big = {n: given[n][0] for n in BIG}
    small = {n: given[n] for n in SMALL}
    shard_shapes = {n: tuple(big[n].shape) for n in BIG}

    packed_a = pack_local({n: big[n].astype(BF16) for n in GROUP_A}, GROUP_A)
    packed_b = pack_local({n: big[n].astype(BF16) for n in GROUP_B}, GROUP_B)
    gathered_a = comm_allgather("comm_allgather_a", packed_a)
    unpack = functools.partial(unpack_a, shard_shapes=shard_shapes)
    w_a = unpack(gathered_a)

    transpose_a = jax.linear_transpose(unpack, jax.ShapeDtypeStruct(gathered_a.shape, BF16))
    loss_part, dx, g_packed_a, g_packed_b, small_grads = local_step(
        x[0], loss_target[0], w_a, small, packed_b, shard_shapes, lambda grads: transpose_a(grads)[0])

    small_sum = comm_allreduce_small("comm_allreduce_small", pack_small(small_grads, loss_part[0, 0]))
    g_small, loss = unpack_small(small_sum)

    grad_w, delta_w, new_m, new_v = {}, {}, {}, {}
    for names, g_packed in ((GROUP_A, g_packed_a), (GROUP_B, g_packed_b)):
        r0 = 0
        for n in names:
            rs, cs = shard_shapes[n]
            nr = _packed_rows((rs, cs))
            g = _from_rows(g_packed[r0:r0 + nr], (rs, cs))
            r0 += nr
            d, m_new, v_new = adamw("adamw_" + n, big[n], g, given["m_" + n][0], given["v_" + n][0])
            grad_w[n], delta_w[n], new_m[n], new_v[n] = g[None], d[None], m_new[None], v_new[None]
    zero = jnp.zeros((), F32)
    d_s, m_s, v_s = adamw("adamw_small", pack_small(small, zero), small_sum * _small_mask(),
                          pack_small({n: given["m_" + n] for n in SMALL}, zero),
                          pack_small({n: given["v_" + n] for n in SMALL}, zero), tr=SMALL_ROWS)
    d_small, _ = unpack_small(d_s)
    m_small, _ = unpack_small(m_s)
    v_small, _ = unpack_small(v_s)
    for n in SMALL:
        shape = given[n].shape
        grad_w[n], delta_w[n] = g_small[n].reshape(shape), d_small[n].reshape(shape)
        new_m[n], new_v[n] = m_small[n].reshape(shape), v_small[n].reshape(shape)

    order = ["attn_norm", "w_in", "fox_f_bias", "q_norm", "w_uq", "kv_norm", "w_ukv", "w_mla_branch", "w_fox_branch",
             "w_out", "mlp_norm", "w_up", "w_down", "final_norm"]
    return (loss, dx[None], *[grad_w[n] for n in order], *[delta_w[n] for n in order],
            *[new_m[n] for n in order], *[new_v[n] for n in order])


def _small_mask():
    used = sum(SMALL_N[n] for n in SMALL)
    return (jnp.arange(SMALL_ROWS * PACK_C) < used).astype(F32).reshape(SMALL_ROWS, PACK_C)
```
